```python
import math
import jax
import jax.numpy as jnp
from jax import lax
import numpy as np

D_MODEL = 4096
BATCH = 4
SEQ = 2048
DEPTH = 2

N_BRANCHES = 4
BRANCH_WIDTH = D_MODEL // N_BRANCHES
MLSTM_HEADS = 4
MLSTM_DV = BRANCH_WIDTH // MLSTM_HEADS
MLSTM_DQK = MLSTM_DV // 2
MLSTM_CONV = 4
MLSTM_CHUNK = 64
MLSTM_FORGET_BIAS = 3.0
CONV_WIDTH = 31
CONV_GROUPS = 4
MOBA_HEADS = 8
MOBA_HD = BRANCH_WIDTH // MOBA_HEADS
MOBA_BLOCK = 256
MOBA_TOPK = 3
MOBA_Q_CHUNK = 32
REL_BUCKETS = 32
REL_MAX_DIST = 128
GLA_HEADS = 4
GLA_DV = BRANCH_WIDTH // GLA_HEADS
GLA_DK = GLA_DV // 2
GLA_RANK = 16
GLA_TAU = 16.0
GLA_CHUNK = 16
N_EXPERTS = 32
TOP_K = 4
EXPERT_FF = D_MODEL // 8
SWIGLU_ALPHA = 1.702
SWIGLU_LIMIT = 7.0
DEEPNORM_ALPHA = (2 * DEPTH) ** 0.25
DEEPNORM_BETA = (8 * DEPTH) ** -0.25
LN_EPS = 1e-5

IN_SIZES = (
    MLSTM_HEADS * MLSTM_DQK,
    MLSTM_HEADS * MLSTM_DQK,
    BRANCH_WIDTH,
    BRANCH_WIDTH,
    2 * MLSTM_HEADS,
    2 * BRANCH_WIDTH,
    3 * BRANCH_WIDTH,
    GLA_HEADS * GLA_DK,
    GLA_HEADS * GLA_DK,
    BRANCH_WIDTH,
    GLA_RANK,
    BRANCH_WIDTH,
    N_BRANCHES * D_MODEL,
)
IN_WIDTH = sum(IN_SIZES)

kernel_name = 'hybrid_mlstm_conv_moba_gla_moe_block'


def layer_norm(x, g, b):
    xf = x.astype(jnp.float32)
    mu = jnp.mean(xf, axis=-1, keepdims=True)
    var = jnp.mean(jnp.square(xf - mu), axis=-1, keepdims=True)
    return ((xf - mu) * lax.rsqrt(var + LN_EPS) * g + b).astype(x.dtype)


def group_norm(x, n_groups, gain, bias=None, rms=False):
    shp = x.shape
    xf = x.astype(jnp.float32).reshape(shp[:-1] + (n_groups, shp[-1] // n_groups))
    if not rms:
        xf = xf - jnp.mean(xf, axis=-1, keepdims=True)
    xf = xf * lax.rsqrt(jnp.mean(jnp.square(xf), axis=-1, keepdims=True) + LN_EPS)
    y = xf.reshape(shp) * gain
    if bias is not None:
        y = y + bias
    return y.astype(x.dtype)


def causal_depthwise_conv(x, w, b):
    width, ch = w.shape
    y = lax.conv_general_dilated(x, w[:, None, :].astype(x.dtype), window_strides=(1,),
                                 padding=[(width - 1, 0)],
                                 dimension_numbers=('NWC', 'WIO', 'NWC'),
                                 feature_group_count=ch)
    return y + b


def t5_bucket(dist):
    max_exact = REL_BUCKETS // 2
    dist = jnp.maximum(dist, 0)
    far = max_exact + (jnp.log(jnp.maximum(dist, 1).astype(jnp.float32) / max_exact)
                       / math.log(REL_MAX_DIST / max_exact)
                       * (REL_BUCKETS - max_exact)).astype(jnp.int32)
    return jnp.where(dist < max_exact, dist, jnp.minimum(far, REL_BUCKETS - 1))


def mlstm_chunkwise(q, k, v, i_pre, f_pre):
    B, S, H, dqk = q.shape
    dv = v.shape[-1]
    L = MLSTM_CHUNK
    nc = S // L

    def chunks(a):
        a = a.astype(jnp.float32).reshape((B, nc, L, H) + a.shape[3:])
        return jnp.moveaxis(a, 3, 1)

    qc, kc, vc = chunks(q), chunks(k) * dqk ** -0.5, chunks(v)
    ig = chunks(i_pre)
    bcum = jnp.cumsum(jax.nn.log_sigmoid(chunks(f_pre)), axis=-1)
    b_last = bcum[..., -1]
    causal = jnp.tril(jnp.ones((L, L), dtype=bool))
    dmat = jnp.where(causal, bcum[..., :, None] - bcum[..., None, :] + ig[..., None, :], -jnp.inf)
    a_end = b_last[..., None] - bcum + ig
    a_max = jnp.max(a_end, axis=-1)
    w_end = jnp.exp(a_end - a_max[..., None])
    kv_chunk = jnp.einsum('bhnld,bhnlv->bhndv', kc * w_end[..., None], vc)
    k_chunk = jnp.einsum('bhnl,bhnld->bhnd', w_end, kc)

    def step(carry, xs):
        c_st, n_st, m_st = carry
        bl, am, kv, ks = xs
        m_new = jnp.maximum(bl + m_st, am)
        f_sc = jnp.exp(bl + m_st - m_new)
        i_sc = jnp.exp(am - m_new)
        c_new = f_sc[..., None, None] * c_st + i_sc[..., None, None] * kv
        n_new = f_sc[..., None] * n_st + i_sc[..., None] * ks
        return (c_new, n_new, m_new), (c_st, n_st, m_st)

    init = (jnp.zeros((B, H, dqk, dv), jnp.float32), jnp.zeros((B, H, dqk), jnp.float32),
            jnp.zeros((B, H), jnp.float32))
    xs = (jnp.moveaxis(b_last, 2, 0), jnp.moveaxis(a_max, 2, 0),
          jnp.moveaxis(kv_chunk, 2, 0), jnp.moveaxis(k_chunk, 2, 0))
    _, (c_prev, n_prev, m_prev) = lax.scan(step, init, xs)
    c_prev = jnp.moveaxis(c_prev, 0, 2)
    n_prev = jnp.moveaxis(n_prev, 0, 2)
    m_prev = jnp.moveaxis(m_prev, 0, 2)
    inter = bcum + m_prev[..., None]
    m_t = jnp.maximum(inter, jnp.max(dmat, axis=-1))
    s_qk = jnp.einsum('bhntd,bhnsd->bhnts', qc, kc) * jnp.exp(dmat - m_t[..., None])
    w_inter = jnp.exp(inter - m_t)
    num = (jnp.einsum('bhnts,bhnsv->bhntv', s_qk, vc)
           + w_inter[..., None] * jnp.einsum('bhntd,bhndv->bhntv', qc, c_prev))
    den = jnp.sum(s_qk, axis=-1) + w_inter * jnp.einsum('bhntd,bhnd->bhnt', qc, n_prev)
    h = num / jnp.maximum(jnp.abs(den), jnp.exp(-m_t))[..., None]
    return jnp.moveaxis(h, 1, 3).reshape(B, S, H, dv)


def gla_chunkwise(q, k, v, log_a):
    B, S, H, dk = q.shape
    dv = v.shape[-1]
    C = GLA_CHUNK
    nc = S // C

    def chunks(a):
        return jnp.moveaxis(a.astype(jnp.float32).reshape(B, nc, C, H, a.shape[-1]), 3, 1)

    qc, kc, vc = chunks(q) * dk ** -0.5, chunks(k), chunks(v)
    bcum = jnp.cumsum(chunks(log_a), axis=3)
    b_last = bcum[..., -1, :]
    causal = jnp.tril(jnp.ones((C, C), dtype=bool))
    decay = jnp.exp(jnp.where(causal[..., None],
                              bcum[..., :, None, :] - bcum[..., None, :, :], -jnp.inf))
    attn = jnp.einsum('bhntsd,bhnsd->bhnts', decay * qc[..., :, None, :], kc)
    o_intra = jnp.einsum('bhnts,bhnsv->bhntv', attn, vc)
    kv_chunk = jnp.einsum('bhnsd,bhnsv->bhndv', kc * jnp.exp(b_last[..., None, :] - bcum), vc)

    def step(state, xs):
        dec, kv = xs
        return dec[..., None] * state + kv, state

    _, s_prev = lax.scan(step, jnp.zeros((B, H, dk, dv), jnp.float32),
                         (jnp.moveaxis(jnp.exp(b_last), 2, 0), jnp.moveaxis(kv_chunk, 2, 0)))
    s_prev = jnp.moveaxis(s_prev, 0, 2)
    o = o_intra + jnp.einsum('bhntd,bhndv->bhntv', qc * jnp.exp(bcum), s_prev)
    return jnp.moveaxis(o, 1, 3).reshape(B, S, H, dv)


def moba_attention(q, k, v, rel_bias):
    B, S, H, hd = q.shape
    n_blk = -(-S // MOBA_BLOCK)
    pad = n_blk * MOBA_BLOCK - S

    def blocks(a):
        a = jnp.pad(a, ((0, 0), (0, pad), (0, 0), (0, 0)))
        return jnp.moveaxis(a.reshape(B, n_blk, MOBA_BLOCK, H, hd), 3, 1)

    kb, vb = blocks(k), blocks(v)
    k_mean = jnp.mean(kb, axis=3)
    n_sel = min(MOBA_TOPK, n_blk - 1)
    scale = hd ** -0.5
    bias_hb = rel_bias.T.astype(jnp.float32)
    head_ix = jnp.arange(H)
    blk_pos = jnp.arange(MOBA_BLOCK)
    gather_blocks = jax.vmap(jax.vmap(lambda arr, ix: arr[ix]))

    def chunk(start):
        qc = jnp.moveaxis(lax.dynamic_slice_in_dim(q, start, MOBA_Q_CHUNK, axis=1), 2, 1)
        t = start + jnp.arange(MOBA_Q_CHUNK)
        own = start // MOBA_BLOCK
        k_own = lax.dynamic_index_in_dim(kb, own, axis=2, keepdims=False)
        v_own = lax.dynamic_index_in_dim(vb, own, axis=2, keepdims=False)
        d_own = t[:, None] - (own * MOBA_BLOCK + blk_pos)[None, :]
        lg_own = (jnp.einsum('bhqd,bhkd->bhqk', qc, k_own).astype(jnp.float32) * scale
                  + bias_hb[:, t5_bucket(d_own)])
        lg_own = jnp.where(d_own >= 0, lg_own, -jnp.inf)
        if n_sel == 0:
            p_own = jax.nn.softmax(lg_own, axis=-1).astype(v.dtype)
            out = jnp.einsum('bhqk,bhkd->bhqd', p_own, v_own)
        else:
            gate = jnp.einsum('bhqd,bhnd->bhqn', qc, k_mean).astype(jnp.float32)
            gate = jnp.where(jnp.arange(n_blk) < own, gate, -jnp.inf)
            _, idx = lax.top_k(gate, n_sel)
            k_sel = gather_blocks(kb, idx)
            v_sel = gather_blocks(vb, idx)
            d_sel = t[:, None, None] - (idx[..., None] * MOBA_BLOCK + blk_pos)
            lg_sel = (jnp.einsum('bhqd,bhqnkd->bhqnk', qc, k_sel).astype(jnp.float32) * scale
                      + bias_hb[head_ix[None, :, None, None, None], t5_bucket(d_sel)])
            valid = jnp.arange(n_sel) < own
            lg_sel = jnp.where(valid[:, None], lg_sel, -jnp.inf)
            lg = jnp.concatenate(
                [lg_own, lg_sel.reshape(B, H, MOBA_Q_CHUNK, n_sel * MOBA_BLOCK)], axis=-1)
            p = jax.nn.softmax(lg, axis=-1).astype(v.dtype)
            p_own = p[..., :MOBA_BLOCK]
            p_sel = p[..., MOBA_BLOCK:].reshape(B, H, MOBA_Q_CHUNK, n_sel, MOBA_BLOCK)
            out = (jnp.einsum('bhqk,bhkd->bhqd', p_own, v_own)
                   + jnp.einsum('bhqnk,bhqnkd->bhqd', p_sel, v_sel))
        return jnp.moveaxis(out, 1, 2)

    outs = lax.map(chunk, jnp.arange(0, S, MOBA_Q_CHUNK))
    return jnp.moveaxis(outs, 0, 1).reshape(B, S, H, hd)


def hybrid_mixer(u, w_in, b_merge, mlstm_gate_b, mlstm_conv_w, mlstm_conv_b, mlstm_norm_g,
                 conf_dw_w, conf_dw_b, conf_norm_g, conf_norm_b, rel_bias,
                 gla_gate_w2, gla_gate_b, gla_norm_g, w_branch, w_out):
    B, S, _ = u.shape
    W = BRANCH_WIDTH
    proj = u @ w_in
    offs = np.cumsum(IN_SIZES)[:-1].tolist()
    (mq, mk, mv, mo, mif, conf_in, moba_qkv, gq, gk, gv, g_lr, g_out,
     merge_pre) = jnp.split(proj, offs, axis=-1)

    qk = jax.nn.silu(causal_depthwise_conv(jnp.concatenate([mq, mk], axis=-1),
                                           mlstm_conv_w, mlstm_conv_b))
    mq, mk = jnp.split(qk, 2, axis=-1)
    gates = (mif + mlstm_gate_b).astype(jnp.float32)
    h = mlstm_chunkwise(mq.reshape(B, S, MLSTM_HEADS, MLSTM_DQK),
                        mk.reshape(B, S, MLSTM_HEADS, MLSTM_DQK),
                        mv.reshape(B, S, MLSTM_HEADS, MLSTM_DV),
                        gates[..., :MLSTM_HEADS], gates[..., MLSTM_HEADS:])
    h = group_norm(h.reshape(B, S, W).astype(u.dtype), MLSTM_HEADS, mlstm_norm_g)
    y_mlstm = jax.nn.sigmoid(mo) * h

    a, g = jnp.split(conf_in, 2, axis=-1)
    y = causal_depthwise_conv(a * jax.nn.sigmoid(g), conf_dw_w, conf_dw_b)
    y_conv = jax.nn.silu(group_norm(y, CONV_GROUPS, conf_norm_g, conf_norm_b))

    aq, ak, av = jnp.split(moba_qkv, 3, axis=-1)
    shp = (B, S, MOBA_HEADS, MOBA_HD)
    y_moba = moba_attention(aq.reshape(shp), ak.reshape(shp), av.reshape(shp),
                            rel_bias).reshape(B, S, W)

    log_a = jax.nn.log_sigmoid((g_lr @ gla_gate_w2 + gla_gate_b).astype(jnp.float32)) / GLA_TAU
    o = gla_chunkwise(gq.reshape(B, S, GLA_HEADS, GLA_DK), gk.reshape(B, S, GLA_HEADS, GLA_DK),
                      gv.reshape(B, S, GLA_HEADS, GLA_DV), log_a.reshape(B, S, GLA_HEADS, GLA_DK))
    o = group_norm(o.reshape(B, S, W).astype(u.dtype), GLA_HEADS, gla_norm_g, rms=True)
    y_gla = o * jax.nn.silu(g_out)

    branches = jnp.stack([y_mlstm, y_conv, y_moba, y_gla], axis=2)
    proj_b = jnp.einsum('bsnk,nkd->bsnd', branches, w_branch)
    gate = jax.nn.sigmoid((merge_pre + b_merge).reshape(B, S, N_BRANCHES, D_MODEL))
    merged = jnp.sum(gate * proj_b, axis=2)
    return merged @ w_out


def moe_ffn(u, router_w, router_b, w_up, b_up, w_down, b_down):
    B, S, D = u.shape
    tok = u.reshape(B * S, D)
    logits = (tok @ router_w + router_b).astype(jnp.float32)
    top_val, top_idx = lax.top_k(logits, TOP_K)
    top_w = jax.nn.softmax(top_val, axis=-1)
    combine = jnp.einsum('tk,tke->te', top_w,
                         jax.nn.one_hot(top_idx, N_EXPERTS, dtype=jnp.float32))
    out = jnp.zeros((B * S, D), jnp.float32)
    for e in range(N_EXPERTS):
        hid = tok @ w_up[e] + b_up[e]
        glu, lin = jnp.split(hid, 2, axis=-1)
        glu = jnp.minimum(glu, SWIGLU_LIMIT)
        lin = jnp.clip(lin, -SWIGLU_LIMIT, SWIGLU_LIMIT)
        act = glu * jax.nn.sigmoid(SWIGLU_ALPHA * glu) * (lin + 1)
        out = out + combine[:, e:e + 1] * (act @ w_down[e] + b_down[e])
    return out.reshape(B, S, D).astype(u.dtype)


def setup_inputs(seed: int = 0) -> dict:
    key = jax.random.key(seed)
    keys = iter(jax.random.split(key, 40))

    def nrm(shape, scale):
        return jax.random.normal(next(keys), shape, jnp.float32) * scale

    L, D, W, E, F = DEPTH, D_MODEL, BRANCH_WIDTH, N_EXPERTS, EXPERT_FF
    if_base = jnp.concatenate([jnp.zeros((MLSTM_HEADS,), jnp.float32),
                               jnp.full((MLSTM_HEADS,), MLSTM_FORGET_BIAS, jnp.float32)])
    return {
        'x': nrm((BATCH, SEQ, D), 1.0),
        'c': nrm((BATCH, D), 1.0),
        'w_ada': nrm((L, D, 6 * D), 0.5 * D ** -0.5),
        'b_ada': nrm((L, 6 * D), 0.02),
        'w_in': nrm((L, D, IN_WIDTH), D ** -0.5),
        'b_merge': nrm((L, N_BRANCHES * D), 0.1),
        'mlstm_gate_b': if_base + nrm((L, 2 * MLSTM_HEADS), 0.1),
        'mlstm_conv_w': nrm((L, MLSTM_CONV, 2 * MLSTM_HEADS * MLSTM_DQK), MLSTM_CONV ** -0.5),
        'mlstm_conv_b': nrm((L, 2 * MLSTM_HEADS * MLSTM_DQK), 0.02),
        'mlstm_norm_g': 1.0 + nrm((L, W), 0.05),
        'conf_dw_w': nrm((L, CONV_WIDTH, W), CONV_WIDTH ** -0.5),
        'conf_dw_b': nrm((L, W), 0.02),
        'conf_norm_g': 1.0 + nrm((L, W), 0.05),
        'conf_norm_b': nrm((L, W), 0.02),
        'rel_bias': nrm((REL_BUCKETS, MOBA_HEADS), 0.5),
        'gla_gate_w2': nrm((L, GLA_RANK, GLA_HEADS * GLA_DK), GLA_RANK ** -0.5),
        'gla_gate_b': nrm((L, GLA_HEADS * GLA_DK), 0.1),
        'gla_norm_g': 1.0 + nrm((L, W), 0.05),
        'w_branch': nrm((L, N_BRANCHES, W, D), W ** -0.5),
        'w_out': nrm((L, D, D), DEEPNORM_BETA * D ** -0.5),
        'ln1_g': 1.0 + nrm((L, D), 0.02),
        'ln1_b': nrm((L, D), 0.02),
        'router_w': nrm((L, D, E), D ** -0.5),
        'router_b': nrm((L, E), 0.01),
        'exp_w_up': nrm((L, E, D, 2 * F), D ** -0.5),
        'exp_b_up': nrm((L, E, 2 * F), 0.02),
        'exp_w_down': nrm((L, E, F, D), DEEPNORM_BETA * F ** -0.5),
        'exp_b_down': nrm((L, E, D), 0.02),
        'ln2_g': 1.0 + nrm((L, D), 0.02),
        'ln2_b': nrm((L, D), 0.02),
    }


def reference(x, c, w_ada, b_ada, w_in, b_merge, mlstm_gate_b, mlstm_conv_w, mlstm_conv_b,
              mlstm_norm_g, conf_dw_w, conf_dw_b, conf_norm_g, conf_norm_b, rel_bias,
              gla_gate_w2, gla_gate_b, gla_norm_g, w_branch, w_out, ln1_g, ln1_b,
              router_w, router_b, exp_w_up, exp_b_up, exp_w_down, exp_b_down, ln2_g, ln2_b):
    mod_all = jnp.einsum('bd,ldk->lbk', jax.nn.silu(c), w_ada) + b_ada[:, None, :]
    for l in range(DEPTH):
        sh1, sc1, g1, sh2, sc2, g2 = jnp.split(mod_all[l][:, None, :], 6, axis=-1)
        u = x * (1 + sc1) + sh1
        mix = hybrid_mixer(u, w_in[l], b_merge[l], mlstm_gate_b[l], mlstm_conv_w[l],
                           mlstm_conv_b[l], mlstm_norm_g[l], conf_dw_w[l], conf_dw_b[l],
                           conf_norm_g[l], conf_norm_b[l], rel_bias, gla_gate_w2[l],
                           gla_gate_b[l], gla_norm_g[l], w_branch[l], w_out[l])
        x = layer_norm(DEEPNORM_ALPHA * x + (1 + g1) * mix, ln1_g[l], ln1_b[l])
        u = x * (1 + sc2) + sh2
        ffn = moe_ffn(u, router_w[l], router_b[l], exp_w_up[l], exp_b_up[l],
                      exp_w_down[l], exp_b_down[l])
        x = layer_norm(DEEPNORM_ALPHA * x + (1 + g2) * ffn, ln2_g[l], ln2_b[l])
    return x
```

```python
import functools
import math

import jax
import jax.numpy as jnp
import numpy as np
from jax import lax
from jax.experimental import pallas as pl
from jax.experimental.pallas import tpu as pltpu

D_MODEL = 4096
BATCH = 4
SEQ = 2048
DEPTH = 2

N_BRANCHES = 4
BRANCH_WIDTH = D_MODEL // N_BRANCHES
MLSTM_HEADS = 4
MLSTM_DV = BRANCH_WIDTH // MLSTM_HEADS
MLSTM_DQK = MLSTM_DV // 2
MLSTM_CONV = 4
MLSTM_CHUNK = 64
CONV_WIDTH = 31
CONV_GROUPS = 4
MOBA_HEADS = 8
MOBA_HD = BRANCH_WIDTH // MOBA_HEADS
MOBA_BLOCK = 256
MOBA_TOPK = 3
MOBA_Q_CHUNK = 32
REL_BUCKETS = 32
REL_MAX_DIST = 128
GLA_HEADS = 4
GLA_DV = BRANCH_WIDTH // GLA_HEADS
GLA_DK = GLA_DV // 2
GLA_RANK = 16
GLA_TAU = 16.0
GLA_CHUNK = 16
N_EXPERTS = 32
TOP_K = 4
EXPERT_FF = D_MODEL // 8
SWIGLU_ALPHA = 1.702
SWIGLU_LIMIT = 7.0
DEEPNORM_ALPHA = (2 * DEPTH) ** 0.25
LN_EPS = 1e-5

IN_SIZES = (
    MLSTM_HEADS * MLSTM_DQK, MLSTM_HEADS * MLSTM_DQK, BRANCH_WIDTH, BRANCH_WIDTH,
    2 * MLSTM_HEADS, 2 * BRANCH_WIDTH, 3 * BRANCH_WIDTH, GLA_HEADS * GLA_DK,
    GLA_HEADS * GLA_DK, BRANCH_WIDTH, GLA_RANK, BRANCH_WIDTH, N_BRANCHES * D_MODEL,
)
IN_WIDTH = sum(IN_SIZES)

VMEM_LIMIT = 56 * 1024 * 1024


def _mm_kernel(a_ref, b_ref, o_ref):
    o_ref[...] = jnp.dot(a_ref[...], b_ref[...],
                         preferred_element_type=jnp.float32).astype(o_ref.dtype)


def matmul(a, b, tm, tn, out_dtype):
    M, K = a.shape
    N = b.shape[1]
    return pl.pallas_call(
        _mm_kernel,
        grid=(N // tn, M // tm),
        in_specs=[pl.BlockSpec((tm, K), lambda j, i: (i, 0)),
                  pl.BlockSpec((K, tn), lambda j, i: (0, j))],
        out_specs=pl.BlockSpec((tm, tn), lambda j, i: (i, j)),
        out_shape=jax.ShapeDtypeStruct((M, N), out_dtype),
        compiler_params=pltpu.CompilerParams(
            dimension_semantics=("parallel", "parallel"), vmem_limit_bytes=VMEM_LIMIT),
    )(a, b)


def layer_norm(x, g, b):
    xf = x.astype(jnp.float32)
    mu = jnp.mean(xf, axis=-1, keepdims=True)
    var = jnp.mean(jnp.square(xf - mu), axis=-1, keepdims=True)
    return ((xf - mu) * lax.rsqrt(var + LN_EPS) * g + b).astype(x.dtype)


def group_norm(x, n_groups, gain, bias=None, rms=False):
    shp = x.shape
    xf = x.astype(jnp.float32).reshape(shp[:-1] + (n_groups, shp[-1] // n_groups))
    if not rms:
        xf = xf - jnp.mean(xf, axis=-1, keepdims=True)
    xf = xf * lax.rsqrt(jnp.mean(jnp.square(xf), axis=-1, keepdims=True) + LN_EPS)
    y = xf.reshape(shp) * gain
    if bias is not None:
        y = y + bias
    return y.astype(x.dtype)


def causal_depthwise_conv(x, w, b):
    width, ch = w.shape
    y = lax.conv_general_dilated(x, w[:, None, :].astype(x.dtype), window_strides=(1,),
                                 padding=[(width - 1, 0)],
                                 dimension_numbers=('NWC', 'WIO', 'NWC'),
                                 feature_group_count=ch)
    return y + b


def t5_bucket(dist):
    max_exact = REL_BUCKETS // 2
    dist = jnp.maximum(dist, 0)
    far = max_exact + (jnp.log(jnp.maximum(dist, 1).astype(jnp.float32) / max_exact)
                       / math.log(REL_MAX_DIST / max_exact)
                       * (REL_BUCKETS - max_exact)).astype(jnp.int32)
    return jnp.where(dist < max_exact, dist, jnp.minimum(far, REL_BUCKETS - 1))


def mlstm_chunkwise(q, k, v, i_pre, f_pre):
    B, S, H, dqk = q.shape
    dv = v.shape[-1]
    L = MLSTM_CHUNK
    nc = S // L

    def chunks(a):
        a = a.astype(jnp.float32).reshape((B, nc, L, H) + a.shape[3:])
        return jnp.moveaxis(a, 3, 1)

    qc, kc, vc = chunks(q), chunks(k) * dqk ** -0.5, chunks(v)
    ig = chunks(i_pre)
    bcum = jnp.cumsum(jax.nn.log_sigmoid(chunks(f_pre)), axis=-1)
    b_last = bcum[..., -1]
    causal = jnp.tril(jnp.ones((L, L), dtype=bool))
    dmat = jnp.where(causal, bcum[..., :, None] - bcum[..., None, :] + ig[..., None, :], -jnp.inf)
    a_end = b_last[..., None] - bcum + ig
    a_max = jnp.max(a_end, axis=-1)
    w_end = jnp.exp(a_end - a_max[..., None])
    kv_chunk = jnp.einsum('bhnld,bhnlv->bhndv', kc * w_end[..., None], vc)
    k_chunk = jnp.einsum('bhnl,bhnld->bhnd', w_end, kc)

    def step(carry, xs):
        c_st, n_st, m_st = carry
        bl, am, kv, ks = xs
        m_new = jnp.maximum(bl + m_st, am)
        f_sc = jnp.exp(bl + m_st - m_new)
        i_sc = jnp.exp(am - m_new)
        c_new = f_sc[..., None, None] * c_st + i_sc[..., None, None] * kv
        n_new = f_sc[..., None] * n_st + i_sc[..., None] * ks
        return (c_new, n_new, m_new), (c_st, n_st, m_st)

    init = (jnp.zeros((B, H, dqk, dv), jnp.float32), jnp.zeros((B, H, dqk), jnp.float32),
            jnp.zeros((B, H), jnp.float32))
    xs = (jnp.moveaxis(b_last, 2, 0), jnp.moveaxis(a_max, 2, 0),
          jnp.moveaxis(kv_chunk, 2, 0), jnp.moveaxis(k_chunk, 2, 0))
    _, (c_prev, n_prev, m_prev) = lax.scan(step, init, xs)
    c_prev = jnp.moveaxis(c_prev, 0, 2)
    n_prev = jnp.moveaxis(n_prev, 0, 2)
    m_prev = jnp.moveaxis(m_prev, 0, 2)
    inter = bcum + m_prev[..., None]
    m_t = jnp.maximum(inter, jnp.max(dmat, axis=-1))
    s_qk = jnp.einsum('bhntd,bhnsd->bhnts', qc, kc) * jnp.exp(dmat - m_t[..., None])
    w_inter = jnp.exp(inter - m_t)
    num = (jnp.einsum('bhnts,bhnsv->bhntv', s_qk, vc)
           + w_inter[..., None] * jnp.einsum('bhntd,bhndv->bhntv', qc, c_prev))
    den = jnp.sum(s_qk, axis=-1) + w_inter * jnp.einsum('bhntd,bhnd->bhnt', qc, n_prev)
    h = num / jnp.maximum(jnp.abs(den), jnp.exp(-m_t))[..., None]
    return jnp.moveaxis(h, 1, 3).reshape(B, S, H, dv)


def gla_chunkwise(q, k, v, log_a):
    B, S, H, dk = q.shape
    dv = v.shape[-1]
    C = GLA_CHUNK
    nc = S // C

    def chunks(a):
        return jnp.moveaxis(a.astype(jnp.float32).reshape(B, nc, C, H, a.shape[-1]), 3, 1)

    qc, kc, vc = chunks(q) * dk ** -0.5, chunks(k), chunks(v)
    bcum = jnp.cumsum(chunks(log_a), axis=3)
    b_last = bcum[..., -1, :]
    causal = jnp.tril(jnp.ones((C, C), dtype=bool))
    decay = jnp.exp(jnp.where(causal[..., None],
                              bcum[..., :, None, :] - bcum[..., None, :, :], -jnp.inf))
    attn = jnp.einsum('bhntsd,bhnsd->bhnts', decay * qc[..., :, None, :], kc)
    o_intra = jnp.einsum('bhnts,bhnsv->bhntv', attn, vc)
    kv_chunk = jnp.einsum('bhnsd,bhnsv->bhndv', kc * jnp.exp(b_last[..., None, :] - bcum), vc)

    def step(state, xs):
        dec, kv = xs
        return dec[..., None] * state + kv, state

    _, s_prev = lax.scan(step, jnp.zeros((B, H, dk, dv), jnp.float32),
                         (jnp.moveaxis(jnp.exp(b_last), 2, 0), jnp.moveaxis(kv_chunk, 2, 0)))
    s_prev = jnp.moveaxis(s_prev, 0, 2)
    o = o_intra + jnp.einsum('bhntd,bhndv->bhntv', qc * jnp.exp(bcum), s_prev)
    return jnp.moveaxis(o, 1, 3).reshape(B, S, H, dv)


def moba_attention(q, k, v, rel_bias):
    B, S, H, hd = q.shape
    n_blk = -(-S // MOBA_BLOCK)
    pad = n_blk * MOBA_BLOCK - S

    def blocks(a):
        a = jnp.pad(a, ((0, 0), (0, pad), (0, 0), (0, 0)))
        return jnp.moveaxis(a.reshape(B, n_blk, MOBA_BLOCK, H, hd), 3, 1)

    kb, vb = blocks(k), blocks(v)
    k_mean = jnp.mean(kb, axis=3)
    n_sel = min(MOBA_TOPK, n_blk - 1)
    scale = hd ** -0.5
    bias_hb = rel_bias.T.astype(jnp.float32)
    head_ix = jnp.arange(H)
    blk_pos = jnp.arange(MOBA_BLOCK)
    gather_blocks = jax.vmap(jax.vmap(lambda arr, ix: arr[ix]))

    def chunk(start):
        qc = jnp.moveaxis(lax.dynamic_slice_in_dim(q, start, MOBA_Q_CHUNK, axis=1), 2, 1)
        t = start + jnp.arange(MOBA_Q_CHUNK)
        own = start // MOBA_BLOCK
        k_own = lax.dynamic_index_in_dim(kb, own, axis=2, keepdims=False)
        v_own = lax.dynamic_index_in_dim(vb, own, axis=2, keepdims=False)
        d_own = t[:, None] - (own * MOBA_BLOCK + blk_pos)[None, :]
        lg_own = (jnp.einsum('bhqd,bhkd->bhqk', qc, k_own).astype(jnp.float32) * scale
                  + bias_hb[:, t5_bucket(d_own)])
        lg_own = jnp.where(d_own >= 0, lg_own, -jnp.inf)
        gate = jnp.einsum('bhqd,bhnd->bhqn', qc, k_mean).astype(jnp.float32)
        gate = jnp.where(jnp.arange(n_blk) < own, gate, -jnp.inf)
        _, idx = lax.top_k(gate, n_sel)
        k_sel = gather_blocks(kb, idx)
        v_sel = gather_blocks(vb, idx)
        d_sel = t[:, None, None] - (idx[..., None] * MOBA_BLOCK + blk_pos)
        lg_sel = (jnp.einsum('bhqd,bhqnkd->bhqnk', qc, k_sel).astype(jnp.float32) * scale
                  + bias_hb[head_ix[None, :, None, None, None], t5_bucket(d_sel)])
        valid = jnp.arange(n_sel) < own
        lg_sel = jnp.where(valid[:, None], lg_sel, -jnp.inf)
        lg = jnp.concatenate(
            [lg_own, lg_sel.reshape(B, H, MOBA_Q_CHUNK, n_sel * MOBA_BLOCK)], axis=-1)
        p = jax.nn.softmax(lg, axis=-1).astype(v.dtype)
        p_own = p[..., :MOBA_BLOCK]
        p_sel = p[..., MOBA_BLOCK:].reshape(B, H, MOBA_Q_CHUNK, n_sel, MOBA_BLOCK)
        out = (jnp.einsum('bhqk,bhkd->bhqd', p_own, v_own)
               + jnp.einsum('bhqnk,bhqnkd->bhqd', p_sel, v_sel))
        return jnp.moveaxis(out, 1, 2)

    outs = lax.map(chunk, jnp.arange(0, S, MOBA_Q_CHUNK))
    return jnp.moveaxis(outs, 0, 1).reshape(B, S, H, hd)


def hybrid_mixer(u, w_in, b_merge, mlstm_gate_b, mlstm_conv_w, mlstm_conv_b, mlstm_norm_g,
                 conf_dw_w, conf_dw_b, conf_norm_g, conf_norm_b, rel_bias,
                 gla_gate_w2, gla_gate_b, gla_norm_g, w_branch, w_out):
    B, S, _ = u.shape
    W = BRANCH_WIDTH
    n_pad = (-IN_WIDTH) % 1024
    w_in_p = jnp.pad(w_in.astype(jnp.bfloat16), ((0, 0), (0, n_pad)))
    proj = matmul(u.reshape(B * S, D_MODEL).astype(jnp.bfloat16), w_in_p, 512, 1024,
                  jnp.float32)[:, :IN_WIDTH].reshape(B, S, IN_WIDTH)
    offs = np.cumsum(IN_SIZES)[:-1].tolist()
    (mq, mk, mv, mo, mif, conf_in, moba_qkv, gq, gk, gv, g_lr, g_out,
     merge_pre) = jnp.split(proj, offs, axis=-1)

    qk = jax.nn.silu(causal_depthwise_conv(jnp.concatenate([mq, mk], axis=-1),
                                           mlstm_conv_w, mlstm_conv_b))
    mq, mk = jnp.split(qk, 2, axis=-1)
    gates = (mif + mlstm_gate_b).astype(jnp.float32)
    h = mlstm_chunkwise(mq.reshape(B, S, MLSTM_HEADS, MLSTM_DQK),
                        mk.reshape(B, S, MLSTM_HEADS, MLSTM_DQK),
                        mv.reshape(B, S, MLSTM_HEADS, MLSTM_DV),
                        gates[..., :MLSTM_HEADS], gates[..., MLSTM_HEADS:])
    h = group_norm(h.reshape(B, S, W).astype(u.dtype), MLSTM_HEADS, mlstm_norm_g)
    y_mlstm = jax.nn.sigmoid(mo) * h

    a, g = jnp.split(conf_in, 2, axis=-1)
    y = causal_depthwise_conv(a * jax.nn.sigmoid(g), conf_dw_w, conf_dw_b)
    y_conv = jax.nn.silu(group_norm(y, CONV_GROUPS, conf_norm_g, conf_norm_b))

    aq, ak, av = jnp.split(moba_qkv, 3, axis=-1)
    shp = (B, S, MOBA_HEADS, MOBA_HD)
    y_moba = moba_attention(aq.reshape(shp), ak.reshape(shp), av.reshape(shp),
                            rel_bias).reshape(B, S, W)

    log_a = jax.nn.log_sigmoid((g_lr @ gla_gate_w2 + gla_gate_b).astype(jnp.float32)) / GLA_TAU
    o = gla_chunkwise(gq.reshape(B, S, GLA_HEADS, GLA_DK), gk.reshape(B, S, GLA_HEADS, GLA_DK),
                      gv.reshape(B, S, GLA_HEADS, GLA_DV), log_a.reshape(B, S, GLA_HEADS, GLA_DK))
    o = group_norm(o.reshape(B, S, W).astype(u.dtype), GLA_HEADS, gla_norm_g, rms=True)
    y_gla = o * jax.nn.silu(g_out)

    branches = jnp.stack([y_mlstm, y_conv, y_moba, y_gla], axis=2)
    proj_b = jnp.einsum('bsnk,nkd->bsnd', branches, w_branch)
    gate = jax.nn.sigmoid((merge_pre + b_merge).reshape(B, S, N_BRANCHES, D_MODEL))
    merged = jnp.sum(gate * proj_b, axis=2)
    return merged @ w_out


def moe_ffn(u, router_w, router_b, w_up, b_up, w_down, b_down):
    B, S, D = u.shape
    tok = u.reshape(B * S, D)
    logits = (tok @ router_w + router_b).astype(jnp.float32)
    top_val, top_idx = lax.top_k(logits, TOP_K)
    top_w = jax.nn.softmax(top_val, axis=-1)
    combine = jnp.einsum('tk,tke->te', top_w,
                         jax.nn.one_hot(top_idx, N_EXPERTS, dtype=jnp.float32))
    out = jnp.zeros((B * S, D), jnp.float32)
    for e in range(N_EXPERTS):
        hid = tok @ w_up[e] + b_up[e]
        glu, lin = jnp.split(hid, 2, axis=-1)
        glu = jnp.minimum(glu, SWIGLU_LIMIT)
        lin = jnp.clip(lin, -SWIGLU_LIMIT, SWIGLU_LIMIT)
        act = glu * jax.nn.sigmoid(SWIGLU_ALPHA * glu) * (lin + 1)
        out = out + combine[:, e:e + 1] * (act @ w_down[e] + b_down[e])
    return out.reshape(B, S, D).astype(u.dtype)


def kernel(x, c, w_ada, b_ada, w_in, b_merge, mlstm_gate_b, mlstm_conv_w, mlstm_conv_b,
           mlstm_norm_g, conf_dw_w, conf_dw_b, conf_norm_g, conf_norm_b, rel_bias,
           gla_gate_w2, gla_gate_b, gla_norm_g, w_branch, w_out, ln1_g, ln1_b,
           router_w, router_b, exp_w_up, exp_b_up, exp_w_down, exp_b_down, ln2_g, ln2_b):
    mod_all = jnp.einsum('bd,ldk->lbk', jax.nn.silu(c), w_ada) + b_ada[:, None, :]
    for l in range(DEPTH):
        sh1, sc1, g1, sh2, sc2, g2 = jnp.split(mod_all[l][:, None, :], 6, axis=-1)
        u = x * (1 + sc1) + sh1
        mix = hybrid_mixer(u, w_in[l], b_merge[l], mlstm_gate_b[l], mlstm_conv_w[l],
                           mlstm_conv_b[l], mlstm_norm_g[l], conf_dw_w[l], conf_dw_b[l],
                           conf_norm_g[l], conf_norm_b[l], rel_bias, gla_gate_w2[l],
                           gla_gate_b[l], gla_norm_g[l], w_branch[l], w_out[l])
        x = layer_norm(DEEPNORM_ALPHA * x + (1 + g1) * mix, ln1_g[l], ln1_b[l])
        u = x * (1 + sc2) + sh2
        ffn = moe_ffn(u, router_w[l], router_b[l], exp_w_up[l], exp_b_up[l],
                      exp_w_down[l], exp_b_down[l])
        x = layer_norm(DEEPNORM_ALPHA * x + (1 + g2) * ffn, ln2_g[l], ln2_b[l])
    return x
```

```python
import functools
import math

import jax
import jax.numpy as jnp
import numpy as np
from jax import lax
from jax.experimental import pallas as pl
from jax.experimental.pallas import tpu as pltpu

D_MODEL = 4096
BATCH = 4
SEQ = 2048
DEPTH = 2

N_BRANCHES = 4
BRANCH_WIDTH = D_MODEL // N_BRANCHES
MLSTM_HEADS = 4
MLSTM_DV = BRANCH_WIDTH // MLSTM_HEADS
MLSTM_DQK = MLSTM_DV // 2
MLSTM_CONV = 4
MLSTM_CHUNK = 64
CONV_WIDTH = 31
CONV_GROUPS = 4
MOBA_HEADS = 8
MOBA_HD = BRANCH_WIDTH // MOBA_HEADS
MOBA_BLOCK = 256
MOBA_TOPK = 3
MOBA_Q_CHUNK = 32
REL_BUCKETS = 32
REL_MAX_DIST = 128
GLA_HEADS = 4
GLA_DV = BRANCH_WIDTH // GLA_HEADS
GLA_DK = GLA_DV // 2
GLA_RANK = 16
GLA_TAU = 16.0
GLA_CHUNK = 16
N_EXPERTS = 32
TOP_K = 4
EXPERT_FF = D_MODEL // 8
SWIGLU_ALPHA = 1.702
SWIGLU_LIMIT = 7.0
DEEPNORM_ALPHA = (2 * DEPTH) ** 0.25
LN_EPS = 1e-5

IN_SIZES = (
    MLSTM_HEADS * MLSTM_DQK, MLSTM_HEADS * MLSTM_DQK, BRANCH_WIDTH, BRANCH_WIDTH,
    2 * MLSTM_HEADS, 2 * BRANCH_WIDTH, 3 * BRANCH_WIDTH, GLA_HEADS * GLA_DK,
    GLA_HEADS * GLA_DK, BRANCH_WIDTH, GLA_RANK, BRANCH_WIDTH, N_BRANCHES * D_MODEL,
)
IN_WIDTH = sum(IN_SIZES)

VMEM_LIMIT = 56 * 1024 * 1024


def _mm_kernel(a_ref, b_ref, o_ref):
    o_ref[...] = jnp.dot(a_ref[...], b_ref[...],
                         preferred_element_type=jnp.float32).astype(o_ref.dtype)


def matmul(a, b, tm, tn, out_dtype):
    M, K = a.shape
    N = b.shape[1]
    return pl.pallas_call(
        _mm_kernel,
        grid=(N // tn, M // tm),
        in_specs=[pl.BlockSpec((tm, K), lambda j, i: (i, 0)),
                  pl.BlockSpec((K, tn), lambda j, i: (0, j))],
        out_specs=pl.BlockSpec((tm, tn), lambda j, i: (i, j)),
        out_shape=jax.ShapeDtypeStruct((M, N), out_dtype),
        compiler_params=pltpu.CompilerParams(
            dimension_semantics=("parallel", "parallel"), vmem_limit_bytes=VMEM_LIMIT),
    )(a, b)


NEG_BIG = -1e30


def _t5_bucket_np(d):
    max_exact = REL_BUCKETS // 2
    dd = np.maximum(d, 1).astype(np.float32)
    far = max_exact + (np.log(dd / np.float32(max_exact))
                       / np.float32(math.log(REL_MAX_DIST / max_exact))
                       * np.float32(REL_BUCKETS - max_exact)).astype(np.int32)
    return np.where(d < max_exact, d, np.minimum(far, REL_BUCKETS - 1)).astype(np.int32)


def _moba_bucket_tables():
    r = np.arange(MOBA_BLOCK)[:, None]
    c = np.arange(MOBA_BLOCK)[None, :]
    own = np.where(r >= c, _t5_bucket_np(np.maximum(r - c, 0)), -1)
    prev = _t5_bucket_np(MOBA_BLOCK + r - c)
    return np.stack([own, prev]).astype(np.int32)


def _bias_tab_kernel(rb_ref, bk_ref, o_ref):
    h = pl.program_id(0)
    bk = bk_ref[...]
    acc = jnp.zeros(bk.shape, jnp.float32)
    for m in range(REL_BUCKETS):
        acc = jnp.where(bk == m, rb_ref[h, m], acc)
    o_ref[0] = jnp.where(bk < 0, NEG_BIG, acc)


def moba_bias_tables(rel_bias_t):
    n_heads = rel_bias_t.shape[0]
    bk = jnp.asarray(_moba_bucket_tables())
    return pl.pallas_call(
        _bias_tab_kernel,
        grid=(n_heads,),
        in_specs=[pl.BlockSpec(memory_space=pltpu.SMEM),
                  pl.BlockSpec((2, MOBA_BLOCK, MOBA_BLOCK), lambda h: (0, 0, 0))],
        out_specs=pl.BlockSpec((1, 2, MOBA_BLOCK, MOBA_BLOCK), lambda h: (h, 0, 0, 0)),
        out_shape=jax.ShapeDtypeStruct((n_heads, 2, MOBA_BLOCK, MOBA_BLOCK), jnp.float32),
    )(rel_bias_t, bk)


def _moba_kernel(rb_ref, q_ref, k_ref, v_ref, bt_ref, o_ref, m_sc, l_sc, acc_sc, *, n_blk):
    h = pl.program_id(1)
    i = pl.program_id(2)
    blk = MOBA_BLOCK
    scale = MOBA_HD ** -0.5
    nt = (((1,), (1,)), ((), ()))
    q = q_ref[...]

    kmean = jnp.mean(k_ref[...].astype(jnp.float32).reshape(n_blk, blk, MOBA_HD), axis=1)
    km_hi = kmean.astype(jnp.bfloat16)
    km_lo = (kmean - km_hi.astype(jnp.float32)).astype(jnp.bfloat16)
    gate = (lax.dot_general(q, km_hi, nt, preferred_element_type=jnp.float32)
            + lax.dot_general(q, km_lo, nt, preferred_element_type=jnp.float32))
    g = [gate[:, n:n + 1] for n in range(n_blk)]

    def selected(j):
        cnt = jnp.zeros((blk, 1), jnp.float32)
        for n in range(n_blk - 1):
            if n == j:
                continue
            ahead = (g[n] >= g[j]) if n < j else (g[n] > g[j])
            cnt = cnt + jnp.where(ahead, jnp.where(n < i, 1.0, 0.0), 0.0)
        return cnt < float(MOBA_TOPK)

    row0 = pl.multiple_of(i * blk, blk)
    k_own = k_ref[pl.ds(row0, blk), :]
    v_own = v_ref[pl.ds(row0, blk), :]
    s = lax.dot_general(q, k_own, nt, preferred_element_type=jnp.float32) * scale + bt_ref[0, 0]
    m0 = jnp.max(s, axis=1, keepdims=True)
    p = jnp.exp(s - m0)
    m_sc[...] = m0
    l_sc[...] = jnp.sum(p, axis=1, keepdims=True)
    acc_sc[...] = jnp.dot(p.astype(jnp.bfloat16), v_own, preferred_element_type=jnp.float32)

    bfar = rb_ref[h, REL_BUCKETS - 1]
    for j in range(n_blk - 1):
        @pl.when(j < i)
        def _():
            k_j = k_ref[j * blk:(j + 1) * blk, :]
            v_j = v_ref[j * blk:(j + 1) * blk, :]
            bias = jnp.where(j == i - 1, bt_ref[0, 1], bfar)
            s = lax.dot_general(q, k_j, nt, preferred_element_type=jnp.float32) * scale + bias
            s = jnp.where(selected(j), s, NEG_BIG)
            m_old = m_sc[...]
            m_new = jnp.maximum(m_old, jnp.max(s, axis=1, keepdims=True))
            alpha = jnp.exp(m_old - m_new)
            p = jnp.exp(s - m_new)
            m_sc[...] = m_new
            l_sc[...] = alpha * l_sc[...] + jnp.sum(p, axis=1, keepdims=True)
            acc_sc[...] = alpha * acc_sc[...] + jnp.dot(
                p.astype(jnp.bfloat16), v_j, preferred_element_type=jnp.float32)

    o_ref[...] = (acc_sc[...] / l_sc[...]).astype(o_ref.dtype)


def moba_pallas(qkv, rel_bias_t, btab, n_batch, seq, q_blk0, k_blk0, v_blk0, out_dtype):
    n_heads = rel_bias_t.shape[0]
    n_blk = seq // MOBA_BLOCK
    assert seq % MOBA_BLOCK == 0
    kern = functools.partial(_moba_kernel, n_blk=n_blk)
    return pl.pallas_call(
        kern,
        grid=(n_batch, n_heads, n_blk),
        in_specs=[
            pl.BlockSpec(memory_space=pltpu.SMEM),
            pl.BlockSpec((MOBA_BLOCK, MOBA_HD), lambda b, h, i: (b * n_blk + i, q_blk0 + h)),
            pl.BlockSpec((seq, MOBA_HD), lambda b, h, i: (b, k_blk0 + h)),
            pl.BlockSpec((seq, MOBA_HD), lambda b, h, i: (b, v_blk0 + h)),
            pl.BlockSpec((1, 2, MOBA_BLOCK, MOBA_BLOCK), lambda b, h, i: (h, 0, 0, 0)),
        ],
        out_specs=pl.BlockSpec((MOBA_BLOCK, MOBA_HD), lambda b, h, i: (b * n_blk + i, h)),
        out_shape=jax.ShapeDtypeStruct((n_batch * seq, n_heads * MOBA_HD), out_dtype),
        scratch_shapes=[pltpu.VMEM((MOBA_BLOCK, 1), jnp.float32),
                        pltpu.VMEM((MOBA_BLOCK, 1), jnp.float32),
                        pltpu.VMEM((MOBA_BLOCK, MOBA_HD), jnp.float32)],
        compiler_params=pltpu.CompilerParams(
            dimension_semantics=("parallel", "parallel", "arbitrary")),
    )(rel_bias_t, qkv, qkv, qkv, btab)


def layer_norm(x, g, b):
    xf = x.astype(jnp.float32)
    mu = jnp.mean(xf, axis=-1, keepdims=True)
    var = jnp.mean(jnp.square(xf - mu), axis=-1, keepdims=True)
    return ((xf - mu) * lax.rsqrt(var + LN_EPS) * g + b).astype(x.dtype)


def group_norm(x, n_groups, gain, bias=None, rms=False):
    shp = x.shape
    xf = x.astype(jnp.float32).reshape(shp[:-1] + (n_groups, shp[-1] // n_groups))
    if not rms:
        xf = xf - jnp.mean(xf, axis=-1, keepdims=True)
    xf = xf * lax.rsqrt(jnp.mean(jnp.square(xf), axis=-1, keepdims=True) + LN_EPS)
    y = xf.reshape(shp) * gain
    if bias is not None:
        y = y + bias
    return y.astype(x.dtype)


def causal_depthwise_conv(x, w, b):
    width, ch = w.shape
    y = lax.conv_general_dilated(x, w[:, None, :].astype(x.dtype), window_strides=(1,),
                                 padding=[(width - 1, 0)],
                                 dimension_numbers=('NWC', 'WIO', 'NWC'),
                                 feature_group_count=ch)
    return y + b


def t5_bucket(dist):
    max_exact = REL_BUCKETS // 2
    dist = jnp.maximum(dist, 0)
    far = max_exact + (jnp.log(jnp.maximum(dist, 1).astype(jnp.float32) / max_exact)
                       / math.log(REL_MAX_DIST / max_exact)
                       * (REL_BUCKETS - max_exact)).astype(jnp.int32)
    return jnp.where(dist < max_exact, dist, jnp.minimum(far, REL_BUCKETS - 1))


def mlstm_chunkwise(q, k, v, i_pre, f_pre):
    B, S, H, dqk = q.shape
    dv = v.shape[-1]
    L = MLSTM_CHUNK
    nc = S // L

    def chunks(a):
        a = a.astype(jnp.float32).reshape((B, nc, L, H) + a.shape[3:])
        return jnp.moveaxis(a, 3, 1)

    qc, kc, vc = chunks(q), chunks(k) * dqk ** -0.5, chunks(v)
    ig = chunks(i_pre)
    bcum = jnp.cumsum(jax.nn.log_sigmoid(chunks(f_pre)), axis=-1)
    b_last = bcum[..., -1]
    causal = jnp.tril(jnp.ones((L, L), dtype=bool))
    dmat = jnp.where(causal, bcum[..., :, None] - bcum[..., None, :] + ig[..., None, :], -jnp.inf)
    a_end = b_last[..., None] - bcum + ig
    a_max = jnp.max(a_end, axis=-1)
    w_end = jnp.exp(a_end - a_max[..., None])
    kv_chunk = jnp.einsum('bhnld,bhnlv->bhndv', kc * w_end[..., None], vc)
    k_chunk = jnp.einsum('bhnl,bhnld->bhnd', w_end, kc)

    def step(carry, xs):
        c_st, n_st, m_st = carry
        bl, am, kv, ks = xs
        m_new = jnp.maximum(bl + m_st, am)
        f_sc = jnp.exp(bl + m_st - m_new)
        i_sc = jnp.exp(am - m_new)
        c_new = f_sc[..., None, None] * c_st + i_sc[..., None, None] * kv
        n_new = f_sc[..., None] * n_st + i_sc[..., None] * ks
        return (c_new, n_new, m_new), (c_st, n_st, m_st)

    init = (jnp.zeros((B, H, dqk, dv), jnp.float32), jnp.zeros((B, H, dqk), jnp.float32),
            jnp.zeros((B, H), jnp.float32))
    xs = (jnp.moveaxis(b_last, 2, 0), jnp.moveaxis(a_max, 2, 0),
          jnp.moveaxis(kv_chunk, 2, 0), jnp.moveaxis(k_chunk, 2, 0))
    _, (c_prev, n_prev, m_prev) = lax.scan(step, init, xs)
    c_prev = jnp.moveaxis(c_prev, 0, 2)
    n_prev = jnp.moveaxis(n_prev, 0, 2)
    m_prev = jnp.moveaxis(m_prev, 0, 2)
    inter = bcum + m_prev[..., None]
    m_t = jnp.maximum(inter, jnp.max(dmat, axis=-1))
    s_qk = jnp.einsum('bhntd,bhnsd->bhnts', qc, kc) * jnp.exp(dmat - m_t[..., None])
    w_inter = jnp.exp(inter - m_t)
    num = (jnp.einsum('bhnts,bhnsv->bhntv', s_qk, vc)
           + w_inter[..., None] * jnp.einsum('bhntd,bhndv->bhntv', qc, c_prev))
    den = jnp.sum(s_qk, axis=-1) + w_inter * jnp.einsum('bhntd,bhnd->bhnt', qc, n_prev)
    h = num / jnp.maximum(jnp.abs(den), jnp.exp(-m_t))[..., None]
    return jnp.moveaxis(h, 1, 3).reshape(B, S, H, dv)


def gla_chunkwise(q, k, v, log_a):
    B, S, H, dk = q.shape
    dv = v.shape[-1]
    C = GLA_CHUNK
    nc = S // C

    def chunks(a):
        return jnp.moveaxis(a.astype(jnp.float32).reshape(B, nc, C, H, a.shape[-1]), 3, 1)

    qc, kc, vc = chunks(q) * dk ** -0.5, chunks(k), chunks(v)
    bcum = jnp.cumsum(chunks(log_a), axis=3)
    b_last = bcum[..., -1, :]
    causal = jnp.tril(jnp.ones((C, C), dtype=bool))
    decay = jnp.exp(jnp.where(causal[..., None],
                              bcum[..., :, None, :] - bcum[..., None, :, :], -jnp.inf))
    attn = jnp.einsum('bhntsd,bhnsd->bhnts', decay * qc[..., :, None, :], kc)
    o_intra = jnp.einsum('bhnts,bhnsv->bhntv', attn, vc)
    kv_chunk = jnp.einsum('bhnsd,bhnsv->bhndv', kc * jnp.exp(b_last[..., None, :] - bcum), vc)

    def step(state, xs):
        dec, kv = xs
        return dec[..., None] * state + kv, state

    _, s_prev = lax.scan(step, jnp.zeros((B, H, dk, dv), jnp.float32),
                         (jnp.moveaxis(jnp.exp(b_last), 2, 0), jnp.moveaxis(kv_chunk, 2, 0)))
    s_prev = jnp.moveaxis(s_prev, 0, 2)
    o = o_intra + jnp.einsum('bhntd,bhndv->bhntv', qc * jnp.exp(bcum), s_prev)
    return jnp.moveaxis(o, 1, 3).reshape(B, S, H, dv)


def moba_attention(q, k, v, rel_bias):
    B, S, H, hd = q.shape
    n_blk = -(-S // MOBA_BLOCK)
    pad = n_blk * MOBA_BLOCK - S

    def blocks(a):
        a = jnp.pad(a, ((0, 0), (0, pad), (0, 0), (0, 0)))
        return jnp.moveaxis(a.reshape(B, n_blk, MOBA_BLOCK, H, hd), 3, 1)

    kb, vb = blocks(k), blocks(v)
    k_mean = jnp.mean(kb, axis=3)
    n_sel = min(MOBA_TOPK, n_blk - 1)
    scale = hd ** -0.5
    bias_hb = rel_bias.T.astype(jnp.float32)
    head_ix = jnp.arange(H)
    blk_pos = jnp.arange(MOBA_BLOCK)
    gather_blocks = jax.vmap(jax.vmap(lambda arr, ix: arr[ix]))

    def chunk(start):
        qc = jnp.moveaxis(lax.dynamic_slice_in_dim(q, start, MOBA_Q_CHUNK, axis=1), 2, 1)
        t = start + jnp.arange(MOBA_Q_CHUNK)
        own = start // MOBA_BLOCK
        k_own = lax.dynamic_index_in_dim(kb, own, axis=2, keepdims=False)
        v_own = lax.dynamic_index_in_dim(vb, own, axis=2, keepdims=False)
        d_own = t[:, None] - (own * MOBA_BLOCK + blk_pos)[None, :]
        lg_own = (jnp.einsum('bhqd,bhkd->bhqk', qc, k_own).astype(jnp.float32) * scale
                  + bias_hb[:, t5_bucket(d_own)])
        lg_own = jnp.where(d_own >= 0, lg_own, -jnp.inf)
        gate = jnp.einsum('bhqd,bhnd->bhqn', qc, k_mean).astype(jnp.float32)
        gate = jnp.where(jnp.arange(n_blk) < own, gate, -jnp.inf)
        _, idx = lax.top_k(gate, n_sel)
        k_sel = gather_blocks(kb, idx)
        v_sel = gather_blocks(vb, idx)
        d_sel = t[:, None, None] - (idx[..., None] * MOBA_BLOCK + blk_pos)
        lg_sel = (jnp.einsum('bhqd,bhqnkd->bhqnk', qc, k_sel).astype(jnp.float32) * scale
                  + bias_hb[head_ix[None, :, None, None, None], t5_bucket(d_sel)])
        valid = jnp.arange(n_sel) < own
        lg_sel = jnp.where(valid[:, None], lg_sel, -jnp.inf)
        lg = jnp.concatenate(
            [lg_own, lg_sel.reshape(B, H, MOBA_Q_CHUNK, n_sel * MOBA_BLOCK)], axis=-1)
        p = jax.nn.softmax(lg, axis=-1).astype(v.dtype)
        p_own = p[..., :MOBA_BLOCK]
        p_sel = p[..., MOBA_BLOCK:].reshape(B, H, MOBA_Q_CHUNK, n_sel, MOBA_BLOCK)
        out = (jnp.einsum('bhqk,bhkd->bhqd', p_own, v_own)
               + jnp.einsum('bhqnk,bhqnkd->bhqd', p_sel, v_sel))
        return jnp.moveaxis(out, 1, 2)

    outs = lax.map(chunk, jnp.arange(0, S, MOBA_Q_CHUNK))
    return jnp.moveaxis(outs, 0, 1).reshape(B, S, H, hd)


def hybrid_mixer(u, w_in, b_merge, mlstm_gate_b, mlstm_conv_w, mlstm_conv_b, mlstm_norm_g,
                 conf_dw_w, conf_dw_b, conf_norm_g, conf_norm_b, rel_bias_t, moba_btab,
                 gla_gate_w2, gla_gate_b, gla_norm_g, w_branch, w_out):
    B, S, _ = u.shape
    W = BRANCH_WIDTH
    offs = [0] + np.cumsum(IN_SIZES).tolist()
    seg = lambda n: w_in[:, offs[n]:offs[n + 1]].astype(jnp.bfloat16)
    w_a = jnp.concatenate([seg(n) for n in (0, 1, 2, 3, 5, 7, 8, 9, 11)], axis=1)
    w_b = jnp.concatenate([seg(6), seg(12)], axis=1)
    w_c = jnp.pad(jnp.concatenate([seg(4), seg(10)], axis=1),
                  ((0, 0), (0, 128 - 2 * MLSTM_HEADS - GLA_RANK)))
    ub = u.reshape(B * S, D_MODEL).astype(jnp.bfloat16)
    proj_a = matmul(ub, w_a, 512, 1024, jnp.float32)
    proj_b = matmul(ub, w_b, 512, 1024, jnp.bfloat16)
    proj_c = matmul(ub, w_c, 512, 128, jnp.float32)
    a_sizes = [IN_SIZES[n] for n in (0, 1, 2, 3, 5, 7, 8, 9)]
    mq, mk, mv, mo, conf_in, gq, gk, gv, g_out = jnp.split(
        proj_a.reshape(B, S, -1), np.cumsum(a_sizes).tolist(), axis=-1)
    mif = proj_c[:, :2 * MLSTM_HEADS].reshape(B, S, -1)
    g_lr = proj_c[:, 2 * MLSTM_HEADS:2 * MLSTM_HEADS + GLA_RANK].reshape(B, S, -1)
    merge_pre = proj_b[:, 3 * W:].astype(jnp.float32).reshape(B, S, -1)

    qk = jax.nn.silu(causal_depthwise_conv(jnp.concatenate([mq, mk], axis=-1),
                                           mlstm_conv_w, mlstm_conv_b))
    mq, mk = jnp.split(qk, 2, axis=-1)
    gates = (mif + mlstm_gate_b).astype(jnp.float32)
    h = mlstm_chunkwise(mq.reshape(B, S, MLSTM_HEADS, MLSTM_DQK),
                        mk.reshape(B, S, MLSTM_HEADS, MLSTM_DQK),
                        mv.reshape(B, S, MLSTM_HEADS, MLSTM_DV),
                        gates[..., :MLSTM_HEADS], gates[..., MLSTM_HEADS:])
    h = group_norm(h.reshape(B, S, W).astype(u.dtype), MLSTM_HEADS, mlstm_norm_g)
    y_mlstm = jax.nn.sigmoid(mo) * h

    a, g = jnp.split(conf_in, 2, axis=-1)
    y = causal_depthwise_conv(a * jax.nn.sigmoid(g), conf_dw_w, conf_dw_b)
    y_conv = jax.nn.silu(group_norm(y, CONV_GROUPS, conf_norm_g, conf_norm_b))

    y_moba = moba_pallas(proj_b, rel_bias_t, moba_btab, B, S, 0, MOBA_HEADS, 2 * MOBA_HEADS,
                         jnp.float32).reshape(B, S, W)

    log_a = jax.nn.log_sigmoid((g_lr @ gla_gate_w2 + gla_gate_b).astype(jnp.float32)) / GLA_TAU
    o = gla_chunkwise(gq.reshape(B, S, GLA_HEADS, GLA_DK), gk.reshape(B, S, GLA_HEADS, GLA_DK),
                      gv.reshape(B, S, GLA_HEADS, GLA_DV), log_a.reshape(B, S, GLA_HEADS, GLA_DK))
    o = group_norm(o.reshape(B, S, W).astype(u.dtype), GLA_HEADS, gla_norm_g, rms=True)
    y_gla = o * jax.nn.silu(g_out)

    branches = jnp.stack([y_mlstm, y_conv, y_moba, y_gla], axis=2)
    proj_b = jnp.einsum('bsnk,nkd->bsnd', branches, w_branch)
    gate = jax.nn.sigmoid((merge_pre + b_merge).reshape(B, S, N_BRANCHES, D_MODEL))
    merged = jnp.sum(gate * proj_b, axis=2)
    return merged @ w_out


def moe_ffn(u, router_w, router_b, w_up, b_up, w_down, b_down):
    B, S, D = u.shape
    tok = u.reshape(B * S, D)
    logits = (tok @ router_w + router_b).astype(jnp.float32)
    top_val, top_idx = lax.top_k(logits, TOP_K)
    top_w = jax.nn.softmax(top_val, axis=-1)
    combine = jnp.einsum('tk,tke->te', top_w,
                         jax.nn.one_hot(top_idx, N_EXPERTS, dtype=jnp.float32))
    out = jnp.zeros((B * S, D), jnp.float32)
    for e in range(N_EXPERTS):
        hid = tok @ w_up[e] + b_up[e]
        glu, lin = jnp.split(hid, 2, axis=-1)
        glu = jnp.minimum(glu, SWIGLU_LIMIT)
        lin = jnp.clip(lin, -SWIGLU_LIMIT, SWIGLU_LIMIT)
        act = glu * jax.nn.sigmoid(SWIGLU_ALPHA * glu) * (lin + 1)
        out = out + combine[:, e:e + 1] * (act @ w_down[e] + b_down[e])
    return out.reshape(B, S, D).astype(u.dtype)


def kernel(x, c, w_ada, b_ada, w_in, b_merge, mlstm_gate_b, mlstm_conv_w, mlstm_conv_b,
           mlstm_norm_g, conf_dw_w, conf_dw_b, conf_norm_g, conf_norm_b, rel_bias,
           gla_gate_w2, gla_gate_b, gla_norm_g, w_branch, w_out, ln1_g, ln1_b,
           router_w, router_b, exp_w_up, exp_b_up, exp_w_down, exp_b_down, ln2_g, ln2_b):
    mod_all = jnp.einsum('bd,ldk->lbk', jax.nn.silu(c), w_ada) + b_ada[:, None, :]
    rel_bias_t = rel_bias.T.astype(jnp.float32)
    moba_btab = moba_bias_tables(rel_bias_t)
    for l in range(DEPTH):
        sh1, sc1, g1, sh2, sc2, g2 = jnp.split(mod_all[l][:, None, :], 6, axis=-1)
        u = x * (1 + sc1) + sh1
        mix = hybrid_mixer(u, w_in[l], b_merge[l], mlstm_gate_b[l], mlstm_conv_w[l],
                           mlstm_conv_b[l], mlstm_norm_g[l], conf_dw_w[l], conf_dw_b[l],
                           conf_norm_g[l], conf_norm_b[l], rel_bias_t, moba_btab, gla_gate_w2[l],
                           gla_gate_b[l], gla_norm_g[l], w_branch[l], w_out[l])
        x = layer_norm(DEEPNORM_ALPHA * x + (1 + g1) * mix, ln1_g[l], ln1_b[l])
        u = x * (1 + sc2) + sh2
        ffn = moe_ffn(u, router_w[l], router_b[l], exp_w_up[l], exp_b_up[l],
                      exp_w_down[l], exp_b_down[l])
        x = layer_norm(DEEPNORM_ALPHA * x + (1 + g2) * ffn, ln2_g[l], ln2_b[l])
    return x
```

```python
import functools
import math

import jax
import jax.numpy as jnp
import numpy as np
from jax import lax
from jax.experimental import pallas as pl
from jax.experimental.pallas import tpu as pltpu

D_MODEL = 4096
BATCH = 4
SEQ = 2048
DEPTH = 2

N_BRANCHES = 4
BRANCH_WIDTH = D_MODEL // N_BRANCHES
MLSTM_HEADS = 4
MLSTM_DV = BRANCH_WIDTH // MLSTM_HEADS
MLSTM_DQK = MLSTM_DV // 2
MLSTM_CONV = 4
MLSTM_CHUNK = 64
CONV_WIDTH = 31
CONV_GROUPS = 4
MOBA_HEADS = 8
MOBA_HD = BRANCH_WIDTH // MOBA_HEADS
MOBA_BLOCK = 256
MOBA_TOPK = 3
MOBA_Q_CHUNK = 32
REL_BUCKETS = 32
REL_MAX_DIST = 128
GLA_HEADS = 4
GLA_DV = BRANCH_WIDTH // GLA_HEADS
GLA_DK = GLA_DV // 2
GLA_RANK = 16
GLA_TAU = 16.0
GLA_CHUNK = 16
N_EXPERTS = 32
TOP_K = 4
EXPERT_FF = D_MODEL // 8
SWIGLU_ALPHA = 1.702
SWIGLU_LIMIT = 7.0
DEEPNORM_ALPHA = (2 * DEPTH) ** 0.25
LN_EPS = 1e-5

IN_SIZES = (
    MLSTM_HEADS * MLSTM_DQK, MLSTM_HEADS * MLSTM_DQK, BRANCH_WIDTH, BRANCH_WIDTH,
    2 * MLSTM_HEADS, 2 * BRANCH_WIDTH, 3 * BRANCH_WIDTH, GLA_HEADS * GLA_DK,
    GLA_HEADS * GLA_DK, BRANCH_WIDTH, GLA_RANK, BRANCH_WIDTH, N_BRANCHES * D_MODEL,
)
IN_WIDTH = sum(IN_SIZES)

VMEM_LIMIT = 56 * 1024 * 1024


def _mm_kernel(a_ref, b_ref, o_ref):
    o_ref[...] = jnp.dot(a_ref[...], b_ref[...],
                         preferred_element_type=jnp.float32).astype(o_ref.dtype)


def matmul(a, b, tm, tn, out_dtype):
    M, K = a.shape
    N = b.shape[1]
    return pl.pallas_call(
        _mm_kernel,
        grid=(N // tn, M // tm),
        in_specs=[pl.BlockSpec((tm, K), lambda j, i: (i, 0)),
                  pl.BlockSpec((K, tn), lambda j, i: (0, j))],
        out_specs=pl.BlockSpec((tm, tn), lambda j, i: (i, j)),
        out_shape=jax.ShapeDtypeStruct((M, N), out_dtype),
        compiler_params=pltpu.CompilerParams(
            dimension_semantics=("parallel", "parallel"), vmem_limit_bytes=VMEM_LIMIT),
    )(a, b)


ROW_TILE = 256
MOE_TILE = 256


def _ada_kernel(c_ref, w_ref, b_ref, o_ref):
    c = c_ref[...]
    cs = (c * jax.nn.sigmoid(c)).astype(jnp.bfloat16)
    o_ref[0] = jnp.dot(cs, w_ref[0].astype(jnp.bfloat16),
                       preferred_element_type=jnp.float32) + b_ref[0]


def ada_modulation(c, w_ada, b_ada, tn=512):
    n_layers, d, n = w_ada.shape
    nb = c.shape[0]
    c8 = jnp.pad(c, ((0, 8 - nb), (0, 0)))
    out = pl.pallas_call(
        _ada_kernel,
        grid=(n_layers, n // tn),
        in_specs=[pl.BlockSpec((8, d), lambda l, j: (0, 0)),
                  pl.BlockSpec((1, d, tn), lambda l, j: (l, 0, j)),
                  pl.BlockSpec((1, 1, tn), lambda l, j: (l, 0, j))],
        out_specs=pl.BlockSpec((1, 8, tn), lambda l, j: (l, 0, j)),
        out_shape=jax.ShapeDtypeStruct((n_layers, 8, n), jnp.float32),
        compiler_params=pltpu.CompilerParams(
            dimension_semantics=("parallel", "parallel"), vmem_limit_bytes=VMEM_LIMIT),
    )(c8, w_ada, b_ada.reshape(n_layers, 1, n))
    return out[:, :nb]


def _modulate_kernel(x_ref, sc_ref, sh_ref, o_ref):
    o_ref[...] = (x_ref[...] * (1.0 + sc_ref[0]) + sh_ref[0]).astype(o_ref.dtype)


def modulate(x2d, sc, sh, seq):
    t, d = x2d.shape
    per_b = seq // ROW_TILE
    mod_spec = pl.BlockSpec((1, 1, d), lambda i: (i // per_b, 0, 0))
    return pl.pallas_call(
        _modulate_kernel,
        grid=(t // ROW_TILE,),
        in_specs=[pl.BlockSpec((ROW_TILE, d), lambda i: (i, 0)), mod_spec, mod_spec],
        out_specs=pl.BlockSpec((ROW_TILE, d), lambda i: (i, 0)),
        out_shape=jax.ShapeDtypeStruct((t, d), jnp.bfloat16),
        compiler_params=pltpu.CompilerParams(dimension_semantics=("parallel",)),
    )(x2d, sc, sh)


def _merge_kernel(y0, y1, y2, y3, wb_ref, g0, g1, g2, g3, bm_ref, o_ref):
    acc = None
    for n, (y, g) in enumerate(((y0, g0), (y1, g1), (y2, g2), (y3, g3))):
        gate = jax.nn.sigmoid(g[...].astype(jnp.float32) + bm_ref[n:n + 1, :])
        term = gate * jnp.dot(y[...], wb_ref[n], preferred_element_type=jnp.float32)
        acc = term if acc is None else acc + term
    o_ref[...] = acc.astype(o_ref.dtype)


def merge_branches(ys, w_branch_bf, proj_b, gate_col0, b_merge, tm=512, tn=1024):
    t, w = ys[0].shape
    d = w_branch_bf.shape[2]
    assert gate_col0 % tn == 0 and d % tn == 0
    y_spec = pl.BlockSpec((tm, w), lambda j, i: (i, 0))
    g_specs = [pl.BlockSpec((tm, tn), lambda j, i, n=n: (i, (gate_col0 + n * d) // tn + j))
               for n in range(N_BRANCHES)]
    return pl.pallas_call(
        _merge_kernel,
        grid=(d // tn, t // tm),
        in_specs=[y_spec] * 4 + [pl.BlockSpec((N_BRANCHES, w, tn), lambda j, i: (0, 0, j))]
        + g_specs + [pl.BlockSpec((N_BRANCHES, tn), lambda j, i: (0, j))],
        out_specs=pl.BlockSpec((tm, tn), lambda j, i: (i, j)),
        out_shape=jax.ShapeDtypeStruct((t, d), jnp.bfloat16),
        compiler_params=pltpu.CompilerParams(
            dimension_semantics=("parallel", "parallel"), vmem_limit_bytes=VMEM_LIMIT),
    )(*ys, w_branch_bf, proj_b, proj_b, proj_b, proj_b, b_merge.reshape(N_BRANCHES, d))


def _pack_pair(lo, hi):
    def rne(x):
        b = lax.bitcast_convert_type(x, jnp.uint32)
        return (b + jnp.uint32(0x7FFF) + ((b >> 16) & jnp.uint32(1))) >> 16
    return rne(lo) | (rne(hi) << 16)


def _unpack_lo(w):
    return lax.bitcast_convert_type(w << 16, jnp.float32)


def _unpack_hi(w):
    return lax.bitcast_convert_type(w & jnp.uint32(0xFFFF0000), jnp.float32)


def _layer_norm_halves(z_lo, z_hi, g_ref, b_ref, half):
    d = 2 * half
    mu = (jnp.sum(z_lo, axis=1, keepdims=True) + jnp.sum(z_hi, axis=1, keepdims=True)) / d
    c_lo, c_hi = z_lo - mu, z_hi - mu
    var = (jnp.sum(c_lo * c_lo, axis=1, keepdims=True)
           + jnp.sum(c_hi * c_hi, axis=1, keepdims=True)) / d
    r = lax.rsqrt(var + LN_EPS)
    return (c_lo * r * g_ref[:, :half] + b_ref[:, :half],
            c_hi * r * g_ref[:, half:] + b_ref[:, half:])


def _ln_router_kernel(x_ref, mix_ref, g1_ref, lng_ref, lnb_ref, sc_ref, sh_ref, rw_ref, rb_ref,
                      x1_ref, upk_ref, tidx_ref, tw_ref, rank_ref, cnt_ref, carry_sc):
    tm, d = x_ref.shape
    half = d // 2
    n_exp = rw_ref.shape[1]

    @pl.when(pl.program_id(0) == 0)
    def _():
        carry_sc[...] = jnp.zeros_like(carry_sc)

    z = DEEPNORM_ALPHA * x_ref[...] + (1.0 + g1_ref[0]) * mix_ref[...]
    x_lo, x_hi = _layer_norm_halves(z[:, :half], z[:, half:], lng_ref, lnb_ref, half)
    x1_ref[:, :half] = x_lo
    x1_ref[:, half:] = x_hi
    u_lo = x_lo * (1.0 + sc_ref[0, :, :half]) + sh_ref[0, :, :half]
    u_hi = x_hi * (1.0 + sc_ref[0, :, half:]) + sh_ref[0, :, half:]
    upk_ref[...] = _pack_pair(u_lo, u_hi)

    logits = (jnp.dot(u_lo.astype(jnp.bfloat16), rw_ref[:half, :],
                      preferred_element_type=jnp.float32)
              + jnp.dot(u_hi.astype(jnp.bfloat16), rw_ref[half:, :],
                        preferred_element_type=jnp.float32) + rb_ref[...])
    lane = lax.broadcasted_iota(jnp.int32, (tm, n_exp), 1).astype(jnp.float32)
    lg = logits
    vals, hots = [], []
    for k in range(TOP_K):
        m = jnp.max(lg, axis=1, keepdims=True)
        ik = jnp.min(jnp.where(lg == m, lane, float(n_exp)), axis=1, keepdims=True)
        hot = lane == ik
        vals.append(m)
        hots.append(hot)
        tidx_ref[:, k:k + 1] = ik.astype(jnp.int32)
        lg = jnp.where(hot, -jnp.inf, lg)
    exps = [jnp.exp(v - vals[0]) for v in vals]
    den = exps[0] + exps[1] + exps[2] + exps[3]
    for k in range(TOP_K):
        tw_ref[:, k:k + 1] = exps[k] / den

    mask = jnp.where(hots[0] | hots[1] | hots[2] | hots[3], 1.0, 0.0)
    row = lax.broadcasted_iota(jnp.int32, (tm, tm), 0)
    col = lax.broadcasted_iota(jnp.int32, (tm, tm), 1)
    lower = jnp.where(row > col, 1.0, 0.0).astype(jnp.bfloat16)
    rank = jnp.dot(lower, mask.astype(jnp.bfloat16),
                   preferred_element_type=jnp.float32) + carry_sc[...]
    for k in range(TOP_K):
        rank_ref[:, k:k + 1] = jnp.sum(jnp.where(hots[k], rank, 0.0), axis=1,
                                       keepdims=True).astype(jnp.int32)
    carry_sc[...] = carry_sc[...] + jnp.sum(mask, axis=0, keepdims=True)
    cnt_ref[...] = carry_sc[...]


def ln_router(x2d, mix, g1, ln_g, ln_b, sc2, sh2, router_w_bf, router_b, seq):
    t, d = x2d.shape
    n_exp = router_w_bf.shape[1]
    per_b = seq // ROW_TILE
    row = lambda w: pl.BlockSpec((ROW_TILE, w), lambda i: (i, 0))
    mod = pl.BlockSpec((1, 1, d), lambda i: (i // per_b, 0, 0))
    vec = lambda w: pl.BlockSpec((1, w), lambda i: (0, 0))
    return pl.pallas_call(
        _ln_router_kernel,
        grid=(t // ROW_TILE,),
        in_specs=[row(d), row(d), mod, vec(d), vec(d), mod, mod,
                  pl.BlockSpec((d, n_exp), lambda i: (0, 0)), vec(n_exp)],
        out_specs=[row(d), row(d // 2), row(TOP_K), row(TOP_K), row(TOP_K), vec(n_exp)],
        out_shape=[jax.ShapeDtypeStruct((t, d), jnp.float32),
                   jax.ShapeDtypeStruct((t, d // 2), jnp.uint32),
                   jax.ShapeDtypeStruct((t, TOP_K), jnp.int32),
                   jax.ShapeDtypeStruct((t, TOP_K), jnp.float32),
                   jax.ShapeDtypeStruct((t, TOP_K), jnp.int32),
                   jax.ShapeDtypeStruct((1, n_exp), jnp.float32)],
        scratch_shapes=[pltpu.VMEM((1, n_exp), jnp.float32)],
        compiler_params=pltpu.CompilerParams(
            dimension_semantics=("arbitrary",), vmem_limit_bytes=VMEM_LIMIT),
    )(x2d, mix, g1, ln_g.reshape(1, d), ln_b.reshape(1, d), sc2, sh2, router_w_bf,
      router_b.reshape(1, n_exp))


def _row_copy(src_hbm, src_row, dst, dst_row, sem):
    return pltpu.make_async_copy(src_hbm.at[pl.ds(src_row, 1)], dst.at[pl.ds(dst_row, 1)], sem)


def _dispatch_kernel(dest_ref, u_hbm, xs_in_hbm, xs_hbm, sem):
    del xs_in_hbm
    base = pl.program_id(0) * ROW_TILE

    def issue(r, carry):
        for k in range(TOP_K):
            _row_copy(u_hbm, base + r, xs_hbm, dest_ref[r * TOP_K + k], sem).start()
        return carry

    def drain(r, carry):
        for k in range(TOP_K):
            _row_copy(u_hbm, 0, xs_hbm, 0, sem).wait()
        return carry

    lax.fori_loop(0, ROW_TILE, issue, 0)
    lax.fori_loop(0, ROW_TILE, drain, 0)


def moe_dispatch(upk, dest_flat, n_rows):
    t, w = upk.shape
    xs0 = jnp.zeros((n_rows, w), upk.dtype)
    return pl.pallas_call(
        _dispatch_kernel,
        grid=(t // ROW_TILE,),
        in_specs=[pl.BlockSpec((ROW_TILE * TOP_K,), lambda i: (i,), memory_space=pltpu.SMEM),
                  pl.BlockSpec(memory_space=pl.ANY),
                  pl.BlockSpec(memory_space=pl.ANY)],
        out_specs=pl.BlockSpec(memory_space=pl.ANY),
        out_shape=jax.ShapeDtypeStruct((n_rows, w), upk.dtype),
        scratch_shapes=[pltpu.SemaphoreType.DMA(())],
        input_output_aliases={2: 0},
        compiler_params=pltpu.CompilerParams(dimension_semantics=("arbitrary",)),
    )(dest_flat, upk, xs0)


def _expert_kernel(te_ref, nt_ref, xs_ref, wu_ref, bu_ref, wd_ref, bd_ref, ys_ref):
    g = pl.program_id(0)
    half = xs_ref.shape[1]
    ff = wd_ref.shape[1]

    @pl.when(g < nt_ref[0])
    def _():
        w = xs_ref[...]
        x_lo = _unpack_lo(w).astype(jnp.bfloat16)
        x_hi = _unpack_hi(w).astype(jnp.bfloat16)
        hid = (jnp.dot(x_lo, wu_ref[0, :half, :], preferred_element_type=jnp.float32)
               + jnp.dot(x_hi, wu_ref[0, half:, :], preferred_element_type=jnp.float32)
               + bu_ref[0])
        glu = jnp.minimum(hid[:, :ff], SWIGLU_LIMIT)
        lin = jnp.clip(hid[:, ff:], -SWIGLU_LIMIT, SWIGLU_LIMIT)
        act = glu * jax.nn.sigmoid(SWIGLU_ALPHA * glu) * (lin + 1.0)
        y = jnp.dot(act.astype(jnp.bfloat16), wd_ref[0],
                    preferred_element_type=jnp.float32) + bd_ref[0]
        ys_ref[...] = _pack_pair(y[:, :half], y[:, half:])

    @pl.when(g >= nt_ref[0])
    def _():
        ys_ref[...] = jnp.zeros_like(ys_ref)


def grouped_experts(xs, tile_expert, n_tiles, w_up_bf, b_up, w_down_bf, b_down):
    n_rows, half = xs.shape
    n_exp, d, ff2 = w_up_bf.shape
    ff = ff2 // 2
    grid_spec = pltpu.PrefetchScalarGridSpec(
        num_scalar_prefetch=2,
        grid=(n_rows // MOE_TILE,),
        in_specs=[pl.BlockSpec((MOE_TILE, half), lambda g, te, nt: (g, 0)),
                  pl.BlockSpec((1, d, ff2), lambda g, te, nt: (te[g], 0, 0)),
                  pl.BlockSpec((1, 1, ff2), lambda g, te, nt: (te[g], 0, 0)),
                  pl.BlockSpec((1, ff, d), lambda g, te, nt: (te[g], 0, 0)),
                  pl.BlockSpec((1, 1, d), lambda g, te, nt: (te[g], 0, 0))],
        out_specs=pl.BlockSpec((MOE_TILE, half), lambda g, te, nt: (g, 0)),
    )
    return pl.pallas_call(
        _expert_kernel,
        grid_spec=grid_spec,
        out_shape=jax.ShapeDtypeStruct((n_rows, half), jnp.uint32),
        compiler_params=pltpu.CompilerParams(
            dimension_semantics=("arbitrary",), vmem_limit_bytes=VMEM_LIMIT),
    )(tile_expert, n_tiles, xs, w_up_bf, b_up.reshape(n_exp, 1, ff2), w_down_bf,
      b_down.reshape(n_exp, 1, d))


def _combine_kernel(dest_ref, tw_ref, x1_ref, g2_ref, lng_ref, lnb_ref, sc_ref, sh_ref, ys_hbm,
                    x2_ref, un_ref, buf, sem):
    tm, d = x1_ref.shape
    half = d // 2

    def issue(r, carry):
        for k in range(TOP_K):
            _row_copy(ys_hbm, dest_ref[r * TOP_K + k], buf.at[k], r, sem).start()
        return carry

    def drain(r, carry):
        for k in range(TOP_K):
            _row_copy(ys_hbm, 0, buf.at[k], 0, sem).wait()
        return carry

    lax.fori_loop(0, tm, issue, 0)
    lax.fori_loop(0, tm, drain, 0)

    f_lo = f_hi = None
    for k in range(TOP_K):
        wk = tw_ref[:, k:k + 1]
        w = buf[k]
        t_lo, t_hi = wk * _unpack_lo(w), wk * _unpack_hi(w)
        f_lo = t_lo if f_lo is None else f_lo + t_lo
        f_hi = t_hi if f_hi is None else f_hi + t_hi
    z_lo = DEEPNORM_ALPHA * x1_ref[:, :half] + (1.0 + g2_ref[0, :, :half]) * f_lo
    z_hi = DEEPNORM_ALPHA * x1_ref[:, half:] + (1.0 + g2_ref[0, :, half:]) * f_hi
    x_lo, x_hi = _layer_norm_halves(z_lo, z_hi, lng_ref, lnb_ref, half)
    x2_ref[:, :half] = x_lo
    x2_ref[:, half:] = x_hi
    un_ref[:, :half] = (x_lo * (1.0 + sc_ref[0, :, :half]) + sh_ref[0, :, :half]).astype(un_ref.dtype)
    un_ref[:, half:] = (x_hi * (1.0 + sc_ref[0, :, half:]) + sh_ref[0, :, half:]).astype(un_ref.dtype)


def combine_ln(dest_flat, top_w, x1, g2, ln_g, ln_b, sc_next, sh_next, ys, seq):
    t, d = x1.shape
    per_b = seq // ROW_TILE
    row = lambda w: pl.BlockSpec((ROW_TILE, w), lambda i: (i, 0))
    mod = pl.BlockSpec((1, 1, d), lambda i: (i // per_b, 0, 0))
    vec = pl.BlockSpec((1, d), lambda i: (0, 0))
    return pl.pallas_call(
        _combine_kernel,
        grid=(t // ROW_TILE,),
        in_specs=[pl.BlockSpec((ROW_TILE * TOP_K,), lambda i: (i,), memory_space=pltpu.SMEM),
                  row(TOP_K), row(d), mod, vec, vec, mod, mod,
                  pl.BlockSpec(memory_space=pl.ANY)],
        out_specs=[row(d), row(d)],
        out_shape=[jax.ShapeDtypeStruct((t, d), jnp.float32),
                   jax.ShapeDtypeStruct((t, d), jnp.bfloat16)],
        scratch_shapes=[pltpu.VMEM((TOP_K, ROW_TILE, d // 2), jnp.uint32),
                        pltpu.SemaphoreType.DMA(())],
        compiler_params=pltpu.CompilerParams(
            dimension_semantics=("arbitrary",), vmem_limit_bytes=VMEM_LIMIT),
    )(dest_flat, top_w, x1, g2, ln_g.reshape(1, d), ln_b.reshape(1, d), sc_next, sh_next, ys)


def moe_layout(counts, top_idx, rank4, n_tiles_max):
    cnt = counts.reshape(-1).astype(jnp.int32)
    tiles = (cnt + MOE_TILE - 1) // MOE_TILE
    tile_end = jnp.cumsum(tiles)
    row_start = (tile_end - tiles) * MOE_TILE
    dest = (row_start[top_idx] + rank4).reshape(-1)
    tile_expert = jnp.minimum(
        jnp.sum(jnp.arange(n_tiles_max)[:, None] >= tile_end[None, :], axis=1),
        cnt.shape[0] - 1).astype(jnp.int32)
    return dest.astype(jnp.int32), tile_expert, tile_end[-1:].astype(jnp.int32)


NEG_BIG = -1e30


def _t5_bucket_np(d):
    max_exact = REL_BUCKETS // 2
    dd = np.maximum(d, 1).astype(np.float32)
    far = max_exact + (np.log(dd / np.float32(max_exact))
                       / np.float32(math.log(REL_MAX_DIST / max_exact))
                       * np.float32(REL_BUCKETS - max_exact)).astype(np.int32)
    return np.where(d < max_exact, d, np.minimum(far, REL_BUCKETS - 1)).astype(np.int32)


def _moba_bucket_tables():
    r = np.arange(MOBA_BLOCK)[:, None]
    c = np.arange(MOBA_BLOCK)[None, :]
    own = np.where(r >= c, _t5_bucket_np(np.maximum(r - c, 0)), -1)
    prev = _t5_bucket_np(MOBA_BLOCK + r - c)
    return np.stack([own, prev]).astype(np.int32)


def _bias_tab_kernel(rb_ref, bk_ref, o_ref):
    h = pl.program_id(0)
    bk = bk_ref[...]
    acc = jnp.zeros(bk.shape, jnp.float32)
    for m in range(REL_BUCKETS):
        acc = jnp.where(bk == m, rb_ref[h, m], acc)
    o_ref[0] = jnp.where(bk < 0, NEG_BIG, acc)


def moba_bias_tables(rel_bias_t):
    n_heads = rel_bias_t.shape[0]
    bk = jnp.asarray(_moba_bucket_tables())
    return pl.pallas_call(
        _bias_tab_kernel,
        grid=(n_heads,),
        in_specs=[pl.BlockSpec(memory_space=pltpu.SMEM),
                  pl.BlockSpec((2, MOBA_BLOCK, MOBA_BLOCK), lambda h: (0, 0, 0))],
        out_specs=pl.BlockSpec((1, 2, MOBA_BLOCK, MOBA_BLOCK), lambda h: (h, 0, 0, 0)),
        out_shape=jax.ShapeDtypeStruct((n_heads, 2, MOBA_BLOCK, MOBA_BLOCK), jnp.float32),
    )(rel_bias_t, bk)


def _moba_kernel(rb_ref, q_ref, k_ref, v_ref, bt_ref, o_ref, m_sc, l_sc, acc_sc, *, n_blk):
    h = pl.program_id(1)
    i = pl.program_id(2)
    blk = MOBA_BLOCK
    scale = MOBA_HD ** -0.5
    nt = (((1,), (1,)), ((), ()))
    q = q_ref[...]

    kmean = jnp.mean(k_ref[...].astype(jnp.float32).reshape(n_blk, blk, MOBA_HD), axis=1)
    km_hi = kmean.astype(jnp.bfloat16)
    km_lo = (kmean - km_hi.astype(jnp.float32)).astype(jnp.bfloat16)
    gate = (lax.dot_general(q, km_hi, nt, preferred_element_type=jnp.float32)
            + lax.dot_general(q, km_lo, nt, preferred_element_type=jnp.float32))
    g = [gate[:, n:n + 1] for n in range(n_blk)]

    def selected(j):
        cnt = jnp.zeros((blk, 1), jnp.float32)
        for n in range(n_blk - 1):
            if n == j:
                continue
            ahead = (g[n] >= g[j]) if n < j else (g[n] > g[j])
            cnt = cnt + jnp.where(ahead, jnp.where(n < i, 1.0, 0.0), 0.0)
        return cnt < float(MOBA_TOPK)

    row0 = pl.multiple_of(i * blk, blk)
    k_own = k_ref[pl.ds(row0, blk), :]
    v_own = v_ref[pl.ds(row0, blk), :]
    s = lax.dot_general(q, k_own, nt, preferred_element_type=jnp.float32) * scale + bt_ref[0, 0]
    m0 = jnp.max(s, axis=1, keepdims=True)
    p = jnp.exp(s - m0)
    m_sc[...] = m0
    l_sc[...] = jnp.sum(p, axis=1, keepdims=True)
    acc_sc[...] = jnp.dot(p.astype(jnp.bfloat16), v_own, preferred_element_type=jnp.float32)

    bfar = rb_ref[h, REL_BUCKETS - 1]
    for j in range(n_blk - 1):
        @pl.when(j < i)
        def _():
            k_j = k_ref[j * blk:(j + 1) * blk, :]
            v_j = v_ref[j * blk:(j + 1) * blk, :]
            bias = jnp.where(j == i - 1, bt_ref[0, 1], bfar)
            s = lax.dot_general(q, k_j, nt, preferred_element_type=jnp.float32) * scale + bias
            s = jnp.where(selected(j), s, NEG_BIG)
            m_old = m_sc[...]
            m_new = jnp.maximum(m_old, jnp.max(s, axis=1, keepdims=True))
            alpha = jnp.exp(m_old - m_new)
            p = jnp.exp(s - m_new)
            m_sc[...] = m_new
            l_sc[...] = alpha * l_sc[...] + jnp.sum(p, axis=1, keepdims=True)
            acc_sc[...] = alpha * acc_sc[...] + jnp.dot(
                p.astype(jnp.bfloat16), v_j, preferred_element_type=jnp.float32)

    o_ref[...] = (acc_sc[...] / l_sc[...]).astype(o_ref.dtype)


def moba_pallas(qkv, rel_bias_t, btab, n_batch, seq, q_blk0, k_blk0, v_blk0, out_dtype):
    n_heads = rel_bias_t.shape[0]
    n_blk = seq // MOBA_BLOCK
    assert seq % MOBA_BLOCK == 0
    kern = functools.partial(_moba_kernel, n_blk=n_blk)
    return pl.pallas_call(
        kern,
        grid=(n_batch, n_heads, n_blk),
        in_specs=[
            pl.BlockSpec(memory_space=pltpu.SMEM),
            pl.BlockSpec((MOBA_BLOCK, MOBA_HD), lambda b, h, i: (b * n_blk + i, q_blk0 + h)),
            pl.BlockSpec((seq, MOBA_HD), lambda b, h, i: (b, k_blk0 + h)),
            pl.BlockSpec((seq, MOBA_HD), lambda b, h, i: (b, v_blk0 + h)),
            pl.BlockSpec((1, 2, MOBA_BLOCK, MOBA_BLOCK), lambda b, h, i: (h, 0, 0, 0)),
        ],
        out_specs=pl.BlockSpec((MOBA_BLOCK, MOBA_HD), lambda b, h, i: (b * n_blk + i, h)),
        out_shape=jax.ShapeDtypeStruct((n_batch * seq, n_heads * MOBA_HD), out_dtype),
        scratch_shapes=[pltpu.VMEM((MOBA_BLOCK, 1), jnp.float32),
                        pltpu.VMEM((MOBA_BLOCK, 1), jnp.float32),
                        pltpu.VMEM((MOBA_BLOCK, MOBA_HD), jnp.float32)],
        compiler_params=pltpu.CompilerParams(
            dimension_semantics=("parallel", "parallel", "arbitrary")),
    )(rel_bias_t, qkv, qkv, qkv, btab)


def layer_norm(x, g, b):
    xf = x.astype(jnp.float32)
    mu = jnp.mean(xf, axis=-1, keepdims=True)
    var = jnp.mean(jnp.square(xf - mu), axis=-1, keepdims=True)
    return ((xf - mu) * lax.rsqrt(var + LN_EPS) * g + b).astype(x.dtype)


def group_norm(x, n_groups, gain, bias=None, rms=False):
    shp = x.shape
    xf = x.astype(jnp.float32).reshape(shp[:-1] + (n_groups, shp[-1] // n_groups))
    if not rms:
        xf = xf - jnp.mean(xf, axis=-1, keepdims=True)
    xf = xf * lax.rsqrt(jnp.mean(jnp.square(xf), axis=-1, keepdims=True) + LN_EPS)
    y = xf.reshape(shp) * gain
    if bias is not None:
        y = y + bias
    return y.astype(x.dtype)


def causal_depthwise_conv(x, w, b):
    width, ch = w.shape
    y = lax.conv_general_dilated(x, w[:, None, :].astype(x.dtype), window_strides=(1,),
                                 padding=[(width - 1, 0)],
                                 dimension_numbers=('NWC', 'WIO', 'NWC'),
                                 feature_group_count=ch)
    return y + b


def t5_bucket(dist):
    max_exact = REL_BUCKETS // 2
    dist = jnp.maximum(dist, 0)
    far = max_exact + (jnp.log(jnp.maximum(dist, 1).astype(jnp.float32) / max_exact)
                       / math.log(REL_MAX_DIST / max_exact)
                       * (REL_BUCKETS - max_exact)).astype(jnp.int32)
    return jnp.where(dist < max_exact, dist, jnp.minimum(far, REL_BUCKETS - 1))


def mlstm_chunkwise(q, k, v, i_pre, f_pre):
    B, S, H, dqk = q.shape
    dv = v.shape[-1]
    L = MLSTM_CHUNK
    nc = S // L

    def chunks(a):
        a = a.astype(jnp.float32).reshape((B, nc, L, H) + a.shape[3:])
        return jnp.moveaxis(a, 3, 1)

    qc, kc, vc = chunks(q), chunks(k) * dqk ** -0.5, chunks(v)
    ig = chunks(i_pre)
    bcum = jnp.cumsum(jax.nn.log_sigmoid(chunks(f_pre)), axis=-1)
    b_last = bcum[..., -1]
    causal = jnp.tril(jnp.ones((L, L), dtype=bool))
    dmat = jnp.where(causal, bcum[..., :, None] - bcum[..., None, :] + ig[..., None, :], -jnp.inf)
    a_end = b_last[..., None] - bcum + ig
    a_max = jnp.max(a_end, axis=-1)
    w_end = jnp.exp(a_end - a_max[..., None])
    kv_chunk = jnp.einsum('bhnld,bhnlv->bhndv', kc * w_end[..., None], vc)
    k_chunk = jnp.einsum('bhnl,bhnld->bhnd', w_end, kc)

    def step(carry, xs):
        c_st, n_st, m_st = carry
        bl, am, kv, ks = xs
        m_new = jnp.maximum(bl + m_st, am)
        f_sc = jnp.exp(bl + m_st - m_new)
        i_sc = jnp.exp(am - m_new)
        c_new = f_sc[..., None, None] * c_st + i_sc[..., None, None] * kv
        n_new = f_sc[..., None] * n_st + i_sc[..., None] * ks
        return (c_new, n_new, m_new), (c_st, n_st, m_st)

    init = (jnp.zeros((B, H, dqk, dv), jnp.float32), jnp.zeros((B, H, dqk), jnp.float32),
            jnp.zeros((B, H), jnp.float32))
    xs = (jnp.moveaxis(b_last, 2, 0), jnp.moveaxis(a_max, 2, 0),
          jnp.moveaxis(kv_chunk, 2, 0), jnp.moveaxis(k_chunk, 2, 0))
    _, (c_prev, n_prev, m_prev) = lax.scan(step, init, xs)
    c_prev = jnp.moveaxis(c_prev, 0, 2)
    n_prev = jnp.moveaxis(n_prev, 0, 2)
    m_prev = jnp.moveaxis(m_prev, 0, 2)
    inter = bcum + m_prev[..., None]
    m_t = jnp.maximum(inter, jnp.max(dmat, axis=-1))
    s_qk = jnp.einsum('bhntd,bhnsd->bhnts', qc, kc) * jnp.exp(dmat - m_t[..., None])
    w_inter = jnp.exp(inter - m_t)
    num = (jnp.einsum('bhnts,bhnsv->bhntv', s_qk, vc)
           + w_inter[..., None] * jnp.einsum('bhntd,bhndv->bhntv', qc, c_prev))
    den = jnp.sum(s_qk, axis=-1) + w_inter * jnp.einsum('bhntd,bhnd->bhnt', qc, n_prev)
    h = num / jnp.maximum(jnp.abs(den), jnp.exp(-m_t))[..., None]
    return jnp.moveaxis(h, 1, 3).reshape(B, S, H, dv)


def gla_chunkwise(q, k, v, log_a):
    B, S, H, dk = q.shape
    dv = v.shape[-1]
    C = GLA_CHUNK
    nc = S // C

    def chunks(a):
        return jnp.moveaxis(a.astype(jnp.float32).reshape(B, nc, C, H, a.shape[-1]), 3, 1)

    qc, kc, vc = chunks(q) * dk ** -0.5, chunks(k), chunks(v)
    bcum = jnp.cumsum(chunks(log_a), axis=3)
    b_last = bcum[..., -1, :]
    causal = jnp.tril(jnp.ones((C, C), dtype=bool))
    decay = jnp.exp(jnp.where(causal[..., None],
                              bcum[..., :, None, :] - bcum[..., None, :, :], -jnp.inf))
    attn = jnp.einsum('bhntsd,bhnsd->bhnts', decay * qc[..., :, None, :], kc)
    o_intra = jnp.einsum('bhnts,bhnsv->bhntv', attn, vc)
    kv_chunk = jnp.einsum('bhnsd,bhnsv->bhndv', kc * jnp.exp(b_last[..., None, :] - bcum), vc)

    def step(state, xs):
        dec, kv = xs
        return dec[..., None] * state + kv, state

    _, s_prev = lax.scan(step, jnp.zeros((B, H, dk, dv), jnp.float32),
                         (jnp.moveaxis(jnp.exp(b_last), 2, 0), jnp.moveaxis(kv_chunk, 2, 0)))
    s_prev = jnp.moveaxis(s_prev, 0, 2)
    o = o_intra + jnp.einsum('bhntd,bhndv->bhntv', qc * jnp.exp(bcum), s_prev)
    return jnp.moveaxis(o, 1, 3).reshape(B, S, H, dv)


def moba_attention(q, k, v, rel_bias):
    B, S, H, hd = q.shape
    n_blk = -(-S // MOBA_BLOCK)
    pad = n_blk * MOBA_BLOCK - S

    def blocks(a):
        a = jnp.pad(a, ((0, 0), (0, pad), (0, 0), (0, 0)))
        return jnp.moveaxis(a.reshape(B, n_blk, MOBA_BLOCK, H, hd), 3, 1)

    kb, vb = blocks(k), blocks(v)
    k_mean = jnp.mean(kb, axis=3)
    n_sel = min(MOBA_TOPK, n_blk - 1)
    scale = hd ** -0.5
    bias_hb = rel_bias.T.astype(jnp.float32)
    head_ix = jnp.arange(H)
    blk_pos = jnp.arange(MOBA_BLOCK)
    gather_blocks = jax.vmap(jax.vmap(lambda arr, ix: arr[ix]))

    def chunk(start):
        qc = jnp.moveaxis(lax.dynamic_slice_in_dim(q, start, MOBA_Q_CHUNK, axis=1), 2, 1)
        t = start + jnp.arange(MOBA_Q_CHUNK)
        own = start // MOBA_BLOCK
        k_own = lax.dynamic_index_in_dim(kb, own, axis=2, keepdims=False)
        v_own = lax.dynamic_index_in_dim(vb, own, axis=2, keepdims=False)
        d_own = t[:, None] - (own * MOBA_BLOCK + blk_pos)[None, :]
        lg_own = (jnp.einsum('bhqd,bhkd->bhqk', qc, k_own).astype(jnp.float32) * scale
                  + bias_hb[:, t5_bucket(d_own)])
        lg_own = jnp.where(d_own >= 0, lg_own, -jnp.inf)
        gate = jnp.einsum('bhqd,bhnd->bhqn', qc, k_mean).astype(jnp.float32)
        gate = jnp.where(jnp.arange(n_blk) < own, gate, -jnp.inf)
        _, idx = lax.top_k(gate, n_sel)
        k_sel = gather_blocks(kb, idx)
        v_sel = gather_blocks(vb, idx)
        d_sel = t[:, None, None] - (idx[..., None] * MOBA_BLOCK + blk_pos)
        lg_sel = (jnp.einsum('bhqd,bhqnkd->bhqnk', qc, k_sel).astype(jnp.float32) * scale
                  + bias_hb[head_ix[None, :, None, None, None], t5_bucket(d_sel)])
        valid = jnp.arange(n_sel) < own
        lg_sel = jnp.where(valid[:, None], lg_sel, -jnp.inf)
        lg = jnp.concatenate(
            [lg_own, lg_sel.reshape(B, H, MOBA_Q_CHUNK, n_sel * MOBA_BLOCK)], axis=-1)
        p = jax.nn.softmax(lg, axis=-1).astype(v.dtype)
        p_own = p[..., :MOBA_BLOCK]
        p_sel = p[..., MOBA_BLOCK:].reshape(B, H, MOBA_Q_CHUNK, n_sel, MOBA_BLOCK)
        out = (jnp.einsum('bhqk,bhkd->bhqd', p_own, v_own)
               + jnp.einsum('bhqnk,bhqnkd->bhqd', p_sel, v_sel))
        return jnp.moveaxis(out, 1, 2)

    outs = lax.map(chunk, jnp.arange(0, S, MOBA_Q_CHUNK))
    return jnp.moveaxis(outs, 0, 1).reshape(B, S, H, hd)


def hybrid_mixer(ub, B, S, w_in, b_merge, mlstm_gate_b, mlstm_conv_w, mlstm_conv_b, mlstm_norm_g,
                 conf_dw_w, conf_dw_b, conf_norm_g, conf_norm_b, rel_bias_t, moba_btab,
                 gla_gate_w2, gla_gate_b, gla_norm_g, w_branch):
    W = BRANCH_WIDTH
    f32 = jnp.float32
    offs = [0] + np.cumsum(IN_SIZES).tolist()
    seg = lambda n: w_in[:, offs[n]:offs[n + 1]].astype(jnp.bfloat16)
    w_a = jnp.concatenate([seg(n) for n in (0, 1, 2, 3, 5, 7, 8, 9, 11)], axis=1)
    w_b = jnp.concatenate([seg(6), seg(12)], axis=1)
    w_c = jnp.pad(jnp.concatenate([seg(4), seg(10)], axis=1),
                  ((0, 0), (0, 128 - 2 * MLSTM_HEADS - GLA_RANK)))
    proj_a = matmul(ub, w_a, 512, 1024, jnp.float32)
    proj_b = matmul(ub, w_b, 512, 1024, jnp.bfloat16)
    proj_c = matmul(ub, w_c, 512, 128, jnp.float32)
    a_sizes = [IN_SIZES[n] for n in (0, 1, 2, 3, 5, 7, 8, 9)]
    mq, mk, mv, mo, conf_in, gq, gk, gv, g_out = jnp.split(
        proj_a.reshape(B, S, -1), np.cumsum(a_sizes).tolist(), axis=-1)
    mif = proj_c[:, :2 * MLSTM_HEADS].reshape(B, S, -1)
    g_lr = proj_c[:, 2 * MLSTM_HEADS:2 * MLSTM_HEADS + GLA_RANK].reshape(B, S, -1)

    qk = jax.nn.silu(causal_depthwise_conv(jnp.concatenate([mq, mk], axis=-1),
                                           mlstm_conv_w, mlstm_conv_b))
    mq, mk = jnp.split(qk, 2, axis=-1)
    gates = (mif + mlstm_gate_b).astype(jnp.float32)
    h = mlstm_chunkwise(mq.reshape(B, S, MLSTM_HEADS, MLSTM_DQK),
                        mk.reshape(B, S, MLSTM_HEADS, MLSTM_DQK),
                        mv.reshape(B, S, MLSTM_HEADS, MLSTM_DV),
                        gates[..., :MLSTM_HEADS], gates[..., MLSTM_HEADS:])
    h = group_norm(h.reshape(B, S, W).astype(f32), MLSTM_HEADS, mlstm_norm_g)
    y_mlstm = jax.nn.sigmoid(mo) * h

    a, g = jnp.split(conf_in, 2, axis=-1)
    y = causal_depthwise_conv(a * jax.nn.sigmoid(g), conf_dw_w, conf_dw_b)
    y_conv = jax.nn.silu(group_norm(y, CONV_GROUPS, conf_norm_g, conf_norm_b))

    y_moba = moba_pallas(proj_b, rel_bias_t, moba_btab, B, S, 0, MOBA_HEADS, 2 * MOBA_HEADS,
                         jnp.bfloat16)

    log_a = jax.nn.log_sigmoid((g_lr @ gla_gate_w2 + gla_gate_b).astype(jnp.float32)) / GLA_TAU
    o = gla_chunkwise(gq.reshape(B, S, GLA_HEADS, GLA_DK), gk.reshape(B, S, GLA_HEADS, GLA_DK),
                      gv.reshape(B, S, GLA_HEADS, GLA_DV), log_a.reshape(B, S, GLA_HEADS, GLA_DK))
    o = group_norm(o.reshape(B, S, W).astype(f32), GLA_HEADS, gla_norm_g, rms=True)
    y_gla = o * jax.nn.silu(g_out)

    flat = lambda y: y.reshape(B * S, W).astype(jnp.bfloat16)
    return merge_branches([flat(y_mlstm), flat(y_conv), y_moba, flat(y_gla)],
                          w_branch.astype(jnp.bfloat16), proj_b, 3 * W, b_merge)


def moe_ffn(u, router_w, router_b, w_up, b_up, w_down, b_down):
    B, S, D = u.shape
    tok = u.reshape(B * S, D)
    logits = (tok @ router_w + router_b).astype(jnp.float32)
    top_val, top_idx = lax.top_k(logits, TOP_K)
    top_w = jax.nn.softmax(top_val, axis=-1)
    combine = jnp.einsum('tk,tke->te', top_w,
                         jax.nn.one_hot(top_idx, N_EXPERTS, dtype=jnp.float32))
    out = jnp.zeros((B * S, D), jnp.float32)
    for e in range(N_EXPERTS):
        hid = tok @ w_up[e] + b_up[e]
        glu, lin = jnp.split(hid, 2, axis=-1)
        glu = jnp.minimum(glu, SWIGLU_LIMIT)
        lin = jnp.clip(lin, -SWIGLU_LIMIT, SWIGLU_LIMIT)
        act = glu * jax.nn.sigmoid(SWIGLU_ALPHA * glu) * (lin + 1)
        out = out + combine[:, e:e + 1] * (act @ w_down[e] + b_down[e])
    return out.reshape(B, S, D).astype(u.dtype)


def kernel(x, c, w_ada, b_ada, w_in, b_merge, mlstm_gate_b, mlstm_conv_w, mlstm_conv_b,
           mlstm_norm_g, conf_dw_w, conf_dw_b, conf_norm_g, conf_norm_b, rel_bias,
           gla_gate_w2, gla_gate_b, gla_norm_g, w_branch, w_out, ln1_g, ln1_b,
           router_w, router_b, exp_w_up, exp_b_up, exp_w_down, exp_b_down, ln2_g, ln2_b):
    B, S, D = x.shape
    bf16 = jnp.bfloat16
    mod_all = ada_modulation(c, w_ada, b_ada)
    mods = [jnp.split(mod_all[l][:, None, :], 6, axis=-1) for l in range(DEPTH)]
    rel_bias_t = rel_bias.T.astype(jnp.float32)
    moba_btab = moba_bias_tables(rel_bias_t)
    n_rows = B * S * TOP_K + N_EXPERTS * MOE_TILE
    x2d = x.reshape(B * S, D)
    ub = modulate(x2d, mods[0][1], mods[0][0], S)
    for l in range(DEPTH):
        sh1, sc1, g1, sh2, sc2, g2 = mods[l]
        merged = hybrid_mixer(ub, B, S, w_in[l], b_merge[l], mlstm_gate_b[l], mlstm_conv_w[l],
                              mlstm_conv_b[l], mlstm_norm_g[l], conf_dw_w[l], conf_dw_b[l],
                              conf_norm_g[l], conf_norm_b[l], rel_bias_t, moba_btab,
                              gla_gate_w2[l], gla_gate_b[l], gla_norm_g[l], w_branch[l])
        mix = matmul(merged, w_out[l].astype(bf16), 512, 1024, jnp.float32)
        x1, upk, top_idx, top_w, rank4, counts = ln_router(
            x2d, mix, g1, ln1_g[l], ln1_b[l], sc2, sh2, router_w[l].astype(bf16), router_b[l], S)
        dest, tile_expert, n_tiles = moe_layout(counts, top_idx, rank4, n_rows // MOE_TILE)
        xs = moe_dispatch(upk, dest, n_rows)
        ys = grouped_experts(xs, tile_expert, n_tiles, exp_w_up[l].astype(bf16), exp_b_up[l],
                             exp_w_down[l].astype(bf16), exp_b_down[l])
        nxt = min(l + 1, DEPTH - 1)
        x2d, ub = combine_ln(dest, top_w, x1, g2, ln2_g[l], ln2_b[l], mods[nxt][1], mods[nxt][0],
                             ys, S)
    return x2d.reshape(B, S, D)
```

```python
import functools
import math

import jax
import jax.numpy as jnp
import numpy as np
from jax import lax
from jax.experimental import pallas as pl
from jax.experimental.pallas import tpu as pltpu

D_MODEL = 4096
BATCH = 4
SEQ = 2048
DEPTH = 2

N_BRANCHES = 4
BRANCH_WIDTH = D_MODEL // N_BRANCHES
MLSTM_HEADS = 4
MLSTM_DV = BRANCH_WIDTH // MLSTM_HEADS
MLSTM_DQK = MLSTM_DV // 2
MLSTM_CONV = 4
MLSTM_CHUNK = 64
CONV_WIDTH = 31
CONV_GROUPS = 4
MOBA_HEADS = 8
MOBA_HD = BRANCH_WIDTH // MOBA_HEADS
MOBA_BLOCK = 256
MOBA_TOPK = 3
MOBA_Q_CHUNK = 32
REL_BUCKETS = 32
REL_MAX_DIST = 128
GLA_HEADS = 4
GLA_DV = BRANCH_WIDTH // GLA_HEADS
GLA_DK = GLA_DV // 2
GLA_RANK = 16
GLA_TAU = 16.0
GLA_CHUNK = 16
N_EXPERTS = 32
TOP_K = 4
EXPERT_FF = D_MODEL // 8
SWIGLU_ALPHA = 1.702
SWIGLU_LIMIT = 7.0
DEEPNORM_ALPHA = (2 * DEPTH) ** 0.25
LN_EPS = 1e-5

IN_SIZES = (
    MLSTM_HEADS * MLSTM_DQK, MLSTM_HEADS * MLSTM_DQK, BRANCH_WIDTH, BRANCH_WIDTH,
    2 * MLSTM_HEADS, 2 * BRANCH_WIDTH, 3 * BRANCH_WIDTH, GLA_HEADS * GLA_DK,
    GLA_HEADS * GLA_DK, BRANCH_WIDTH, GLA_RANK, BRANCH_WIDTH, N_BRANCHES * D_MODEL,
)
IN_WIDTH = sum(IN_SIZES)

VMEM_LIMIT = 56 * 1024 * 1024


def _mm_kernel(a_ref, b_ref, o_ref):
    o_ref[...] = jnp.dot(a_ref[...], b_ref[...],
                         preferred_element_type=jnp.float32).astype(o_ref.dtype)


def matmul(a, b, tm, tn, out_dtype):
    M, K = a.shape
    N = b.shape[1]
    return pl.pallas_call(
        _mm_kernel,
        grid=(N // tn, M // tm),
        in_specs=[pl.BlockSpec((tm, K), lambda j, i: (i, 0)),
                  pl.BlockSpec((K, tn), lambda j, i: (0, j))],
        out_specs=pl.BlockSpec((tm, tn), lambda j, i: (i, j)),
        out_shape=jax.ShapeDtypeStruct((M, N), out_dtype),
        compiler_params=pltpu.CompilerParams(
            dimension_semantics=("parallel", "parallel"), vmem_limit_bytes=VMEM_LIMIT),
    )(a, b)


ROW_TILE = 256
MOE_TILE = 256


def _ada_kernel(c_ref, w_ref, b_ref, o_ref):
    c = c_ref[...]
    cs = (c * jax.nn.sigmoid(c)).astype(jnp.bfloat16)
    o_ref[0] = jnp.dot(cs, w_ref[0].astype(jnp.bfloat16),
                       preferred_element_type=jnp.float32) + b_ref[0]


def ada_modulation(c, w_ada, b_ada, tn=512):
    n_layers, d, n = w_ada.shape
    nb = c.shape[0]
    c8 = jnp.pad(c, ((0, 8 - nb), (0, 0)))
    out = pl.pallas_call(
        _ada_kernel,
        grid=(n_layers, n // tn),
        in_specs=[pl.BlockSpec((8, d), lambda l, j: (0, 0)),
                  pl.BlockSpec((1, d, tn), lambda l, j: (l, 0, j)),
                  pl.BlockSpec((1, 1, tn), lambda l, j: (l, 0, j))],
        out_specs=pl.BlockSpec((1, 8, tn), lambda l, j: (l, 0, j)),
        out_shape=jax.ShapeDtypeStruct((n_layers, 8, n), jnp.float32),
        compiler_params=pltpu.CompilerParams(
            dimension_semantics=("parallel", "parallel"), vmem_limit_bytes=VMEM_LIMIT),
    )(c8, w_ada, b_ada.reshape(n_layers, 1, n))
    return out[:, :nb]


def _modulate_kernel(x_ref, sc_ref, sh_ref, o_ref):
    o_ref[...] = (x_ref[...] * (1.0 + sc_ref[0]) + sh_ref[0]).astype(o_ref.dtype)


def modulate(x2d, sc, sh, seq):
    t, d = x2d.shape
    per_b = seq // ROW_TILE
    mod_spec = pl.BlockSpec((1, 1, d), lambda i: (i // per_b, 0, 0))
    return pl.pallas_call(
        _modulate_kernel,
        grid=(t // ROW_TILE,),
        in_specs=[pl.BlockSpec((ROW_TILE, d), lambda i: (i, 0)), mod_spec, mod_spec],
        out_specs=pl.BlockSpec((ROW_TILE, d), lambda i: (i, 0)),
        out_shape=jax.ShapeDtypeStruct((t, d), jnp.bfloat16),
        compiler_params=pltpu.CompilerParams(dimension_semantics=("parallel",)),
    )(x2d, sc, sh)


def _merge_kernel(y0, y1, y2, y3, wb_ref, g0, g1, g2, g3, bm_ref, o_ref):
    acc = None
    for n, (y, g) in enumerate(((y0, g0), (y1, g1), (y2, g2), (y3, g3))):
        gate = jax.nn.sigmoid(g[...].astype(jnp.float32) + bm_ref[n:n + 1, :])
        term = gate * jnp.dot(y[...], wb_ref[n], preferred_element_type=jnp.float32)
        acc = term if acc is None else acc + term
    o_ref[...] = acc.astype(o_ref.dtype)


def merge_branches(ys, w_branch_bf, proj_b, gate_col0, b_merge, tm=512, tn=1024):
    t, w = ys[0].shape
    d = w_branch_bf.shape[2]
    assert gate_col0 % tn == 0 and d % tn == 0
    y_spec = pl.BlockSpec((tm, w), lambda j, i: (i, 0))
    g_specs = [pl.BlockSpec((tm, tn), lambda j, i, n=n: (i, (gate_col0 + n * d) // tn + j))
               for n in range(N_BRANCHES)]
    return pl.pallas_call(
        _merge_kernel,
        grid=(d // tn, t // tm),
        in_specs=[y_spec] * 4 + [pl.BlockSpec((N_BRANCHES, w, tn), lambda j, i: (0, 0, j))]
        + g_specs + [pl.BlockSpec((N_BRANCHES, tn), lambda j, i: (0, j))],
        out_specs=pl.BlockSpec((tm, tn), lambda j, i: (i, j)),
        out_shape=jax.ShapeDtypeStruct((t, d), jnp.bfloat16),
        compiler_params=pltpu.CompilerParams(
            dimension_semantics=("parallel", "parallel"), vmem_limit_bytes=VMEM_LIMIT),
    )(*ys, w_branch_bf, proj_b, proj_b, proj_b, proj_b, b_merge.reshape(N_BRANCHES, d))


def _pack_pair(lo, hi):
    def rne(x):
        b = lax.bitcast_convert_type(x, jnp.uint32)
        return (b + jnp.uint32(0x7FFF) + ((b >> 16) & jnp.uint32(1))) >> 16
    return rne(lo) | (rne(hi) << 16)


def _unpack_lo(w):
    return lax.bitcast_convert_type(w << 16, jnp.float32)


def _unpack_hi(w):
    return lax.bitcast_convert_type(w & jnp.uint32(0xFFFF0000), jnp.float32)


def _layer_norm_halves(z_lo, z_hi, g_ref, b_ref, half):
    d = 2 * half
    mu = (jnp.sum(z_lo, axis=1, keepdims=True) + jnp.sum(z_hi, axis=1, keepdims=True)) / d
    c_lo, c_hi = z_lo - mu, z_hi - mu
    var = (jnp.sum(c_lo * c_lo, axis=1, keepdims=True)
           + jnp.sum(c_hi * c_hi, axis=1, keepdims=True)) / d
    r = lax.rsqrt(var + LN_EPS)
    return (c_lo * r * g_ref[:, :half] + b_ref[:, :half],
            c_hi * r * g_ref[:, half:] + b_ref[:, half:])


def _ln_router_kernel(x_ref, mix_ref, g1_ref, lng_ref, lnb_ref, sc_ref, sh_ref, rw_ref, rb_ref,
                      x1_ref, upk_ref, tidx_ref, tw_ref, rank_ref, cnt_ref, carry_sc):
    tm, d = x_ref.shape
    half = d // 2
    n_exp = rw_ref.shape[1]

    @pl.when(pl.program_id(0) == 0)
    def _():
        carry_sc[...] = jnp.zeros_like(carry_sc)

    z = DEEPNORM_ALPHA * x_ref[...] + (1.0 + g1_ref[0]) * mix_ref[...]
    x_lo, x_hi = _layer_norm_halves(z[:, :half], z[:, half:], lng_ref, lnb_ref, half)
    x1_ref[:, :half] = x_lo
    x1_ref[:, half:] = x_hi
    u_lo = x_lo * (1.0 + sc_ref[0, :, :half]) + sh_ref[0, :, :half]
    u_hi = x_hi * (1.0 + sc_ref[0, :, half:]) + sh_ref[0, :, half:]
    upk_ref[...] = _pack_pair(u_lo, u_hi)

    logits = (jnp.dot(u_lo.astype(jnp.bfloat16), rw_ref[:half, :],
                      preferred_element_type=jnp.float32)
              + jnp.dot(u_hi.astype(jnp.bfloat16), rw_ref[half:, :],
                        preferred_element_type=jnp.float32) + rb_ref[...])
    lane = lax.broadcasted_iota(jnp.int32, (tm, n_exp), 1).astype(jnp.float32)
    lg = logits
    vals, hots = [], []
    for k in range(TOP_K):
        m = jnp.max(lg, axis=1, keepdims=True)
        ik = jnp.min(jnp.where(lg == m, lane, float(n_exp)), axis=1, keepdims=True)
        hot = lane == ik
        vals.append(m)
        hots.append(hot)
        tidx_ref[:, k:k + 1] = ik.astype(jnp.int32)
        lg = jnp.where(hot, -jnp.inf, lg)
    exps = [jnp.exp(v - vals[0]) for v in vals]
    den = exps[0] + exps[1] + exps[2] + exps[3]
    for k in range(TOP_K):
        tw_ref[:, k:k + 1] = exps[k] / den

    mask = jnp.where(hots[0] | hots[1] | hots[2] | hots[3], 1.0, 0.0)
    row = lax.broadcasted_iota(jnp.int32, (tm, tm), 0)
    col = lax.broadcasted_iota(jnp.int32, (tm, tm), 1)
    lower = jnp.where(row > col, 1.0, 0.0).astype(jnp.bfloat16)
    rank = jnp.dot(lower, mask.astype(jnp.bfloat16),
                   preferred_element_type=jnp.float32) + carry_sc[...]
    for k in range(TOP_K):
        rank_ref[:, k:k + 1] = jnp.sum(jnp.where(hots[k], rank, 0.0), axis=1,
                                       keepdims=True).astype(jnp.int32)
    carry_sc[...] = carry_sc[...] + jnp.sum(mask, axis=0, keepdims=True)
    cnt_ref[...] = carry_sc[...]


def ln_router(x2d, mix, g1, ln_g, ln_b, sc2, sh2, router_w_bf, router_b, seq):
    t, d = x2d.shape
    n_exp = router_w_bf.shape[1]
    per_b = seq // ROW_TILE
    row = lambda w: pl.BlockSpec((ROW_TILE, w), lambda i: (i, 0))
    mod = pl.BlockSpec((1, 1, d), lambda i: (i // per_b, 0, 0))
    vec = lambda w: pl.BlockSpec((1, w), lambda i: (0, 0))
    return pl.pallas_call(
        _ln_router_kernel,
        grid=(t // ROW_TILE,),
        in_specs=[row(d), row(d), mod, vec(d), vec(d), mod, mod,
                  pl.BlockSpec((d, n_exp), lambda i: (0, 0)), vec(n_exp)],
        out_specs=[row(d), row(d // 2), row(TOP_K), row(TOP_K), row(TOP_K), vec(n_exp)],
        out_shape=[jax.ShapeDtypeStruct((t, d), jnp.float32),
                   jax.ShapeDtypeStruct((t, d // 2), jnp.uint32),
                   jax.ShapeDtypeStruct((t, TOP_K), jnp.int32),
                   jax.ShapeDtypeStruct((t, TOP_K), jnp.float32),
                   jax.ShapeDtypeStruct((t, TOP_K), jnp.int32),
                   jax.ShapeDtypeStruct((1, n_exp), jnp.float32)],
        scratch_shapes=[pltpu.VMEM((1, n_exp), jnp.float32)],
        compiler_params=pltpu.CompilerParams(
            dimension_semantics=("arbitrary",), vmem_limit_bytes=VMEM_LIMIT),
    )(x2d, mix, g1, ln_g.reshape(1, d), ln_b.reshape(1, d), sc2, sh2, router_w_bf,
      router_b.reshape(1, n_exp))


def _row_copy(src_hbm, src_row, dst, dst_row, sem):
    return pltpu.make_async_copy(src_hbm.at[pl.ds(src_row, 1)], dst.at[pl.ds(dst_row, 1)], sem)


def _dispatch_kernel(dest_ref, u_ref, xs_in_hbm, xs_hbm, sem):
    del xs_in_hbm

    def issue(r, carry):
        for k in range(TOP_K):
            _row_copy(u_ref, r, xs_hbm, dest_ref[r * TOP_K + k], sem).start()
        return carry

    def drain(r, carry):
        for k in range(TOP_K):
            _row_copy(u_ref, 0, xs_hbm, 0, sem).wait()
        return carry

    lax.fori_loop(0, ROW_TILE, issue, 0)
    lax.fori_loop(0, ROW_TILE, drain, 0)


def moe_dispatch(upk, dest_flat, n_rows):
    t, w = upk.shape
    xs0 = jnp.zeros((n_rows, w), upk.dtype)
    return pl.pallas_call(
        _dispatch_kernel,
        grid=(t // ROW_TILE,),
        in_specs=[pl.BlockSpec((ROW_TILE * TOP_K,), lambda i: (i,), memory_space=pltpu.SMEM),
                  pl.BlockSpec((ROW_TILE, w), lambda i: (i, 0)),
                  pl.BlockSpec(memory_space=pl.ANY)],
        out_specs=pl.BlockSpec(memory_space=pl.ANY),
        out_shape=jax.ShapeDtypeStruct((n_rows, w), upk.dtype),
        scratch_shapes=[pltpu.SemaphoreType.DMA(())],
        input_output_aliases={2: 0},
        compiler_params=pltpu.CompilerParams(dimension_semantics=("arbitrary",)),
    )(dest_flat, upk, xs0)


def _expert_kernel(te_ref, nt_ref, xs_ref, wu_ref, bu_ref, wd_ref, bd_ref, ys_ref):
    g = pl.program_id(0)
    half = xs_ref.shape[1]
    ff = wd_ref.shape[1]

    @pl.when(g < nt_ref[0])
    def _():
        w = xs_ref[...]
        x_lo = _unpack_lo(w).astype(jnp.bfloat16)
        x_hi = _unpack_hi(w).astype(jnp.bfloat16)
        hid = (jnp.dot(x_lo, wu_ref[0, :half, :], preferred_element_type=jnp.float32)
               + jnp.dot(x_hi, wu_ref[0, half:, :], preferred_element_type=jnp.float32)
               + bu_ref[0])
        glu = jnp.minimum(hid[:, :ff], SWIGLU_LIMIT)
        lin = jnp.clip(hid[:, ff:], -SWIGLU_LIMIT, SWIGLU_LIMIT)
        act = glu * jax.nn.sigmoid(SWIGLU_ALPHA * glu) * (lin + 1.0)
        y = jnp.dot(act.astype(jnp.bfloat16), wd_ref[0],
                    preferred_element_type=jnp.float32) + bd_ref[0]
        ys_ref[...] = _pack_pair(y[:, :half], y[:, half:])

    @pl.when(g >= nt_ref[0])
    def _():
        ys_ref[...] = jnp.zeros_like(ys_ref)


def grouped_experts(xs, tile_expert, n_tiles, w_up_bf, b_up, w_down_bf, b_down):
    n_rows, half = xs.shape
    n_exp, d, ff2 = w_up_bf.shape
    ff = ff2 // 2
    grid_spec = pltpu.PrefetchScalarGridSpec(
        num_scalar_prefetch=2,
        grid=(n_rows // MOE_TILE,),
        in_specs=[pl.BlockSpec((MOE_TILE, half), lambda g, te, nt: (g, 0)),
                  pl.BlockSpec((1, d, ff2), lambda g, te, nt: (te[g], 0, 0)),
                  pl.BlockSpec((1, 1, ff2), lambda g, te, nt: (te[g], 0, 0)),
                  pl.BlockSpec((1, ff, d), lambda g, te, nt: (te[g], 0, 0)),
                  pl.BlockSpec((1, 1, d), lambda g, te, nt: (te[g], 0, 0))],
        out_specs=pl.BlockSpec((MOE_TILE, half), lambda g, te, nt: (g, 0)),
    )
    return pl.pallas_call(
        _expert_kernel,
        grid_spec=grid_spec,
        out_shape=jax.ShapeDtypeStruct((n_rows, half), jnp.uint32),
        compiler_params=pltpu.CompilerParams(
            dimension_semantics=("arbitrary",), vmem_limit_bytes=VMEM_LIMIT),
    )(tile_expert, n_tiles, xs, w_up_bf, b_up.reshape(n_exp, 1, ff2), w_down_bf,
      b_down.reshape(n_exp, 1, d))


def _combine_kernel(dest_ref, tw_ref, x1_ref, g2_ref, lng_ref, lnb_ref, sc_ref, sh_ref, ys_hbm,
                    x2_ref, un_ref, buf, sem):
    tm, d = x1_ref.shape
    half = d // 2

    def issue(r, carry):
        for k in range(TOP_K):
            _row_copy(ys_hbm, dest_ref[r * TOP_K + k], buf.at[k], r, sem).start()
        return carry

    def drain(r, carry):
        for k in range(TOP_K):
            _row_copy(ys_hbm, 0, buf.at[k], 0, sem).wait()
        return carry

    lax.fori_loop(0, tm, issue, 0)
    lax.fori_loop(0, tm, drain, 0)

    f_lo = f_hi = None
    for k in range(TOP_K):
        wk = tw_ref[:, k:k + 1]
        w = buf[k]
        t_lo, t_hi = wk * _unpack_lo(w), wk * _unpack_hi(w)
        f_lo = t_lo if f_lo is None else f_lo + t_lo
        f_hi = t_hi if f_hi is None else f_hi + t_hi
    z_lo = DEEPNORM_ALPHA * x1_ref[:, :half] + (1.0 + g2_ref[0, :, :half]) * f_lo
    z_hi = DEEPNORM_ALPHA * x1_ref[:, half:] + (1.0 + g2_ref[0, :, half:]) * f_hi
    x_lo, x_hi = _layer_norm_halves(z_lo, z_hi, lng_ref, lnb_ref, half)
    x2_ref[:, :half] = x_lo
    x2_ref[:, half:] = x_hi
    un_ref[:, :half] = (x_lo * (1.0 + sc_ref[0, :, :half]) + sh_ref[0, :, :half]).astype(un_ref.dtype)
    un_ref[:, half:] = (x_hi * (1.0 + sc_ref[0, :, half:]) + sh_ref[0, :, half:]).astype(un_ref.dtype)


def combine_ln(dest_flat, top_w, x1, g2, ln_g, ln_b, sc_next, sh_next, ys, seq):
    t, d = x1.shape
    per_b = seq // ROW_TILE
    row = lambda w: pl.BlockSpec((ROW_TILE, w), lambda i: (i, 0))
    mod = pl.BlockSpec((1, 1, d), lambda i: (i // per_b, 0, 0))
    vec = pl.BlockSpec((1, d), lambda i: (0, 0))
    return pl.pallas_call(
        _combine_kernel,
        grid=(t // ROW_TILE,),
        in_specs=[pl.BlockSpec((ROW_TILE * TOP_K,), lambda i: (i,), memory_space=pltpu.SMEM),
                  row(TOP_K), row(d), mod, vec, vec, mod, mod,
                  pl.BlockSpec(memory_space=pl.ANY)],
        out_specs=[row(d), row(d)],
        out_shape=[jax.ShapeDtypeStruct((t, d), jnp.float32),
                   jax.ShapeDtypeStruct((t, d), jnp.bfloat16)],
        scratch_shapes=[pltpu.VMEM((TOP_K, ROW_TILE, d // 2), jnp.uint32),
                        pltpu.SemaphoreType.DMA(())],
        compiler_params=pltpu.CompilerParams(
            dimension_semantics=("arbitrary",), vmem_limit_bytes=VMEM_LIMIT),
    )(dest_flat, top_w, x1, g2, ln_g.reshape(1, d), ln_b.reshape(1, d), sc_next, sh_next, ys)


def moe_layout(counts, top_idx, rank4, n_tiles_max):
    cnt = counts.reshape(-1).astype(jnp.int32)
    tiles = (cnt + MOE_TILE - 1) // MOE_TILE
    tile_end = jnp.cumsum(tiles)
    row_start = (tile_end - tiles) * MOE_TILE
    dest = (row_start[top_idx] + rank4).reshape(-1)
    tile_expert = jnp.minimum(
        jnp.sum(jnp.arange(n_tiles_max)[:, None] >= tile_end[None, :], axis=1),
        cnt.shape[0] - 1).astype(jnp.int32)
    return dest.astype(jnp.int32), tile_expert, tile_end[-1:].astype(jnp.int32)


NEG_BIG = -1e30


def _t5_bucket_np(d):
    max_exact = REL_BUCKETS // 2
    dd = np.maximum(d, 1).astype(np.float32)
    far = max_exact + (np.log(dd / np.float32(max_exact))
                       / np.float32(math.log(REL_MAX_DIST / max_exact))
                       * np.float32(REL_BUCKETS - max_exact)).astype(np.int32)
    return np.where(d < max_exact, d, np.minimum(far, REL_BUCKETS - 1)).astype(np.int32)


def _moba_bucket_tables():
    r = np.arange(MOBA_BLOCK)[:, None]
    c = np.arange(MOBA_BLOCK)[None, :]
    own = np.where(r >= c, _t5_bucket_np(np.maximum(r - c, 0)), -1)
    prev = _t5_bucket_np(MOBA_BLOCK + r - c)
    return np.stack([own, prev]).astype(np.int32)


def _bias_tab_kernel(rb_ref, bk_ref, o_ref):
    h = pl.program_id(0)
    bk = bk_ref[...]
    acc = jnp.zeros(bk.shape, jnp.float32)
    for m in range(REL_BUCKETS):
        acc = jnp.where(bk == m, rb_ref[h, m], acc)
    o_ref[0] = jnp.where(bk < 0, NEG_BIG, acc)


def moba_bias_tables(rel_bias_t):
    n_heads = rel_bias_t.shape[0]
    bk = jnp.asarray(_moba_bucket_tables())
    return pl.pallas_call(
        _bias_tab_kernel,
        grid=(n_heads,),
        in_specs=[pl.BlockSpec(memory_space=pltpu.SMEM),
                  pl.BlockSpec((2, MOBA_BLOCK, MOBA_BLOCK), lambda h: (0, 0, 0))],
        out_specs=pl.BlockSpec((1, 2, MOBA_BLOCK, MOBA_BLOCK), lambda h: (h, 0, 0, 0)),
        out_shape=jax.ShapeDtypeStruct((n_heads, 2, MOBA_BLOCK, MOBA_BLOCK), jnp.float32),
    )(rel_bias_t, bk)


def _moba_kernel(rb_ref, q_ref, k_ref, v_ref, bt_ref, o_ref, m_sc, l_sc, acc_sc, *, n_blk):
    h = pl.program_id(1)
    i = pl.program_id(2)
    blk = MOBA_BLOCK
    scale = MOBA_HD ** -0.5
    nt = (((1,), (1,)), ((), ()))
    q = q_ref[...]

    kmean = jnp.mean(k_ref[...].astype(jnp.float32).reshape(n_blk, blk, MOBA_HD), axis=1)
    km_hi = kmean.astype(jnp.bfloat16)
    km_lo = (kmean - km_hi.astype(jnp.float32)).astype(jnp.bfloat16)
    gate = (lax.dot_general(q, km_hi, nt, preferred_element_type=jnp.float32)
            + lax.dot_general(q, km_lo, nt, preferred_element_type=jnp.float32))
    g = [gate[:, n:n + 1] for n in range(n_blk)]

    def selected(j):
        cnt = jnp.zeros((blk, 1), jnp.float32)
        for n in range(n_blk - 1):
            if n == j:
                continue
            ahead = (g[n] >= g[j]) if n < j else (g[n] > g[j])
            cnt = cnt + jnp.where(ahead, jnp.where(n < i, 1.0, 0.0), 0.0)
        return cnt < float(MOBA_TOPK)

    row0 = pl.multiple_of(i * blk, blk)
    k_own = k_ref[pl.ds(row0, blk), :]
    v_own = v_ref[pl.ds(row0, blk), :]
    s = lax.dot_general(q, k_own, nt, preferred_element_type=jnp.float32) * scale + bt_ref[0, 0]
    m0 = jnp.max(s, axis=1, keepdims=True)
    p = jnp.exp(s - m0)
    m_sc[...] = m0
    l_sc[...] = jnp.sum(p, axis=1, keepdims=True)
    acc_sc[...] = jnp.dot(p.astype(jnp.bfloat16), v_own, preferred_element_type=jnp.float32)

    bfar = rb_ref[h, REL_BUCKETS - 1]
    for j in range(n_blk - 1):
        @pl.when(j < i)
        def _():
            k_j = k_ref[j * blk:(j + 1) * blk, :]
            v_j = v_ref[j * blk:(j + 1) * blk, :]
            bias = jnp.where(j == i - 1, bt_ref[0, 1], bfar)
            s = lax.dot_general(q, k_j, nt, preferred_element_type=jnp.float32) * scale + bias
            s = jnp.where(selected(j), s, NEG_BIG)
            m_old = m_sc[...]
            m_new = jnp.maximum(m_old, jnp.max(s, axis=1, keepdims=True))
            alpha = jnp.exp(m_old - m_new)
            p = jnp.exp(s - m_new)
            m_sc[...] = m_new
            l_sc[...] = alpha * l_sc[...] + jnp.sum(p, axis=1, keepdims=True)
            acc_sc[...] = alpha * acc_sc[...] + jnp.dot(
                p.astype(jnp.bfloat16), v_j, preferred_element_type=jnp.float32)

    o_ref[...] = (acc_sc[...] / l_sc[...]).astype(o_ref.dtype)


def moba_pallas(qkv, rel_bias_t, btab, n_batch, seq, q_blk0, k_blk0, v_blk0, out_dtype):
    n_heads = rel_bias_t.shape[0]
    n_blk = seq // MOBA_BLOCK
    assert seq % MOBA_BLOCK == 0
    kern = functools.partial(_moba_kernel, n_blk=n_blk)
    return pl.pallas_call(
        kern,
        grid=(n_batch, n_heads, n_blk),
        in_specs=[
            pl.BlockSpec(memory_space=pltpu.SMEM),
            pl.BlockSpec((MOBA_BLOCK, MOBA_HD), lambda b, h, i: (b * n_blk + i, q_blk0 + h)),
            pl.BlockSpec((seq, MOBA_HD), lambda b, h, i: (b, k_blk0 + h)),
            pl.BlockSpec((seq, MOBA_HD), lambda b, h, i: (b, v_blk0 + h)),
            pl.BlockSpec((1, 2, MOBA_BLOCK, MOBA_BLOCK), lambda b, h, i: (h, 0, 0, 0)),
        ],
        out_specs=pl.BlockSpec((MOBA_BLOCK, MOBA_HD), lambda b, h, i: (b * n_blk + i, h)),
        out_shape=jax.ShapeDtypeStruct((n_batch * seq, n_heads * MOBA_HD), out_dtype),
        scratch_shapes=[pltpu.VMEM((MOBA_BLOCK, 1), jnp.float32),
                        pltpu.VMEM((MOBA_BLOCK, 1), jnp.float32),
                        pltpu.VMEM((MOBA_BLOCK, MOBA_HD), jnp.float32)],
        compiler_params=pltpu.CompilerParams(
            dimension_semantics=("parallel", "parallel", "arbitrary")),
    )(rel_bias_t, qkv, qkv, qkv, btab)


def _log_sigmoid(x):
    return jnp.minimum(x, 0.0) - jnp.log(1.0 + jnp.exp(-jnp.abs(x)))


def _causal_shift(x, shift):
    if shift == 0:
        return x
    row = lax.broadcasted_iota(jnp.int32, x.shape, 0)
    return jnp.where(row >= shift, pltpu.roll(x, shift, 0), 0.0)


def _mlstm_kernel(q_ref, k_ref, v_ref, og_ref, gc_ref, gr_ref, cwq_ref, cwk_ref, cbq_ref, cbk_ref,
                  ng_ref, o_ref, qc_sc, kc_sc):
    seq, dqk = q_ref.shape
    dv = v_ref.shape[1]
    L = MLSTM_CHUNK
    nt = (((1,), (1,)), ((), ()))

    def conv_silu(x, w_ref, b_ref):
        y = b_ref[...]
        for j in range(MLSTM_CONV):
            y = y + w_ref[j:j + 1, :] * _causal_shift(x, MLSTM_CONV - 1 - j)
        return y * jax.nn.sigmoid(y)

    qc_sc[...] = conv_silu(q_ref[...], cwq_ref, cbq_ref).astype(qc_sc.dtype)
    kc_sc[...] = (conv_silu(k_ref[...], cwk_ref, cbk_ref) * dqk ** -0.5).astype(kc_sc.dtype)

    t_i = lax.broadcasted_iota(jnp.int32, (L, L), 0)
    s_i = lax.broadcasted_iota(jnp.int32, (L, L), 1)
    causal = s_i <= t_i

    def chunk(c, carry):
        c_st, n_st, m_st = carry
        r0 = pl.multiple_of(c * L, L)
        q = qc_sc[pl.ds(r0, L), :]
        k = kc_sc[pl.ds(r0, L), :]
        v = v_ref[pl.ds(r0, L), :].astype(jnp.bfloat16)
        i_col = gc_ref[0, 0, pl.ds(r0, L), :]
        lf_col = _log_sigmoid(gc_ref[1, 0, pl.ds(r0, L), :])
        i_row = gr_ref[0, 0, 0, pl.ds(c, 1), :]
        lf_row = _log_sigmoid(gr_ref[0, 1, 0, pl.ds(c, 1), :])
        bcum_col = jnp.sum(jnp.where(causal, lf_row, 0.0), axis=1, keepdims=True)
        bcum_row = jnp.sum(jnp.where(t_i <= s_i, lf_col, 0.0), axis=0, keepdims=True)
        b_last = bcum_row[:, L - 1:L]
        dmat = jnp.where(causal, bcum_col - bcum_row + i_row, NEG_BIG)
        a_max = jnp.max(b_last - bcum_row + i_row, axis=1, keepdims=True)
        w_end = jnp.exp(b_last - bcum_col + i_col - a_max)
        inter = bcum_col + m_st
        m_t = jnp.maximum(inter, jnp.max(dmat, axis=1, keepdims=True))
        s_qk = lax.dot_general(q, k, nt, preferred_element_type=jnp.float32) * jnp.exp(dmat - m_t)
        w_inter = jnp.exp(inter - m_t)
        num = (jnp.dot(s_qk.astype(jnp.bfloat16), v, preferred_element_type=jnp.float32)
               + w_inter * jnp.dot(q, c_st.astype(jnp.bfloat16),
                                   preferred_element_type=jnp.float32))
        den = (jnp.sum(s_qk, axis=1, keepdims=True)
               + w_inter * jnp.sum(q.astype(jnp.float32) * n_st, axis=1, keepdims=True))
        h = num / jnp.maximum(jnp.abs(den), jnp.exp(-m_t))
        hc = h - jnp.mean(h, axis=1, keepdims=True)
        hn = hc * lax.rsqrt(jnp.mean(hc * hc, axis=1, keepdims=True) + LN_EPS) * ng_ref[...]
        o_ref[pl.ds(r0, L), :] = (jax.nn.sigmoid(og_ref[pl.ds(r0, L), :]) * hn).astype(o_ref.dtype)
        m_new = jnp.maximum(b_last + m_st, a_max)
        f_sc = jnp.exp(b_last + m_st - m_new)
        i_sc = jnp.exp(a_max - m_new)
        kw = k.astype(jnp.float32) * w_end
        kv = jnp.dot(kw.T.astype(jnp.bfloat16), v, preferred_element_type=jnp.float32)
        c_new = f_sc * c_st + i_sc * kv
        n_new = f_sc * n_st + i_sc * jnp.sum(kw, axis=0, keepdims=True)
        return c_new, n_new, m_new

    init = (jnp.zeros((dqk, dv), jnp.float32), jnp.zeros((1, dqk), jnp.float32),
            jnp.zeros((1, 1), jnp.float32))
    lax.fori_loop(0, seq // L, chunk, init)


def mlstm_pallas(proj_a, gates, conv_w, conv_b, norm_g, n_batch, seq, n_heads, dqk, dv,
                 q_col0, k_col0, v_col0, o_col0):
    nc = seq // MLSTM_CHUNK
    g_col = gates.T.reshape(2, n_heads, n_batch * seq, 1)
    g_row = jnp.moveaxis(gates.reshape(n_batch, nc, MLSTM_CHUNK, 2, n_heads), (3, 4), (1, 2))
    return pl.pallas_call(
        _mlstm_kernel,
        grid=(n_batch, n_heads),
        in_specs=[
            pl.BlockSpec((seq, dqk), lambda b, h: (b, q_col0 // dqk + h)),
            pl.BlockSpec((seq, dqk), lambda b, h: (b, k_col0 // dqk + h)),
            pl.BlockSpec((seq, dv), lambda b, h: (b, v_col0 // dv + h)),
            pl.BlockSpec((seq, dv), lambda b, h: (b, o_col0 // dv + h)),
            pl.BlockSpec((2, 1, seq, 1), lambda b, h: (0, h, b, 0)),
            pl.BlockSpec((1, 2, 1, nc, MLSTM_CHUNK), lambda b, h: (b, 0, h, 0, 0)),
            pl.BlockSpec((MLSTM_CONV, dqk), lambda b, h: (0, h)),
            pl.BlockSpec((MLSTM_CONV, dqk), lambda b, h: (0, n_heads + h)),
            pl.BlockSpec((1, dqk), lambda b, h: (0, h)),
            pl.BlockSpec((1, dqk), lambda b, h: (0, n_heads + h)),
            pl.BlockSpec((1, dv), lambda b, h: (0, h)),
        ],
        out_specs=pl.BlockSpec((seq, dv), lambda b, h: (b, h)),
        out_shape=jax.ShapeDtypeStruct((n_batch * seq, n_heads * dv), jnp.bfloat16),
        scratch_shapes=[pltpu.VMEM((seq, dqk), jnp.bfloat16),
                        pltpu.VMEM((seq, dqk), jnp.bfloat16)],
        compiler_params=pltpu.CompilerParams(
            dimension_semantics=("parallel", "parallel"), vmem_limit_bytes=VMEM_LIMIT),
    )(proj_a, proj_a, proj_a, proj_a, g_col, g_row, conv_w, conv_w,
      conv_b.reshape(1, -1), conv_b.reshape(1, -1), norm_g.reshape(1, -1))


GLA_TILE = 256


def _gla_kernel(q_ref, k_ref, v_ref, go_ref, lr_ref, w2_ref, gb_ref, ng_ref, o_ref, bc_sc, oi_sc,
                *, lr_col0):
    seq, dk = q_ref.shape
    dv = v_ref.shape[1]
    R, C = GLA_TILE, GLA_CHUNK
    scale = dk ** -0.5
    nt = (((1,), (1,)), ((), ()))
    bf16 = jnp.bfloat16
    row = lax.broadcasted_iota(jnp.int32, (R, R), 0)
    col = lax.broadcasted_iota(jnp.int32, (R, R), 1)
    lag = jnp.where((row // C) == (col // C), jnp.where(col <= row, row - col, -1), -1)
    tril16 = jnp.where(lag >= 0, 1.0, 0.0).astype(bf16)
    ones = jnp.ones((dk, R), bf16)
    w2 = w2_ref[...].astype(bf16)

    def tile(i, carry):
        r0 = pl.multiple_of(i * R, R)
        z = jnp.dot(lr_ref[pl.ds(r0, R), lr_col0:lr_col0 + GLA_RANK].astype(bf16), w2,
                    preferred_element_type=jnp.float32) + gb_ref[...]
        la = _log_sigmoid(z) * (1.0 / GLA_TAU)
        la_hi = la.astype(bf16)
        la_lo = (la - la_hi.astype(jnp.float32)).astype(bf16)
        bc = (jnp.dot(tril16, la_hi, preferred_element_type=jnp.float32)
              + jnp.dot(tril16, la_lo, preferred_element_type=jnp.float32))
        bc_sc[pl.ds(r0, R), :] = bc
        q = q_ref[pl.ds(r0, R), :] * scale
        k = k_ref[pl.ds(r0, R), :]
        attn = jnp.zeros((R, R), jnp.float32)
        for d in range(C):
            if d == 0:
                prod = q * k
            else:
                prod = q * pltpu.roll(k, d, 0) * jnp.exp(bc - pltpu.roll(bc, d, 0))
            a_d = jnp.dot(prod.astype(bf16), ones, preferred_element_type=jnp.float32)
            attn = jnp.where(lag == d, a_d, attn)
        oi_sc[pl.ds(r0, R), :] = jnp.dot(attn.astype(bf16), v_ref[pl.ds(r0, R), :].astype(bf16),
                                         preferred_element_type=jnp.float32)
        return carry

    lax.fori_loop(0, seq // R, tile, 0)

    def chunk(c, st):
        r0 = pl.multiple_of(c * C, C)
        bc = bc_sc[pl.ds(r0, C), :]
        bl = bc[C - 1:C, :]
        qt = (q_ref[pl.ds(r0, C), :] * scale * jnp.exp(bc)).astype(bf16)
        kt = (k_ref[pl.ds(r0, C), :] * jnp.exp(bl - bc)).astype(bf16)
        oi_sc[pl.ds(r0, C), :] += lax.dot_general(qt, st.astype(bf16), nt,
                                                  preferred_element_type=jnp.float32)
        vt = v_ref[pl.ds(r0, C), :].T.astype(bf16)
        return jnp.exp(bl) * st + jnp.dot(vt, kt, preferred_element_type=jnp.float32)

    lax.fori_loop(0, seq // C, chunk, jnp.zeros((dv, dk), jnp.float32))

    def finish(i, carry):
        r0 = pl.multiple_of(i * R, R)
        o = oi_sc[pl.ds(r0, R), :]
        on = o * lax.rsqrt(jnp.mean(o * o, axis=1, keepdims=True) + LN_EPS) * ng_ref[...]
        g = go_ref[pl.ds(r0, R), :]
        o_ref[pl.ds(r0, R), :] = (on * (g * jax.nn.sigmoid(g))).astype(o_ref.dtype)
        return carry

    lax.fori_loop(0, seq // R, finish, 0)


def gla_pallas(proj_a, proj_c, lr_col0, gate_w2, gate_b, norm_g, n_batch, seq, n_heads, dk, dv,
               q_col0, k_col0, v_col0, go_col0):
    kern = functools.partial(_gla_kernel, lr_col0=lr_col0)
    return pl.pallas_call(
        kern,
        grid=(n_batch, n_heads),
        in_specs=[
            pl.BlockSpec((seq, dk), lambda b, h: (b, q_col0 // dk + h)),
            pl.BlockSpec((seq, dk), lambda b, h: (b, k_col0 // dk + h)),
            pl.BlockSpec((seq, dv), lambda b, h: (b, v_col0 // dv + h)),
            pl.BlockSpec((seq, dv), lambda b, h: (b, go_col0 // dv + h)),
            pl.BlockSpec((seq, proj_c.shape[1]), lambda b, h: (b, 0)),
            pl.BlockSpec((GLA_RANK, dk), lambda b, h: (0, h)),
            pl.BlockSpec((1, dk), lambda b, h: (0, h)),
            pl.BlockSpec((1, dv), lambda b, h: (0, h)),
        ],
        out_specs=pl.BlockSpec((seq, dv), lambda b, h: (b, h)),
        out_shape=jax.ShapeDtypeStruct((n_batch * seq, n_heads * dv), jnp.bfloat16),
        scratch_shapes=[pltpu.VMEM((seq, dk), jnp.float32), pltpu.VMEM((seq, dv), jnp.float32)],
        compiler_params=pltpu.CompilerParams(
            dimension_semantics=("parallel", "parallel"), vmem_limit_bytes=VMEM_LIMIT),
    )(proj_a, proj_a, proj_a, proj_a, proj_c, gate_w2, gate_b.reshape(1, -1),
      norm_g.reshape(1, -1))


CONV_TILE = 256
CONV_HALO = 32


def _conf_kernel(a_ref, g_ref, w_ref, b_ref, ng_ref, nb_ref, o_ref, x_sc):
    seq, ch = a_ref.shape
    R, H = CONV_TILE, CONV_HALO
    x_sc[0:H, :] = jnp.zeros((H, ch), jnp.float32)
    x_sc[H:, :] = a_ref[...] * jax.nn.sigmoid(g_ref[...])

    def tile(i, carry):
        r0 = pl.multiple_of(i * R, R)
        win = x_sc[pl.ds(r0, R + H), :]
        acc = jnp.zeros((R, ch), jnp.float32) + b_ref[...]
        for sub in range(8):
            sh = pltpu.roll(win, sub, 0) if sub else win
            for blk in range(H // 8):
                off = 8 * blk + sub
                if off >= CONV_WIDTH:
                    continue
                j = CONV_WIDTH - 1 - off
                acc = acc + w_ref[j:j + 1, :] * sh[H - 8 * blk:H - 8 * blk + R, :]
        yc = acc - jnp.mean(acc, axis=1, keepdims=True)
        yn = (yc * lax.rsqrt(jnp.mean(yc * yc, axis=1, keepdims=True) + LN_EPS) * ng_ref[...]
              + nb_ref[...])
        o_ref[pl.ds(r0, R), :] = (yn * jax.nn.sigmoid(yn)).astype(o_ref.dtype)
        return carry

    lax.fori_loop(0, seq // R, tile, 0)


def conformer_conv_pallas(proj_a, a_col0, g_col0, dw_w, dw_b, norm_g, norm_b, n_batch, seq,
                          n_groups, width):
    ch = width // n_groups
    vec = pl.BlockSpec((1, ch), lambda b, g: (0, g))
    return pl.pallas_call(
        _conf_kernel,
        grid=(n_batch, n_groups),
        in_specs=[pl.BlockSpec((seq, ch), lambda b, g: (b, a_col0 // ch + g)),
                  pl.BlockSpec((seq, ch), lambda b, g: (b, g_col0 // ch + g)),
                  pl.BlockSpec((CONV_WIDTH, ch), lambda b, g: (0, g)),
                  vec, vec, vec],
        out_specs=pl.BlockSpec((seq, ch), lambda b, g: (b, g)),
        out_shape=jax.ShapeDtypeStruct((n_batch * seq, width), jnp.bfloat16),
        scratch_shapes=[pltpu.VMEM((seq + CONV_HALO, ch), jnp.float32)],
        compiler_params=pltpu.CompilerParams(
            dimension_semantics=("parallel", "parallel"), vmem_limit_bytes=VMEM_LIMIT),
    )(proj_a, proj_a, dw_w, dw_b.reshape(1, -1), norm_g.reshape(1, -1), norm_b.reshape(1, -1))


def hybrid_mixer(ub, B, S, w_in, b_merge, mlstm_gate_b, mlstm_conv_w, mlstm_conv_b, mlstm_norm_g,
                 conf_dw_w, conf_dw_b, conf_norm_g, conf_norm_b, rel_bias_t, moba_btab,
                 gla_gate_w2, gla_gate_b, gla_norm_g, w_branch):
    W = BRANCH_WIDTH
    offs = [0] + np.cumsum(IN_SIZES).tolist()
    seg = lambda n: w_in[:, offs[n]:offs[n + 1]].astype(jnp.bfloat16)
    a_order = (0, 1, 2, 3, 5, 7, 8, 9, 11)
    w_a = jnp.concatenate([seg(n) for n in a_order], axis=1)
    w_b = jnp.concatenate([seg(6), seg(12)], axis=1)
    w_c = jnp.pad(jnp.concatenate([seg(4), seg(10)], axis=1),
                  ((0, 0), (0, 128 - 2 * MLSTM_HEADS - GLA_RANK)))
    proj_a = matmul(ub, w_a, 512, 1024, jnp.float32)
    proj_b = matmul(ub, w_b, 512, 1024, jnp.bfloat16)
    proj_c = matmul(ub, w_c, 512, 128, jnp.float32)
    a_off = dict(zip(a_order, [0] + np.cumsum([IN_SIZES[n] for n in a_order]).tolist()))

    gates = proj_c[:, :2 * MLSTM_HEADS] + mlstm_gate_b
    y_mlstm = mlstm_pallas(proj_a, gates, mlstm_conv_w, mlstm_conv_b, mlstm_norm_g, B, S,
                           MLSTM_HEADS, MLSTM_DQK, MLSTM_DV, a_off[0], a_off[1], a_off[2], a_off[3])
    y_conv = conformer_conv_pallas(proj_a, a_off[5], a_off[5] + W, conf_dw_w, conf_dw_b,
                                   conf_norm_g, conf_norm_b, B, S, CONV_GROUPS, W)
    y_moba = moba_pallas(proj_b, rel_bias_t, moba_btab, B, S, 0, MOBA_HEADS, 2 * MOBA_HEADS,
                         jnp.bfloat16)
    y_gla = gla_pallas(proj_a, proj_c, 2 * MLSTM_HEADS, gla_gate_w2, gla_gate_b, gla_norm_g, B, S,
                       GLA_HEADS, GLA_DK, GLA_DV, a_off[7], a_off[8], a_off[9], a_off[11])
    return merge_branches([y_mlstm, y_conv, y_moba, y_gla],
                          w_branch.astype(jnp.bfloat16), proj_b, 3 * W, b_merge)


def kernel(x, c, w_ada, b_ada, w_in, b_merge, mlstm_gate_b, mlstm_conv_w, mlstm_conv_b,
           mlstm_norm_g, conf_dw_w, conf_dw_b, conf_norm_g, conf_norm_b, rel_bias,
           gla_gate_w2, gla_gate_b, gla_norm_g, w_branch, w_out, ln1_g, ln1_b,
           router_w, router_b, exp_w_up, exp_b_up, exp_w_down, exp_b_down, ln2_g, ln2_b):
    B, S, D = x.shape
    bf16 = jnp.bfloat16
    mod_all = ada_modulation(c, w_ada, b_ada)
    mods = [jnp.split(mod_all[l][:, None, :], 6, axis=-1) for l in range(DEPTH)]
    rel_bias_t = rel_bias.T.astype(jnp.float32)
    moba_btab = moba_bias_tables(rel_bias_t)
    n_rows = B * S * TOP_K + N_EXPERTS * MOE_TILE
    x2d = x.reshape(B * S, D)
    ub = modulate(x2d, mods[0][1], mods[0][0], S)
    for l in range(DEPTH):
        sh1, sc1, g1, sh2, sc2, g2 = mods[l]
        merged = hybrid_mixer(ub, B, S, w_in[l], b_merge[l], mlstm_gate_b[l], mlstm_conv_w[l],
                              mlstm_conv_b[l], mlstm_norm_g[l], conf_dw_w[l], conf_dw_b[l],
                              conf_norm_g[l], conf_norm_b[l], rel_bias_t, moba_btab,
                              gla_gate_w2[l], gla_gate_b[l], gla_norm_g[l], w_branch[l])
        mix = matmul(merged, w_out[l].astype(bf16), 512, 1024, jnp.float32)
        x1, upk, top_idx, top_w, rank4, counts = ln_router(
            x2d, mix, g1, ln1_g[l], ln1_b[l], sc2, sh2, router_w[l].astype(bf16), router_b[l], S)
        dest, tile_expert, n_tiles = moe_layout(counts, top_idx, rank4, n_rows // MOE_TILE)
        xs = moe_dispatch(upk, dest, n_rows)
        ys = grouped_experts(xs, tile_expert, n_tiles, exp_w_up[l].astype(bf16), exp_b_up[l],
                             exp_w_down[l].astype(bf16), exp_b_down[l])
        nxt = min(l + 1, DEPTH - 1)
        x2d, ub = combine_ln(dest, top_w, x1, g2, ln2_g[l], ln2_b[l], mods[nxt][1], mods[nxt][0],
                             ys, S)
    return x2d.reshape(B, S, D)
```

```python
import functools
import math

import jax
import jax.numpy as jnp
import numpy as np
from jax import lax
from jax.experimental import pallas as pl
from jax.experimental.pallas import tpu as pltpu

D_MODEL = 4096
BATCH = 4
SEQ = 2048
DEPTH = 2

N_BRANCHES = 4
BRANCH_WIDTH = D_MODEL // N_BRANCHES
MLSTM_HEADS = 4
MLSTM_DV = BRANCH_WIDTH // MLSTM_HEADS
MLSTM_DQK = MLSTM_DV // 2
MLSTM_CONV = 4
MLSTM_CHUNK = 64
CONV_WIDTH = 31
CONV_GROUPS = 4
MOBA_HEADS = 8
MOBA_HD = BRANCH_WIDTH // MOBA_HEADS
MOBA_BLOCK = 256
MOBA_TOPK = 3
MOBA_Q_CHUNK = 32
REL_BUCKETS = 32
REL_MAX_DIST = 128
GLA_HEADS = 4
GLA_DV = BRANCH_WIDTH // GLA_HEADS
GLA_DK = GLA_DV // 2
GLA_RANK = 16
GLA_TAU = 16.0
GLA_CHUNK = 16
N_EXPERTS = 32
TOP_K = 4
EXPERT_FF = D_MODEL // 8
SWIGLU_ALPHA = 1.702
SWIGLU_LIMIT = 7.0
DEEPNORM_ALPHA = (2 * DEPTH) ** 0.25
LN_EPS = 1e-5

IN_SIZES = (
    MLSTM_HEADS * MLSTM_DQK, MLSTM_HEADS * MLSTM_DQK, BRANCH_WIDTH, BRANCH_WIDTH,
    2 * MLSTM_HEADS, 2 * BRANCH_WIDTH, 3 * BRANCH_WIDTH, GLA_HEADS * GLA_DK,
    GLA_HEADS * GLA_DK, BRANCH_WIDTH, GLA_RANK, BRANCH_WIDTH, N_BRANCHES * D_MODEL,
)
IN_WIDTH = sum(IN_SIZES)

VMEM_LIMIT = 56 * 1024 * 1024


def _mm_kernel(a_ref, b_ref, o_ref):
    o_ref[...] = jnp.dot(a_ref[...], b_ref[...],
                         preferred_element_type=jnp.float32).astype(o_ref.dtype)


def matmul(a, b, tm, tn, out_dtype, col0=0, ncols=None):
    M, K = a.shape
    N = b.shape[1] - col0 if ncols is None else ncols
    assert col0 % tn == 0 and N % tn == 0 and M % tm == 0
    jb = col0 // tn
    return pl.pallas_call(
        _mm_kernel,
        grid=(N // tn, M // tm),
        in_specs=[pl.BlockSpec((tm, K), lambda j, i: (i, 0)),
                  pl.BlockSpec((K, tn), lambda j, i: (0, jb + j))],
        out_specs=pl.BlockSpec((tm, tn), lambda j, i: (i, j)),
        out_shape=jax.ShapeDtypeStruct((M, N), out_dtype),
        compiler_params=pltpu.CompilerParams(
            dimension_semantics=("parallel", "parallel"), vmem_limit_bytes=VMEM_LIMIT),
    )(a, b)


ROW_TILE = 256
MOE_TILE = 256


def _ada_kernel(c_ref, w_ref, b_ref, o_ref):
    c = c_ref[...]
    cs = (c * jax.nn.sigmoid(c)).astype(jnp.bfloat16)
    o_ref[0] = jnp.dot(cs, w_ref[0].astype(jnp.bfloat16),
                       preferred_element_type=jnp.float32) + b_ref[0]


def ada_modulation(c, w_ada, b_ada, tn=512):
    n_layers, d, n = w_ada.shape
    nb = c.shape[0]
    c8 = jnp.pad(c, ((0, 8 - nb), (0, 0)))
    out = pl.pallas_call(
        _ada_kernel,
        grid=(n_layers, n // tn),
        in_specs=[pl.BlockSpec((8, d), lambda l, j: (0, 0)),
                  pl.BlockSpec((1, d, tn), lambda l, j: (l, 0, j)),
                  pl.BlockSpec((1, 1, tn), lambda l, j: (l, 0, j))],
        out_specs=pl.BlockSpec((1, 8, tn), lambda l, j: (l, 0, j)),
        out_shape=jax.ShapeDtypeStruct((n_layers, 8, n), jnp.float32),
        compiler_params=pltpu.CompilerParams(
            dimension_semantics=("parallel", "parallel"), vmem_limit_bytes=VMEM_LIMIT),
    )(c8, w_ada, b_ada.reshape(n_layers, 1, n))
    return out[:, :nb]


def _modulate_kernel(x_ref, sc_ref, sh_ref, o_ref):
    o_ref[...] = (x_ref[...] * (1.0 + sc_ref[0]) + sh_ref[0]).astype(o_ref.dtype)


def modulate(x2d, sc, sh, seq):
    t, d = x2d.shape
    per_b = seq // ROW_TILE
    mod_spec = pl.BlockSpec((1, 1, d), lambda i: (i // per_b, 0, 0))
    return pl.pallas_call(
        _modulate_kernel,
        grid=(t // ROW_TILE,),
        in_specs=[pl.BlockSpec((ROW_TILE, d), lambda i: (i, 0)), mod_spec, mod_spec],
        out_specs=pl.BlockSpec((ROW_TILE, d), lambda i: (i, 0)),
        out_shape=jax.ShapeDtypeStruct((t, d), jnp.bfloat16),
        compiler_params=pltpu.CompilerParams(dimension_semantics=("parallel",)),
    )(x2d, sc, sh)


def _merge_kernel(y0, y1, y2, y3, wb_ref, g0, g1, g2, g3, bm_ref, o_ref):
    acc = None
    for n, (y, g) in enumerate(((y0, g0), (y1, g1), (y2, g2), (y3, g3))):
        gate = jax.nn.sigmoid(g[...].astype(jnp.float32) + bm_ref[n:n + 1, :])
        term = gate * jnp.dot(y[...], wb_ref[n], preferred_element_type=jnp.float32)
        acc = term if acc is None else acc + term
    o_ref[...] = acc.astype(o_ref.dtype)


def merge_branches(ys, w_branch_bf, proj_b, gate_col0, b_merge, tm=512, tn=1024):
    t, w = ys[0].shape
    d = w_branch_bf.shape[2]
    assert gate_col0 % tn == 0 and d % tn == 0
    y_spec = pl.BlockSpec((tm, w), lambda j, i: (i, 0))
    g_specs = [pl.BlockSpec((tm, tn), lambda j, i, n=n: (i, (gate_col0 + n * d) // tn + j))
               for n in range(N_BRANCHES)]
    return pl.pallas_call(
        _merge_kernel,
        grid=(d // tn, t // tm),
        in_specs=[y_spec] * 4 + [pl.BlockSpec((N_BRANCHES, w, tn), lambda j, i: (0, 0, j))]
        + g_specs + [pl.BlockSpec((N_BRANCHES, tn), lambda j, i: (0, j))],
        out_specs=pl.BlockSpec((tm, tn), lambda j, i: (i, j)),
        out_shape=jax.ShapeDtypeStruct((t, d), jnp.bfloat16),
        compiler_params=pltpu.CompilerParams(
            dimension_semantics=("parallel", "parallel"), vmem_limit_bytes=VMEM_LIMIT),
    )(*ys, w_branch_bf, proj_b, proj_b, proj_b, proj_b, b_merge.reshape(N_BRANCHES, d))


def _pack_pair(lo, hi):
    def rne(x):
        b = lax.bitcast_convert_type(x, jnp.uint32)
        return (b + jnp.uint32(0x7FFF) + ((b >> 16) & jnp.uint32(1))) >> 16
    return rne(lo) | (rne(hi) << 16)


def _unpack_lo(w):
    return lax.bitcast_convert_type(w << 16, jnp.float32)


def _unpack_hi(w):
    return lax.bitcast_convert_type(w & jnp.uint32(0xFFFF0000), jnp.float32)


def _layer_norm_halves(z_lo, z_hi, g_ref, b_ref, half):
    d = 2 * half
    mu = (jnp.sum(z_lo, axis=1, keepdims=True) + jnp.sum(z_hi, axis=1, keepdims=True)) / d
    c_lo, c_hi = z_lo - mu, z_hi - mu
    var = (jnp.sum(c_lo * c_lo, axis=1, keepdims=True)
           + jnp.sum(c_hi * c_hi, axis=1, keepdims=True)) / d
    r = lax.rsqrt(var + LN_EPS)
    return (c_lo * r * g_ref[:, :half] + b_ref[:, :half],
            c_hi * r * g_ref[:, half:] + b_ref[:, half:])


def _ln_router_kernel(x_ref, mix_ref, g1_ref, lng_ref, lnb_ref, sc_ref, sh_ref, rw_ref, rb_ref,
                      x1_ref, upk_ref, tidx_ref, tw_ref, rank_ref, cnt_ref, carry_sc):
    tm, d = x_ref.shape
    half = d // 2
    n_exp = rw_ref.shape[1]

    @pl.when(pl.program_id(0) == 0)
    def _():
        carry_sc[...] = jnp.zeros_like(carry_sc)

    z = DEEPNORM_ALPHA * x_ref[...] + (1.0 + g1_ref[0]) * mix_ref[...]
    x_lo, x_hi = _layer_norm_halves(z[:, :half], z[:, half:], lng_ref, lnb_ref, half)
    x1_ref[:, :half] = x_lo
    x1_ref[:, half:] = x_hi
    u_lo = x_lo * (1.0 + sc_ref[0, :, :half]) + sh_ref[0, :, :half]
    u_hi = x_hi * (1.0 + sc_ref[0, :, half:]) + sh_ref[0, :, half:]
    upk_ref[...] = _pack_pair(u_lo, u_hi)

    logits = (jnp.dot(u_lo.astype(jnp.bfloat16), rw_ref[:half, :],
                      preferred_element_type=jnp.float32)
              + jnp.dot(u_hi.astype(jnp.bfloat16), rw_ref[half:, :],
                        preferred_element_type=jnp.float32) + rb_ref[...])
    lane = lax.broadcasted_iota(jnp.int32, (tm, n_exp), 1).astype(jnp.float32)
    lg = logits
    vals, hots = [], []
    for k in range(TOP_K):
        m = jnp.max(lg, axis=1, keepdims=True)
        ik = jnp.min(jnp.where(lg == m, lane, float(n_exp)), axis=1, keepdims=True)
        hot = lane == ik
        vals.append(m)
        hots.append(hot)
        tidx_ref[:, k:k + 1] = ik.astype(jnp.int32)
        lg = jnp.where(hot, -jnp.inf, lg)
    exps = [jnp.exp(v - vals[0]) for v in vals]
    den = exps[0] + exps[1] + exps[2] + exps[3]
    for k in range(TOP_K):
        tw_ref[:, k:k + 1] = exps[k] / den

    mask = jnp.where(hots[0] | hots[1] | hots[2] | hots[3], 1.0, 0.0)
    row = lax.broadcasted_iota(jnp.int32, (tm, tm), 0)
    col = lax.broadcasted_iota(jnp.int32, (tm, tm), 1)
    lower = jnp.where(row > col, 1.0, 0.0).astype(jnp.bfloat16)
    rank = jnp.dot(lower, mask.astype(jnp.bfloat16),
                   preferred_element_type=jnp.float32) + carry_sc[...]
    for k in range(TOP_K):
        rank_ref[:, k:k + 1] = jnp.sum(jnp.where(hots[k], rank, 0.0), axis=1,
                                       keepdims=True).astype(jnp.int32)
    carry_sc[...] = carry_sc[...] + jnp.sum(mask, axis=0, keepdims=True)
    cnt_ref[...] = carry_sc[...]


def ln_router(x2d, mix, g1, ln_g, ln_b, sc2, sh2, router_w_bf, router_b, seq):
    t, d = x2d.shape
    n_exp = router_w_bf.shape[1]
    per_b = seq // ROW_TILE
    row = lambda w: pl.BlockSpec((ROW_TILE, w), lambda i: (i, 0))
    mod = pl.BlockSpec((1, 1, d), lambda i: (i // per_b, 0, 0))
    vec = lambda w: pl.BlockSpec((1, w), lambda i: (0, 0))
    return pl.pallas_call(
        _ln_router_kernel,
        grid=(t // ROW_TILE,),
        in_specs=[row(d), row(d), mod, vec(d), vec(d), mod, mod,
                  pl.BlockSpec((d, n_exp), lambda i: (0, 0)), vec(n_exp)],
        out_specs=[row(d), row(d // 2), row(TOP_K), row(TOP_K), row(TOP_K), vec(n_exp)],
        out_shape=[jax.ShapeDtypeStruct((t, d), jnp.float32),
                   jax.ShapeDtypeStruct((t, d // 2), jnp.uint32),
                   jax.ShapeDtypeStruct((t, TOP_K), jnp.int32),
                   jax.ShapeDtypeStruct((t, TOP_K), jnp.float32),
                   jax.ShapeDtypeStruct((t, TOP_K), jnp.int32),
                   jax.ShapeDtypeStruct((1, n_exp), jnp.float32)],
        scratch_shapes=[pltpu.VMEM((1, n_exp), jnp.float32)],
        compiler_params=pltpu.CompilerParams(
            dimension_semantics=("arbitrary",), vmem_limit_bytes=VMEM_LIMIT),
    )(x2d, mix, g1, ln_g.reshape(1, d), ln_b.reshape(1, d), sc2, sh2, router_w_bf,
      router_b.reshape(1, n_exp))


def _row_copy(src_hbm, src_row, dst, dst_row, sem):
    return pltpu.make_async_copy(src_hbm.at[pl.ds(src_row, 1)], dst.at[pl.ds(dst_row, 1)], sem)


def _dispatch_kernel(dest_ref, u_ref, xs_in_hbm, xs_hbm, sem):
    del xs_in_hbm

    def issue(r, carry):
        for k in range(TOP_K):
            _row_copy(u_ref, r, xs_hbm, dest_ref[r * TOP_K + k], sem).start(priority=k % 2)
        return carry

    def drain(r, carry):
        for k in range(TOP_K):
            _row_copy(u_ref, 0, xs_hbm, 0, sem).wait()
        return carry

    lax.fori_loop(0, ROW_TILE, issue, 0)
    lax.fori_loop(0, ROW_TILE, drain, 0)


def moe_dispatch(upk, dest_flat, n_rows):
    t, w = upk.shape
    xs0 = jnp.zeros((n_rows, w), upk.dtype)
    return pl.pallas_call(
        _dispatch_kernel,
        grid=(t // ROW_TILE,),
        in_specs=[pl.BlockSpec((ROW_TILE * TOP_K,), lambda i: (i,), memory_space=pltpu.SMEM),
                  pl.BlockSpec((ROW_TILE, w), lambda i: (i, 0)),
                  pl.BlockSpec(memory_space=pl.ANY)],
        out_specs=pl.BlockSpec(memory_space=pl.ANY),
        out_shape=jax.ShapeDtypeStruct((n_rows, w), upk.dtype),
        scratch_shapes=[pltpu.SemaphoreType.DMA(())],
        input_output_aliases={2: 0},
        compiler_params=pltpu.CompilerParams(dimension_semantics=("arbitrary",)),
    )(dest_flat, upk, xs0)


def _expert_kernel(te_ref, nt_ref, xs_ref, wu_ref, bu_ref, wd_ref, bd_ref, ys_ref):
    g = pl.program_id(0)
    half = xs_ref.shape[1]
    ff = wd_ref.shape[1]

    @pl.when(g < nt_ref[0])
    def _():
        w = xs_ref[...]
        x_lo = _unpack_lo(w).astype(jnp.bfloat16)
        x_hi = _unpack_hi(w).astype(jnp.bfloat16)
        hid = (jnp.dot(x_lo, wu_ref[0, :half, :], preferred_element_type=jnp.float32)
               + jnp.dot(x_hi, wu_ref[0, half:, :], preferred_element_type=jnp.float32)
               + bu_ref[0])
        glu = jnp.minimum(hid[:, :ff], SWIGLU_LIMIT)
        lin = jnp.clip(hid[:, ff:], -SWIGLU_LIMIT, SWIGLU_LIMIT)
        act = glu * jax.nn.sigmoid(SWIGLU_ALPHA * glu) * (lin + 1.0)
        y = jnp.dot(act.astype(jnp.bfloat16), wd_ref[0],
                    preferred_element_type=jnp.float32) + bd_ref[0]
        ys_ref[...] = _pack_pair(y[:, :half], y[:, half:])

    @pl.when(g >= nt_ref[0])
    def _():
        ys_ref[...] = jnp.zeros_like(ys_ref)


def grouped_experts(xs, tile_expert, n_tiles, w_up_bf, b_up, w_down_bf, b_down):
    n_rows, half = xs.shape
    n_exp, d, ff2 = w_up_bf.shape
    ff = ff2 // 2
    grid_spec = pltpu.PrefetchScalarGridSpec(
        num_scalar_prefetch=2,
        grid=(n_rows // MOE_TILE,),
        in_specs=[pl.BlockSpec((MOE_TILE, half), lambda g, te, nt: (g, 0)),
                  pl.BlockSpec((1, d, ff2), lambda g, te, nt: (te[g], 0, 0)),
                  pl.BlockSpec((1, 1, ff2), lambda g, te, nt: (te[g], 0, 0)),
                  pl.BlockSpec((1, ff, d), lambda g, te, nt: (te[g], 0, 0)),
                  pl.BlockSpec((1, 1, d), lambda g, te, nt: (te[g], 0, 0))],
        out_specs=pl.BlockSpec((MOE_TILE, half), lambda g, te, nt: (g, 0)),
    )
    return pl.pallas_call(
        _expert_kernel,
        grid_spec=grid_spec,
        out_shape=jax.ShapeDtypeStruct((n_rows, half), jnp.uint32),
        compiler_params=pltpu.CompilerParams(
            dimension_semantics=("arbitrary",), vmem_limit_bytes=VMEM_LIMIT),
    )(tile_expert, n_tiles, xs, w_up_bf, b_up.reshape(n_exp, 1, ff2), w_down_bf,
      b_down.reshape(n_exp, 1, d))


def _combine_kernel(dest_ref, dest_next_ref, tw_ref, x1_ref, g2_ref, lng_ref, lnb_ref, sc_ref,
                    sh_ref, ys_hbm, x2_ref, un_ref, buf, sem):
    tm, d = x1_ref.shape
    half = d // 2
    i = pl.program_id(0)
    slot = lax.rem(i, 2)

    def gather(d_ref, s):
        def issue(r, carry):
            for k in range(TOP_K):
                _row_copy(ys_hbm, d_ref[r * TOP_K + k], buf.at[s, k], r,
                          sem.at[s]).start(priority=k % 2)
            return carry
        lax.fori_loop(0, tm, issue, 0)

    @pl.when(i == 0)
    def _():
        gather(dest_ref, 0)

    @pl.when(i + 1 < pl.num_programs(0))
    def _():
        gather(dest_next_ref, 1 - slot)

    def drain(r, carry):
        for k in range(TOP_K):
            _row_copy(ys_hbm, 0, buf.at[slot, k], 0, sem.at[slot]).wait()
        return carry

    lax.fori_loop(0, tm, drain, 0)

    f_lo = f_hi = None
    for k in range(TOP_K):
        wk = tw_ref[:, k:k + 1]
        w = buf[slot, k]
        t_lo, t_hi = wk * _unpack_lo(w), wk * _unpack_hi(w)
        f_lo = t_lo if f_lo is None else f_lo + t_lo
        f_hi = t_hi if f_hi is None else f_hi + t_hi
    z_lo = DEEPNORM_ALPHA * x1_ref[:, :half] + (1.0 + g2_ref[0, :, :half]) * f_lo
    z_hi = DEEPNORM_ALPHA * x1_ref[:, half:] + (1.0 + g2_ref[0, :, half:]) * f_hi
    x_lo, x_hi = _layer_norm_halves(z_lo, z_hi, lng_ref, lnb_ref, half)
    x2_ref[:, :half] = x_lo
    x2_ref[:, half:] = x_hi
    un_ref[:, :half] = (x_lo * (1.0 + sc_ref[0, :, :half]) + sh_ref[0, :, :half]).astype(un_ref.dtype)
    un_ref[:, half:] = (x_hi * (1.0 + sc_ref[0, :, half:]) + sh_ref[0, :, half:]).astype(un_ref.dtype)


def combine_ln(dest_flat, top_w, x1, g2, ln_g, ln_b, sc_next, sh_next, ys, seq):
    t, d = x1.shape
    per_b = seq // ROW_TILE
    n_steps = t // ROW_TILE
    row = lambda w: pl.BlockSpec((ROW_TILE, w), lambda i: (i, 0))
    mod = pl.BlockSpec((1, 1, d), lambda i: (i // per_b, 0, 0))
    vec = pl.BlockSpec((1, d), lambda i: (0, 0))
    return pl.pallas_call(
        _combine_kernel,
        grid=(n_steps,),
        in_specs=[pl.BlockSpec((ROW_TILE * TOP_K,), lambda i: (i,), memory_space=pltpu.SMEM),
                  pl.BlockSpec((ROW_TILE * TOP_K,), lambda i: (jnp.minimum(i + 1, n_steps - 1),),
                               memory_space=pltpu.SMEM),
                  row(TOP_K), row(d), mod, vec, vec, mod, mod,
                  pl.BlockSpec(memory_space=pl.ANY)],
        out_specs=[row(d), row(d)],
        out_shape=[jax.ShapeDtypeStruct((t, d), jnp.float32),
                   jax.ShapeDtypeStruct((t, d), jnp.bfloat16)],
        scratch_shapes=[pltpu.VMEM((2, TOP_K, ROW_TILE, d // 2), jnp.uint32),
                        pltpu.SemaphoreType.DMA((2,))],
        compiler_params=pltpu.CompilerParams(
            dimension_semantics=("arbitrary",), vmem_limit_bytes=VMEM_LIMIT),
    )(dest_flat, dest_flat, top_w, x1, g2, ln_g.reshape(1, d), ln_b.reshape(1, d), sc_next,
      sh_next, ys)


def moe_layout(counts, top_idx, rank4, n_tiles_max):
    cnt = counts.reshape(-1).astype(jnp.int32)
    tiles = (cnt + MOE_TILE - 1) // MOE_TILE
    tile_end = jnp.cumsum(tiles)
    row_start = (tile_end - tiles) * MOE_TILE
    dest = (row_start[top_idx] + rank4).reshape(-1)
    tile_expert = jnp.minimum(
        jnp.sum(jnp.arange(n_tiles_max)[:, None] >= tile_end[None, :], axis=1),
        cnt.shape[0] - 1).astype(jnp.int32)
    return dest.astype(jnp.int32), tile_expert, tile_end[-1:].astype(jnp.int32)


NEG_BIG = -1e30


def _t5_bucket_np(d):
    max_exact = REL_BUCKETS // 2
    dd = np.maximum(d, 1).astype(np.float32)
    far = max_exact + (np.log(dd / np.float32(max_exact))
                       / np.float32(math.log(REL_MAX_DIST / max_exact))
                       * np.float32(REL_BUCKETS - max_exact)).astype(np.int32)
    return np.where(d < max_exact, d, np.minimum(far, REL_BUCKETS - 1)).astype(np.int32)


def _moba_bucket_tables():
    c = np.arange(MOBA_BLOCK)[:, None]
    r = np.arange(MOBA_BLOCK)[None, :]
    own = np.where(r >= c, _t5_bucket_np(np.maximum(r - c, 0)), -1)
    prev = _t5_bucket_np(MOBA_BLOCK + r - c)
    return np.stack([own, prev]).astype(np.int32)


def _bias_tab_kernel(rb_ref, bk_ref, o_ref):
    h = pl.program_id(0)
    bk = bk_ref[...]
    acc = jnp.zeros(bk.shape, jnp.float32)
    for m in range(REL_BUCKETS):
        acc = jnp.where(bk == m, rb_ref[h, m], acc)
    o_ref[0] = jnp.where(bk < 0, NEG_BIG, acc)


def moba_bias_tables(rel_bias_t):
    n_heads = rel_bias_t.shape[0]
    bk = jnp.asarray(_moba_bucket_tables())
    return pl.pallas_call(
        _bias_tab_kernel,
        grid=(n_heads,),
        in_specs=[pl.BlockSpec(memory_space=pltpu.SMEM),
                  pl.BlockSpec((2, MOBA_BLOCK, MOBA_BLOCK), lambda h: (0, 0, 0))],
        out_specs=pl.BlockSpec((1, 2, MOBA_BLOCK, MOBA_BLOCK), lambda h: (h, 0, 0, 0)),
        out_shape=jax.ShapeDtypeStruct((n_heads, 2, MOBA_BLOCK, MOBA_BLOCK), jnp.float32),
    )(rel_bias_t, bk)


def _moba_kernel(rb_ref, q_ref, k_ref, v_ref, bt_ref, o_ref, km_sc, vt_sc, m_sc, l_sc, acc_sc,
                 *, n_blk):
    h = pl.program_id(1)
    i = pl.program_id(2)
    blk = MOBA_BLOCK
    seq = n_blk * blk
    scale = MOBA_HD ** -0.5
    nt = (((1,), (1,)), ((), ()))
    bf16 = jnp.bfloat16

    @pl.when(i == 0)
    def _():
        blk_of_col = lax.broadcasted_iota(jnp.int32, (n_blk, seq), 1) // blk
        blk_row = lax.broadcasted_iota(jnp.int32, (n_blk, seq), 0)
        avg = jnp.where(blk_of_col == blk_row, 1.0 / blk, 0.0).astype(bf16)
        km_sc[...] = jnp.dot(avg, k_ref[...], preferred_element_type=jnp.float32)
        for j in range(n_blk):
            vt_sc[j] = v_ref[j * blk:(j + 1) * blk, :].astype(jnp.float32).T.astype(bf16)

    q = q_ref[...]
    kmean = km_sc[...]
    km_hi = kmean.astype(bf16)
    km_lo = (kmean - km_hi.astype(jnp.float32)).astype(bf16)
    gate = (lax.dot_general(km_hi, q, nt, preferred_element_type=jnp.float32)
            + lax.dot_general(km_lo, q, nt, preferred_element_type=jnp.float32))
    g = [gate[n:n + 1, :] for n in range(n_blk)]

    def selected(j):
        cnt = jnp.zeros((1, blk), jnp.float32)
        for n in range(n_blk - 1):
            if n == j:
                continue
            ahead = (g[n] >= g[j]) if n < j else (g[n] > g[j])
            cnt = cnt + jnp.where(ahead, jnp.where(n < i, 1.0, 0.0), 0.0)
        return cnt < float(MOBA_TOPK)

    row0 = pl.multiple_of(i * blk, blk)
    k_own = k_ref[pl.ds(row0, blk), :]
    s = lax.dot_general(k_own, q, nt, preferred_element_type=jnp.float32) * scale + bt_ref[0, 0]
    m0 = jnp.max(s, axis=0, keepdims=True)
    p = jnp.exp(s - m0)
    m_sc[...] = m0
    l_sc[...] = jnp.sum(p, axis=0, keepdims=True)
    acc_sc[...] = jnp.dot(vt_sc[i], p.astype(bf16), preferred_element_type=jnp.float32)

    bfar = rb_ref[h, REL_BUCKETS - 1]
    for j in range(n_blk - 1):
        @pl.when(j < i)
        def _():
            k_j = k_ref[j * blk:(j + 1) * blk, :]
            bias = jnp.where(j == i - 1, bt_ref[0, 1], bfar)
            s = lax.dot_general(k_j, q, nt, preferred_element_type=jnp.float32) * scale + bias
            s = jnp.where(selected(j), s, NEG_BIG)
            m_old = m_sc[...]
            m_new = jnp.maximum(m_old, jnp.max(s, axis=0, keepdims=True))
            alpha = jnp.exp(m_old - m_new)
            p = jnp.exp(s - m_new)
            m_sc[...] = m_new
            l_sc[...] = alpha * l_sc[...] + jnp.sum(p, axis=0, keepdims=True)
            acc_sc[...] = alpha * acc_sc[...] + jnp.dot(
                vt_sc[j], p.astype(bf16), preferred_element_type=jnp.float32)

    o_ref[...] = (acc_sc[...] / l_sc[...]).T.astype(o_ref.dtype)


def moba_pallas(qkv, rel_bias_t, btab, n_batch, seq, q_blk0, k_blk0, v_blk0, out_dtype):
    n_heads = rel_bias_t.shape[0]
    n_blk = seq // MOBA_BLOCK
    assert seq % MOBA_BLOCK == 0
    kern = functools.partial(_moba_kernel, n_blk=n_blk)
    return pl.pallas_call(
        kern,
        grid=(n_batch, n_heads, n_blk),
        in_specs=[
            pl.BlockSpec(memory_space=pltpu.SMEM),
            pl.BlockSpec((MOBA_BLOCK, MOBA_HD), lambda b, h, i: (b * n_blk + i, q_blk0 + h)),
            pl.BlockSpec((seq, MOBA_HD), lambda b, h, i: (b, k_blk0 + h)),
            pl.BlockSpec((seq, MOBA_HD), lambda b, h, i: (b, v_blk0 + h)),
            pl.BlockSpec((1, 2, MOBA_BLOCK, MOBA_BLOCK), lambda b, h, i: (h, 0, 0, 0)),
        ],
        out_specs=pl.BlockSpec((MOBA_BLOCK, MOBA_HD), lambda b, h, i: (b * n_blk + i, h)),
        out_shape=jax.ShapeDtypeStruct((n_batch * seq, n_heads * MOBA_HD), out_dtype),
        scratch_shapes=[pltpu.VMEM((n_blk, MOBA_HD), jnp.float32),
                        pltpu.VMEM((n_blk, MOBA_HD, MOBA_BLOCK), jnp.bfloat16),
                        pltpu.VMEM((1, MOBA_BLOCK), jnp.float32),
                        pltpu.VMEM((1, MOBA_BLOCK), jnp.float32),
                        pltpu.VMEM((MOBA_HD, MOBA_BLOCK), jnp.float32)],
        compiler_params=pltpu.CompilerParams(
            dimension_semantics=("parallel", "parallel", "arbitrary")),
    )(rel_bias_t, qkv, qkv, qkv, btab)


def _log_sigmoid(x):
    return jnp.minimum(x, 0.0) - jnp.log(1.0 + jnp.exp(-jnp.abs(x)))


def _causal_shift(x, shift):
    if shift == 0:
        return x
    row = lax.broadcasted_iota(jnp.int32, x.shape, 0)
    return jnp.where(row >= shift, pltpu.roll(x, shift, 0), 0.0)


def _mlstm_kernel(q_ref, k_ref, v_ref, og_ref, gc_ref, gr_ref, cwq_ref, cwk_ref, cbq_ref, cbk_ref,
                  ng_ref, o_ref, qc_sc, kc_sc):
    seq, dqk = q_ref.shape
    dv = v_ref.shape[1]
    L = MLSTM_CHUNK
    nt = (((1,), (1,)), ((), ()))

    def conv_silu(x, w_ref, b_ref):
        y = b_ref[...]
        for j in range(MLSTM_CONV):
            y = y + w_ref[j:j + 1, :] * _causal_shift(x, MLSTM_CONV - 1 - j)
        return y * jax.nn.sigmoid(y)

    qc_sc[...] = conv_silu(q_ref[...], cwq_ref, cbq_ref).astype(qc_sc.dtype)
    kc_sc[...] = (conv_silu(k_ref[...], cwk_ref, cbk_ref) * dqk ** -0.5).astype(kc_sc.dtype)

    t_i = lax.broadcasted_iota(jnp.int32, (L, L), 0)
    s_i = lax.broadcasted_iota(jnp.int32, (L, L), 1)
    causal = s_i <= t_i

    def chunk(c, carry):
        c_st, n_st, m_st = carry
        r0 = pl.multiple_of(c * L, L)
        q = qc_sc[pl.ds(r0, L), :]
        k = kc_sc[pl.ds(r0, L), :]
        v = v_ref[pl.ds(r0, L), :].astype(jnp.bfloat16)
        i_col = gc_ref[0, 0, pl.ds(r0, L), :]
        lf_col = _log_sigmoid(gc_ref[1, 0, pl.ds(r0, L), :])
        i_row = gr_ref[0, 0, 0, pl.ds(c, 1), :]
        lf_row = _log_sigmoid(gr_ref[0, 1, 0, pl.ds(c, 1), :])
        bcum_col = jnp.sum(jnp.where(causal, lf_row, 0.0), axis=1, keepdims=True)
        bcum_row = jnp.sum(jnp.where(t_i <= s_i, lf_col, 0.0), axis=0, keepdims=True)
        b_last = bcum_row[:, L - 1:L]
        dmat = jnp.where(causal, bcum_col - bcum_row + i_row, NEG_BIG)
        a_max = jnp.max(b_last - bcum_row + i_row, axis=1, keepdims=True)
        w_end = jnp.exp(b_last - bcum_col + i_col - a_max)
        inter = bcum_col + m_st
        m_t = jnp.maximum(inter, jnp.max(dmat, axis=1, keepdims=True))
        s_qk = lax.dot_general(q, k, nt, preferred_element_type=jnp.float32) * jnp.exp(dmat - m_t)
        w_inter = jnp.exp(inter - m_t)
        num = (jnp.dot(s_qk.astype(jnp.bfloat16), v, preferred_element_type=jnp.float32)
               + w_inter * jnp.dot(q, c_st.astype(jnp.bfloat16),
                                   preferred_element_type=jnp.float32))
        den = (jnp.sum(s_qk, axis=1, keepdims=True)
               + w_inter * jnp.sum(q.astype(jnp.float32) * n_st, axis=1, keepdims=True))
        h = num / jnp.maximum(jnp.abs(den), jnp.exp(-m_t))
        hc = h - jnp.mean(h, axis=1, keepdims=True)
        hn = hc * lax.rsqrt(jnp.mean(hc * hc, axis=1, keepdims=True) + LN_EPS) * ng_ref[...]
        o_ref[pl.ds(r0, L), :] = (jax.nn.sigmoid(og_ref[pl.ds(r0, L), :]) * hn).astype(o_ref.dtype)
        m_new = jnp.maximum(b_last + m_st, a_max)
        f_sc = jnp.exp(b_last + m_st - m_new)
        i_sc = jnp.exp(a_max - m_new)
        kw = k.astype(jnp.float32) * w_end
        kv = jnp.dot(kw.T.astype(jnp.bfloat16), v, preferred_element_type=jnp.float32)
        c_new = f_sc * c_st + i_sc * kv
        n_new = f_sc * n_st + i_sc * jnp.sum(kw, axis=0, keepdims=True)
        return c_new, n_new, m_new

    init = (jnp.zeros((dqk, dv), jnp.float32), jnp.zeros((1, dqk), jnp.float32),
            jnp.zeros((1, 1), jnp.float32))
    lax.fori_loop(0, seq // L, chunk, init, unroll=4)


def mlstm_pallas(proj_a, gates, conv_w, conv_b, norm_g, n_batch, seq, n_heads, dqk, dv,
                 q_col0, k_col0, v_col0, o_col0):
    nc = seq // MLSTM_CHUNK
    g_col = gates.T.reshape(2, n_heads, n_batch * seq, 1)
    g_row = jnp.moveaxis(gates.reshape(n_batch, nc, MLSTM_CHUNK, 2, n_heads), (3, 4), (1, 2))
    return pl.pallas_call(
        _mlstm_kernel,
        grid=(n_batch, n_heads),
        in_specs=[
            pl.BlockSpec((seq, dqk), lambda b, h: (b, q_col0 // dqk + h)),
            pl.BlockSpec((seq, dqk), lambda b, h: (b, k_col0 // dqk + h)),
            pl.BlockSpec((seq, dv), lambda b, h: (b, v_col0 // dv + h)),
            pl.BlockSpec((seq, dv), lambda b, h: (b, o_col0 // dv + h)),
            pl.BlockSpec((2, 1, seq, 1), lambda b, h: (0, h, b, 0)),
            pl.BlockSpec((1, 2, 1, nc, MLSTM_CHUNK), lambda b, h: (b, 0, h, 0, 0)),
            pl.BlockSpec((MLSTM_CONV, dqk), lambda b, h: (0, h)),
            pl.BlockSpec((MLSTM_CONV, dqk), lambda b, h: (0, n_heads + h)),
            pl.BlockSpec((1, dqk), lambda b, h: (0, h)),
            pl.BlockSpec((1, dqk), lambda b, h: (0, n_heads + h)),
            pl.BlockSpec((1, dv), lambda b, h: (0, h)),
        ],
        out_specs=pl.BlockSpec((seq, dv), lambda b, h: (b, h)),
        out_shape=jax.ShapeDtypeStruct((n_batch * seq, n_heads * dv), jnp.bfloat16),
        scratch_shapes=[pltpu.VMEM((seq, dqk), jnp.bfloat16),
                        pltpu.VMEM((seq, dqk), jnp.bfloat16)],
        compiler_params=pltpu.CompilerParams(
            dimension_semantics=("parallel", "parallel"), vmem_limit_bytes=VMEM_LIMIT),
    )(proj_a, proj_a, proj_a, proj_a, g_col, g_row, conv_w, conv_w,
      conv_b.reshape(1, -1), conv_b.reshape(1, -1), norm_g.reshape(1, -1))


GLA_TILE = 256


def _gla_kernel(q_ref, k_ref, v_ref, go_ref, lr_ref, w2_ref, gb_ref, ng_ref, o_ref, bc_sc, oi_sc,
                *, lr_col0):
    seq, dk = q_ref.shape
    dv = v_ref.shape[1]
    R, C = GLA_TILE, GLA_CHUNK
    scale = dk ** -0.5
    nt = (((1,), (1,)), ((), ()))
    bf16 = jnp.bfloat16
    row = lax.broadcasted_iota(jnp.int32, (R, R), 0)
    col = lax.broadcasted_iota(jnp.int32, (R, R), 1)
    lag = jnp.where((row // C) == (col // C), jnp.where(col <= row, row - col, -1), -1)
    tril16 = jnp.where(lag >= 0, 1.0, 0.0).astype(bf16)
    ones = jnp.ones((dk, R), bf16)
    w2 = w2_ref[...].astype(bf16)

    def tile(i, carry):
        r0 = pl.multiple_of(i * R, R)
        z = jnp.dot(lr_ref[pl.ds(r0, R), lr_col0:lr_col0 + GLA_RANK].astype(bf16), w2,
                    preferred_element_type=jnp.float32) + gb_ref[...]
        la = _log_sigmoid(z) * (1.0 / GLA_TAU)
        la_hi = la.astype(bf16)
        la_lo = (la - la_hi.astype(jnp.float32)).astype(bf16)
        bc = (jnp.dot(tril16, la_hi, preferred_element_type=jnp.float32)
              + jnp.dot(tril16, la_lo, preferred_element_type=jnp.float32))
        bc_sc[pl.ds(r0, R), :] = bc
        q = q_ref[pl.ds(r0, R), :] * scale
        k = k_ref[pl.ds(r0, R), :]
        attn = jnp.zeros((R, R), jnp.float32)
        for d in range(C):
            if d == 0:
                prod = q * k
            else:
                prod = q * pltpu.roll(k, d, 0) * jnp.exp(bc - pltpu.roll(bc, d, 0))
            a_d = jnp.dot(prod.astype(bf16), ones, preferred_element_type=jnp.float32)
            attn = jnp.where(lag == d, a_d, attn)
        oi_sc[pl.ds(r0, R), :] = jnp.dot(attn.astype(bf16), v_ref[pl.ds(r0, R), :].astype(bf16),
                                         preferred_element_type=jnp.float32)
        return carry

    lax.fori_loop(0, seq // R, tile, 0)

    def chunk(c, st):
        r0 = pl.multiple_of(c * C, C)
        bc = bc_sc[pl.ds(r0, C), :]
        bl = bc[C - 1:C, :]
        qt = (q_ref[pl.ds(r0, C), :] * scale * jnp.exp(bc)).astype(bf16)
        kt = (k_ref[pl.ds(r0, C), :] * jnp.exp(bl - bc)).astype(bf16)
        oi_sc[pl.ds(r0, C), :] += lax.dot_general(qt, st.astype(bf16), nt,
                                                  preferred_element_type=jnp.float32)
        vt = v_ref[pl.ds(r0, C), :].T.astype(bf16)
        return jnp.exp(bl) * st + jnp.dot(vt, kt, preferred_element_type=jnp.float32)

    lax.fori_loop(0, seq // C, chunk, jnp.zeros((dv, dk), jnp.float32), unroll=16)

    def finish(i, carry):
        r0 = pl.multiple_of(i * R, R)
        o = oi_sc[pl.ds(r0, R), :]
        on = o * lax.rsqrt(jnp.mean(o * o, axis=1, keepdims=True) + LN_EPS) * ng_ref[...]
        g = go_ref[pl.ds(r0, R), :]
        o_ref[pl.ds(r0, R), :] = (on * (g * jax.nn.sigmoid(g))).astype(o_ref.dtype)
        return carry

    lax.fori_loop(0, seq // R, finish, 0)


def gla_pallas(proj_a, proj_c, lr_col0, gate_w2, gate_b, norm_g, n_batch, seq, n_heads, dk, dv,
               q_col0, k_col0, v_col0, go_col0):
    kern = functools.partial(_gla_kernel, lr_col0=lr_col0)
    return pl.pallas_call(
        kern,
        grid=(n_batch, n_heads),
        in_specs=[
            pl.BlockSpec((seq, dk), lambda b, h: (b, q_col0 // dk + h)),
            pl.BlockSpec((seq, dk), lambda b, h: (b, k_col0 // dk + h)),
            pl.BlockSpec((seq, dv), lambda b, h: (b, v_col0 // dv + h)),
            pl.BlockSpec((seq, dv), lambda b, h: (b, go_col0 // dv + h)),
            pl.BlockSpec((seq, proj_c.shape[1]), lambda b, h: (b, 0)),
            pl.BlockSpec((GLA_RANK, dk), lambda b, h: (0, h)),
            pl.BlockSpec((1, dk), lambda b, h: (0, h)),
            pl.BlockSpec((1, dv), lambda b, h: (0, h)),
        ],
        out_specs=pl.BlockSpec((seq, dv), lambda b, h: (b, h)),
        out_shape=jax.ShapeDtypeStruct((n_batch * seq, n_heads * dv), jnp.bfloat16),
        scratch_shapes=[pltpu.VMEM((seq, dk), jnp.float32), pltpu.VMEM((seq, dv), jnp.float32)],
        compiler_params=pltpu.CompilerParams(
            dimension_semantics=("parallel", "parallel"), vmem_limit_bytes=VMEM_LIMIT),
    )(proj_a, proj_a, proj_a, proj_a, proj_c, gate_w2, gate_b.reshape(1, -1),
      norm_g.reshape(1, -1))


CONV_TILE = 256
CONV_HALO = 32


def _conf_kernel(a_ref, g_ref, w_ref, b_ref, ng_ref, nb_ref, o_ref, x_sc):
    seq, ch = a_ref.shape
    R, H = CONV_TILE, CONV_HALO
    x_sc[0:H, :] = jnp.zeros((H, ch), jnp.float32)
    x_sc[H:, :] = a_ref[...] * jax.nn.sigmoid(g_ref[...])

    def tile(i, carry):
        r0 = pl.multiple_of(i * R, R)
        win = x_sc[pl.ds(r0, R + H), :]
        acc = jnp.zeros((R, ch), jnp.float32) + b_ref[...]
        for sub in range(8):
            sh = pltpu.roll(win, sub, 0) if sub else win
            for blk in range(H // 8):
                off = 8 * blk + sub
                if off >= CONV_WIDTH:
                    continue
                j = CONV_WIDTH - 1 - off
                acc = acc + w_ref[j:j + 1, :] * sh[H - 8 * blk:H - 8 * blk + R, :]
        yc = acc - jnp.mean(acc, axis=1, keepdims=True)
        yn = (yc * lax.rsqrt(jnp.mean(yc * yc, axis=1, keepdims=True) + LN_EPS) * ng_ref[...]
              + nb_ref[...])
        o_ref[pl.ds(r0, R), :] = (yn * jax.nn.sigmoid(yn)).astype(o_ref.dtype)
        return carry

    lax.fori_loop(0, seq // R, tile, 0)


def conformer_conv_pallas(proj_a, a_col0, g_col0, dw_w, dw_b, norm_g, norm_b, n_batch, seq,
                          n_groups, width):
    ch = width // n_groups
    vec = pl.BlockSpec((1, ch), lambda b, g: (0, g))
    return pl.pallas_call(
        _conf_kernel,
        grid=(n_batch, n_groups),
        in_specs=[pl.BlockSpec((seq, ch), lambda b, g: (b, a_col0 // ch + g)),
                  pl.BlockSpec((seq, ch), lambda b, g: (b, g_col0 // ch + g)),
                  pl.BlockSpec((CONV_WIDTH, ch), lambda b, g: (0, g)),
                  vec, vec, vec],
        out_specs=pl.BlockSpec((seq, ch), lambda b, g: (b, g)),
        out_shape=jax.ShapeDtypeStruct((n_batch * seq, width), jnp.bfloat16),
        scratch_shapes=[pltpu.VMEM((seq + CONV_HALO, ch), jnp.float32)],
        compiler_params=pltpu.CompilerParams(
            dimension_semantics=("parallel", "parallel"), vmem_limit_bytes=VMEM_LIMIT),
    )(proj_a, proj_a, dw_w, dw_b.reshape(1, -1), norm_g.reshape(1, -1), norm_b.reshape(1, -1))


def hybrid_mixer(ub, B, S, w_in, b_merge, mlstm_gate_b, mlstm_conv_w, mlstm_conv_b, mlstm_norm_g,
                 conf_dw_w, conf_dw_b, conf_norm_g, conf_norm_b, rel_bias_t, moba_btab,
                 gla_gate_w2, gla_gate_b, gla_norm_g, w_branch):
    W = BRANCH_WIDTH
    bf16, f32 = jnp.bfloat16, jnp.float32
    offs = [0] + np.cumsum(IN_SIZES).tolist()
    w_r = jnp.concatenate([w_in[:, :offs[4]], w_in[:, offs[5]:offs[10]], w_in[:, offs[11]:]],
                          axis=1).astype(bf16)
    w_c = jnp.pad(jnp.concatenate([w_in[:, offs[4]:offs[5]], w_in[:, offs[10]:offs[11]]], axis=1),
                  ((0, 0), (0, 128 - 2 * MLSTM_HEADS - GLA_RANK))).astype(bf16)
    c_conf, c_moba, c_gla, c_merge = 3 * W, 5 * W, 8 * W, 11 * W
    proj_a = matmul(ub, w_r, 512, 1024, f32, 0, c_moba)
    proj_m = matmul(ub, w_r, 512, 1024, bf16, c_moba, c_gla - c_moba)
    proj_g = matmul(ub, w_r, 512, 1024, f32, c_gla, c_merge - c_gla)
    proj_b = matmul(ub, w_r, 512, 1024, bf16, c_merge)
    proj_c = matmul(ub, w_c, 512, 128, f32)

    gates = proj_c[:, :2 * MLSTM_HEADS] + mlstm_gate_b
    y_mlstm = mlstm_pallas(proj_a, gates, mlstm_conv_w, mlstm_conv_b, mlstm_norm_g, B, S,
                           MLSTM_HEADS, MLSTM_DQK, MLSTM_DV, 0, W // 2, W, 2 * W)
    y_conv = conformer_conv_pallas(proj_a, c_conf, c_conf + W, conf_dw_w, conf_dw_b,
                                   conf_norm_g, conf_norm_b, B, S, CONV_GROUPS, W)
    y_moba = moba_pallas(proj_m, rel_bias_t, moba_btab, B, S, 0, MOBA_HEADS, 2 * MOBA_HEADS, bf16)
    y_gla = gla_pallas(proj_g, proj_c, 2 * MLSTM_HEADS, gla_gate_w2, gla_gate_b, gla_norm_g, B, S,
                       GLA_HEADS, GLA_DK, GLA_DV, 0, W // 2, W, 2 * W)
    return merge_branches([y_mlstm, y_conv, y_moba, y_gla], w_branch.astype(bf16), proj_b, 0,
                          b_merge)


def kernel(x, c, w_ada, b_ada, w_in, b_merge, mlstm_gate_b, mlstm_conv_w, mlstm_conv_b,
           mlstm_norm_g, conf_dw_w, conf_dw_b, conf_norm_g, conf_norm_b, rel_bias,
           gla_gate_w2, gla_gate_b, gla_norm_g, w_branch, w_out, ln1_g, ln1_b,
           router_w, router_b, exp_w_up, exp_b_up, exp_w_down, exp_b_down, ln2_g, ln2_b):
    B, S, D = x.shape
    bf16 = jnp.bfloat16
    mod_all = ada_modulation(c, w_ada, b_ada)
    mods = [jnp.split(mod_all[l][:, None, :], 6, axis=-1) for l in range(DEPTH)]
    rel_bias_t = rel_bias.T.astype(jnp.float32)
    moba_btab = moba_bias_tables(rel_bias_t)
    n_rows = B * S * TOP_K + N_EXPERTS * MOE_TILE
    x2d = x.reshape(B * S, D)
    ub = modulate(x2d, mods[0][1], mods[0][0], S)
    for l in range(DEPTH):
        sh1, sc1, g1, sh2, sc2, g2 = mods[l]
        merged = hybrid_mixer(ub, B, S, w_in[l], b_merge[l], mlstm_gate_b[l], mlstm_conv_w[l],
                              mlstm_conv_b[l], mlstm_norm_g[l], conf_dw_w[l], conf_dw_b[l],
                              conf_norm_g[l], conf_norm_b[l], rel_bias_t, moba_btab,
                              gla_gate_w2[l], gla_gate_b[l], gla_norm_g[l], w_branch[l])
        mix = matmul(merged, w_out[l].astype(bf16), 512, 1024, jnp.float32)
        x1, upk, top_idx, top_w, rank4, counts = ln_router(
            x2d, mix, g1, ln1_g[l], ln1_b[l], sc2, sh2, router_w[l].astype(bf16), router_b[l], S)
        dest, tile_expert, n_tiles = moe_layout(counts, top_idx, rank4, n_rows // MOE_TILE)
        xs = moe_dispatch(upk, dest, n_rows)
        ys = grouped_experts(xs, tile_expert, n_tiles, exp_w_up[l].astype(bf16), exp_b_up[l],
                             exp_w_down[l].astype(bf16), exp_b_down[l])
        nxt = min(l + 1, DEPTH - 1)
        x2d, ub = combine_ln(dest, top_w, x1, g2, ln2_g[l], ln2_b[l], mods[nxt][1], mods[nxt][0],
                             ys, S)
    return x2d.reshape(B, S, D)
```

```python
import functools
import math

import jax
import jax.numpy as jnp
import numpy as np
from jax import lax
from jax.experimental import pallas as pl
from jax.experimental.pallas import tpu as pltpu

D_MODEL = 4096
BATCH = 4
SEQ = 2048
DEPTH = 2

N_BRANCHES = 4
BRANCH_WIDTH = D_MODEL // N_BRANCHES
MLSTM_HEADS = 4
MLSTM_DV = BRANCH_WIDTH // MLSTM_HEADS
MLSTM_DQK = MLSTM_DV // 2
MLSTM_CONV = 4
MLSTM_CHUNK = 64
CONV_WIDTH = 31
CONV_GROUPS = 4
MOBA_HEADS = 8
MOBA_HD = BRANCH_WIDTH // MOBA_HEADS
MOBA_BLOCK = 256
MOBA_TOPK = 3
MOBA_Q_CHUNK = 32
REL_BUCKETS = 32
REL_MAX_DIST = 128
GLA_HEADS = 4
GLA_DV = BRANCH_WIDTH // GLA_HEADS
GLA_DK = GLA_DV // 2
GLA_RANK = 16
GLA_TAU = 16.0
GLA_CHUNK = 16
N_EXPERTS = 32
TOP_K = 4
EXPERT_FF = D_MODEL // 8
SWIGLU_ALPHA = 1.702
SWIGLU_LIMIT = 7.0
DEEPNORM_ALPHA = (2 * DEPTH) ** 0.25
LN_EPS = 1e-5

IN_SIZES = (
    MLSTM_HEADS * MLSTM_DQK, MLSTM_HEADS * MLSTM_DQK, BRANCH_WIDTH, BRANCH_WIDTH,
    2 * MLSTM_HEADS, 2 * BRANCH_WIDTH, 3 * BRANCH_WIDTH, GLA_HEADS * GLA_DK,
    GLA_HEADS * GLA_DK, BRANCH_WIDTH, GLA_RANK, BRANCH_WIDTH, N_BRANCHES * D_MODEL,
)
IN_WIDTH = sum(IN_SIZES)

VMEM_LIMIT = 56 * 1024 * 1024


def _mm_kernel(a_ref, b_ref, o_ref):
    o_ref[...] = jnp.dot(a_ref[...], b_ref[...],
                         preferred_element_type=jnp.float32).astype(o_ref.dtype)


def matmul(a, b, tm, tn, out_dtype, col0=0, ncols=None):
    M, K = a.shape
    N = b.shape[1] - col0 if ncols is None else ncols
    assert col0 % tn == 0 and N % tn == 0 and M % tm == 0
    jb = col0 // tn
    return pl.pallas_call(
        _mm_kernel,
        grid=(N // tn, M // tm),
        in_specs=[pl.BlockSpec((tm, K), lambda j, i: (i, 0)),
                  pl.BlockSpec((K, tn), lambda j, i: (0, jb + j))],
        out_specs=pl.BlockSpec((tm, tn), lambda j, i: (i, j)),
        out_shape=jax.ShapeDtypeStruct((M, N), out_dtype),
        compiler_params=pltpu.CompilerParams(
            dimension_semantics=("parallel", "parallel"), vmem_limit_bytes=VMEM_LIMIT),
    )(a, b)


def _mm_f32w_kernel(a_ref, b_ref, o_ref, wb_sc):
    @pl.when(pl.program_id(1) == 0)
    def _():
        wb_sc[...] = b_ref[...].astype(wb_sc.dtype)

    o_ref[...] = jnp.dot(a_ref[...], wb_sc[...],
                         preferred_element_type=jnp.float32).astype(o_ref.dtype)


def matmul_f32w(a, b, tm, tn, out_dtype, col0=0, ncols=None):
    M, K = a.shape
    N = b.shape[1] - col0 if ncols is None else ncols
    assert col0 % tn == 0 and N % tn == 0 and M % tm == 0
    jb = col0 // tn
    return pl.pallas_call(
        _mm_f32w_kernel,
        grid=(N // tn, M // tm),
        in_specs=[pl.BlockSpec((tm, K), lambda j, i: (i, 0)),
                  pl.BlockSpec((K, tn), lambda j, i: (0, jb + j))],
        out_specs=pl.BlockSpec((tm, tn), lambda j, i: (i, j)),
        out_shape=jax.ShapeDtypeStruct((M, N), out_dtype),
        scratch_shapes=[pltpu.VMEM((K, tn), jnp.bfloat16)],
        compiler_params=pltpu.CompilerParams(
            dimension_semantics=("parallel", "arbitrary"), vmem_limit_bytes=VMEM_LIMIT),
    )(a, b)


LANES = 128
REPACK_ROWS = 1024
REPACK_COLS = 1024


def _repack_kernel(*refs, shift):
    if shift:
        main_ref, edge_ref, o_ref = refs
        o_ref[...] = jnp.concatenate([main_ref[:, shift:], edge_ref[:, :shift]],
                                     axis=1).astype(o_ref.dtype)
    else:
        main_ref, o_ref = refs
        o_ref[...] = main_ref[...].astype(o_ref.dtype)


def repack_columns(w, col0, ncols):
    k, n = w.shape
    tk, tn = min(REPACK_ROWS, k), REPACK_COLS
    shift = col0 % LANES
    base = col0 - shift
    assert base % tn == 0 and ncols % tn == 0 and k % tk == 0
    in_specs = [pl.BlockSpec((tk, tn), lambda j, r: (r, base // tn + j))]
    args = [w]
    if shift:
        in_specs.append(pl.BlockSpec((tk, LANES), lambda j, r: (r, (base + (j + 1) * tn) // LANES)))
        args.append(w)
    return pl.pallas_call(
        functools.partial(_repack_kernel, shift=shift),
        grid=(ncols // tn, k // tk),
        in_specs=in_specs,
        out_specs=pl.BlockSpec((tk, tn), lambda j, r: (r, j)),
        out_shape=jax.ShapeDtypeStruct((k, ncols), jnp.bfloat16),
        compiler_params=pltpu.CompilerParams(dimension_semantics=("parallel", "parallel")),
    )(*args)


ROW_TILE = 256
MOE_TILE = 256


def _ada_kernel(c_ref, w_ref, b_ref, o_ref):
    c = c_ref[...]
    cs = (c * jax.nn.sigmoid(c)).astype(jnp.bfloat16)
    o_ref[0] = jnp.dot(cs, w_ref[0].astype(jnp.bfloat16),
                       preferred_element_type=jnp.float32) + b_ref[0]


def ada_modulation(c, w_ada, b_ada, tn=512):
    n_layers, d, n = w_ada.shape
    nb = c.shape[0]
    c8 = jnp.pad(c, ((0, 8 - nb), (0, 0)))
    out = pl.pallas_call(
        _ada_kernel,
        grid=(n_layers, n // tn),
        in_specs=[pl.BlockSpec((8, d), lambda l, j: (0, 0)),
                  pl.BlockSpec((1, d, tn), lambda l, j: (l, 0, j)),
                  pl.BlockSpec((1, 1, tn), lambda l, j: (l, 0, j))],
        out_specs=pl.BlockSpec((1, 8, tn), lambda l, j: (l, 0, j)),
        out_shape=jax.ShapeDtypeStruct((n_layers, 8, n), jnp.float32),
        compiler_params=pltpu.CompilerParams(
            dimension_semantics=("parallel", "parallel"), vmem_limit_bytes=VMEM_LIMIT),
    )(c8, w_ada, b_ada.reshape(n_layers, 1, n))
    return out[:, :nb]


def _modulate_kernel(x_ref, sc_ref, sh_ref, o_ref):
    o_ref[...] = (x_ref[...] * (1.0 + sc_ref[0]) + sh_ref[0]).astype(o_ref.dtype)


def modulate(x2d, sc, sh, seq):
    t, d = x2d.shape
    per_b = seq // ROW_TILE
    mod_spec = pl.BlockSpec((1, 1, d), lambda i: (i // per_b, 0, 0))
    return pl.pallas_call(
        _modulate_kernel,
        grid=(t // ROW_TILE,),
        in_specs=[pl.BlockSpec((ROW_TILE, d), lambda i: (i, 0)), mod_spec, mod_spec],
        out_specs=pl.BlockSpec((ROW_TILE, d), lambda i: (i, 0)),
        out_shape=jax.ShapeDtypeStruct((t, d), jnp.bfloat16),
        compiler_params=pltpu.CompilerParams(dimension_semantics=("parallel",)),
    )(x2d, sc, sh)


def _merge_kernel(y0, y1, y2, y3, wb_ref, g0, g1, g2, g3, bm_ref, o_ref, wb_sc):
    @pl.when(pl.program_id(1) == 0)
    def _():
        wb_sc[...] = wb_ref[...].astype(wb_sc.dtype)

    acc = None
    for n, (y, g) in enumerate(((y0, g0), (y1, g1), (y2, g2), (y3, g3))):
        gate = jax.nn.sigmoid(g[...].astype(jnp.float32) + bm_ref[n:n + 1, :])
        term = gate * jnp.dot(y[...], wb_sc[n], preferred_element_type=jnp.float32)
        acc = term if acc is None else acc + term
    o_ref[...] = acc.astype(o_ref.dtype)


def merge_branches(ys, w_branch, proj_b, gate_col0, b_merge, tm=512, tn=512):
    t, w = ys[0].shape
    d = w_branch.shape[2]
    assert gate_col0 % tn == 0 and d % tn == 0
    y_spec = pl.BlockSpec((tm, w), lambda j, i: (i, 0))
    g_specs = [pl.BlockSpec((tm, tn), lambda j, i, n=n: (i, (gate_col0 + n * d) // tn + j))
               for n in range(N_BRANCHES)]
    return pl.pallas_call(
        _merge_kernel,
        grid=(d // tn, t // tm),
        in_specs=[y_spec] * 4 + [pl.BlockSpec((N_BRANCHES, w, tn), lambda j, i: (0, 0, j))]
        + g_specs + [pl.BlockSpec((N_BRANCHES, tn), lambda j, i: (0, j))],
        out_specs=pl.BlockSpec((tm, tn), lambda j, i: (i, j)),
        out_shape=jax.ShapeDtypeStruct((t, d), jnp.bfloat16),
        scratch_shapes=[pltpu.VMEM((N_BRANCHES, w, tn), jnp.bfloat16)],
        compiler_params=pltpu.CompilerParams(
            dimension_semantics=("parallel", "arbitrary"), vmem_limit_bytes=VMEM_LIMIT),
    )(*ys, w_branch, proj_b, proj_b, proj_b, proj_b, b_merge.reshape(N_BRANCHES, d))


def _pack_pair(lo, hi):
    def rne(x):
        b = lax.bitcast_convert_type(x, jnp.uint32)
        return (b + jnp.uint32(0x7FFF) + ((b >> 16) & jnp.uint32(1))) >> 16
    return rne(lo) | (rne(hi) << 16)


def _unpack_lo(w):
    return lax.bitcast_convert_type(w << 16, jnp.float32)


def _unpack_hi(w):
    return lax.bitcast_convert_type(w & jnp.uint32(0xFFFF0000), jnp.float32)


def _layer_norm_halves(z_lo, z_hi, g_ref, b_ref, half):
    d = 2 * half
    mu = (jnp.sum(z_lo, axis=1, keepdims=True) + jnp.sum(z_hi, axis=1, keepdims=True)) / d
    c_lo, c_hi = z_lo - mu, z_hi - mu
    var = (jnp.sum(c_lo * c_lo, axis=1, keepdims=True)
           + jnp.sum(c_hi * c_hi, axis=1, keepdims=True)) / d
    r = lax.rsqrt(var + LN_EPS)
    return (c_lo * r * g_ref[:, :half] + b_ref[:, :half],
            c_hi * r * g_ref[:, half:] + b_ref[:, half:])


def _ln_router_kernel(x_ref, mix_ref, g1_ref, lng_ref, lnb_ref, sc_ref, sh_ref, rw_ref, rb_ref,
                      x1_ref, upk_ref, tidx_ref, tw_ref, rank_ref, cnt_ref, carry_sc):
    tm, d = x_ref.shape
    half = d // 2
    n_exp = rw_ref.shape[1]

    @pl.when(pl.program_id(0) == 0)
    def _():
        carry_sc[...] = jnp.zeros_like(carry_sc)

    z = DEEPNORM_ALPHA * x_ref[...] + (1.0 + g1_ref[0]) * mix_ref[...]
    x_lo, x_hi = _layer_norm_halves(z[:, :half], z[:, half:], lng_ref, lnb_ref, half)
    x1_ref[:, :half] = x_lo
    x1_ref[:, half:] = x_hi
    u_lo = x_lo * (1.0 + sc_ref[0, :, :half]) + sh_ref[0, :, :half]
    u_hi = x_hi * (1.0 + sc_ref[0, :, half:]) + sh_ref[0, :, half:]
    upk_ref[...] = _pack_pair(u_lo, u_hi)

    logits = (jnp.dot(u_lo.astype(jnp.bfloat16), rw_ref[:half, :],
                      preferred_element_type=jnp.float32)
              + jnp.dot(u_hi.astype(jnp.bfloat16), rw_ref[half:, :],
                        preferred_element_type=jnp.float32) + rb_ref[...])
    lane = lax.broadcasted_iota(jnp.int32, (tm, n_exp), 1).astype(jnp.float32)
    lg = logits
    vals, hots = [], []
    for k in range(TOP_K):
        m = jnp.max(lg, axis=1, keepdims=True)
        ik = jnp.min(jnp.where(lg == m, lane, float(n_exp)), axis=1, keepdims=True)
        hot = lane == ik
        vals.append(m)
        hots.append(hot)
        tidx_ref[:, k:k + 1] = ik.astype(jnp.int32)
        lg = jnp.where(hot, -jnp.inf, lg)
    exps = [jnp.exp(v - vals[0]) for v in vals]
    den = exps[0] + exps[1] + exps[2] + exps[3]
    for k in range(TOP_K):
        tw_ref[:, k:k + 1] = exps[k] / den

    mask = jnp.where(hots[0] | hots[1] | hots[2] | hots[3], 1.0, 0.0)
    row = lax.broadcasted_iota(jnp.int32, (tm, tm), 0)
    col = lax.broadcasted_iota(jnp.int32, (tm, tm), 1)
    lower = jnp.where(row > col, 1.0, 0.0).astype(jnp.bfloat16)
    rank = jnp.dot(lower, mask.astype(jnp.bfloat16),
                   preferred_element_type=jnp.float32) + carry_sc[...]
    for k in range(TOP_K):
        rank_ref[:, k:k + 1] = jnp.sum(jnp.where(hots[k], rank, 0.0), axis=1,
                                       keepdims=True).astype(jnp.int32)
    carry_sc[...] = carry_sc[...] + jnp.sum(mask, axis=0, keepdims=True)
    cnt_ref[...] = carry_sc[...]


def ln_router(x2d, mix, g1, ln_g, ln_b, sc2, sh2, router_w_bf, router_b, seq):
    t, d = x2d.shape
    n_exp = router_w_bf.shape[1]
    per_b = seq // ROW_TILE
    row = lambda w: pl.BlockSpec((ROW_TILE, w), lambda i: (i, 0))
    mod = pl.BlockSpec((1, 1, d), lambda i: (i // per_b, 0, 0))
    vec = lambda w: pl.BlockSpec((1, w), lambda i: (0, 0))
    return pl.pallas_call(
        _ln_router_kernel,
        grid=(t // ROW_TILE,),
        in_specs=[row(d), row(d), mod, vec(d), vec(d), mod, mod,
                  pl.BlockSpec((d, n_exp), lambda i: (0, 0)), vec(n_exp)],
        out_specs=[row(d), row(d // 2), row(TOP_K), row(TOP_K), row(TOP_K), vec(n_exp)],
        out_shape=[jax.ShapeDtypeStruct((t, d), jnp.float32),
                   jax.ShapeDtypeStruct((t, d // 2), jnp.uint32),
                   jax.ShapeDtypeStruct((t, TOP_K), jnp.int32),
                   jax.ShapeDtypeStruct((t, TOP_K), jnp.float32),
                   jax.ShapeDtypeStruct((t, TOP_K), jnp.int32),
                   jax.ShapeDtypeStruct((1, n_exp), jnp.float32)],
        scratch_shapes=[pltpu.VMEM((1, n_exp), jnp.float32)],
        compiler_params=pltpu.CompilerParams(
            dimension_semantics=("arbitrary",), vmem_limit_bytes=VMEM_LIMIT),
    )(x2d, mix, g1, ln_g.reshape(1, d), ln_b.reshape(1, d), sc2, sh2, router_w_bf,
      router_b.reshape(1, n_exp))


def _row_copy(src_hbm, src_row, dst, dst_row, sem):
    return pltpu.make_async_copy(src_hbm.at[pl.ds(src_row, 1)], dst.at[pl.ds(dst_row, 1)], sem)


def _dispatch_kernel(dest_ref, u_ref, xs_in_hbm, xs_hbm, sem):
    del xs_in_hbm

    def issue(r, carry):
        for k in range(TOP_K):
            _row_copy(u_ref, r, xs_hbm, dest_ref[r * TOP_K + k], sem).start(priority=k % 2)
        return carry

    def drain(r, carry):
        for k in range(TOP_K):
            _row_copy(u_ref, 0, xs_hbm, 0, sem).wait()
        return carry

    lax.fori_loop(0, ROW_TILE, issue, 0)
    lax.fori_loop(0, ROW_TILE, drain, 0)


def moe_dispatch(upk, dest_flat, n_rows):
    t, w = upk.shape
    xs0 = jnp.zeros((n_rows, w), upk.dtype)
    return pl.pallas_call(
        _dispatch_kernel,
        grid=(t // ROW_TILE,),
        in_specs=[pl.BlockSpec((ROW_TILE * TOP_K,), lambda i: (i,), memory_space=pltpu.SMEM),
                  pl.BlockSpec((ROW_TILE, w), lambda i: (i, 0)),
                  pl.BlockSpec(memory_space=pl.ANY)],
        out_specs=pl.BlockSpec(memory_space=pl.ANY),
        out_shape=jax.ShapeDtypeStruct((n_rows, w), upk.dtype),
        scratch_shapes=[pltpu.SemaphoreType.DMA(())],
        input_output_aliases={2: 0},
        compiler_params=pltpu.CompilerParams(dimension_semantics=("arbitrary",)),
    )(dest_flat, upk, xs0)


def _expert_kernel(te_ref, nt_ref, xs_ref, wu_ref, bu_ref, wd_ref, bd_ref, ys_ref):
    g = pl.program_id(0)
    half = xs_ref.shape[1]
    ff = wd_ref.shape[1]

    @pl.when(g < nt_ref[0])
    def _():
        w = xs_ref[...]
        x_lo = _unpack_lo(w).astype(jnp.bfloat16)
        x_hi = _unpack_hi(w).astype(jnp.bfloat16)
        hid = (jnp.dot(x_lo, wu_ref[0, :half, :], preferred_element_type=jnp.float32)
               + jnp.dot(x_hi, wu_ref[0, half:, :], preferred_element_type=jnp.float32)
               + bu_ref[0])
        glu = jnp.minimum(hid[:, :ff], SWIGLU_LIMIT)
        lin = jnp.clip(hid[:, ff:], -SWIGLU_LIMIT, SWIGLU_LIMIT)
        act = glu * jax.nn.sigmoid(SWIGLU_ALPHA * glu) * (lin + 1.0)
        y = jnp.dot(act.astype(jnp.bfloat16), wd_ref[0],
                    preferred_element_type=jnp.float32) + bd_ref[0]
        ys_ref[...] = _pack_pair(y[:, :half], y[:, half:])

    @pl.when(g >= nt_ref[0])
    def _():
        ys_ref[...] = jnp.zeros_like(ys_ref)


def grouped_experts(xs, tile_expert, n_tiles, w_up_bf, b_up, w_down_bf, b_down):
    n_rows, half = xs.shape
    n_exp, d, ff2 = w_up_bf.shape
    ff = ff2 // 2
    grid_spec = pltpu.PrefetchScalarGridSpec(
        num_scalar_prefetch=2,
        grid=(n_rows // MOE_TILE,),
        in_specs=[pl.BlockSpec((MOE_TILE, half), lambda g, te, nt: (g, 0)),
                  pl.BlockSpec((1, d, ff2), lambda g, te, nt: (te[g], 0, 0)),
                  pl.BlockSpec((1, 1, ff2), lambda g, te, nt: (te[g], 0, 0)),
                  pl.BlockSpec((1, ff, d), lambda g, te, nt: (te[g], 0, 0)),
                  pl.BlockSpec((1, 1, d), lambda g, te, nt: (te[g], 0, 0))],
        out_specs=pl.BlockSpec((MOE_TILE, half), lambda g, te, nt: (g, 0)),
    )
    return pl.pallas_call(
        _expert_kernel,
        grid_spec=grid_spec,
        out_shape=jax.ShapeDtypeStruct((n_rows, half), jnp.uint32),
        compiler_params=pltpu.CompilerParams(
            dimension_semantics=("arbitrary",), vmem_limit_bytes=VMEM_LIMIT),
    )(tile_expert, n_tiles, xs, w_up_bf, b_up.reshape(n_exp, 1, ff2), w_down_bf,
      b_down.reshape(n_exp, 1, d))


COMBINE_ROWS = 16


def _combine_kernel(dest_ref, dest_next_ref, tw_ref, x1_ref, g2_ref, lng_ref, lnb_ref, sc_ref,
                    sh_ref, ys_hbm, x2_ref, un_ref, buf, sem):
    tm, d = x1_ref.shape
    half = d // 2
    i = pl.program_id(0)
    slot = lax.rem(i, 2)

    has_next = i + 1 < pl.num_programs(0)
    sub = COMBINE_ROWS

    def gather_rows(d_ref, s, r0):
        for rr in range(sub):
            for k in range(TOP_K):
                _row_copy(ys_hbm, d_ref[(r0 + rr) * TOP_K + k], buf.at[s, k], r0 + rr,
                          sem.at[s]).start(priority=k % 2)

    @pl.when(i == 0)
    def _():
        def first(t, carry):
            gather_rows(dest_ref, 0, t * sub)
            return carry
        lax.fori_loop(0, tm // sub, first, 0)

    def drain(r, carry):
        for k in range(TOP_K):
            _row_copy(ys_hbm, 0, buf.at[slot, k], 0, sem.at[slot]).wait()
        return carry

    lax.fori_loop(0, tm, drain, 0)

    def rows(t, carry):
        r0 = pl.multiple_of(t * sub, sub)
        rs = pl.ds(r0, sub)

        gather_rows(dest_next_ref, 1 - slot, r0)

        f_lo = f_hi = None
        for k in range(TOP_K):
            wk = tw_ref[rs, k:k + 1]
            w = buf[slot, k, rs, :]
            t_lo, t_hi = wk * _unpack_lo(w), wk * _unpack_hi(w)
            f_lo = t_lo if f_lo is None else f_lo + t_lo
            f_hi = t_hi if f_hi is None else f_hi + t_hi
        z_lo = DEEPNORM_ALPHA * x1_ref[rs, :half] + (1.0 + g2_ref[0, :, :half]) * f_lo
        z_hi = DEEPNORM_ALPHA * x1_ref[rs, half:] + (1.0 + g2_ref[0, :, half:]) * f_hi
        x_lo, x_hi = _layer_norm_halves(z_lo, z_hi, lng_ref, lnb_ref, half)
        x2_ref[rs, :half] = x_lo
        x2_ref[rs, half:] = x_hi
        un_ref[rs, :half] = (x_lo * (1.0 + sc_ref[0, :, :half])
                             + sh_ref[0, :, :half]).astype(un_ref.dtype)
        un_ref[rs, half:] = (x_hi * (1.0 + sc_ref[0, :, half:])
                             + sh_ref[0, :, half:]).astype(un_ref.dtype)
        return carry

    lax.fori_loop(0, tm // sub, rows, 0)

    @pl.when(jnp.logical_not(has_next))
    def _():
        def drain_other(r, carry):
            for k in range(TOP_K):
                _row_copy(ys_hbm, 0, buf.at[1 - slot, k], 0, sem.at[1 - slot]).wait()
            return carry
        lax.fori_loop(0, tm, drain_other, 0)


def combine_ln(dest_flat, top_w, x1, g2, ln_g, ln_b, sc_next, sh_next, ys, seq):
    t, d = x1.shape
    per_b = seq // ROW_TILE
    n_steps = t // ROW_TILE
    row = lambda w: pl.BlockSpec((ROW_TILE, w), lambda i: (i, 0))
    mod = pl.BlockSpec((1, 1, d), lambda i: (i // per_b, 0, 0))
    vec = pl.BlockSpec((1, d), lambda i: (0, 0))
    return pl.pallas_call(
        _combine_kernel,
        grid=(n_steps,),
        in_specs=[pl.BlockSpec((ROW_TILE * TOP_K,), lambda i: (i,), memory_space=pltpu.SMEM),
                  pl.BlockSpec((ROW_TILE * TOP_K,), lambda i: (jnp.minimum(i + 1, n_steps - 1),),
                               memory_space=pltpu.SMEM),
                  row(TOP_K), row(d), mod, vec, vec, mod, mod,
                  pl.BlockSpec(memory_space=pl.ANY)],
        out_specs=[row(d), row(d)],
        out_shape=[jax.ShapeDtypeStruct((t, d), jnp.float32),
                   jax.ShapeDtypeStruct((t, d), jnp.bfloat16)],
        scratch_shapes=[pltpu.VMEM((2, TOP_K, ROW_TILE, d // 2), jnp.uint32),
                        pltpu.SemaphoreType.DMA((2,))],
        compiler_params=pltpu.CompilerParams(
            dimension_semantics=("arbitrary",), vmem_limit_bytes=VMEM_LIMIT),
    )(dest_flat, dest_flat, top_w, x1, g2, ln_g.reshape(1, d), ln_b.reshape(1, d), sc_next,
      sh_next, ys)


def moe_layout(counts, top_idx, rank4, n_tiles_max):
    cnt = counts.reshape(-1).astype(jnp.int32)
    tiles = (cnt + MOE_TILE - 1) // MOE_TILE
    tile_end = jnp.cumsum(tiles)
    row_start = (tile_end - tiles) * MOE_TILE
    dest = (row_start[top_idx] + rank4).reshape(-1)
    tile_expert = jnp.minimum(
        jnp.sum(jnp.arange(n_tiles_max)[:, None] >= tile_end[None, :], axis=1),
        cnt.shape[0] - 1).astype(jnp.int32)
    return dest.astype(jnp.int32), tile_expert, tile_end[-1:].astype(jnp.int32)


NEG_BIG = -1e30


def _t5_bucket_np(d):
    max_exact = REL_BUCKETS // 2
    dd = np.maximum(d, 1).astype(np.float32)
    far = max_exact + (np.log(dd / np.float32(max_exact))
                       / np.float32(math.log(REL_MAX_DIST / max_exact))
                       * np.float32(REL_BUCKETS - max_exact)).astype(np.int32)
    return np.where(d < max_exact, d, np.minimum(far, REL_BUCKETS - 1)).astype(np.int32)


def _moba_bucket_tables():
    c = np.arange(MOBA_BLOCK)[:, None]
    r = np.arange(MOBA_BLOCK)[None, :]
    own = np.where(r >= c, _t5_bucket_np(np.maximum(r - c, 0)), -1)
    prev = _t5_bucket_np(MOBA_BLOCK + r - c)
    return np.stack([own, prev]).astype(np.int32)


def _bias_tab_kernel(rb_ref, bk_ref, o_ref):
    h = pl.program_id(0)
    bk = bk_ref[...]
    acc = jnp.zeros(bk.shape, jnp.float32)
    for m in range(REL_BUCKETS):
        acc = jnp.where(bk == m, rb_ref[h, m], acc)
    o_ref[0] = jnp.where(bk < 0, NEG_BIG, acc)


def moba_bias_tables(rel_bias_t):
    n_heads = rel_bias_t.shape[0]
    bk = jnp.asarray(_moba_bucket_tables())
    return pl.pallas_call(
        _bias_tab_kernel,
        grid=(n_heads,),
        in_specs=[pl.BlockSpec(memory_space=pltpu.SMEM),
                  pl.BlockSpec((2, MOBA_BLOCK, MOBA_BLOCK), lambda h: (0, 0, 0))],
        out_specs=pl.BlockSpec((1, 2, MOBA_BLOCK, MOBA_BLOCK), lambda h: (h, 0, 0, 0)),
        out_shape=jax.ShapeDtypeStruct((n_heads, 2, MOBA_BLOCK, MOBA_BLOCK), jnp.float32),
    )(rel_bias_t, bk)


def _moba_kernel(rb_ref, q_ref, k_ref, v_ref, bt_ref, o_ref, km_sc, vt_sc, m_sc, l_sc, acc_sc,
                 *, n_blk):
    h = pl.program_id(1)
    i = pl.program_id(2)
    blk = MOBA_BLOCK
    seq = n_blk * blk
    scale = MOBA_HD ** -0.5
    nt = (((1,), (1,)), ((), ()))
    bf16 = jnp.bfloat16

    @pl.when(i == 0)
    def _():
        blk_of_col = lax.broadcasted_iota(jnp.int32, (n_blk, seq), 1) // blk
        blk_row = lax.broadcasted_iota(jnp.int32, (n_blk, seq), 0)
        avg = jnp.where(blk_of_col == blk_row, 1.0 / blk, 0.0).astype(bf16)
        km_sc[...] = jnp.dot(avg, k_ref[...], preferred_element_type=jnp.float32)
        for j in range(n_blk):
            vt_sc[j] = v_ref[j * blk:(j + 1) * blk, :].astype(jnp.float32).T.astype(bf16)

    q = q_ref[...]
    kmean = km_sc[...]
    km_hi = kmean.astype(bf16)
    km_lo = (kmean - km_hi.astype(jnp.float32)).astype(bf16)
    gate = (lax.dot_general(km_hi, q, nt, preferred_element_type=jnp.float32)
            + lax.dot_general(km_lo, q, nt, preferred_element_type=jnp.float32))
    g = [gate[n:n + 1, :] for n in range(n_blk)]

    def selected(j):
        cnt = jnp.zeros((1, blk), jnp.float32)
        for n in range(n_blk - 1):
            if n == j:
                continue
            ahead = (g[n] >= g[j]) if n < j else (g[n] > g[j])
            cnt = cnt + jnp.where(ahead, jnp.where(n < i, 1.0, 0.0), 0.0)
        return cnt < float(MOBA_TOPK)

    row0 = pl.multiple_of(i * blk, blk)
    k_own = k_ref[pl.ds(row0, blk), :]
    s = lax.dot_general(k_own, q, nt, preferred_element_type=jnp.float32) * scale + bt_ref[0, 0]
    m0 = jnp.max(s, axis=0, keepdims=True)
    p = jnp.exp(s - m0)
    m_sc[...] = m0
    l_sc[...] = jnp.sum(p, axis=0, keepdims=True)
    acc_sc[...] = jnp.dot(vt_sc[i], p.astype(bf16), preferred_element_type=jnp.float32)

    bfar = rb_ref[h, REL_BUCKETS - 1]
    for j in range(n_blk - 1):
        @pl.when(j < i)
        def _():
            k_j = k_ref[j * blk:(j + 1) * blk, :]
            bias = jnp.where(j == i - 1, bt_ref[0, 1], bfar)
            s = lax.dot_general(k_j, q, nt, preferred_element_type=jnp.float32) * scale + bias
            s = jnp.where(selected(j), s, NEG_BIG)
            m_old = m_sc[...]
            m_new = jnp.maximum(m_old, jnp.max(s, axis=0, keepdims=True))
            alpha = jnp.exp(m_old - m_new)
            p = jnp.exp(s - m_new)
            m_sc[...] = m_new
            l_sc[...] = alpha * l_sc[...] + jnp.sum(p, axis=0, keepdims=True)
            acc_sc[...] = alpha * acc_sc[...] + jnp.dot(
                vt_sc[j], p.astype(bf16), preferred_element_type=jnp.float32)

    o_ref[...] = (acc_sc[...] / l_sc[...]).T.astype(o_ref.dtype)


def moba_pallas(qkv, rel_bias_t, btab, n_batch, seq, q_blk0, k_blk0, v_blk0, out_dtype):
    n_heads = rel_bias_t.shape[0]
    n_blk = seq // MOBA_BLOCK
    assert seq % MOBA_BLOCK == 0
    kern = functools.partial(_moba_kernel, n_blk=n_blk)
    return pl.pallas_call(
        kern,
        grid=(n_batch, n_heads, n_blk),
        in_specs=[
            pl.BlockSpec(memory_space=pltpu.SMEM),
            pl.BlockSpec((MOBA_BLOCK, MOBA_HD), lambda b, h, i: (b * n_blk + i, q_blk0 + h)),
            pl.BlockSpec((seq, MOBA_HD), lambda b, h, i: (b, k_blk0 + h)),
            pl.BlockSpec((seq, MOBA_HD), lambda b, h, i: (b, v_blk0 + h)),
            pl.BlockSpec((1, 2, MOBA_BLOCK, MOBA_BLOCK), lambda b, h, i: (h, 0, 0, 0)),
        ],
        out_specs=pl.BlockSpec((MOBA_BLOCK, MOBA_HD), lambda b, h, i: (b * n_blk + i, h)),
        out_shape=jax.ShapeDtypeStruct((n_batch * seq, n_heads * MOBA_HD), out_dtype),
        scratch_shapes=[pltpu.VMEM((n_blk, MOBA_HD), jnp.float32),
                        pltpu.VMEM((n_blk, MOBA_HD, MOBA_BLOCK), jnp.bfloat16),
                        pltpu.VMEM((1, MOBA_BLOCK), jnp.float32),
                        pltpu.VMEM((1, MOBA_BLOCK), jnp.float32),
                        pltpu.VMEM((MOBA_HD, MOBA_BLOCK), jnp.float32)],
        compiler_params=pltpu.CompilerParams(
            dimension_semantics=("parallel", "parallel", "arbitrary")),
    )(rel_bias_t, qkv, qkv, qkv, btab)


def _log_sigmoid(x):
    return jnp.minimum(x, 0.0) - jnp.log(1.0 + jnp.exp(-jnp.abs(x)))


def _causal_shift(x, shift):
    if shift == 0:
        return x
    row = lax.broadcasted_iota(jnp.int32, x.shape, 0)
    return jnp.where(row >= shift, pltpu.roll(x, shift, 0), 0.0)


def _mlstm_kernel(q_ref, k_ref, v_ref, og_ref, gc_ref, gr_ref, cwq_ref, cwk_ref, cbq_ref, cbk_ref,
                  ng_ref, o_ref, qc_sc, kc_sc):
    seq, dqk = q_ref.shape
    dv = v_ref.shape[1]
    L = MLSTM_CHUNK
    nt = (((1,), (1,)), ((), ()))

    def conv_silu(x, w_ref, b_ref):
        y = b_ref[...]
        for j in range(MLSTM_CONV):
            y = y + w_ref[j:j + 1, :] * _causal_shift(x, MLSTM_CONV - 1 - j)
        return y * jax.nn.sigmoid(y)

    qc_sc[...] = conv_silu(q_ref[...], cwq_ref, cbq_ref).astype(qc_sc.dtype)
    kc_sc[...] = (conv_silu(k_ref[...], cwk_ref, cbk_ref) * dqk ** -0.5).astype(kc_sc.dtype)

    t_i = lax.broadcasted_iota(jnp.int32, (L, L), 0)
    s_i = lax.broadcasted_iota(jnp.int32, (L, L), 1)
    causal = s_i <= t_i

    def chunk(c, carry):
        c_st, n_st, m_st = carry
        r0 = pl.multiple_of(c * L, L)
        q = qc_sc[pl.ds(r0, L), :]
        k = kc_sc[pl.ds(r0, L), :]
        v = v_ref[pl.ds(r0, L), :].astype(jnp.bfloat16)
        i_col = gc_ref[0, 0, pl.ds(r0, L), :]
        lf_col = _log_sigmoid(gc_ref[1, 0, pl.ds(r0, L), :])
        i_row = gr_ref[0, 0, 0, pl.ds(c, 1), :]
        lf_row = _log_sigmoid(gr_ref[0, 1, 0, pl.ds(c, 1), :])
        bcum_col = jnp.sum(jnp.where(causal, lf_row, 0.0), axis=1, keepdims=True)
        bcum_row = jnp.sum(jnp.where(t_i <= s_i, lf_col, 0.0), axis=0, keepdims=True)
        b_last = bcum_row[:, L - 1:L]
        dmat = jnp.where(causal, bcum_col - bcum_row + i_row, NEG_BIG)
        a_max = jnp.max(b_last - bcum_row + i_row, axis=1, keepdims=True)
        w_end = jnp.exp(b_last - bcum_col + i_col - a_max)
        inter = bcum_col + m_st
        m_t = jnp.maximum(inter, jnp.max(dmat, axis=1, keepdims=True))
        s_qk = lax.dot_general(q, k, nt, preferred_element_type=jnp.float32) * jnp.exp(dmat - m_t)
        w_inter = jnp.exp(inter - m_t)
        num = (jnp.dot(s_qk.astype(jnp.bfloat16), v, preferred_element_type=jnp.float32)
               + w_inter * jnp.dot(q, c_st.astype(jnp.bfloat16),
                                   preferred_element_type=jnp.float32))
        den = (jnp.sum(s_qk, axis=1, keepdims=True)
               + w_inter * jnp.sum(q.astype(jnp.float32) * n_st, axis=1, keepdims=True))
        h = num / jnp.maximum(jnp.abs(den), jnp.exp(-m_t))
        hc = h - jnp.mean(h, axis=1, keepdims=True)
        hn = hc * lax.rsqrt(jnp.mean(hc * hc, axis=1, keepdims=True) + LN_EPS) * ng_ref[...]
        o_ref[pl.ds(r0, L), :] = (jax.nn.sigmoid(og_ref[pl.ds(r0, L), :]) * hn).astype(o_ref.dtype)
        m_new = jnp.maximum(b_last + m_st, a_max)
        f_sc = jnp.exp(b_last + m_st - m_new)
        i_sc = jnp.exp(a_max - m_new)
        kw = k.astype(jnp.float32) * w_end
        kv = jnp.dot(kw.T.astype(jnp.bfloat16), v, preferred_element_type=jnp.float32)
        c_new = f_sc * c_st + i_sc * kv
        n_new = f_sc * n_st + i_sc * jnp.sum(kw, axis=0, keepdims=True)
        return c_new, n_new, m_new

    init = (jnp.zeros((dqk, dv), jnp.float32), jnp.zeros((1, dqk), jnp.float32),
            jnp.zeros((1, 1), jnp.float32))
    lax.fori_loop(0, seq // L, chunk, init, unroll=4)


def mlstm_pallas(proj_a, gates, conv_w, conv_b, norm_g, n_batch, seq, n_heads, dqk, dv,
                 q_col0, k_col0, v_col0, o_col0):
    nc = seq // MLSTM_CHUNK
    g_col = gates.T.reshape(2, n_heads, n_batch * seq, 1)
    g_row = jnp.moveaxis(gates.reshape(n_batch, nc, MLSTM_CHUNK, 2, n_heads), (3, 4), (1, 2))
    return pl.pallas_call(
        _mlstm_kernel,
        grid=(n_batch, n_heads),
        in_specs=[
            pl.BlockSpec((seq, dqk), lambda b, h: (b, q_col0 // dqk + h)),
            pl.BlockSpec((seq, dqk), lambda b, h: (b, k_col0 // dqk + h)),
            pl.BlockSpec((seq, dv), lambda b, h: (b, v_col0 // dv + h)),
            pl.BlockSpec((seq, dv), lambda b, h: (b, o_col0 // dv + h)),
            pl.BlockSpec((2, 1, seq, 1), lambda b, h: (0, h, b, 0)),
            pl.BlockSpec((1, 2, 1, nc, MLSTM_CHUNK), lambda b, h: (b, 0, h, 0, 0)),
            pl.BlockSpec((MLSTM_CONV, dqk), lambda b, h: (0, h)),
            pl.BlockSpec((MLSTM_CONV, dqk), lambda b, h: (0, n_heads + h)),
            pl.BlockSpec((1, dqk), lambda b, h: (0, h)),
            pl.BlockSpec((1, dqk), lambda b, h: (0, n_heads + h)),
            pl.BlockSpec((1, dv), lambda b, h: (0, h)),
        ],
        out_specs=pl.BlockSpec((seq, dv), lambda b, h: (b, h)),
        out_shape=jax.ShapeDtypeStruct((n_batch * seq, n_heads * dv), jnp.bfloat16),
        scratch_shapes=[pltpu.VMEM((seq, dqk), jnp.bfloat16),
                        pltpu.VMEM((seq, dqk), jnp.bfloat16)],
        compiler_params=pltpu.CompilerParams(
            dimension_semantics=("parallel", "parallel"), vmem_limit_bytes=VMEM_LIMIT),
    )(proj_a, proj_a, proj_a, proj_a, g_col, g_row, conv_w, conv_w,
      conv_b.reshape(1, -1), conv_b.reshape(1, -1), norm_g.reshape(1, -1))


GLA_TILE = 256


def _gla_kernel(q_ref, k_ref, v_ref, go_ref, lr_ref, w2_ref, gb_ref, ng_ref, o_ref, bc_sc, oi_sc,
                *, lr_col0):
    seq, dk = q_ref.shape
    dv = v_ref.shape[1]
    R, C = GLA_TILE, GLA_CHUNK
    scale = dk ** -0.5
    nt = (((1,), (1,)), ((), ()))
    bf16 = jnp.bfloat16
    row = lax.broadcasted_iota(jnp.int32, (R, R), 0)
    col = lax.broadcasted_iota(jnp.int32, (R, R), 1)
    lag = jnp.where((row // C) == (col // C), jnp.where(col <= row, row - col, -1), -1)
    tril16 = jnp.where(lag >= 0, 1.0, 0.0).astype(bf16)
    ones = jnp.ones((dk, R), bf16)
    w2 = w2_ref[...].astype(bf16)

    def tile(i, carry):
        r0 = pl.multiple_of(i * R, R)
        z = jnp.dot(lr_ref[pl.ds(r0, R), lr_col0:lr_col0 + GLA_RANK].astype(bf16), w2,
                    preferred_element_type=jnp.float32) + gb_ref[...]
        la = _log_sigmoid(z) * (1.0 / GLA_TAU)
        la_hi = la.astype(bf16)
        la_lo = (la - la_hi.astype(jnp.float32)).astype(bf16)
        bc = (jnp.dot(tril16, la_hi, preferred_element_type=jnp.float32)
              + jnp.dot(tril16, la_lo, preferred_element_type=jnp.float32))
        bc_sc[pl.ds(r0, R), :] = bc
        q = q_ref[pl.ds(r0, R), :] * scale
        k = k_ref[pl.ds(r0, R), :]
        attn = jnp.zeros((R, R), jnp.float32)
        for d in range(C):
            if d == 0:
                prod = q * k
            else:
                prod = q * pltpu.roll(k, d, 0) * jnp.exp(bc - pltpu.roll(bc, d, 0))
            a_d = jnp.dot(prod.astype(bf16), ones, preferred_element_type=jnp.float32)
            attn = jnp.where(lag == d, a_d, attn)
        oi_sc[pl.ds(r0, R), :] = jnp.dot(attn.astype(bf16), v_ref[pl.ds(r0, R), :].astype(bf16),
                                         preferred_element_type=jnp.float32)
        return carry

    lax.fori_loop(0, seq // R, tile, 0)

    def chunk(c, st):
        r0 = pl.multiple_of(c * C, C)
        bc = bc_sc[pl.ds(r0, C), :]
        bl = bc[C - 1:C, :]
        qt = (q_ref[pl.ds(r0, C), :] * scale * jnp.exp(bc)).astype(bf16)
        kt = (k_ref[pl.ds(r0, C), :] * jnp.exp(bl - bc)).astype(bf16)
        oi_sc[pl.ds(r0, C), :] += lax.dot_general(qt, st.astype(bf16), nt,
                                                  preferred_element_type=jnp.float32)
        vt = v_ref[pl.ds(r0, C), :].T.astype(bf16)
        return jnp.exp(bl) * st + jnp.dot(vt, kt, preferred_element_type=jnp.float32)

    lax.fori_loop(0, seq // C, chunk, jnp.zeros((dv, dk), jnp.float32), unroll=16)

    def finish(i, carry):
        r0 = pl.multiple_of(i * R, R)
        o = oi_sc[pl.ds(r0, R), :]
        on = o * lax.rsqrt(jnp.mean(o * o, axis=1, keepdims=True) + LN_EPS) * ng_ref[...]
        g = go_ref[pl.ds(r0, R), :]
        o_ref[pl.ds(r0, R), :] = (on * (g * jax.nn.sigmoid(g))).astype(o_ref.dtype)
        return carry

    lax.fori_loop(0, seq // R, finish, 0)


def gla_pallas(proj_a, proj_go, proj_c, lr_col0, gate_w2, gate_b, norm_g, n_batch, seq, n_heads,
               dk, dv, q_col0, k_col0, v_col0, go_col0):
    kern = functools.partial(_gla_kernel, lr_col0=lr_col0)
    return pl.pallas_call(
        kern,
        grid=(n_batch, n_heads),
        in_specs=[
            pl.BlockSpec((seq, dk), lambda b, h: (b, q_col0 // dk + h)),
            pl.BlockSpec((seq, dk), lambda b, h: (b, k_col0 // dk + h)),
            pl.BlockSpec((seq, dv), lambda b, h: (b, v_col0 // dv + h)),
            pl.BlockSpec((seq, dv), lambda b, h: (b, go_col0 // dv + h)),
            pl.BlockSpec((seq, proj_c.shape[1]), lambda b, h: (b, 0)),
            pl.BlockSpec((GLA_RANK, dk), lambda b, h: (0, h)),
            pl.BlockSpec((1, dk), lambda b, h: (0, h)),
            pl.BlockSpec((1, dv), lambda b, h: (0, h)),
        ],
        out_specs=pl.BlockSpec((seq, dv), lambda b, h: (b, h)),
        out_shape=jax.ShapeDtypeStruct((n_batch * seq, n_heads * dv), jnp.bfloat16),
        scratch_shapes=[pltpu.VMEM((seq, dk), jnp.float32), pltpu.VMEM((seq, dv), jnp.float32)],
        compiler_params=pltpu.CompilerParams(
            dimension_semantics=("parallel", "parallel"), vmem_limit_bytes=VMEM_LIMIT),
    )(proj_a, proj_a, proj_a, proj_go, proj_c, gate_w2, gate_b.reshape(1, -1),
      norm_g.reshape(1, -1))


CONV_TILE = 256
CONV_HALO = 32


def _conf_kernel(a_ref, g_ref, w_ref, b_ref, ng_ref, nb_ref, o_ref, x_sc):
    seq, ch = a_ref.shape
    R, H = CONV_TILE, CONV_HALO
    x_sc[0:H, :] = jnp.zeros((H, ch), jnp.float32)
    x_sc[H:, :] = a_ref[...] * jax.nn.sigmoid(g_ref[...])

    def tile(i, carry):
        r0 = pl.multiple_of(i * R, R)
        win = x_sc[pl.ds(r0, R + H), :]
        acc = jnp.zeros((R, ch), jnp.float32) + b_ref[...]
        for sub in range(8):
            sh = pltpu.roll(win, sub, 0) if sub else win
            for blk in range(H // 8):
                off = 8 * blk + sub
                if off >= CONV_WIDTH:
                    continue
                j = CONV_WIDTH - 1 - off
                acc = acc + w_ref[j:j + 1, :] * sh[H - 8 * blk:H - 8 * blk + R, :]
        yc = acc - jnp.mean(acc, axis=1, keepdims=True)
        yn = (yc * lax.rsqrt(jnp.mean(yc * yc, axis=1, keepdims=True) + LN_EPS) * ng_ref[...]
              + nb_ref[...])
        o_ref[pl.ds(r0, R), :] = (yn * jax.nn.sigmoid(yn)).astype(o_ref.dtype)
        return carry

    lax.fori_loop(0, seq // R, tile, 0)


def conformer_conv_pallas(proj_a, a_col0, g_col0, dw_w, dw_b, norm_g, norm_b, n_batch, seq,
                          n_groups, width):
    ch = width // n_groups
    vec = pl.BlockSpec((1, ch), lambda b, g: (0, g))
    return pl.pallas_call(
        _conf_kernel,
        grid=(n_batch, n_groups),
        in_specs=[pl.BlockSpec((seq, ch), lambda b, g: (b, a_col0 // ch + g)),
                  pl.BlockSpec((seq, ch), lambda b, g: (b, g_col0 // ch + g)),
                  pl.BlockSpec((CONV_WIDTH, ch), lambda b, g: (0, g)),
                  vec, vec, vec],
        out_specs=pl.BlockSpec((seq, ch), lambda b, g: (b, g)),
        out_shape=jax.ShapeDtypeStruct((n_batch * seq, width), jnp.bfloat16),
        scratch_shapes=[pltpu.VMEM((seq + CONV_HALO, ch), jnp.float32)],
        compiler_params=pltpu.CompilerParams(
            dimension_semantics=("parallel", "parallel"), vmem_limit_bytes=VMEM_LIMIT),
    )(proj_a, proj_a, dw_w, dw_b.reshape(1, -1), norm_g.reshape(1, -1), norm_b.reshape(1, -1))


def hybrid_mixer(ub, B, S, w_in, b_merge, mlstm_gate_b, mlstm_conv_w, mlstm_conv_b, mlstm_norm_g,
                 conf_dw_w, conf_dw_b, conf_norm_g, conf_norm_b, rel_bias_t, moba_btab,
                 gla_gate_w2, gla_gate_b, gla_norm_g, w_branch):
    W = BRANCH_WIDTH
    bf16, f32 = jnp.bfloat16, jnp.float32
    offs = [0] + np.cumsum(IN_SIZES).tolist()
    w_1 = repack_columns(w_in, 0, offs[4])
    w_2 = repack_columns(w_in, offs[5], offs[10] - offs[5])
    w_3 = repack_columns(w_in, offs[11], offs[13] - offs[11])
    proj_ml = matmul(ub, w_1, 512, 1024, f32)
    proj_cf = matmul(ub, w_2, 512, 1024, f32, 0, 2 * W)
    proj_mb = matmul(ub, w_2, 512, 1024, bf16, 2 * W, 3 * W)
    proj_gl = matmul(ub, w_2, 512, 1024, f32, 5 * W, 2 * W)
    proj_go = matmul(ub, w_3, 512, 1024, f32, 0, W)
    proj_b = matmul(ub, w_3, 512, 1024, bf16, W)
    c_if = offs[4] - offs[4] % LANES
    c_lr = offs[10] - offs[10] % LANES
    proj_if = matmul_f32w(ub, w_in, 512, LANES, f32, c_if, LANES)
    proj_lr = matmul_f32w(ub, w_in, 512, LANES, f32, c_lr, LANES)

    gates = proj_if[:, offs[4] - c_if:offs[5] - c_if] + mlstm_gate_b
    y_mlstm = mlstm_pallas(proj_ml, gates, mlstm_conv_w, mlstm_conv_b, mlstm_norm_g, B, S,
                           MLSTM_HEADS, MLSTM_DQK, MLSTM_DV, 0, W // 2, W, 2 * W)
    y_conv = conformer_conv_pallas(proj_cf, 0, W, conf_dw_w, conf_dw_b, conf_norm_g, conf_norm_b,
                                   B, S, CONV_GROUPS, W)
    y_moba = moba_pallas(proj_mb, rel_bias_t, moba_btab, B, S, 0, MOBA_HEADS, 2 * MOBA_HEADS, bf16)
    y_gla = gla_pallas(proj_gl, proj_go, proj_lr, offs[10] - c_lr, gla_gate_w2, gla_gate_b,
                       gla_norm_g, B, S, GLA_HEADS, GLA_DK, GLA_DV, 0, W // 2, W, 0)
    return merge_branches([y_mlstm, y_conv, y_moba, y_gla], w_branch, proj_b, 0, b_merge)


def kernel(x, c, w_ada, b_ada, w_in, b_merge, mlstm_gate_b, mlstm_conv_w, mlstm_conv_b,
           mlstm_norm_g, conf_dw_w, conf_dw_b, conf_norm_g, conf_norm_b, rel_bias,
           gla_gate_w2, gla_gate_b, gla_norm_g, w_branch, w_out, ln1_g, ln1_b,
           router_w, router_b, exp_w_up, exp_b_up, exp_w_down, exp_b_down, ln2_g, ln2_b):
    B, S, D = x.shape
    bf16 = jnp.bfloat16
    mod_all = ada_modulation(c, w_ada, b_ada)
    mods = [jnp.split(mod_all[l][:, None, :], 6, axis=-1) for l in range(DEPTH)]
    rel_bias_t = rel_bias.T.astype(jnp.float32)
    moba_btab = moba_bias_tables(rel_bias_t)
    n_rows = B * S * TOP_K + N_EXPERTS * MOE_TILE
    x2d = x.reshape(B * S, D)
    ub = modulate(x2d, mods[0][1], mods[0][0], S)
    for l in range(DEPTH):
        sh1, sc1, g1, sh2, sc2, g2 = mods[l]
        merged = hybrid_mixer(ub, B, S, w_in[l], b_merge[l], mlstm_gate_b[l], mlstm_conv_w[l],
                              mlstm_conv_b[l], mlstm_norm_g[l], conf_dw_w[l], conf_dw_b[l],
                              conf_norm_g[l], conf_norm_b[l], rel_bias_t, moba_btab,
                              gla_gate_w2[l], gla_gate_b[l], gla_norm_g[l], w_branch[l])
        mix = matmul_f32w(merged, w_out[l], 512, 512, jnp.float32)
        x1, upk, top_idx, top_w, rank4, counts = ln_router(
            x2d, mix, g1, ln1_g[l], ln1_b[l], sc2, sh2, router_w[l].astype(bf16), router_b[l], S)
        dest, tile_expert, n_tiles = moe_layout(counts, top_idx, rank4, n_rows // MOE_TILE)
        xs = moe_dispatch(upk, dest, n_rows)
        ys = grouped_experts(xs, tile_expert, n_tiles, exp_w_up[l].astype(bf16), exp_b_up[l],
                             exp_w_down[l].astype(bf16), exp_b_down[l])
        nxt = min(l + 1, DEPTH - 1)
        x2d, ub = combine_ln(dest, top_w, x1, g2, ln2_g[l], ln2_b[l], mods[nxt][1], mods[nxt][0],
                             ys, S)
    return x2d.reshape(B, S, D)
```

```python
import functools
import math

import jax
import jax.numpy as jnp
import numpy as np
from jax import lax
from jax.experimental import pallas as pl
from jax.experimental.pallas import tpu as pltpu

D_MODEL = 4096
BATCH = 4
SEQ = 2048
DEPTH = 2

N_BRANCHES = 4
BRANCH_WIDTH = D_MODEL // N_BRANCHES
MLSTM_HEADS = 4
MLSTM_DV = BRANCH_WIDTH // MLSTM_HEADS
MLSTM_DQK = MLSTM_DV // 2
MLSTM_CONV = 4
MLSTM_CHUNK = 64
CONV_WIDTH = 31
CONV_GROUPS = 4
MOBA_HEADS = 8
MOBA_HD = BRANCH_WIDTH // MOBA_HEADS
MOBA_BLOCK = 256
MOBA_TOPK = 3
MOBA_Q_CHUNK = 32
REL_BUCKETS = 32
REL_MAX_DIST = 128
GLA_HEADS = 4
GLA_DV = BRANCH_WIDTH // GLA_HEADS
GLA_DK = GLA_DV // 2
GLA_RANK = 16
GLA_TAU = 16.0
GLA_CHUNK = 16
N_EXPERTS = 32
TOP_K = 4
EXPERT_FF = D_MODEL // 8
SWIGLU_ALPHA = 1.702
SWIGLU_LIMIT = 7.0
DEEPNORM_ALPHA = (2 * DEPTH) ** 0.25
LN_EPS = 1e-5

IN_SIZES = (
    MLSTM_HEADS * MLSTM_DQK, MLSTM_HEADS * MLSTM_DQK, BRANCH_WIDTH, BRANCH_WIDTH,
    2 * MLSTM_HEADS, 2 * BRANCH_WIDTH, 3 * BRANCH_WIDTH, GLA_HEADS * GLA_DK,
    GLA_HEADS * GLA_DK, BRANCH_WIDTH, GLA_RANK, BRANCH_WIDTH, N_BRANCHES * D_MODEL,
)
IN_WIDTH = sum(IN_SIZES)

VMEM_LIMIT = 56 * 1024 * 1024


def _mm_kernel(a_ref, b_ref, o_ref):
    o_ref[...] = jnp.dot(a_ref[...], b_ref[...],
                         preferred_element_type=jnp.float32).astype(o_ref.dtype)


def matmul(a, b, tm, tn, out_dtype, col0=0, ncols=None):
    M, K = a.shape
    N = b.shape[1] - col0 if ncols is None else ncols
    assert col0 % tn == 0 and N % tn == 0 and M % tm == 0
    jb = col0 // tn
    return pl.pallas_call(
        _mm_kernel,
        grid=(N // tn, M // tm),
        in_specs=[pl.BlockSpec((tm, K), lambda j, i: (i, 0)),
                  pl.BlockSpec((K, tn), lambda j, i: (0, jb + j))],
        out_specs=pl.BlockSpec((tm, tn), lambda j, i: (i, j)),
        out_shape=jax.ShapeDtypeStruct((M, N), out_dtype),
        compiler_params=pltpu.CompilerParams(
            dimension_semantics=("parallel", "parallel"), vmem_limit_bytes=VMEM_LIMIT),
    )(a, b)


def _mm_f32w_kernel(a_ref, b_ref, o_ref, wb_sc):
    @pl.when(pl.program_id(1) == 0)
    def _():
        wb_sc[...] = b_ref[...].astype(wb_sc.dtype)

    o_ref[...] = jnp.dot(a_ref[...], wb_sc[...],
                         preferred_element_type=jnp.float32).astype(o_ref.dtype)


def matmul_f32w(a, b, layer, tm, tn, out_dtype, col0=0, ncols=None):
    M, K = a.shape
    N = b.shape[2] - col0 if ncols is None else ncols
    assert col0 % tn == 0 and N % tn == 0 and M % tm == 0
    jb = col0 // tn
    return pl.pallas_call(
        _mm_f32w_kernel,
        grid=(N // tn, M // tm),
        in_specs=[pl.BlockSpec((tm, K), lambda j, i: (i, 0)),
                  pl.BlockSpec((None, K, tn), lambda j, i: (layer, 0, jb + j))],
        out_specs=pl.BlockSpec((tm, tn), lambda j, i: (i, j)),
        out_shape=jax.ShapeDtypeStruct((M, N), out_dtype),
        scratch_shapes=[pltpu.VMEM((K, tn), jnp.bfloat16)],
        compiler_params=pltpu.CompilerParams(
            dimension_semantics=("parallel", "arbitrary"), vmem_limit_bytes=VMEM_LIMIT),
    )(a, b)


LANES = 128
REPACK_ROWS = 1024
REPACK_COLS = 1024


def _repack_kernel(*refs, shift):
    if shift:
        main_ref, edge_ref, o_ref = refs
        o_ref[...] = jnp.concatenate([main_ref[:, shift:], edge_ref[:, :shift]],
                                     axis=1).astype(o_ref.dtype)
    else:
        main_ref, o_ref = refs
        o_ref[...] = main_ref[...].astype(o_ref.dtype)


def repack_columns(w, layer, col0, ncols):
    _, k, n = w.shape
    tk, tn = min(REPACK_ROWS, k), REPACK_COLS
    shift = col0 % LANES
    base = col0 - shift
    assert base % tn == 0 and ncols % tn == 0 and k % tk == 0
    in_specs = [pl.BlockSpec((None, tk, tn), lambda j, r: (layer, r, base // tn + j))]
    args = [w]
    if shift:
        in_specs.append(pl.BlockSpec((None, tk, LANES),
                                     lambda j, r: (layer, r, (base + (j + 1) * tn) // LANES)))
        args.append(w)
    return pl.pallas_call(
        functools.partial(_repack_kernel, shift=shift),
        grid=(ncols // tn, k // tk),
        in_specs=in_specs,
        out_specs=pl.BlockSpec((tk, tn), lambda j, r: (r, j)),
        out_shape=jax.ShapeDtypeStruct((k, ncols), jnp.bfloat16),
        compiler_params=pltpu.CompilerParams(dimension_semantics=("parallel", "parallel")),
    )(*args)


ROW_TILE = 256
MOE_TILE = 256


def _ada_kernel(c_ref, w_ref, b_ref, o_ref):
    c = c_ref[...]
    cs = (c * jax.nn.sigmoid(c)).astype(jnp.bfloat16)
    o_ref[0] = jnp.dot(cs, w_ref[0].astype(jnp.bfloat16),
                       preferred_element_type=jnp.float32) + b_ref[0]


def ada_modulation(c, w_ada, b_ada, tn=512):
    n_layers, d, n = w_ada.shape
    nb = c.shape[0]
    c8 = jnp.pad(c, ((0, 8 - nb), (0, 0)))
    out = pl.pallas_call(
        _ada_kernel,
        grid=(n_layers, n // tn),
        in_specs=[pl.BlockSpec((8, d), lambda l, j: (0, 0)),
                  pl.BlockSpec((1, d, tn), lambda l, j: (l, 0, j)),
                  pl.BlockSpec((1, 1, tn), lambda l, j: (l, 0, j))],
        out_specs=pl.BlockSpec((1, 8, tn), lambda l, j: (l, 0, j)),
        out_shape=jax.ShapeDtypeStruct((n_layers, 8, n), jnp.float32),
        compiler_params=pltpu.CompilerParams(
            dimension_semantics=("parallel", "parallel"), vmem_limit_bytes=VMEM_LIMIT),
    )(c8, w_ada, b_ada.reshape(n_layers, 1, n))
    return out[:, :nb]


def _modulate_kernel(x_ref, sc_ref, sh_ref, o_ref):
    o_ref[...] = (x_ref[...] * (1.0 + sc_ref[0]) + sh_ref[0]).astype(o_ref.dtype)


def modulate(x2d, sc, sh, seq):
    t, d = x2d.shape
    per_b = seq // ROW_TILE
    mod_spec = pl.BlockSpec((1, 1, d), lambda i: (i // per_b, 0, 0))
    return pl.pallas_call(
        _modulate_kernel,
        grid=(t // ROW_TILE,),
        in_specs=[pl.BlockSpec((ROW_TILE, d), lambda i: (i, 0)), mod_spec, mod_spec],
        out_specs=pl.BlockSpec((ROW_TILE, d), lambda i: (i, 0)),
        out_shape=jax.ShapeDtypeStruct((t, d), jnp.bfloat16),
        compiler_params=pltpu.CompilerParams(dimension_semantics=("parallel",)),
    )(x2d, sc, sh)


def _merge_kernel(y0, y1, y2, y3, wb_ref, g0, g1, g2, g3, bm_ref, o_ref, wb_sc):
    @pl.when(pl.program_id(1) == 0)
    def _():
        wb_sc[...] = wb_ref[...].astype(wb_sc.dtype)

    acc = None
    for n, (y, g) in enumerate(((y0, g0), (y1, g1), (y2, g2), (y3, g3))):
        gate = jax.nn.sigmoid(g[...].astype(jnp.float32) + bm_ref[n:n + 1, :])
        term = gate * jnp.dot(y[...], wb_sc[n], preferred_element_type=jnp.float32)
        acc = term if acc is None else acc + term
    o_ref[...] = acc.astype(o_ref.dtype)


def merge_branches(ys, w_branch, layer, proj_b, gate_col0, b_merge, tm=512, tn=512):
    t, w = ys[0].shape
    d = w_branch.shape[3]
    assert gate_col0 % tn == 0 and d % tn == 0
    y_spec = pl.BlockSpec((tm, w), lambda j, i: (i, 0))
    g_specs = [pl.BlockSpec((tm, tn), lambda j, i, n=n: (i, (gate_col0 + n * d) // tn + j))
               for n in range(N_BRANCHES)]
    return pl.pallas_call(
        _merge_kernel,
        grid=(d // tn, t // tm),
        in_specs=[y_spec] * 4 + [pl.BlockSpec((None, N_BRANCHES, w, tn),
                                              lambda j, i: (layer, 0, 0, j))]
        + g_specs + [pl.BlockSpec((N_BRANCHES, tn), lambda j, i: (0, j))],
        out_specs=pl.BlockSpec((tm, tn), lambda j, i: (i, j)),
        out_shape=jax.ShapeDtypeStruct((t, d), jnp.bfloat16),
        scratch_shapes=[pltpu.VMEM((N_BRANCHES, w, tn), jnp.bfloat16)],
        compiler_params=pltpu.CompilerParams(
            dimension_semantics=("parallel", "arbitrary"), vmem_limit_bytes=VMEM_LIMIT),
    )(*ys, w_branch, proj_b, proj_b, proj_b, proj_b, b_merge.reshape(N_BRANCHES, d))


def _pack_pair(lo, hi):
    def rne(x):
        b = lax.bitcast_convert_type(x, jnp.uint32)
        return (b + jnp.uint32(0x7FFF) + ((b >> 16) & jnp.uint32(1))) >> 16
    return rne(lo) | (rne(hi) << 16)


def _unpack_lo(w):
    return lax.bitcast_convert_type(w << 16, jnp.float32)


def _unpack_hi(w):
    return lax.bitcast_convert_type(w & jnp.uint32(0xFFFF0000), jnp.float32)


def _layer_norm_halves(z_lo, z_hi, g_ref, b_ref, half):
    d = 2 * half
    mu = (jnp.sum(z_lo, axis=1, keepdims=True) + jnp.sum(z_hi, axis=1, keepdims=True)) / d
    c_lo, c_hi = z_lo - mu, z_hi - mu
    var = (jnp.sum(c_lo * c_lo, axis=1, keepdims=True)
           + jnp.sum(c_hi * c_hi, axis=1, keepdims=True)) / d
    r = lax.rsqrt(var + LN_EPS)
    return (c_lo * r * g_ref[:, :half] + b_ref[:, :half],
            c_hi * r * g_ref[:, half:] + b_ref[:, half:])


def _ln_router_kernel(x_ref, mix_ref, g1_ref, lng_ref, lnb_ref, sc_ref, sh_ref, rw_ref, rb_ref,
                      x1_ref, upk_ref, tidx_ref, tw_ref, rank_ref, cnt_ref, carry_sc):
    tm, d = x_ref.shape
    half = d // 2
    n_exp = rw_ref.shape[1]

    @pl.when(pl.program_id(0) == 0)
    def _():
        carry_sc[...] = jnp.zeros_like(carry_sc)

    z = DEEPNORM_ALPHA * x_ref[...] + (1.0 + g1_ref[0]) * mix_ref[...]
    x_lo, x_hi = _layer_norm_halves(z[:, :half], z[:, half:], lng_ref, lnb_ref, half)
    x1_ref[:, :half] = x_lo
    x1_ref[:, half:] = x_hi
    u_lo = x_lo * (1.0 + sc_ref[0, :, :half]) + sh_ref[0, :, :half]
    u_hi = x_hi * (1.0 + sc_ref[0, :, half:]) + sh_ref[0, :, half:]
    upk_ref[...] = _pack_pair(u_lo, u_hi)

    logits = (jnp.dot(u_lo.astype(jnp.bfloat16), rw_ref[:half, :],
                      preferred_element_type=jnp.float32)
              + jnp.dot(u_hi.astype(jnp.bfloat16), rw_ref[half:, :],
                        preferred_element_type=jnp.float32) + rb_ref[...])
    lane = lax.broadcasted_iota(jnp.int32, (tm, n_exp), 1).astype(jnp.float32)
    lg = logits
    vals, hots = [], []
    for k in range(TOP_K):
        m = jnp.max(lg, axis=1, keepdims=True)
        ik = jnp.min(jnp.where(lg == m, lane, float(n_exp)), axis=1, keepdims=True)
        hot = lane == ik
        vals.append(m)
        hots.append(hot)
        tidx_ref[:, k:k + 1] = ik.astype(jnp.int32)
        lg = jnp.where(hot, -jnp.inf, lg)
    exps = [jnp.exp(v - vals[0]) for v in vals]
    den = exps[0] + exps[1] + exps[2] + exps[3]
    for k in range(TOP_K):
        tw_ref[:, k:k + 1] = exps[k] / den

    mask = jnp.where(hots[0] | hots[1] | hots[2] | hots[3], 1.0, 0.0)
    row = lax.broadcasted_iota(jnp.int32, (tm, tm), 0)
    col = lax.broadcasted_iota(jnp.int32, (tm, tm), 1)
    lower = jnp.where(row > col, 1.0, 0.0).astype(jnp.bfloat16)
    rank = jnp.dot(lower, mask.astype(jnp.bfloat16),
                   preferred_element_type=jnp.float32) + carry_sc[...]
    for k in range(TOP_K):
        rank_ref[:, k:k + 1] = jnp.sum(jnp.where(hots[k], rank, 0.0), axis=1,
                                       keepdims=True).astype(jnp.int32)
    carry_sc[...] = carry_sc[...] + jnp.sum(mask, axis=0, keepdims=True)
    cnt_ref[...] = carry_sc[...]


def ln_router(x2d, mix, g1, ln_g, ln_b, sc2, sh2, router_w_bf, router_b, seq):
    t, d = x2d.shape
    n_exp = router_w_bf.shape[1]
    per_b = seq // ROW_TILE
    row = lambda w: pl.BlockSpec((ROW_TILE, w), lambda i: (i, 0))
    mod = pl.BlockSpec((1, 1, d), lambda i: (i // per_b, 0, 0))
    vec = lambda w: pl.BlockSpec((1, w), lambda i: (0, 0))
    return pl.pallas_call(
        _ln_router_kernel,
        grid=(t // ROW_TILE,),
        in_specs=[row(d), row(d), mod, vec(d), vec(d), mod, mod,
                  pl.BlockSpec((d, n_exp), lambda i: (0, 0)), vec(n_exp)],
        out_specs=[row(d), row(d // 2), row(TOP_K), row(TOP_K), row(TOP_K), vec(n_exp)],
        out_shape=[jax.ShapeDtypeStruct((t, d), jnp.float32),
                   jax.ShapeDtypeStruct((t, d // 2), jnp.uint32),
                   jax.ShapeDtypeStruct((t, TOP_K), jnp.int32),
                   jax.ShapeDtypeStruct((t, TOP_K), jnp.float32),
                   jax.ShapeDtypeStruct((t, TOP_K), jnp.int32),
                   jax.ShapeDtypeStruct((1, n_exp), jnp.float32)],
        scratch_shapes=[pltpu.VMEM((1, n_exp), jnp.float32)],
        compiler_params=pltpu.CompilerParams(
            dimension_semantics=("arbitrary",), vmem_limit_bytes=VMEM_LIMIT),
    )(x2d, mix, g1, ln_g.reshape(1, d), ln_b.reshape(1, d), sc2, sh2, router_w_bf,
      router_b.reshape(1, n_exp))


def _row_copy(src_hbm, src_row, dst, dst_row, sem):
    return pltpu.make_async_copy(src_hbm.at[pl.ds(src_row, 1)], dst.at[pl.ds(dst_row, 1)], sem)


def _dispatch_kernel(dest_ref, u_ref, xs_in_hbm, xs_hbm, sem):
    del xs_in_hbm

    def issue(r, carry):
        for k in range(TOP_K):
            _row_copy(u_ref, r, xs_hbm, dest_ref[r * TOP_K + k], sem).start(priority=k % 2)
        return carry

    def drain(r, carry):
        for k in range(TOP_K):
            _row_copy(u_ref, 0, xs_hbm, 0, sem).wait()
        return carry

    lax.fori_loop(0, ROW_TILE, issue, 0)
    lax.fori_loop(0, ROW_TILE, drain, 0)


def moe_dispatch(upk, dest_flat, n_rows):
    t, w = upk.shape
    xs0 = jnp.zeros((n_rows, w), upk.dtype)
    return pl.pallas_call(
        _dispatch_kernel,
        grid=(t // ROW_TILE,),
        in_specs=[pl.BlockSpec((ROW_TILE * TOP_K,), lambda i: (i,), memory_space=pltpu.SMEM),
                  pl.BlockSpec((ROW_TILE, w), lambda i: (i, 0)),
                  pl.BlockSpec(memory_space=pl.ANY)],
        out_specs=pl.BlockSpec(memory_space=pl.ANY),
        out_shape=jax.ShapeDtypeStruct((n_rows, w), upk.dtype),
        scratch_shapes=[pltpu.SemaphoreType.DMA(())],
        input_output_aliases={2: 0},
        compiler_params=pltpu.CompilerParams(dimension_semantics=("arbitrary",)),
    )(dest_flat, upk, xs0)


UP_CHUNKS = 4


def _expert_kernel(te_ref, nt_ref, xs_ref, wu_hbm, bu_ref, wd_hbm, bd_ref, ys_ref,
                   stage_u, stage_d, wu_sc, wd_sc, sem, *, layer):
    g = pl.program_id(0)
    half = xs_ref.shape[1]
    ff = wd_sc.shape[0]
    rows = stage_u.shape[1]
    e = te_ref[g]
    new_expert = jnp.logical_or(g == 0, e != te_ref[jnp.maximum(g - 1, 0)])

    @pl.when(jnp.logical_and(g < nt_ref[0], new_expert))
    def _():
        def up_copy(c):
            return pltpu.make_async_copy(wu_hbm.at[layer, e, pl.ds(c * rows, rows)],
                                         stage_u.at[c % 2], sem.at[c % 2])
        down_copy = pltpu.make_async_copy(wd_hbm.at[layer, e], stage_d, sem.at[2])
        up_copy(0).start()
        up_copy(1).start()
        down_copy.start()
        for c in range(UP_CHUNKS):
            up_copy(c).wait()
            wu_sc[c * rows:(c + 1) * rows, :] = stage_u[c % 2].astype(wu_sc.dtype)
            if c + 2 < UP_CHUNKS:
                up_copy(c + 2).start()
        down_copy.wait()
        wd_sc[...] = stage_d[...].astype(wd_sc.dtype)

    @pl.when(g < nt_ref[0])
    def _():
        w = xs_ref[...]
        x_lo = _unpack_lo(w).astype(jnp.bfloat16)
        x_hi = _unpack_hi(w).astype(jnp.bfloat16)
        hid = (jnp.dot(x_lo, wu_sc[:half, :], preferred_element_type=jnp.float32)
               + jnp.dot(x_hi, wu_sc[half:, :], preferred_element_type=jnp.float32)
               + bu_ref[0, 0])
        glu = jnp.minimum(hid[:, :ff], SWIGLU_LIMIT)
        lin = jnp.clip(hid[:, ff:], -SWIGLU_LIMIT, SWIGLU_LIMIT)
        act = glu * jax.nn.sigmoid(SWIGLU_ALPHA * glu) * (lin + 1.0)
        y = jnp.dot(act.astype(jnp.bfloat16), wd_sc[...],
                    preferred_element_type=jnp.float32) + bd_ref[0, 0]
        ys_ref[...] = _pack_pair(y[:, :half], y[:, half:])

    @pl.when(g >= nt_ref[0])
    def _():
        ys_ref[...] = jnp.zeros_like(ys_ref)


def grouped_experts(xs, tile_expert, n_tiles, w_up, b_up, w_down, b_down, layer):
    n_rows, half = xs.shape
    _, n_exp, d, ff2 = w_up.shape
    ff = ff2 // 2
    grid_spec = pltpu.PrefetchScalarGridSpec(
        num_scalar_prefetch=2,
        grid=(n_rows // MOE_TILE,),
        in_specs=[pl.BlockSpec((MOE_TILE, half), lambda g, te, nt: (g, 0)),
                  pl.BlockSpec(memory_space=pl.ANY),
                  pl.BlockSpec((1, 1, 1, ff2), lambda g, te, nt: (layer, te[g], 0, 0)),
                  pl.BlockSpec(memory_space=pl.ANY),
                  pl.BlockSpec((1, 1, 1, d), lambda g, te, nt: (layer, te[g], 0, 0))],
        out_specs=pl.BlockSpec((MOE_TILE, half), lambda g, te, nt: (g, 0)),
        scratch_shapes=[pltpu.VMEM((2, d // UP_CHUNKS, ff2), jnp.float32),
                        pltpu.VMEM((ff, d), jnp.float32),
                        pltpu.VMEM((d, ff2), jnp.bfloat16),
                        pltpu.VMEM((ff, d), jnp.bfloat16),
                        pltpu.SemaphoreType.DMA((3,))],
    )
    return pl.pallas_call(
        functools.partial(_expert_kernel, layer=layer),
        grid_spec=grid_spec,
        out_shape=jax.ShapeDtypeStruct((n_rows, half), jnp.uint32),
        compiler_params=pltpu.CompilerParams(
            dimension_semantics=("arbitrary",), vmem_limit_bytes=VMEM_LIMIT),
    )(tile_expert, n_tiles, xs, w_up, b_up.reshape(b_up.shape[0], n_exp, 1, ff2), w_down,
      b_down.reshape(b_down.shape[0], n_exp, 1, d))


COMBINE_ROWS = 16


def _combine_kernel(dest_ref, dest_next_ref, tw_ref, x1_ref, g2_ref, lng_ref, lnb_ref, sc_ref,
                    sh_ref, ys_hbm, x2_ref, un_ref, buf, sem):
    tm, d = x1_ref.shape
    half = d // 2
    i = pl.program_id(0)
    slot = lax.rem(i, 2)

    has_next = i + 1 < pl.num_programs(0)
    sub = COMBINE_ROWS

    def gather_rows(d_ref, s, r0):
        for rr in range(sub):
            for k in range(TOP_K):
                _row_copy(ys_hbm, d_ref[(r0 + rr) * TOP_K + k], buf.at[s, k], r0 + rr,
                          sem.at[s]).start(priority=k % 2)

    @pl.when(i == 0)
    def _():
        def first(t, carry):
            gather_rows(dest_ref, 0, t * sub)
            return carry
        lax.fori_loop(0, tm // sub, first, 0)

    def drain(r, carry):
        for k in range(TOP_K):
            _row_copy(ys_hbm, 0, buf.at[slot, k], 0, sem.at[slot]).wait()
        return carry

    lax.fori_loop(0, tm, drain, 0)

    def rows(t, carry):
        r0 = pl.multiple_of(t * sub, sub)
        rs = pl.ds(r0, sub)

        gather_rows(dest_next_ref, 1 - slot, r0)

        f_lo = f_hi = None
        for k in range(TOP_K):
            wk = tw_ref[rs, k:k + 1]
            w = buf[slot, k, rs, :]
            t_lo, t_hi = wk * _unpack_lo(w), wk * _unpack_hi(w)
            f_lo = t_lo if f_lo is None else f_lo + t_lo
            f_hi = t_hi if f_hi is None else f_hi + t_hi
        z_lo = DEEPNORM_ALPHA * x1_ref[rs, :half] + (1.0 + g2_ref[0, :, :half]) * f_lo
        z_hi = DEEPNORM_ALPHA * x1_ref[rs, half:] + (1.0 + g2_ref[0, :, half:]) * f_hi
        x_lo, x_hi = _layer_norm_halves(z_lo, z_hi, lng_ref, lnb_ref, half)
        x2_ref[rs, :half] = x_lo
        x2_ref[rs, half:] = x_hi
        un_ref[rs, :half] = (x_lo * (1.0 + sc_ref[0, :, :half])
                             + sh_ref[0, :, :half]).astype(un_ref.dtype)
        un_ref[rs, half:] = (x_hi * (1.0 + sc_ref[0, :, half:])
                             + sh_ref[0, :, half:]).astype(un_ref.dtype)
        return carry

    lax.fori_loop(0, tm // sub, rows, 0)

    @pl.when(jnp.logical_not(has_next))
    def _():
        def drain_other(r, carry):
            for k in range(TOP_K):
                _row_copy(ys_hbm, 0, buf.at[1 - slot, k], 0, sem.at[1 - slot]).wait()
            return carry
        lax.fori_loop(0, tm, drain_other, 0)


def combine_ln(dest_flat, top_w, x1, g2, ln_g, ln_b, sc_next, sh_next, ys, seq):
    t, d = x1.shape
    per_b = seq // ROW_TILE
    n_steps = t // ROW_TILE
    row = lambda w: pl.BlockSpec((ROW_TILE, w), lambda i: (i, 0))
    mod = pl.BlockSpec((1, 1, d), lambda i: (i // per_b, 0, 0))
    vec = pl.BlockSpec((1, d), lambda i: (0, 0))
    return pl.pallas_call(
        _combine_kernel,
        grid=(n_steps,),
        in_specs=[pl.BlockSpec((ROW_TILE * TOP_K,), lambda i: (i,), memory_space=pltpu.SMEM),
                  pl.BlockSpec((ROW_TILE * TOP_K,), lambda i: (jnp.minimum(i + 1, n_steps - 1),),
                               memory_space=pltpu.SMEM),
                  row(TOP_K), row(d), mod, vec, vec, mod, mod,
                  pl.BlockSpec(memory_space=pl.ANY)],
        out_specs=[row(d), row(d)],
        out_shape=[jax.ShapeDtypeStruct((t, d), jnp.float32),
                   jax.ShapeDtypeStruct((t, d), jnp.bfloat16)],
        scratch_shapes=[pltpu.VMEM((2, TOP_K, ROW_TILE, d // 2), jnp.uint32),
                        pltpu.SemaphoreType.DMA((2,))],
        compiler_params=pltpu.CompilerParams(
            dimension_semantics=("arbitrary",), vmem_limit_bytes=VMEM_LIMIT),
    )(dest_flat, dest_flat, top_w, x1, g2, ln_g.reshape(1, d), ln_b.reshape(1, d), sc_next,
      sh_next, ys)


def moe_layout(counts, top_idx, rank4, n_tiles_max):
    cnt = counts.reshape(-1).astype(jnp.int32)
    tiles = (cnt + MOE_TILE - 1) // MOE_TILE
    tile_end = jnp.cumsum(tiles)
    row_start = (tile_end - tiles) * MOE_TILE
    dest = (row_start[top_idx] + rank4).reshape(-1)
    tile_expert = jnp.minimum(
        jnp.sum(jnp.arange(n_tiles_max)[:, None] >= tile_end[None, :], axis=1),
        cnt.shape[0] - 1).astype(jnp.int32)
    return dest.astype(jnp.int32), tile_expert, tile_end[-1:].astype(jnp.int32)


NEG_BIG = -1e30


def _t5_bucket_np(d):
    max_exact = REL_BUCKETS // 2
    dd = np.maximum(d, 1).astype(np.float32)
    far = max_exact + (np.log(dd / np.float32(max_exact))
                       / np.float32(math.log(REL_MAX_DIST / max_exact))
                       * np.float32(REL_BUCKETS - max_exact)).astype(np.int32)
    return np.where(d < max_exact, d, np.minimum(far, REL_BUCKETS - 1)).astype(np.int32)


def _moba_bucket_tables():
    c = np.arange(MOBA_BLOCK)[:, None]
    r = np.arange(MOBA_BLOCK)[None, :]
    own = np.where(r >= c, _t5_bucket_np(np.maximum(r - c, 0)), -1)
    prev = _t5_bucket_np(MOBA_BLOCK + r - c)
    return np.stack([own, prev]).astype(np.int32)


def _bias_tab_kernel(rb_ref, bk_ref, o_ref):
    h = pl.program_id(0)
    bk = bk_ref[...]
    acc = jnp.zeros(bk.shape, jnp.float32)
    for m in range(REL_BUCKETS):
        acc = jnp.where(bk == m, rb_ref[h, m], acc)
    o_ref[0] = jnp.where(bk < 0, NEG_BIG, acc)


def moba_bias_tables(rel_bias_t):
    n_heads = rel_bias_t.shape[0]
    bk = jnp.asarray(_moba_bucket_tables())
    return pl.pallas_call(
        _bias_tab_kernel,
        grid=(n_heads,),
        in_specs=[pl.BlockSpec(memory_space=pltpu.SMEM),
                  pl.BlockSpec((2, MOBA_BLOCK, MOBA_BLOCK), lambda h: (0, 0, 0))],
        out_specs=pl.BlockSpec((1, 2, MOBA_BLOCK, MOBA_BLOCK), lambda h: (h, 0, 0, 0)),
        out_shape=jax.ShapeDtypeStruct((n_heads, 2, MOBA_BLOCK, MOBA_BLOCK), jnp.float32),
    )(rel_bias_t, bk)


def _moba_kernel(rb_ref, q_ref, k_ref, v_ref, bt_ref, o_ref, km_sc, vt_sc, m_sc, l_sc, acc_sc,
                 *, n_blk):
    h = pl.program_id(1)
    i = pl.program_id(2)
    blk = MOBA_BLOCK
    seq = n_blk * blk
    scale = MOBA_HD ** -0.5
    nt = (((1,), (1,)), ((), ()))
    bf16 = jnp.bfloat16

    @pl.when(i == 0)
    def _():
        blk_of_col = lax.broadcasted_iota(jnp.int32, (n_blk, seq), 1) // blk
        blk_row = lax.broadcasted_iota(jnp.int32, (n_blk, seq), 0)
        avg = jnp.where(blk_of_col == blk_row, 1.0 / blk, 0.0).astype(bf16)
        km_sc[...] = jnp.dot(avg, k_ref[...], preferred_element_type=jnp.float32)
        for j in range(n_blk):
            vt_sc[j] = v_ref[j * blk:(j + 1) * blk, :].astype(jnp.float32).T.astype(bf16)

    q = q_ref[...]
    kmean = km_sc[...]
    km_hi = kmean.astype(bf16)
    km_lo = (kmean - km_hi.astype(jnp.float32)).astype(bf16)
    gate = (lax.dot_general(km_hi, q, nt, preferred_element_type=jnp.float32)
            + lax.dot_general(km_lo, q, nt, preferred_element_type=jnp.float32))
    g = [gate[n:n + 1, :] for n in range(n_blk)]

    def selected(j):
        cnt = jnp.zeros((1, blk), jnp.float32)
        for n in range(n_blk - 1):
            if n == j:
                continue
            ahead = (g[n] >= g[j]) if n < j else (g[n] > g[j])
            cnt = cnt + jnp.where(ahead, jnp.where(n < i, 1.0, 0.0), 0.0)
        return cnt < float(MOBA_TOPK)

    row0 = pl.multiple_of(i * blk, blk)
    k_own = k_ref[pl.ds(row0, blk), :]
    s = lax.dot_general(k_own, q, nt, preferred_element_type=jnp.float32) * scale + bt_ref[0, 0]
    m0 = jnp.max(s, axis=0, keepdims=True)
    p = jnp.exp(s - m0)
    m_sc[...] = m0
    l_sc[...] = jnp.sum(p, axis=0, keepdims=True)
    acc_sc[...] = jnp.dot(vt_sc[i], p.astype(bf16), preferred_element_type=jnp.float32)

    bfar = rb_ref[h, REL_BUCKETS - 1]
    for j in range(n_blk - 1):
        @pl.when(j < i)
        def _():
            k_j = k_ref[j * blk:(j + 1) * blk, :]
            bias = jnp.where(j == i - 1, bt_ref[0, 1], bfar)
            s = lax.dot_general(k_j, q, nt, preferred_element_type=jnp.float32) * scale + bias
            s = jnp.where(selected(j), s, NEG_BIG)
            m_old = m_sc[...]
            m_new = jnp.maximum(m_old, jnp.max(s, axis=0, keepdims=True))
            alpha = jnp.exp(m_old - m_new)
            p = jnp.exp(s - m_new)
            m_sc[...] = m_new
            l_sc[...] = alpha * l_sc[...] + jnp.sum(p, axis=0, keepdims=True)
            acc_sc[...] = alpha * acc_sc[...] + jnp.dot(
                vt_sc[j], p.astype(bf16), preferred_element_type=jnp.float32)

    o_ref[...] = (acc_sc[...] / l_sc[...]).T.astype(o_ref.dtype)


def moba_pallas(qkv, rel_bias_t, btab, n_batch, seq, q_blk0, k_blk0, v_blk0, out_dtype):
    n_heads = rel_bias_t.shape[0]
    n_blk = seq // MOBA_BLOCK
    assert seq % MOBA_BLOCK == 0
    kern = functools.partial(_moba_kernel, n_blk=n_blk)
    return pl.pallas_call(
        kern,
        grid=(n_batch, n_heads, n_blk),
        in_specs=[
            pl.BlockSpec(memory_space=pltpu.SMEM),
            pl.BlockSpec((MOBA_BLOCK, MOBA_HD), lambda b, h, i: (b * n_blk + i, q_blk0 + h)),
            pl.BlockSpec((seq, MOBA_HD), lambda b, h, i: (b, k_blk0 + h)),
            pl.BlockSpec((seq, MOBA_HD), lambda b, h, i: (b, v_blk0 + h)),
            pl.BlockSpec((1, 2, MOBA_BLOCK, MOBA_BLOCK), lambda b, h, i: (h, 0, 0, 0)),
        ],
        out_specs=pl.BlockSpec((MOBA_BLOCK, MOBA_HD), lambda b, h, i: (b * n_blk + i, h)),
        out_shape=jax.ShapeDtypeStruct((n_batch * seq, n_heads * MOBA_HD), out_dtype),
        scratch_shapes=[pltpu.VMEM((n_blk, MOBA_HD), jnp.float32),
                        pltpu.VMEM((n_blk, MOBA_HD, MOBA_BLOCK), jnp.bfloat16),
                        pltpu.VMEM((1, MOBA_BLOCK), jnp.float32),
                        pltpu.VMEM((1, MOBA_BLOCK), jnp.float32),
                        pltpu.VMEM((MOBA_HD, MOBA_BLOCK), jnp.float32)],
        compiler_params=pltpu.CompilerParams(
            dimension_semantics=("parallel", "parallel", "arbitrary")),
    )(rel_bias_t, qkv, qkv, qkv, btab)


def _log_sigmoid(x):
    return jnp.minimum(x, 0.0) - jnp.log(1.0 + jnp.exp(-jnp.abs(x)))


def _causal_shift(x, shift):
    if shift == 0:
        return x
    row = lax.broadcasted_iota(jnp.int32, x.shape, 0)
    return jnp.where(row >= shift, pltpu.roll(x, shift, 0), 0.0)


def _mlstm_kernel(q_ref, k_ref, v_ref, og_ref, gc_ref, gr_ref, cwq_ref, cwk_ref, cbq_ref, cbk_ref,
                  ng_ref, o_ref, qc_sc, kc_sc):
    seq, dqk = q_ref.shape
    dv = v_ref.shape[1]
    L = MLSTM_CHUNK
    nt = (((1,), (1,)), ((), ()))

    def conv_silu(x, w_ref, b_ref):
        y = b_ref[...]
        for j in range(MLSTM_CONV):
            y = y + w_ref[j:j + 1, :] * _causal_shift(x, MLSTM_CONV - 1 - j)
        return y * jax.nn.sigmoid(y)

    qc_sc[...] = conv_silu(q_ref[...], cwq_ref, cbq_ref).astype(qc_sc.dtype)
    kc_sc[...] = (conv_silu(k_ref[...], cwk_ref, cbk_ref) * dqk ** -0.5).astype(kc_sc.dtype)

    t_i = lax.broadcasted_iota(jnp.int32, (L, L), 0)
    s_i = lax.broadcasted_iota(jnp.int32, (L, L), 1)
    causal = s_i <= t_i

    def chunk(c, carry):
        c_st, n_st, m_st = carry
        r0 = pl.multiple_of(c * L, L)
        q = qc_sc[pl.ds(r0, L), :]
        k = kc_sc[pl.ds(r0, L), :]
        v = v_ref[pl.ds(r0, L), :].astype(jnp.bfloat16)
        i_col = gc_ref[0, 0, pl.ds(r0, L), :]
        lf_col = _log_sigmoid(gc_ref[1, 0, pl.ds(r0, L), :])
        i_row = gr_ref[0, 0, 0, pl.ds(c, 1), :]
        lf_row = _log_sigmoid(gr_ref[0, 1, 0, pl.ds(c, 1), :])
        bcum_col = jnp.sum(jnp.where(causal, lf_row, 0.0), axis=1, keepdims=True)
        bcum_row = jnp.sum(jnp.where(t_i <= s_i, lf_col, 0.0), axis=0, keepdims=True)
        b_last = bcum_row[:, L - 1:L]
        dmat = jnp.where(causal, bcum_col - bcum_row + i_row, NEG_BIG)
        a_max = jnp.max(b_last - bcum_row + i_row, axis=1, keepdims=True)
        w_end = jnp.exp(b_last - bcum_col + i_col - a_max)
        inter = bcum_col + m_st
        m_t = jnp.maximum(inter, jnp.max(dmat, axis=1, keepdims=True))
        s_qk = lax.dot_general(q, k, nt, preferred_element_type=jnp.float32) * jnp.exp(dmat - m_t)
        w_inter = jnp.exp(inter - m_t)
        num = (jnp.dot(s_qk.astype(jnp.bfloat16), v, preferred_element_type=jnp.float32)
               + w_inter * jnp.dot(q, c_st.astype(jnp.bfloat16),
                                   preferred_element_type=jnp.float32))
        den = (jnp.sum(s_qk, axis=1, keepdims=True)
               + w_inter * jnp.sum(q.astype(jnp.float32) * n_st, axis=1, keepdims=True))
        h = num / jnp.maximum(jnp.abs(den), jnp.exp(-m_t))
        hc = h - jnp.mean(h, axis=1, keepdims=True)
        hn = hc * lax.rsqrt(jnp.mean(hc * hc, axis=1, keepdims=True) + LN_EPS) * ng_ref[...]
        o_ref[pl.ds(r0, L), :] = (jax.nn.sigmoid(og_ref[pl.ds(r0, L), :]) * hn).astype(o_ref.dtype)
        m_new = jnp.maximum(b_last + m_st, a_max)
        f_sc = jnp.exp(b_last + m_st - m_new)
        i_sc = jnp.exp(a_max - m_new)
        kw = k.astype(jnp.float32) * w_end
        kv = jnp.dot(kw.T.astype(jnp.bfloat16), v, preferred_element_type=jnp.float32)
        c_new = f_sc * c_st + i_sc * kv
        n_new = f_sc * n_st + i_sc * jnp.sum(kw, axis=0, keepdims=True)
        return c_new, n_new, m_new

    init = (jnp.zeros((dqk, dv), jnp.float32), jnp.zeros((1, dqk), jnp.float32),
            jnp.zeros((1, 1), jnp.float32))
    lax.fori_loop(0, seq // L, chunk, init, unroll=4)


def mlstm_pallas(proj_a, gates, conv_w, conv_b, norm_g, n_batch, seq, n_heads, dqk, dv,
                 q_col0, k_col0, v_col0, o_col0):
    nc = seq // MLSTM_CHUNK
    g_col = gates.T.reshape(2, n_heads, n_batch * seq, 1)
    g_row = jnp.moveaxis(gates.reshape(n_batch, nc, MLSTM_CHUNK, 2, n_heads), (3, 4), (1, 2))
    return pl.pallas_call(
        _mlstm_kernel,
        grid=(n_batch, n_heads),
        in_specs=[
            pl.BlockSpec((seq, dqk), lambda b, h: (b, q_col0 // dqk + h)),
            pl.BlockSpec((seq, dqk), lambda b, h: (b, k_col0 // dqk + h)),
            pl.BlockSpec((seq, dv), lambda b, h: (b, v_col0 // dv + h)),
            pl.BlockSpec((seq, dv), lambda b, h: (b, o_col0 // dv + h)),
            pl.BlockSpec((2, 1, seq, 1), lambda b, h: (0, h, b, 0)),
            pl.BlockSpec((1, 2, 1, nc, MLSTM_CHUNK), lambda b, h: (b, 0, h, 0, 0)),
            pl.BlockSpec((MLSTM_CONV, dqk), lambda b, h: (0, h)),
            pl.BlockSpec((MLSTM_CONV, dqk), lambda b, h: (0, n_heads + h)),
            pl.BlockSpec((1, dqk), lambda b, h: (0, h)),
            pl.BlockSpec((1, dqk), lambda b, h: (0, n_heads + h)),
            pl.BlockSpec((1, dv), lambda b, h: (0, h)),
        ],
        out_specs=pl.BlockSpec((seq, dv), lambda b, h: (b, h)),
        out_shape=jax.ShapeDtypeStruct((n_batch * seq, n_heads * dv), jnp.bfloat16),
        scratch_shapes=[pltpu.VMEM((seq, dqk), jnp.bfloat16),
                        pltpu.VMEM((seq, dqk), jnp.bfloat16)],
        compiler_params=pltpu.CompilerParams(
            dimension_semantics=("parallel", "parallel"), vmem_limit_bytes=VMEM_LIMIT),
    )(proj_a, proj_a, proj_a, proj_a, g_col, g_row, conv_w, conv_w,
      conv_b.reshape(1, -1), conv_b.reshape(1, -1), norm_g.reshape(1, -1))


GLA_TILE = 256


def _gla_kernel(q_ref, k_ref, v_ref, go_ref, lr_ref, w2_ref, gb_ref, ng_ref, o_ref, bc_sc, oi_sc,
                *, lr_col0):
    seq, dk = q_ref.shape
    dv = v_ref.shape[1]
    R, C = GLA_TILE, GLA_CHUNK
    scale = dk ** -0.5
    nt = (((1,), (1,)), ((), ()))
    bf16 = jnp.bfloat16
    row = lax.broadcasted_iota(jnp.int32, (R, R), 0)
    col = lax.broadcasted_iota(jnp.int32, (R, R), 1)
    lag = jnp.where((row // C) == (col // C), jnp.where(col <= row, row - col, -1), -1)
    tril16 = jnp.where(lag >= 0, 1.0, 0.0).astype(bf16)
    ones = jnp.ones((dk, R), bf16)
    w2 = w2_ref[...].astype(bf16)

    def tile(i, carry):
        r0 = pl.multiple_of(i * R, R)
        z = jnp.dot(lr_ref[pl.ds(r0, R), lr_col0:lr_col0 + GLA_RANK].astype(bf16), w2,
                    preferred_element_type=jnp.float32) + gb_ref[...]
        la = _log_sigmoid(z) * (1.0 / GLA_TAU)
        la_hi = la.astype(bf16)
        la_lo = (la - la_hi.astype(jnp.float32)).astype(bf16)
        bc = (jnp.dot(tril16, la_hi, preferred_element_type=jnp.float32)
              + jnp.dot(tril16, la_lo, preferred_element_type=jnp.float32))
        bc_sc[pl.ds(r0, R), :] = bc
        q = q_ref[pl.ds(r0, R), :] * scale
        k = k_ref[pl.ds(r0, R), :]
        attn = jnp.zeros((R, R), jnp.float32)
        for d in range(C):
            if d == 0:
                prod = q * k
            else:
                prod = q * pltpu.roll(k, d, 0) * jnp.exp(bc - pltpu.roll(bc, d, 0))
            a_d = jnp.dot(prod.astype(bf16), ones, preferred_element_type=jnp.float32)
            attn = jnp.where(lag == d, a_d, attn)
        oi_sc[pl.ds(r0, R), :] = jnp.dot(attn.astype(bf16), v_ref[pl.ds(r0, R), :].astype(bf16),
                                         preferred_element_type=jnp.float32)
        return carry

    lax.fori_loop(0, seq // R, tile, 0)

    def chunk(c, st):
        r0 = pl.multiple_of(c * C, C)
        bc = bc_sc[pl.ds(r0, C), :]
        bl = bc[C - 1:C, :]
        qt = (q_ref[pl.ds(r0, C), :] * scale * jnp.exp(bc)).astype(bf16)
        kt = (k_ref[pl.ds(r0, C), :] * jnp.exp(bl - bc)).astype(bf16)
        oi_sc[pl.ds(r0, C), :] += lax.dot_general(qt, st.astype(bf16), nt,
                                                  preferred_element_type=jnp.float32)
        vt = v_ref[pl.ds(r0, C), :].T.astype(bf16)
        return jnp.exp(bl) * st + jnp.dot(vt, kt, preferred_element_type=jnp.float32)

    lax.fori_loop(0, seq // C, chunk, jnp.zeros((dv, dk), jnp.float32), unroll=16)

    def finish(i, carry):
        r0 = pl.multiple_of(i * R, R)
        o = oi_sc[pl.ds(r0, R), :]
        on = o * lax.rsqrt(jnp.mean(o * o, axis=1, keepdims=True) + LN_EPS) * ng_ref[...]
        g = go_ref[pl.ds(r0, R), :]
        o_ref[pl.ds(r0, R), :] = (on * (g * jax.nn.sigmoid(g))).astype(o_ref.dtype)
        return carry

    lax.fori_loop(0, seq // R, finish, 0)


def gla_pallas(proj_a, proj_go, proj_c, lr_col0, gate_w2, gate_b, norm_g, n_batch, seq, n_heads,
               dk, dv, q_col0, k_col0, v_col0, go_col0):
    kern = functools.partial(_gla_kernel, lr_col0=lr_col0)
    return pl.pallas_call(
        kern,
        grid=(n_batch, n_heads),
        in_specs=[
            pl.BlockSpec((seq, dk), lambda b, h: (b, q_col0 // dk + h)),
            pl.BlockSpec((seq, dk), lambda b, h: (b, k_col0 // dk + h)),
            pl.BlockSpec((seq, dv), lambda b, h: (b, v_col0 // dv + h)),
            pl.BlockSpec((seq, dv), lambda b, h: (b, go_col0 // dv + h)),
            pl.BlockSpec((seq, proj_c.shape[1]), lambda b, h: (b, 0)),
            pl.BlockSpec((GLA_RANK, dk), lambda b, h: (0, h)),
            pl.BlockSpec((1, dk), lambda b, h: (0, h)),
            pl.BlockSpec((1, dv), lambda b, h: (0, h)),
        ],
        out_specs=pl.BlockSpec((seq, dv), lambda b, h: (b, h)),
        out_shape=jax.ShapeDtypeStruct((n_batch * seq, n_heads * dv), jnp.bfloat16),
        scratch_shapes=[pltpu.VMEM((seq, dk), jnp.float32), pltpu.VMEM((seq, dv), jnp.float32)],
        compiler_params=pltpu.CompilerParams(
            dimension_semantics=("parallel", "parallel"), vmem_limit_bytes=VMEM_LIMIT),
    )(proj_a, proj_a, proj_a, proj_go, proj_c, gate_w2, gate_b.reshape(1, -1),
      norm_g.reshape(1, -1))


CONV_TILE = 256
CONV_HALO = 32


def _conf_kernel(a_ref, g_ref, w_ref, b_ref, ng_ref, nb_ref, o_ref, x_sc):
    seq, ch = a_ref.shape
    R, H = CONV_TILE, CONV_HALO
    x_sc[0:H, :] = jnp.zeros((H, ch), jnp.float32)
    x_sc[H:, :] = a_ref[...] * jax.nn.sigmoid(g_ref[...])

    def tile(i, carry):
        r0 = pl.multiple_of(i * R, R)
        win = x_sc[pl.ds(r0, R + H), :]
        acc = jnp.zeros((R, ch), jnp.float32) + b_ref[...]
        for sub in range(8):
            sh = pltpu.roll(win, sub, 0) if sub else win
            for blk in range(H // 8):
                off = 8 * blk + sub
                if off >= CONV_WIDTH:
                    continue
                j = CONV_WIDTH - 1 - off
                acc = acc + w_ref[j:j + 1, :] * sh[H - 8 * blk:H - 8 * blk + R, :]
        yc = acc - jnp.mean(acc, axis=1, keepdims=True)
        yn = (yc * lax.rsqrt(jnp.mean(yc * yc, axis=1, keepdims=True) + LN_EPS) * ng_ref[...]
              + nb_ref[...])
        o_ref[pl.ds(r0, R), :] = (yn * jax.nn.sigmoid(yn)).astype(o_ref.dtype)
        return carry

    lax.fori_loop(0, seq // R, tile, 0)


def conformer_conv_pallas(proj_a, a_col0, g_col0, dw_w, dw_b, norm_g, norm_b, n_batch, seq,
                          n_groups, width):
    ch = width // n_groups
    vec = pl.BlockSpec((1, ch), lambda b, g: (0, g))
    return pl.pallas_call(
        _conf_kernel,
        grid=(n_batch, n_groups),
        in_specs=[pl.BlockSpec((seq, ch), lambda b, g: (b, a_col0 // ch + g)),
                  pl.BlockSpec((seq, ch), lambda b, g: (b, g_col0 // ch + g)),
                  pl.BlockSpec((CONV_WIDTH, ch), lambda b, g: (0, g)),
                  vec, vec, vec],
        out_specs=pl.BlockSpec((seq, ch), lambda b, g: (b, g)),
        out_shape=jax.ShapeDtypeStruct((n_batch * seq, width), jnp.bfloat16),
        scratch_shapes=[pltpu.VMEM((seq + CONV_HALO, ch), jnp.float32)],
        compiler_params=pltpu.CompilerParams(
            dimension_semantics=("parallel", "parallel"), vmem_limit_bytes=VMEM_LIMIT),
    )(proj_a, proj_a, dw_w, dw_b.reshape(1, -1), norm_g.reshape(1, -1), norm_b.reshape(1, -1))


def hybrid_mixer(ub, B, S, layer, w_in, b_merge, mlstm_gate_b, mlstm_conv_w, mlstm_conv_b,
                 mlstm_norm_g, conf_dw_w, conf_dw_b, conf_norm_g, conf_norm_b, rel_bias_t, moba_btab,
                 gla_gate_w2, gla_gate_b, gla_norm_g, w_branch):
    W = BRANCH_WIDTH
    bf16, f32 = jnp.bfloat16, jnp.float32
    offs = [0] + np.cumsum(IN_SIZES).tolist()
    w_1 = repack_columns(w_in, layer, 0, offs[4])
    w_2 = repack_columns(w_in, layer, offs[5], offs[10] - offs[5])
    w_3 = repack_columns(w_in, layer, offs[11], offs[13] - offs[11])
    proj_ml = matmul(ub, w_1, 512, 1024, f32)
    proj_cf = matmul(ub, w_2, 512, 1024, f32, 0, 2 * W)
    proj_mb = matmul(ub, w_2, 512, 1024, bf16, 2 * W, 3 * W)
    proj_gl = matmul(ub, w_2, 512, 1024, f32, 5 * W, 2 * W)
    proj_go = matmul(ub, w_3, 512, 1024, f32, 0, W)
    proj_b = matmul(ub, w_3, 512, 1024, bf16, W)
    c_if = offs[4] - offs[4] % LANES
    c_lr = offs[10] - offs[10] % LANES
    proj_if = matmul_f32w(ub, w_in, layer, 512, LANES, f32, c_if, LANES)
    proj_lr = matmul_f32w(ub, w_in, layer, 512, LANES, f32, c_lr, LANES)

    gates = proj_if[:, offs[4] - c_if:offs[5] - c_if] + mlstm_gate_b
    y_mlstm = mlstm_pallas(proj_ml, gates, mlstm_conv_w, mlstm_conv_b, mlstm_norm_g, B, S,
                           MLSTM_HEADS, MLSTM_DQK, MLSTM_DV, 0, W // 2, W, 2 * W)
    y_conv = conformer_conv_pallas(proj_cf, 0, W, conf_dw_w, conf_dw_b, conf_norm_g, conf_norm_b,
                                   B, S, CONV_GROUPS, W)
    y_moba = moba_pallas(proj_mb, rel_bias_t, moba_btab, B, S, 0, MOBA_HEADS, 2 * MOBA_HEADS, bf16)
    y_gla = gla_pallas(proj_gl, proj_go, proj_lr, offs[10] - c_lr, gla_gate_w2, gla_gate_b,
                       gla_norm_g, B, S, GLA_HEADS, GLA_DK, GLA_DV, 0, W // 2, W, 0)
    return merge_branches([y_mlstm, y_conv, y_moba, y_gla], w_branch, layer, proj_b, 0, b_merge)


def kernel(x, c, w_ada, b_ada, w_in, b_merge, mlstm_gate_b, mlstm_conv_w, mlstm_conv_b,
           mlstm_norm_g, conf_dw_w, conf_dw_b, conf_norm_g, conf_norm_b, rel_bias,
           gla_gate_w2, gla_gate_b, gla_norm_g, w_branch, w_out, ln1_g, ln1_b,
           router_w, router_b, exp_w_up, exp_b_up, exp_w_down, exp_b_down, ln2_g, ln2_b):
    B, S, D = x.shape
    bf16 = jnp.bfloat16
    mod_all = ada_modulation(c, w_ada, b_ada)
    mods = [jnp.split(mod_all[l][:, None, :], 6, axis=-1) for l in range(DEPTH)]
    rel_bias_t = rel_bias.T.astype(jnp.float32)
    moba_btab = moba_bias_tables(rel_bias_t)
    n_rows = B * S * TOP_K + N_EXPERTS * MOE_TILE
    x2d = x.reshape(B * S, D)
    ub = modulate(x2d, mods[0][1], mods[0][0], S)
    for l in range(DEPTH):
        sh1, sc1, g1, sh2, sc2, g2 = mods[l]
        merged = hybrid_mixer(ub, B, S, l, w_in, b_merge[l], mlstm_gate_b[l], mlstm_conv_w[l],
                              mlstm_conv_b[l], mlstm_norm_g[l], conf_dw_w[l], conf_dw_b[l],
                              conf_norm_g[l], conf_norm_b[l], rel_bias_t, moba_btab,
                              gla_gate_w2[l], gla_gate_b[l], gla_norm_g[l], w_branch)
        mix = matmul_f32w(merged, w_out, l, 512, 512, jnp.float32)
        x1, upk, top_idx, top_w, rank4, counts = ln_router(
            x2d, mix, g1, ln1_g[l], ln1_b[l], sc2, sh2, router_w[l].astype(bf16), router_b[l], S)
        dest, tile_expert, n_tiles = moe_layout(counts, top_idx, rank4, n_rows // MOE_TILE)
        xs = moe_dispatch(upk, dest, n_rows)
        ys = grouped_experts(xs, tile_expert, n_tiles, exp_w_up, exp_b_up, exp_w_down, exp_b_down, l)
        nxt = min(l + 1, DEPTH - 1)
        x2d, ub = combine_ln(dest, top_w, x1, g2, ln2_g[l], ln2_b[l], mods[nxt][1], mods[nxt][0],
                             ys, S)
    return x2d.reshape(B, S, D)
```

```python
import functools
import math

import jax
import jax.numpy as jnp
import numpy as np
from jax import lax
from jax.experimental import pallas as pl
from jax.experimental.pallas import tpu as pltpu

D_MODEL = 4096
BATCH = 4
SEQ = 2048
DEPTH = 2

N_BRANCHES = 4
BRANCH_WIDTH = D_MODEL // N_BRANCHES
MLSTM_HEADS = 4
MLSTM_DV = BRANCH_WIDTH // MLSTM_HEADS
MLSTM_DQK = MLSTM_DV // 2
MLSTM_CONV = 4
MLSTM_CHUNK = 64
CONV_WIDTH = 31
CONV_GROUPS = 4
MOBA_HEADS = 8
MOBA_HD = BRANCH_WIDTH // MOBA_HEADS
MOBA_BLOCK = 256
MOBA_TOPK = 3
MOBA_Q_CHUNK = 32
REL_BUCKETS = 32
REL_MAX_DIST = 128
GLA_HEADS = 4
GLA_DV = BRANCH_WIDTH // GLA_HEADS
GLA_DK = GLA_DV // 2
GLA_RANK = 16
GLA_TAU = 16.0
GLA_CHUNK = 16
N_EXPERTS = 32
TOP_K = 4
EXPERT_FF = D_MODEL // 8
SWIGLU_ALPHA = 1.702
SWIGLU_LIMIT = 7.0
DEEPNORM_ALPHA = (2 * DEPTH) ** 0.25
LN_EPS = 1e-5

IN_SIZES = (
    MLSTM_HEADS * MLSTM_DQK, MLSTM_HEADS * MLSTM_DQK, BRANCH_WIDTH, BRANCH_WIDTH,
    2 * MLSTM_HEADS, 2 * BRANCH_WIDTH, 3 * BRANCH_WIDTH, GLA_HEADS * GLA_DK,
    GLA_HEADS * GLA_DK, BRANCH_WIDTH, GLA_RANK, BRANCH_WIDTH, N_BRANCHES * D_MODEL,
)
IN_WIDTH = sum(IN_SIZES)

VMEM_LIMIT = 56 * 1024 * 1024


def _mm_kernel(a_ref, b_ref, o_ref):
    o_ref[...] = jnp.dot(a_ref[...], b_ref[...],
                         preferred_element_type=jnp.float32).astype(o_ref.dtype)


def matmul(a, b, tm, tn, out_dtype, col0=0, ncols=None):
    M, K = a.shape
    N = b.shape[1] - col0 if ncols is None else ncols
    assert col0 % tn == 0 and N % tn == 0 and M % tm == 0
    jb = col0 // tn
    return pl.pallas_call(
        _mm_kernel,
        grid=(N // tn, M // tm),
        in_specs=[pl.BlockSpec((tm, K), lambda j, i: (i, 0)),
                  pl.BlockSpec((K, tn), lambda j, i: (0, jb + j))],
        out_specs=pl.BlockSpec((tm, tn), lambda j, i: (i, j)),
        out_shape=jax.ShapeDtypeStruct((M, N), out_dtype),
        compiler_params=pltpu.CompilerParams(
            dimension_semantics=("parallel", "parallel"), vmem_limit_bytes=VMEM_LIMIT),
    )(a, b)


def _mm_f32w_kernel(a_ref, b_ref, o_ref, wb_sc):
    @pl.when(pl.program_id(1) == 0)
    def _():
        wb_sc[...] = b_ref[...].astype(wb_sc.dtype)

    o_ref[...] = jnp.dot(a_ref[...], wb_sc[...],
                         preferred_element_type=jnp.float32).astype(o_ref.dtype)


def matmul_f32w(a, b, layer, tm, tn, out_dtype, col0=0, ncols=None):
    M, K = a.shape
    N = b.shape[2] - col0 if ncols is None else ncols
    assert col0 % tn == 0 and N % tn == 0 and M % tm == 0
    jb = col0 // tn
    return pl.pallas_call(
        _mm_f32w_kernel,
        grid=(N // tn, M // tm),
        in_specs=[pl.BlockSpec((tm, K), lambda j, i: (i, 0)),
                  pl.BlockSpec((None, K, tn), lambda j, i: (layer, 0, jb + j))],
        out_specs=pl.BlockSpec((tm, tn), lambda j, i: (i, j)),
        out_shape=jax.ShapeDtypeStruct((M, N), out_dtype),
        scratch_shapes=[pltpu.VMEM((K, tn), jnp.bfloat16)],
        compiler_params=pltpu.CompilerParams(
            dimension_semantics=("parallel", "arbitrary"), vmem_limit_bytes=VMEM_LIMIT),
    )(a, b)


LANES = 128
SUBLANES = 8
REPACK_ROWS = 1024
REPACK_COLS = 1024
REPACK_EDGE = 32


def _repack_kernel(*refs, shift):
    if shift:
        main_ref, edge_ref, o_ref = refs
        x = jnp.concatenate([main_ref[shift:, :], edge_ref[:shift, :]], axis=0)
    else:
        main_ref, o_ref = refs
        x = main_ref[...]
    o_ref[...] = x.T.astype(o_ref.dtype)


def repack_columns(w_t, layer, col0, ncols):
    _, n, k = w_t.shape
    tk, tn = min(REPACK_ROWS, k), REPACK_COLS
    shift = col0 % tn
    base = col0 - shift
    assert shift % SUBLANES == 0 and shift <= REPACK_EDGE and ncols % tn == 0 and k % tk == 0
    in_specs = [pl.BlockSpec((None, tn, tk), lambda j, r: (layer, base // tn + j, r))]
    args = [w_t]
    if shift:
        in_specs.append(pl.BlockSpec(
            (None, REPACK_EDGE, tk), lambda j, r: (layer, (base + (j + 1) * tn) // REPACK_EDGE, r)))
        args.append(w_t)
    return pl.pallas_call(
        functools.partial(_repack_kernel, shift=shift),
        grid=(ncols // tn, k // tk),
        in_specs=in_specs,
        out_specs=pl.BlockSpec((tk, tn), lambda j, r: (r, j)),
        out_shape=jax.ShapeDtypeStruct((k, ncols), jnp.bfloat16),
        compiler_params=pltpu.CompilerParams(
            dimension_semantics=("parallel", "parallel"), vmem_limit_bytes=VMEM_LIMIT),
    )(*args)


def _mm_nt_kernel(a_ref, bt_ref, o_ref):
    o_ref[...] = lax.dot_general(a_ref[...], bt_ref[...].astype(jnp.bfloat16),
                                 (((1,), (1,)), ((), ())),
                                 preferred_element_type=jnp.float32).astype(o_ref.dtype)


def matmul_rows_t(a, w_t, layer, row0, nrows, tm, out_dtype):
    M, K = a.shape
    assert row0 % nrows == 0 and M % tm == 0
    return pl.pallas_call(
        _mm_nt_kernel,
        grid=(M // tm,),
        in_specs=[pl.BlockSpec((tm, K), lambda i: (i, 0)),
                  pl.BlockSpec((None, nrows, K), lambda i: (layer, row0 // nrows, 0))],
        out_specs=pl.BlockSpec((tm, nrows), lambda i: (i, 0)),
        out_shape=jax.ShapeDtypeStruct((M, nrows), out_dtype),
        compiler_params=pltpu.CompilerParams(
            dimension_semantics=("parallel",), vmem_limit_bytes=VMEM_LIMIT),
    )(a, w_t)


ROW_TILE = 256
MOE_TILE = 256


def _ada_kernel(c_ref, w_ref, b_ref, o_ref):
    c = c_ref[...]
    cs = (c * jax.nn.sigmoid(c)).astype(jnp.bfloat16)
    o_ref[0] = jnp.dot(cs, w_ref[0].astype(jnp.bfloat16),
                       preferred_element_type=jnp.float32) + b_ref[0]


def ada_modulation(c, w_ada, b_ada, tn=512):
    n_layers, d, n = w_ada.shape
    nb = c.shape[0]
    c8 = jnp.pad(c, ((0, 8 - nb), (0, 0)))
    out = pl.pallas_call(
        _ada_kernel,
        grid=(n_layers, n // tn),
        in_specs=[pl.BlockSpec((8, d), lambda l, j: (0, 0)),
                  pl.BlockSpec((1, d, tn), lambda l, j: (l, 0, j)),
                  pl.BlockSpec((1, 1, tn), lambda l, j: (l, 0, j))],
        out_specs=pl.BlockSpec((1, 8, tn), lambda l, j: (l, 0, j)),
        out_shape=jax.ShapeDtypeStruct((n_layers, 8, n), jnp.float32),
        compiler_params=pltpu.CompilerParams(
            dimension_semantics=("parallel", "parallel"), vmem_limit_bytes=VMEM_LIMIT),
    )(c8, w_ada, b_ada.reshape(n_layers, 1, n))
    return out[:, :nb]


def _modulate_kernel(x_ref, sc_ref, sh_ref, o_ref):
    o_ref[...] = (x_ref[...] * (1.0 + sc_ref[0]) + sh_ref[0]).astype(o_ref.dtype)


def modulate(x2d, sc, sh, seq):
    t, d = x2d.shape
    per_b = seq // ROW_TILE
    mod_spec = pl.BlockSpec((1, 1, d), lambda i: (i // per_b, 0, 0))
    return pl.pallas_call(
        _modulate_kernel,
        grid=(t // ROW_TILE,),
        in_specs=[pl.BlockSpec((ROW_TILE, d), lambda i: (i, 0)), mod_spec, mod_spec],
        out_specs=pl.BlockSpec((ROW_TILE, d), lambda i: (i, 0)),
        out_shape=jax.ShapeDtypeStruct((t, d), jnp.bfloat16),
        compiler_params=pltpu.CompilerParams(dimension_semantics=("parallel",)),
    )(x2d, sc, sh)


def _merge_kernel(y0, y1, y2, y3, wb_ref, g0, g1, g2, g3, bm_ref, o_ref, wb_sc):
    @pl.when(pl.program_id(1) == 0)
    def _():
        wb_sc[...] = wb_ref[...].astype(wb_sc.dtype)

    acc = None
    for n, (y, g) in enumerate(((y0, g0), (y1, g1), (y2, g2), (y3, g3))):
        gate = jax.nn.sigmoid(g[...].astype(jnp.float32) + bm_ref[n:n + 1, :])
        term = gate * jnp.dot(y[...], wb_sc[n], preferred_element_type=jnp.float32)
        acc = term if acc is None else acc + term
    o_ref[...] = acc.astype(o_ref.dtype)


def merge_branches(ys, w_branch, layer, proj_b, gate_col0, b_merge, tm=512, tn=512):
    t, w = ys[0].shape
    d = w_branch.shape[3]
    assert gate_col0 % tn == 0 and d % tn == 0
    y_spec = pl.BlockSpec((tm, w), lambda j, i: (i, 0))
    g_specs = [pl.BlockSpec((tm, tn), lambda j, i, n=n: (i, (gate_col0 + n * d) // tn + j))
               for n in range(N_BRANCHES)]
    return pl.pallas_call(
        _merge_kernel,
        grid=(d // tn, t // tm),
        in_specs=[y_spec] * 4 + [pl.BlockSpec((None, N_BRANCHES, w, tn),
                                              lambda j, i: (layer, 0, 0, j))]
        + g_specs + [pl.BlockSpec((N_BRANCHES, tn), lambda j, i: (0, j))],
        out_specs=pl.BlockSpec((tm, tn), lambda j, i: (i, j)),
        out_shape=jax.ShapeDtypeStruct((t, d), jnp.bfloat16),
        scratch_shapes=[pltpu.VMEM((N_BRANCHES, w, tn), jnp.bfloat16)],
        compiler_params=pltpu.CompilerParams(
            dimension_semantics=("parallel", "arbitrary"), vmem_limit_bytes=VMEM_LIMIT),
    )(*ys, w_branch, proj_b, proj_b, proj_b, proj_b, b_merge.reshape(N_BRANCHES, d))


def _pack_pair(lo, hi):
    def rne(x):
        b = lax.bitcast_convert_type(x, jnp.uint32)
        return (b + jnp.uint32(0x7FFF) + ((b >> 16) & jnp.uint32(1))) >> 16
    return rne(lo) | (rne(hi) << 16)


def _unpack_lo(w):
    return lax.bitcast_convert_type(w << 16, jnp.float32)


def _unpack_hi(w):
    return lax.bitcast_convert_type(w & jnp.uint32(0xFFFF0000), jnp.float32)


def _layer_norm_halves(z_lo, z_hi, g_ref, b_ref, half):
    d = 2 * half
    mu = (jnp.sum(z_lo, axis=1, keepdims=True) + jnp.sum(z_hi, axis=1, keepdims=True)) / d
    c_lo, c_hi = z_lo - mu, z_hi - mu
    var = (jnp.sum(c_lo * c_lo, axis=1, keepdims=True)
           + jnp.sum(c_hi * c_hi, axis=1, keepdims=True)) / d
    r = lax.rsqrt(var + LN_EPS)
    return (c_lo * r * g_ref[:, :half] + b_ref[:, :half],
            c_hi * r * g_ref[:, half:] + b_ref[:, half:])


def _ln_router_kernel(x_ref, mix_ref, g1_ref, lng_ref, lnb_ref, sc_ref, sh_ref, rw_ref, rb_ref,
                      x1_ref, upk_ref, tidx_ref, tw_ref, rank_ref, cnt_ref, carry_sc):
    tm, d = x_ref.shape
    half = d // 2
    n_exp = rw_ref.shape[1]

    @pl.when(pl.program_id(0) == 0)
    def _():
        carry_sc[...] = jnp.zeros_like(carry_sc)

    z = DEEPNORM_ALPHA * x_ref[...] + (1.0 + g1_ref[0]) * mix_ref[...]
    x_lo, x_hi = _layer_norm_halves(z[:, :half], z[:, half:], lng_ref, lnb_ref, half)
    x1_ref[:, :half] = x_lo
    x1_ref[:, half:] = x_hi
    u_lo = x_lo * (1.0 + sc_ref[0, :, :half]) + sh_ref[0, :, :half]
    u_hi = x_hi * (1.0 + sc_ref[0, :, half:]) + sh_ref[0, :, half:]
    upk_ref[...] = _pack_pair(u_lo, u_hi)

    logits = (jnp.dot(u_lo.astype(jnp.bfloat16), rw_ref[:half, :],
                      preferred_element_type=jnp.float32)
              + jnp.dot(u_hi.astype(jnp.bfloat16), rw_ref[half:, :],
                        preferred_element_type=jnp.float32) + rb_ref[...])
    lane = lax.broadcasted_iota(jnp.int32, (tm, n_exp), 1).astype(jnp.float32)
    lg = logits
    vals, hots = [], []
    for k in range(TOP_K):
        m = jnp.max(lg, axis=1, keepdims=True)
        ik = jnp.min(jnp.where(lg == m, lane, float(n_exp)), axis=1, keepdims=True)
        hot = lane == ik
        vals.append(m)
        hots.append(hot)
        tidx_ref[:, k:k + 1] = ik.astype(jnp.int32)
        lg = jnp.where(hot, -jnp.inf, lg)
    exps = [jnp.exp(v - vals[0]) for v in vals]
    den = exps[0] + exps[1] + exps[2] + exps[3]
    for k in range(TOP_K):
        tw_ref[:, k:k + 1] = exps[k] / den

    mask = jnp.where(hots[0] | hots[1] | hots[2] | hots[3], 1.0, 0.0)
    row = lax.broadcasted_iota(jnp.int32, (tm, tm), 0)
    col = lax.broadcasted_iota(jnp.int32, (tm, tm), 1)
    lower = jnp.where(row > col, 1.0, 0.0).astype(jnp.bfloat16)
    rank = jnp.dot(lower, mask.astype(jnp.bfloat16),
                   preferred_element_type=jnp.float32) + carry_sc[...]
    for k in range(TOP_K):
        rank_ref[:, k:k + 1] = jnp.sum(jnp.where(hots[k], rank, 0.0), axis=1,
                                       keepdims=True).astype(jnp.int32)
    carry_sc[...] = carry_sc[...] + jnp.sum(mask, axis=0, keepdims=True)
    cnt_ref[...] = carry_sc[...]


def ln_router(x2d, mix, g1, ln_g, ln_b, sc2, sh2, router_w_bf, router_b, seq):
    t, d = x2d.shape
    n_exp = router_w_bf.shape[1]
    per_b = seq // ROW_TILE
    row = lambda w: pl.BlockSpec((ROW_TILE, w), lambda i: (i, 0))
    mod = pl.BlockSpec((1, 1, d), lambda i: (i // per_b, 0, 0))
    vec = lambda w: pl.BlockSpec((1, w), lambda i: (0, 0))
    return pl.pallas_call(
        _ln_router_kernel,
        grid=(t // ROW_TILE,),
        in_specs=[row(d), row(d), mod, vec(d), vec(d), mod, mod,
                  pl.BlockSpec((d, n_exp), lambda i: (0, 0)), vec(n_exp)],
        out_specs=[row(d), row(d // 2), row(TOP_K), row(TOP_K), row(TOP_K), vec(n_exp)],
        out_shape=[jax.ShapeDtypeStruct((t, d), jnp.float32),
                   jax.ShapeDtypeStruct((t, d // 2), jnp.uint32),
                   jax.ShapeDtypeStruct((t, TOP_K), jnp.int32),
                   jax.ShapeDtypeStruct((t, TOP_K), jnp.float32),
                   jax.ShapeDtypeStruct((t, TOP_K), jnp.int32),
                   jax.ShapeDtypeStruct((1, n_exp), jnp.float32)],
        scratch_shapes=[pltpu.VMEM((1, n_exp), jnp.float32)],
        compiler_params=pltpu.CompilerParams(
            dimension_semantics=("arbitrary",), vmem_limit_bytes=VMEM_LIMIT),
    )(x2d, mix, g1, ln_g.reshape(1, d), ln_b.reshape(1, d), sc2, sh2, router_w_bf,
      router_b.reshape(1, n_exp))


def _row_copy(src_hbm, src_row, dst, dst_row, sem):
    return pltpu.make_async_copy(src_hbm.at[pl.ds(src_row, 1)], dst.at[pl.ds(dst_row, 1)], sem)


def _dispatch_kernel(dest_ref, u_ref, xs_in_hbm, xs_hbm, sem):
    del xs_in_hbm

    def issue(r, carry):
        for k in range(TOP_K):
            _row_copy(u_ref, r, xs_hbm, dest_ref[r * TOP_K + k], sem).start(priority=k % 2)
        return carry

    def drain(r, carry):
        for k in range(TOP_K):
            _row_copy(u_ref, 0, xs_hbm, 0, sem).wait()
        return carry

    lax.fori_loop(0, ROW_TILE, issue, 0)
    lax.fori_loop(0, ROW_TILE, drain, 0)


def moe_dispatch(upk, dest_flat, n_rows):
    t, w = upk.shape
    xs0 = jnp.zeros((n_rows, w), upk.dtype)
    return pl.pallas_call(
        _dispatch_kernel,
        grid=(t // ROW_TILE,),
        in_specs=[pl.BlockSpec((ROW_TILE * TOP_K,), lambda i: (i,), memory_space=pltpu.SMEM),
                  pl.BlockSpec((ROW_TILE, w), lambda i: (i, 0)),
                  pl.BlockSpec(memory_space=pl.ANY)],
        out_specs=pl.BlockSpec(memory_space=pl.ANY),
        out_shape=jax.ShapeDtypeStruct((n_rows, w), upk.dtype),
        scratch_shapes=[pltpu.SemaphoreType.DMA(())],
        input_output_aliases={2: 0},
        compiler_params=pltpu.CompilerParams(dimension_semantics=("arbitrary",)),
    )(dest_flat, upk, xs0)


UP_CHUNKS = 4


def _expert_kernel(te_ref, nt_ref, xs_ref, wu_hbm, bu_ref, wd_hbm, bd_ref, ys_ref,
                   stage_u, stage_d, wu_sc, wd_sc, sem, *, layer):
    g = pl.program_id(0)
    half = xs_ref.shape[1]
    ff = wd_sc.shape[0]
    rows = stage_u.shape[1]
    e = te_ref[g]
    new_expert = jnp.logical_or(g == 0, e != te_ref[jnp.maximum(g - 1, 0)])

    @pl.when(jnp.logical_and(g < nt_ref[0], new_expert))
    def _():
        def up_copy(c):
            return pltpu.make_async_copy(wu_hbm.at[layer, e, pl.ds(c * rows, rows)],
                                         stage_u.at[c % 2], sem.at[c % 2])
        down_copy = pltpu.make_async_copy(wd_hbm.at[layer, e], stage_d, sem.at[2])
        up_copy(0).start()
        up_copy(1).start()
        down_copy.start()
        for c in range(UP_CHUNKS):
            up_copy(c).wait()
            wu_sc[c * rows:(c + 1) * rows, :] = stage_u[c % 2].astype(wu_sc.dtype)
            if c + 2 < UP_CHUNKS:
                up_copy(c + 2).start()
        down_copy.wait()
        wd_sc[...] = stage_d[...].astype(wd_sc.dtype)

    @pl.when(g < nt_ref[0])
    def _():
        w = xs_ref[...]
        x_lo = _unpack_lo(w).astype(jnp.bfloat16)
        x_hi = _unpack_hi(w).astype(jnp.bfloat16)
        hid = (jnp.dot(x_lo, wu_sc[:half, :], preferred_element_type=jnp.float32)
               + jnp.dot(x_hi, wu_sc[half:, :], preferred_element_type=jnp.float32)
               + bu_ref[0, 0])
        glu = jnp.minimum(hid[:, :ff], SWIGLU_LIMIT)
        lin = jnp.clip(hid[:, ff:], -SWIGLU_LIMIT, SWIGLU_LIMIT)
        act = glu * jax.nn.sigmoid(SWIGLU_ALPHA * glu) * (lin + 1.0)
        y = jnp.dot(act.astype(jnp.bfloat16), wd_sc[...],
                    preferred_element_type=jnp.float32) + bd_ref[0, 0]
        ys_ref[...] = _pack_pair(y[:, :half], y[:, half:])

    @pl.when(g >= nt_ref[0])
    def _():
        ys_ref[...] = jnp.zeros_like(ys_ref)


def grouped_experts(xs, tile_expert, n_tiles, w_up, b_up, w_down, b_down, layer):
    n_rows, half = xs.shape
    _, n_exp, d, ff2 = w_up.shape
    ff = ff2 // 2
    grid_spec = pltpu.PrefetchScalarGridSpec(
        num_scalar_prefetch=2,
        grid=(n_rows // MOE_TILE,),
        in_specs=[pl.BlockSpec((MOE_TILE, half), lambda g, te, nt: (g, 0)),
                  pl.BlockSpec(memory_space=pl.ANY),
                  pl.BlockSpec((1, 1, 1, ff2), lambda g, te, nt: (layer, te[g], 0, 0)),
                  pl.BlockSpec(memory_space=pl.ANY),
                  pl.BlockSpec((1, 1, 1, d), lambda g, te, nt: (layer, te[g], 0, 0))],
        out_specs=pl.BlockSpec((MOE_TILE, half), lambda g, te, nt: (g, 0)),
        scratch_shapes=[pltpu.VMEM((2, d // UP_CHUNKS, ff2), jnp.float32),
                        pltpu.VMEM((ff, d), jnp.float32),
                        pltpu.VMEM((d, ff2), jnp.bfloat16),
                        pltpu.VMEM((ff, d), jnp.bfloat16),
                        pltpu.SemaphoreType.DMA((3,))],
    )
    return pl.pallas_call(
        functools.partial(_expert_kernel, layer=layer),
        grid_spec=grid_spec,
        out_shape=jax.ShapeDtypeStruct((n_rows, half), jnp.uint32),
        compiler_params=pltpu.CompilerParams(
            dimension_semantics=("arbitrary",), vmem_limit_bytes=VMEM_LIMIT),
    )(tile_expert, n_tiles, xs, w_up, b_up.reshape(b_up.shape[0], n_exp, 1, ff2), w_down,
      b_down.reshape(b_down.shape[0], n_exp, 1, d))


COMBINE_ROWS = 16


def _combine_kernel(dest_ref, dest_next_ref, tw_ref, x1_ref, g2_ref, lng_ref, lnb_ref, sc_ref,
                    sh_ref, ys_hbm, x2_ref, un_ref, buf, sem):
    tm, d = x1_ref.shape
    half = d // 2
    i = pl.program_id(0)
    slot = lax.rem(i, 2)

    has_next = i + 1 < pl.num_programs(0)
    sub = COMBINE_ROWS

    def gather_rows(d_ref, s, r0):
        for rr in range(sub):
            for k in range(TOP_K):
                _row_copy(ys_hbm, d_ref[(r0 + rr) * TOP_K + k], buf.at[s, k], r0 + rr,
                          sem.at[s]).start(priority=k % 2)

    @pl.when(i == 0)
    def _():
        def first(t, carry):
            gather_rows(dest_ref, 0, t * sub)
            return carry
        lax.fori_loop(0, tm // sub, first, 0)

    def drain(r, carry):
        for k in range(TOP_K):
            _row_copy(ys_hbm, 0, buf.at[slot, k], 0, sem.at[slot]).wait()
        return carry

    lax.fori_loop(0, tm, drain, 0)

    def rows(t, carry):
        r0 = pl.multiple_of(t * sub, sub)
        rs = pl.ds(r0, sub)

        gather_rows(dest_next_ref, 1 - slot, r0)

        f_lo = f_hi = None
        for k in range(TOP_K):
            wk = tw_ref[rs, k:k + 1]
            w = buf[slot, k, rs, :]
            t_lo, t_hi = wk * _unpack_lo(w), wk * _unpack_hi(w)
            f_lo = t_lo if f_lo is None else f_lo + t_lo
            f_hi = t_hi if f_hi is None else f_hi + t_hi
        z_lo = DEEPNORM_ALPHA * x1_ref[rs, :half] + (1.0 + g2_ref[0, :, :half]) * f_lo
        z_hi = DEEPNORM_ALPHA * x1_ref[rs, half:] + (1.0 + g2_ref[0, :, half:]) * f_hi
        x_lo, x_hi = _layer_norm_halves(z_lo, z_hi, lng_ref, lnb_ref, half)
        x2_ref[rs, :half] = x_lo
        x2_ref[rs, half:] = x_hi
        un_ref[rs, :half] = (x_lo * (1.0 + sc_ref[0, :, :half])
                             + sh_ref[0, :, :half]).astype(un_ref.dtype)
        un_ref[rs, half:] = (x_hi * (1.0 + sc_ref[0, :, half:])
                             + sh_ref[0, :, half:]).astype(un_ref.dtype)
        return carry

    lax.fori_loop(0, tm // sub, rows, 0)

    @pl.when(jnp.logical_not(has_next))
    def _():
        def drain_other(r, carry):
            for k in range(TOP_K):
                _row_copy(ys_hbm, 0, buf.at[1 - slot, k], 0, sem.at[1 - slot]).wait()
            return carry
        lax.fori_loop(0, tm, drain_other, 0)


def combine_ln(dest_flat, top_w, x1, g2, ln_g, ln_b, sc_next, sh_next, ys, seq):
    t, d = x1.shape
    per_b = seq // ROW_TILE
    n_steps = t // ROW_TILE
    row = lambda w: pl.BlockSpec((ROW_TILE, w), lambda i: (i, 0))
    mod = pl.BlockSpec((1, 1, d), lambda i: (i // per_b, 0, 0))
    vec = pl.BlockSpec((1, d), lambda i: (0, 0))
    return pl.pallas_call(
        _combine_kernel,
        grid=(n_steps,),
        in_specs=[pl.BlockSpec((ROW_TILE * TOP_K,), lambda i: (i,), memory_space=pltpu.SMEM),
                  pl.BlockSpec((ROW_TILE * TOP_K,), lambda i: (jnp.minimum(i + 1, n_steps - 1),),
                               memory_space=pltpu.SMEM),
                  row(TOP_K), row(d), mod, vec, vec, mod, mod,
                  pl.BlockSpec(memory_space=pl.ANY)],
        out_specs=[row(d), row(d)],
        out_shape=[jax.ShapeDtypeStruct((t, d), jnp.float32),
                   jax.ShapeDtypeStruct((t, d), jnp.bfloat16)],
        scratch_shapes=[pltpu.VMEM((2, TOP_K, ROW_TILE, d // 2), jnp.uint32),
                        pltpu.SemaphoreType.DMA((2,))],
        compiler_params=pltpu.CompilerParams(
            dimension_semantics=("arbitrary",), vmem_limit_bytes=VMEM_LIMIT),
    )(dest_flat, dest_flat, top_w, x1, g2, ln_g.reshape(1, d), ln_b.reshape(1, d), sc_next,
      sh_next, ys)


def moe_layout(counts, top_idx, rank4, n_tiles_max):
    cnt = counts.reshape(-1).astype(jnp.int32)
    tiles = (cnt + MOE_TILE - 1) // MOE_TILE
    tile_end = jnp.cumsum(tiles)
    row_start = (tile_end - tiles) * MOE_TILE
    dest = (row_start[top_idx] + rank4).reshape(-1)
    tile_expert = jnp.minimum(
        jnp.sum(jnp.arange(n_tiles_max)[:, None] >= tile_end[None, :], axis=1),
        cnt.shape[0] - 1).astype(jnp.int32)
    return dest.astype(jnp.int32), tile_expert, tile_end[-1:].astype(jnp.int32)


NEG_BIG = -1e30


def _t5_bucket_np(d):
    max_exact = REL_BUCKETS // 2
    dd = np.maximum(d, 1).astype(np.float32)
    far = max_exact + (np.log(dd / np.float32(max_exact))
                       / np.float32(math.log(REL_MAX_DIST / max_exact))
                       * np.float32(REL_BUCKETS - max_exact)).astype(np.int32)
    return np.where(d < max_exact, d, np.minimum(far, REL_BUCKETS - 1)).astype(np.int32)


def _moba_bucket_tables():
    c = np.arange(MOBA_BLOCK)[:, None]
    r = np.arange(MOBA_BLOCK)[None, :]
    own = np.where(r >= c, _t5_bucket_np(np.maximum(r - c, 0)), -1)
    prev = _t5_bucket_np(MOBA_BLOCK + r - c)
    return np.stack([own, prev]).astype(np.int32)


def _bias_tab_kernel(rb_ref, bk_ref, o_ref):
    h = pl.program_id(0)
    bk = bk_ref[...]
    acc = jnp.zeros(bk.shape, jnp.float32)
    for m in range(REL_BUCKETS):
        acc = jnp.where(bk == m, rb_ref[h, m], acc)
    o_ref[0] = jnp.where(bk < 0, NEG_BIG, acc)


def moba_bias_tables(rel_bias_t):
    n_heads = rel_bias_t.shape[0]
    bk = jnp.asarray(_moba_bucket_tables())
    return pl.pallas_call(
        _bias_tab_kernel,
        grid=(n_heads,),
        in_specs=[pl.BlockSpec(memory_space=pltpu.SMEM),
                  pl.BlockSpec((2, MOBA_BLOCK, MOBA_BLOCK), lambda h: (0, 0, 0))],
        out_specs=pl.BlockSpec((1, 2, MOBA_BLOCK, MOBA_BLOCK), lambda h: (h, 0, 0, 0)),
        out_shape=jax.ShapeDtypeStruct((n_heads, 2, MOBA_BLOCK, MOBA_BLOCK), jnp.float32),
    )(rel_bias_t, bk)


def _moba_kernel(rb_ref, q_ref, k_ref, v_ref, bt_ref, o_ref, km_sc, vt_sc, m_sc, l_sc, acc_sc,
                 *, n_blk):
    h = pl.program_id(1)
    i = pl.program_id(2)
    blk = MOBA_BLOCK
    seq = n_blk * blk
    scale = MOBA_HD ** -0.5
    nt = (((1,), (1,)), ((), ()))
    bf16 = jnp.bfloat16

    @pl.when(i == 0)
    def _():
        blk_of_col = lax.broadcasted_iota(jnp.int32, (n_blk, seq), 1) // blk
        blk_row = lax.broadcasted_iota(jnp.int32, (n_blk, seq), 0)
        avg = jnp.where(blk_of_col == blk_row, 1.0 / blk, 0.0).astype(bf16)
        km_sc[...] = jnp.dot(avg, k_ref[...], preferred_element_type=jnp.float32)
        for j in range(n_blk):
            vt_sc[j] = v_ref[j * blk:(j + 1) * blk, :].astype(jnp.float32).T.astype(bf16)

    q = q_ref[...]
    kmean = km_sc[...]
    km_hi = kmean.astype(bf16)
    km_lo = (kmean - km_hi.astype(jnp.float32)).astype(bf16)
    gate = (lax.dot_general(km_hi, q, nt, preferred_element_type=jnp.float32)
            + lax.dot_general(km_lo, q, nt, preferred_element_type=jnp.float32))
    g = [gate[n:n + 1, :] for n in range(n_blk)]

    def selected(j):
        cnt = jnp.zeros((1, blk), jnp.float32)
        for n in range(n_blk - 1):
            if n == j:
                continue
            ahead = (g[n] >= g[j]) if n < j else (g[n] > g[j])
            cnt = cnt + jnp.where(ahead, jnp.where(n < i, 1.0, 0.0), 0.0)
        return cnt < float(MOBA_TOPK)

    row0 = pl.multiple_of(i * blk, blk)
    k_own = k_ref[pl.ds(row0, blk), :]
    s = lax.dot_general(k_own, q, nt, preferred_element_type=jnp.float32) * scale + bt_ref[0, 0]
    m0 = jnp.max(s, axis=0, keepdims=True)
    p = jnp.exp(s - m0)
    m_sc[...] = m0
    l_sc[...] = jnp.sum(p, axis=0, keepdims=True)
    acc_sc[...] = jnp.dot(vt_sc[i], p.astype(bf16), preferred_element_type=jnp.float32)

    bfar = rb_ref[h, REL_BUCKETS - 1]
    for j in range(n_blk - 1):
        @pl.when(j < i)
        def _():
            k_j = k_ref[j * blk:(j + 1) * blk, :]
            bias = jnp.where(j == i - 1, bt_ref[0, 1], bfar)
            s = lax.dot_general(k_j, q, nt, preferred_element_type=jnp.float32) * scale + bias
            s = jnp.where(selected(j), s, NEG_BIG)
            m_old = m_sc[...]
            m_new = jnp.maximum(m_old, jnp.max(s, axis=0, keepdims=True))
            alpha = jnp.exp(m_old - m_new)
            p = jnp.exp(s - m_new)
            m_sc[...] = m_new
            l_sc[...] = alpha * l_sc[...] + jnp.sum(p, axis=0, keepdims=True)
            acc_sc[...] = alpha * acc_sc[...] + jnp.dot(
                vt_sc[j], p.astype(bf16), preferred_element_type=jnp.float32)

    o_ref[...] = (acc_sc[...] / l_sc[...]).T.astype(o_ref.dtype)


def moba_pallas(qkv, rel_bias_t, btab, n_batch, seq, q_blk0, k_blk0, v_blk0, out_dtype):
    n_heads = rel_bias_t.shape[0]
    n_blk = seq // MOBA_BLOCK
    assert seq % MOBA_BLOCK == 0
    kern = functools.partial(_moba_kernel, n_blk=n_blk)
    return pl.pallas_call(
        kern,
        grid=(n_batch, n_heads, n_blk),
        in_specs=[
            pl.BlockSpec(memory_space=pltpu.SMEM),
            pl.BlockSpec((MOBA_BLOCK, MOBA_HD), lambda b, h, i: (b * n_blk + i, q_blk0 + h)),
            pl.BlockSpec((seq, MOBA_HD), lambda b, h, i: (b, k_blk0 + h)),
            pl.BlockSpec((seq, MOBA_HD), lambda b, h, i: (b, v_blk0 + h)),
            pl.BlockSpec((1, 2, MOBA_BLOCK, MOBA_BLOCK), lambda b, h, i: (h, 0, 0, 0)),
        ],
        out_specs=pl.BlockSpec((MOBA_BLOCK, MOBA_HD), lambda b, h, i: (b * n_blk + i, h)),
        out_shape=jax.ShapeDtypeStruct((n_batch * seq, n_heads * MOBA_HD), out_dtype),
        scratch_shapes=[pltpu.VMEM((n_blk, MOBA_HD), jnp.float32),
                        pltpu.VMEM((n_blk, MOBA_HD, MOBA_BLOCK), jnp.bfloat16),
                        pltpu.VMEM((1, MOBA_BLOCK), jnp.float32),
                        pltpu.VMEM((1, MOBA_BLOCK), jnp.float32),
                        pltpu.VMEM((MOBA_HD, MOBA_BLOCK), jnp.float32)],
        compiler_params=pltpu.CompilerParams(
            dimension_semantics=("parallel", "parallel", "arbitrary")),
    )(rel_bias_t, qkv, qkv, qkv, btab)


def _log_sigmoid(x):
    return jnp.minimum(x, 0.0) - jnp.log(1.0 + jnp.exp(-jnp.abs(x)))


def _causal_shift(x, shift):
    if shift == 0:
        return x
    row = lax.broadcasted_iota(jnp.int32, x.shape, 0)
    return jnp.where(row >= shift, pltpu.roll(x, shift, 0), 0.0)


def _mlstm_kernel(q_ref, k_ref, v_ref, og_ref, gc_ref, gr_ref, cwq_ref, cwk_ref, cbq_ref, cbk_ref,
                  ng_ref, o_ref, qc_sc, kc_sc):
    seq, dqk = q_ref.shape
    dv = v_ref.shape[1]
    L = MLSTM_CHUNK
    nt = (((1,), (1,)), ((), ()))

    def conv_silu(x, w_ref, b_ref):
        y = b_ref[...]
        for j in range(MLSTM_CONV):
            y = y + w_ref[j:j + 1, :] * _causal_shift(x, MLSTM_CONV - 1 - j)
        return y * jax.nn.sigmoid(y)

    qc_sc[...] = conv_silu(q_ref[...], cwq_ref, cbq_ref).astype(qc_sc.dtype)
    kc_sc[...] = (conv_silu(k_ref[...], cwk_ref, cbk_ref) * dqk ** -0.5).astype(kc_sc.dtype)

    t_i = lax.broadcasted_iota(jnp.int32, (L, L), 0)
    s_i = lax.broadcasted_iota(jnp.int32, (L, L), 1)
    causal = s_i <= t_i

    def chunk(c, carry):
        c_st, n_st, m_st = carry
        r0 = pl.multiple_of(c * L, L)
        q = qc_sc[pl.ds(r0, L), :]
        k = kc_sc[pl.ds(r0, L), :]
        v = v_ref[pl.ds(r0, L), :].astype(jnp.bfloat16)
        i_col = gc_ref[0, 0, pl.ds(r0, L), :]
        lf_col = _log_sigmoid(gc_ref[1, 0, pl.ds(r0, L), :])
        i_row = gr_ref[0, 0, 0, pl.ds(c, 1), :]
        lf_row = _log_sigmoid(gr_ref[0, 1, 0, pl.ds(c, 1), :])
        bcum_col = jnp.sum(jnp.where(causal, lf_row, 0.0), axis=1, keepdims=True)
        bcum_row = jnp.sum(jnp.where(t_i <= s_i, lf_col, 0.0), axis=0, keepdims=True)
        b_last = bcum_row[:, L - 1:L]
        dmat = jnp.where(causal, bcum_col - bcum_row + i_row, NEG_BIG)
        a_max = jnp.max(b_last - bcum_row + i_row, axis=1, keepdims=True)
        w_end = jnp.exp(b_last - bcum_col + i_col - a_max)
        inter = bcum_col + m_st
        m_t = jnp.maximum(inter, jnp.max(dmat, axis=1, keepdims=True))
        s_qk = lax.dot_general(q, k, nt, preferred_element_type=jnp.float32) * jnp.exp(dmat - m_t)
        w_inter = jnp.exp(inter - m_t)
        num = (jnp.dot(s_qk.astype(jnp.bfloat16), v, preferred_element_type=jnp.float32)
               + w_inter * jnp.dot(q, c_st.astype(jnp.bfloat16),
                                   preferred_element_type=jnp.float32))
        den = (jnp.sum(s_qk, axis=1, keepdims=True)
               + w_inter * jnp.sum(q.astype(jnp.float32) * n_st, axis=1, keepdims=True))
        h = num / jnp.maximum(jnp.abs(den), jnp.exp(-m_t))
        hc = h - jnp.mean(h, axis=1, keepdims=True)
        hn = hc * lax.rsqrt(jnp.mean(hc * hc, axis=1, keepdims=True) + LN_EPS) * ng_ref[...]
        o_ref[pl.ds(r0, L), :] = (jax.nn.sigmoid(og_ref[pl.ds(r0, L), :]) * hn).astype(o_ref.dtype)
        m_new = jnp.maximum(b_last + m_st, a_max)
        f_sc = jnp.exp(b_last + m_st - m_new)
        i_sc = jnp.exp(a_max - m_new)
        kw = k.astype(jnp.float32) * w_end
        kv = jnp.dot(kw.T.astype(jnp.bfloat16), v, preferred_element_type=jnp.float32)
        c_new = f_sc * c_st + i_sc * kv
        n_new = f_sc * n_st + i_sc * jnp.sum(kw, axis=0, keepdims=True)
        return c_new, n_new, m_new

    init = (jnp.zeros((dqk, dv), jnp.float32), jnp.zeros((1, dqk), jnp.float32),
            jnp.zeros((1, 1), jnp.float32))
    lax.fori_loop(0, seq // L, chunk, init, unroll=4)


def mlstm_pallas(proj_a, gates, conv_w, conv_b, norm_g, n_batch, seq, n_heads, dqk, dv,
                 q_col0, k_col0, v_col0, o_col0):
    nc = seq // MLSTM_CHUNK
    g_col = gates.T.reshape(2, n_heads, n_batch * seq, 1)
    g_row = jnp.moveaxis(gates.reshape(n_batch, nc, MLSTM_CHUNK, 2, n_heads), (3, 4), (1, 2))
    return pl.pallas_call(
        _mlstm_kernel,
        grid=(n_batch, n_heads),
        in_specs=[
            pl.BlockSpec((seq, dqk), lambda b, h: (b, q_col0 // dqk + h)),
            pl.BlockSpec((seq, dqk), lambda b, h: (b, k_col0 // dqk + h)),
            pl.BlockSpec((seq, dv), lambda b, h: (b, v_col0 // dv + h)),
            pl.BlockSpec((seq, dv), lambda b, h: (b, o_col0 // dv + h)),
            pl.BlockSpec((2, 1, seq, 1), lambda b, h: (0, h, b, 0)),
            pl.BlockSpec((1, 2, 1, nc, MLSTM_CHUNK), lambda b, h: (b, 0, h, 0, 0)),
            pl.BlockSpec((MLSTM_CONV, dqk), lambda b, h: (0, h)),
            pl.BlockSpec((MLSTM_CONV, dqk), lambda b, h: (0, n_heads + h)),
            pl.BlockSpec((1, dqk), lambda b, h: (0, h)),
            pl.BlockSpec((1, dqk), lambda b, h: (0, n_heads + h)),
            pl.BlockSpec((1, dv), lambda b, h: (0, h)),
        ],
        out_specs=pl.BlockSpec((seq, dv), lambda b, h: (b, h)),
        out_shape=jax.ShapeDtypeStruct((n_batch * seq, n_heads * dv), jnp.bfloat16),
        scratch_shapes=[pltpu.VMEM((seq, dqk), jnp.bfloat16),
                        pltpu.VMEM((seq, dqk), jnp.bfloat16)],
        compiler_params=pltpu.CompilerParams(
            dimension_semantics=("parallel", "parallel"), vmem_limit_bytes=VMEM_LIMIT),
    )(proj_a, proj_a, proj_a, proj_a, g_col, g_row, conv_w, conv_w,
      conv_b.reshape(1, -1), conv_b.reshape(1, -1), norm_g.reshape(1, -1))


GLA_TILE = 256


def _gla_kernel(q_ref, k_ref, v_ref, go_ref, lr_ref, w2_ref, gb_ref, ng_ref, o_ref, bc_sc, oi_sc,
                *, lr_col0):
    seq, dk = q_ref.shape
    dv = v_ref.shape[1]
    R, C = GLA_TILE, GLA_CHUNK
    scale = dk ** -0.5
    nt = (((1,), (1,)), ((), ()))
    bf16 = jnp.bfloat16
    row = lax.broadcasted_iota(jnp.int32, (R, R), 0)
    col = lax.broadcasted_iota(jnp.int32, (R, R), 1)
    lag = jnp.where((row // C) == (col // C), jnp.where(col <= row, row - col, -1), -1)
    tril16 = jnp.where(lag >= 0, 1.0, 0.0).astype(bf16)
    ones = jnp.ones((dk, R), bf16)
    w2 = w2_ref[...].astype(bf16)

    def tile(i, carry):
        r0 = pl.multiple_of(i * R, R)
        z = jnp.dot(lr_ref[pl.ds(r0, R), lr_col0:lr_col0 + GLA_RANK].astype(bf16), w2,
                    preferred_element_type=jnp.float32) + gb_ref[...]
        la = _log_sigmoid(z) * (1.0 / GLA_TAU)
        la_hi = la.astype(bf16)
        la_lo = (la - la_hi.astype(jnp.float32)).astype(bf16)
        bc = (jnp.dot(tril16, la_hi, preferred_element_type=jnp.float32)
              + jnp.dot(tril16, la_lo, preferred_element_type=jnp.float32))
        bc_sc[pl.ds(r0, R), :] = bc
        q = q_ref[pl.ds(r0, R), :] * scale
        k = k_ref[pl.ds(r0, R), :]
        attn = jnp.zeros((R, R), jnp.float32)
        for d in range(C):
            if d == 0:
                prod = q * k
            else:
                prod = q * pltpu.roll(k, d, 0) * jnp.exp(bc - pltpu.roll(bc, d, 0))
            a_d = jnp.dot(prod.astype(bf16), ones, preferred_element_type=jnp.float32)
            attn = jnp.where(lag == d, a_d, attn)
        oi_sc[pl.ds(r0, R), :] = jnp.dot(attn.astype(bf16), v_ref[pl.ds(r0, R), :].astype(bf16),
                                         preferred_element_type=jnp.float32)
        return carry

    lax.fori_loop(0, seq // R, tile, 0)

    def chunk(c, st):
        r0 = pl.multiple_of(c * C, C)
        bc = bc_sc[pl.ds(r0, C), :]
        bl = bc[C - 1:C, :]
        qt = (q_ref[pl.ds(r0, C), :] * scale * jnp.exp(bc)).astype(bf16)
        kt = (k_ref[pl.ds(r0, C), :] * jnp.exp(bl - bc)).astype(bf16)
        oi_sc[pl.ds(r0, C), :] += lax.dot_general(qt, st.astype(bf16), nt,
                                                  preferred_element_type=jnp.float32)
        vt = v_ref[pl.ds(r0, C), :].T.astype(bf16)
        return jnp.exp(bl) * st + jnp.dot(vt, kt, preferred_element_type=jnp.float32)

    lax.fori_loop(0, seq // C, chunk, jnp.zeros((dv, dk), jnp.float32), unroll=16)

    def finish(i, carry):
        r0 = pl.multiple_of(i * R, R)
        o = oi_sc[pl.ds(r0, R), :]
        on = o * lax.rsqrt(jnp.mean(o * o, axis=1, keepdims=True) + LN_EPS) * ng_ref[...]
        g = go_ref[pl.ds(r0, R), :]
        o_ref[pl.ds(r0, R), :] = (on * (g * jax.nn.sigmoid(g))).astype(o_ref.dtype)
        return carry

    lax.fori_loop(0, seq // R, finish, 0)


def gla_pallas(proj_a, proj_go, proj_c, lr_col0, gate_w2, gate_b, norm_g, n_batch, seq, n_heads,
               dk, dv, q_col0, k_col0, v_col0, go_col0):
    kern = functools.partial(_gla_kernel, lr_col0=lr_col0)
    return pl.pallas_call(
        kern,
        grid=(n_batch, n_heads),
        in_specs=[
            pl.BlockSpec((seq, dk), lambda b, h: (b, q_col0 // dk + h)),
            pl.BlockSpec((seq, dk), lambda b, h: (b, k_col0 // dk + h)),
            pl.BlockSpec((seq, dv), lambda b, h: (b, v_col0 // dv + h)),
            pl.BlockSpec((seq, dv), lambda b, h: (b, go_col0 // dv + h)),
            pl.BlockSpec((seq, proj_c.shape[1]), lambda b, h: (b, 0)),
            pl.BlockSpec((GLA_RANK, dk), lambda b, h: (0, h)),
            pl.BlockSpec((1, dk), lambda b, h: (0, h)),
            pl.BlockSpec((1, dv), lambda b, h: (0, h)),
        ],
        out_specs=pl.BlockSpec((seq, dv), lambda b, h: (b, h)),
        out_shape=jax.ShapeDtypeStruct((n_batch * seq, n_heads * dv), jnp.bfloat16),
        scratch_shapes=[pltpu.VMEM((seq, dk), jnp.float32), pltpu.VMEM((seq, dv), jnp.float32)],
        compiler_params=pltpu.CompilerParams(
            dimension_semantics=("parallel", "parallel"), vmem_limit_bytes=VMEM_LIMIT),
    )(proj_a, proj_a, proj_a, proj_go, proj_c, gate_w2, gate_b.reshape(1, -1),
      norm_g.reshape(1, -1))


CONV_TILE = 256
CONV_HALO = 32


def _conf_kernel(a_ref, g_ref, w_ref, b_ref, ng_ref, nb_ref, o_ref, x_sc):
    seq, ch = a_ref.shape
    R, H = CONV_TILE, CONV_HALO
    x_sc[0:H, :] = jnp.zeros((H, ch), jnp.float32)
    x_sc[H:, :] = a_ref[...] * jax.nn.sigmoid(g_ref[...])

    def tile(i, carry):
        r0 = pl.multiple_of(i * R, R)
        win = x_sc[pl.ds(r0, R + H), :]
        acc = jnp.zeros((R, ch), jnp.float32) + b_ref[...]
        for sub in range(8):
            sh = pltpu.roll(win, sub, 0) if sub else win
            for blk in range(H // 8):
                off = 8 * blk + sub
                if off >= CONV_WIDTH:
                    continue
                j = CONV_WIDTH - 1 - off
                acc = acc + w_ref[j:j + 1, :] * sh[H - 8 * blk:H - 8 * blk + R, :]
        yc = acc - jnp.mean(acc, axis=1, keepdims=True)
        yn = (yc * lax.rsqrt(jnp.mean(yc * yc, axis=1, keepdims=True) + LN_EPS) * ng_ref[...]
              + nb_ref[...])
        o_ref[pl.ds(r0, R), :] = (yn * jax.nn.sigmoid(yn)).astype(o_ref.dtype)
        return carry

    lax.fori_loop(0, seq // R, tile, 0)


def conformer_conv_pallas(proj_a, a_col0, g_col0, dw_w, dw_b, norm_g, norm_b, n_batch, seq,
                          n_groups, width):
    ch = width // n_groups
    vec = pl.BlockSpec((1, ch), lambda b, g: (0, g))
    return pl.pallas_call(
        _conf_kernel,
        grid=(n_batch, n_groups),
        in_specs=[pl.BlockSpec((seq, ch), lambda b, g: (b, a_col0 // ch + g)),
                  pl.BlockSpec((seq, ch), lambda b, g: (b, g_col0 // ch + g)),
                  pl.BlockSpec((CONV_WIDTH, ch), lambda b, g: (0, g)),
                  vec, vec, vec],
        out_specs=pl.BlockSpec((seq, ch), lambda b, g: (b, g)),
        out_shape=jax.ShapeDtypeStruct((n_batch * seq, width), jnp.bfloat16),
        scratch_shapes=[pltpu.VMEM((seq + CONV_HALO, ch), jnp.float32)],
        compiler_params=pltpu.CompilerParams(
            dimension_semantics=("parallel", "parallel"), vmem_limit_bytes=VMEM_LIMIT),
    )(proj_a, proj_a, dw_w, dw_b.reshape(1, -1), norm_g.reshape(1, -1), norm_b.reshape(1, -1))


def hybrid_mixer(ub, B, S, layer, w_in, b_merge, mlstm_gate_b, mlstm_conv_w, mlstm_conv_b,
                 mlstm_norm_g, conf_dw_w, conf_dw_b, conf_norm_g, conf_norm_b, rel_bias_t, moba_btab,
                 gla_gate_w2, gla_gate_b, gla_norm_g, w_branch):
    W = BRANCH_WIDTH
    bf16, f32 = jnp.bfloat16, jnp.float32
    offs = [0] + np.cumsum(IN_SIZES).tolist()
    w_1 = repack_columns(w_in, layer, 0, offs[4])
    w_2 = repack_columns(w_in, layer, offs[5], offs[10] - offs[5])
    w_3 = repack_columns(w_in, layer, offs[11], offs[13] - offs[11])
    proj_ml = matmul(ub, w_1, 512, 1024, f32)
    proj_cf = matmul(ub, w_2, 512, 1024, f32, 0, 2 * W)
    proj_mb = matmul(ub, w_2, 512, 1024, bf16, 2 * W, 3 * W)
    proj_gl = matmul(ub, w_2, 512, 1024, f32, 5 * W, 2 * W)
    proj_go = matmul(ub, w_3, 512, 1024, f32, 0, W)
    proj_b = matmul(ub, w_3, 512, 1024, bf16, W)
    c_if = offs[4] - offs[4] % LANES
    c_lr = offs[10] - offs[10] % LANES
    proj_if = matmul_rows_t(ub, w_in, layer, c_if, LANES, 512, f32)
    proj_lr = matmul_rows_t(ub, w_in, layer, c_lr, LANES, 512, f32)

    gates = proj_if[:, offs[4] - c_if:offs[5] - c_if] + mlstm_gate_b
    y_mlstm = mlstm_pallas(proj_ml, gates, mlstm_conv_w, mlstm_conv_b, mlstm_norm_g, B, S,
                           MLSTM_HEADS, MLSTM_DQK, MLSTM_DV, 0, W // 2, W, 2 * W)
    y_conv = conformer_conv_pallas(proj_cf, 0, W, conf_dw_w, conf_dw_b, conf_norm_g, conf_norm_b,
                                   B, S, CONV_GROUPS, W)
    y_moba = moba_pallas(proj_mb, rel_bias_t, moba_btab, B, S, 0, MOBA_HEADS, 2 * MOBA_HEADS, bf16)
    y_gla = gla_pallas(proj_gl, proj_go, proj_lr, offs[10] - c_lr, gla_gate_w2, gla_gate_b,
                       gla_norm_g, B, S, GLA_HEADS, GLA_DK, GLA_DV, 0, W // 2, W, 0)
    return merge_branches([y_mlstm, y_conv, y_moba, y_gla], w_branch, layer, proj_b, 0, b_merge)


def kernel(x, c, w_ada, b_ada, w_in, b_merge, mlstm_gate_b, mlstm_conv_w, mlstm_conv_b,
           mlstm_norm_g, conf_dw_w, conf_dw_b, conf_norm_g, conf_norm_b, rel_bias,
           gla_gate_w2, gla_gate_b, gla_norm_g, w_branch, w_out, ln1_g, ln1_b,
           router_w, router_b, exp_w_up, exp_b_up, exp_w_down, exp_b_down, ln2_g, ln2_b):
    B, S, D = x.shape
    bf16 = jnp.bfloat16
    mod_all = ada_modulation(c, w_ada, b_ada)
    mods = [jnp.split(mod_all[l][:, None, :], 6, axis=-1) for l in range(DEPTH)]
    rel_bias_t = rel_bias.T.astype(jnp.float32)
    moba_btab = moba_bias_tables(rel_bias_t)
    n_rows = B * S * TOP_K + N_EXPERTS * MOE_TILE
    x2d = x.reshape(B * S, D)
    w_in_t = jnp.swapaxes(w_in, 1, 2)
    ub = modulate(x2d, mods[0][1], mods[0][0], S)
    for l in range(DEPTH):
        sh1, sc1, g1, sh2, sc2, g2 = mods[l]
        merged = hybrid_mixer(ub, B, S, l, w_in_t, b_merge[l], mlstm_gate_b[l], mlstm_conv_w[l],
                              mlstm_conv_b[l], mlstm_norm_g[l], conf_dw_w[l], conf_dw_b[l],
                              conf_norm_g[l], conf_norm_b[l], rel_bias_t, moba_btab,
                              gla_gate_w2[l], gla_gate_b[l], gla_norm_g[l], w_branch)
        mix = matmul_f32w(merged, w_out, l, 512, 512, jnp.float32)
        x1, upk, top_idx, top_w, rank4, counts = ln_router(
            x2d, mix, g1, ln1_g[l], ln1_b[l], sc2, sh2, router_w[l].astype(bf16), router_b[l], S)
        dest, tile_expert, n_tiles = moe_layout(counts, top_idx, rank4, n_rows // MOE_TILE)
        xs = moe_dispatch(upk, dest, n_rows)
        ys = grouped_experts(xs, tile_expert, n_tiles, exp_w_up, exp_b_up, exp_w_down, exp_b_down, l)
        nxt = min(l + 1, DEPTH - 1)
        x2d, ub = combine_ln(dest, top_w, x1, g2, ln2_g[l], ln2_b[l], mods[nxt][1], mods[nxt][0],
                             ys, S)
    return x2d.reshape(B, S, D)
```

```python
import functools
import math

import jax
import jax.numpy as jnp
import numpy as np
from jax import lax
from jax.experimental import pallas as pl
from jax.experimental.pallas import tpu as pltpu

D_MODEL = 4096
BATCH = 4
SEQ = 2048
DEPTH = 2

N_BRANCHES = 4
BRANCH_WIDTH = D_MODEL // N_BRANCHES
MLSTM_HEADS = 4
MLSTM_DV = BRANCH_WIDTH // MLSTM_HEADS
MLSTM_DQK = MLSTM_DV // 2
MLSTM_CONV = 4
MLSTM_CHUNK = 64
CONV_WIDTH = 31
CONV_GROUPS = 4
MOBA_HEADS = 8
MOBA_HD = BRANCH_WIDTH // MOBA_HEADS
MOBA_BLOCK = 256
MOBA_TOPK = 3
MOBA_Q_CHUNK = 32
REL_BUCKETS = 32
REL_MAX_DIST = 128
GLA_HEADS = 4
GLA_DV = BRANCH_WIDTH // GLA_HEADS
GLA_DK = GLA_DV // 2
GLA_RANK = 16
GLA_TAU = 16.0
GLA_CHUNK = 16
N_EXPERTS = 32
TOP_K = 4
EXPERT_FF = D_MODEL // 8
SWIGLU_ALPHA = 1.702
SWIGLU_LIMIT = 7.0
DEEPNORM_ALPHA = (2 * DEPTH) ** 0.25
LN_EPS = 1e-5

IN_SIZES = (
    MLSTM_HEADS * MLSTM_DQK, MLSTM_HEADS * MLSTM_DQK, BRANCH_WIDTH, BRANCH_WIDTH,
    2 * MLSTM_HEADS, 2 * BRANCH_WIDTH, 3 * BRANCH_WIDTH, GLA_HEADS * GLA_DK,
    GLA_HEADS * GLA_DK, BRANCH_WIDTH, GLA_RANK, BRANCH_WIDTH, N_BRANCHES * D_MODEL,
)
IN_WIDTH = sum(IN_SIZES)

VMEM_LIMIT = 56 * 1024 * 1024


def _mm_kernel(a_ref, b_ref, o_ref):
    o_ref[...] = jnp.dot(a_ref[...], b_ref[...],
                         preferred_element_type=jnp.float32).astype(o_ref.dtype)


def matmul(a, b, tm, tn, out_dtype, col0=0, ncols=None):
    M, K = a.shape
    N = b.shape[1] - col0 if ncols is None else ncols
    assert col0 % tn == 0 and N % tn == 0 and M % tm == 0
    jb = col0 // tn
    return pl.pallas_call(
        _mm_kernel,
        grid=(N // tn, M // tm),
        in_specs=[pl.BlockSpec((tm, K), lambda j, i: (i, 0)),
                  pl.BlockSpec((K, tn), lambda j, i: (0, jb + j))],
        out_specs=pl.BlockSpec((tm, tn), lambda j, i: (i, j)),
        out_shape=jax.ShapeDtypeStruct((M, N), out_dtype),
        compiler_params=pltpu.CompilerParams(
            dimension_semantics=("parallel", "parallel"), vmem_limit_bytes=VMEM_LIMIT),
    )(a, b)


def _mm_f32w_kernel(a_ref, b_ref, o_ref, wb_sc):
    @pl.when(pl.program_id(1) == 0)
    def _():
        wb_sc[...] = b_ref[...].astype(wb_sc.dtype)

    o_ref[...] = jnp.dot(a_ref[...], wb_sc[...],
                         preferred_element_type=jnp.float32).astype(o_ref.dtype)


def matmul_f32w(a, b, layer, tm, tn, out_dtype, col0=0, ncols=None):
    M, K = a.shape
    N = b.shape[2] - col0 if ncols is None else ncols
    assert col0 % tn == 0 and N % tn == 0 and M % tm == 0
    jb = col0 // tn
    return pl.pallas_call(
        _mm_f32w_kernel,
        grid=(N // tn, M // tm),
        in_specs=[pl.BlockSpec((tm, K), lambda j, i: (i, 0)),
                  pl.BlockSpec((None, K, tn), lambda j, i: (layer, 0, jb + j))],
        out_specs=pl.BlockSpec((tm, tn), lambda j, i: (i, j)),
        out_shape=jax.ShapeDtypeStruct((M, N), out_dtype),
        scratch_shapes=[pltpu.VMEM((K, tn), jnp.bfloat16)],
        compiler_params=pltpu.CompilerParams(
            dimension_semantics=("parallel", "arbitrary"), vmem_limit_bytes=VMEM_LIMIT),
    )(a, b)


LANES = 128
SUBLANES = 8
REPACK_ROWS = 1024
REPACK_COLS = 1024
REPACK_EDGE = 32


def _repack_kernel(*refs, shift):
    if shift:
        main_ref, edge_ref, o_ref = refs
        x = jnp.concatenate([main_ref[shift:, :], edge_ref[:shift, :]], axis=0)
    else:
        main_ref, o_ref = refs
        x = main_ref[...]
    o_ref[...] = x.T.astype(o_ref.dtype)


def repack_columns(w_t, layer, col0, ncols):
    _, n, k = w_t.shape
    tk, tn = min(REPACK_ROWS, k), REPACK_COLS
    shift = col0 % tn
    base = col0 - shift
    assert shift % SUBLANES == 0 and shift <= REPACK_EDGE and ncols % tn == 0 and k % tk == 0
    in_specs = [pl.BlockSpec((None, tn, tk), lambda j, r: (layer, base // tn + j, r))]
    args = [w_t]
    if shift:
        in_specs.append(pl.BlockSpec(
            (None, REPACK_EDGE, tk), lambda j, r: (layer, (base + (j + 1) * tn) // REPACK_EDGE, r)))
        args.append(w_t)
    return pl.pallas_call(
        functools.partial(_repack_kernel, shift=shift),
        grid=(ncols // tn, k // tk),
        in_specs=in_specs,
        out_specs=pl.BlockSpec((tk, tn), lambda j, r: (r, j)),
        out_shape=jax.ShapeDtypeStruct((k, ncols), jnp.bfloat16),
        compiler_params=pltpu.CompilerParams(
            dimension_semantics=("parallel", "parallel"), vmem_limit_bytes=VMEM_LIMIT),
    )(*args)


def _mm_nt_kernel(a_ref, bt_ref, o_ref):
    o_ref[...] = lax.dot_general(a_ref[...], bt_ref[...].astype(jnp.bfloat16),
                                 (((1,), (1,)), ((), ())),
                                 preferred_element_type=jnp.float32).astype(o_ref.dtype)


def matmul_rows_t(a, w_t, layer, row0, nrows, tm, out_dtype):
    M, K = a.shape
    assert row0 % nrows == 0 and M % tm == 0
    return pl.pallas_call(
        _mm_nt_kernel,
        grid=(M // tm,),
        in_specs=[pl.BlockSpec((tm, K), lambda i: (i, 0)),
                  pl.BlockSpec((None, nrows, K), lambda i: (layer, row0 // nrows, 0))],
        out_specs=pl.BlockSpec((tm, nrows), lambda i: (i, 0)),
        out_shape=jax.ShapeDtypeStruct((M, nrows), out_dtype),
        compiler_params=pltpu.CompilerParams(
            dimension_semantics=("parallel",), vmem_limit_bytes=VMEM_LIMIT),
    )(a, w_t)


ROW_TILE = 256
MOE_TILE = 256


def _ada_kernel(c_ref, w_ref, b_ref, o_ref):
    c = c_ref[...]
    cs = (c * jax.nn.sigmoid(c)).astype(jnp.bfloat16)
    o_ref[0] = jnp.dot(cs, w_ref[0].astype(jnp.bfloat16),
                       preferred_element_type=jnp.float32) + b_ref[0]


def ada_modulation(c, w_ada, b_ada, tn=512):
    n_layers, d, n = w_ada.shape
    nb = c.shape[0]
    c8 = jnp.pad(c, ((0, 8 - nb), (0, 0)))
    out = pl.pallas_call(
        _ada_kernel,
        grid=(n_layers, n // tn),
        in_specs=[pl.BlockSpec((8, d), lambda l, j: (0, 0)),
                  pl.BlockSpec((1, d, tn), lambda l, j: (l, 0, j)),
                  pl.BlockSpec((1, 1, tn), lambda l, j: (l, 0, j))],
        out_specs=pl.BlockSpec((1, 8, tn), lambda l, j: (l, 0, j)),
        out_shape=jax.ShapeDtypeStruct((n_layers, 8, n), jnp.float32),
        compiler_params=pltpu.CompilerParams(
            dimension_semantics=("parallel", "parallel"), vmem_limit_bytes=VMEM_LIMIT),
    )(c8, w_ada, b_ada.reshape(n_layers, 1, n))
    return out[:, :nb]


def _modulate_kernel(x_ref, sc_ref, sh_ref, o_ref):
    o_ref[...] = (x_ref[...] * (1.0 + sc_ref[0]) + sh_ref[0]).astype(o_ref.dtype)


def modulate(x2d, sc, sh, seq):
    t, d = x2d.shape
    per_b = seq // ROW_TILE
    mod_spec = pl.BlockSpec((1, 1, d), lambda i: (i // per_b, 0, 0))
    return pl.pallas_call(
        _modulate_kernel,
        grid=(t // ROW_TILE,),
        in_specs=[pl.BlockSpec((ROW_TILE, d), lambda i: (i, 0)), mod_spec, mod_spec],
        out_specs=pl.BlockSpec((ROW_TILE, d), lambda i: (i, 0)),
        out_shape=jax.ShapeDtypeStruct((t, d), jnp.bfloat16),
        compiler_params=pltpu.CompilerParams(dimension_semantics=("parallel",)),
    )(x2d, sc, sh)


def _merge_kernel(y0, y1, y2, y3, wb_ref, g0, g1, g2, g3, bm_ref, o_ref, wb_sc):
    @pl.when(pl.program_id(1) == 0)
    def _():
        wb_sc[...] = wb_ref[...].astype(wb_sc.dtype)

    acc = None
    for n, (y, g) in enumerate(((y0, g0), (y1, g1), (y2, g2), (y3, g3))):
        gate = jax.nn.sigmoid(g[...].astype(jnp.float32) + bm_ref[n:n + 1, :])
        term = gate * jnp.dot(y[...], wb_sc[n], preferred_element_type=jnp.float32)
        acc = term if acc is None else acc + term
    o_ref[...] = acc.astype(o_ref.dtype)


def merge_branches(ys, w_branch, layer, proj_b, gate_col0, b_merge, tm=512, tn=512):
    t, w = ys[0].shape
    d = w_branch.shape[3]
    assert gate_col0 % tn == 0 and d % tn == 0
    y_spec = pl.BlockSpec((tm, w), lambda j, i: (i, 0))
    g_specs = [pl.BlockSpec((tm, tn), lambda j, i, n=n: (i, (gate_col0 + n * d) // tn + j))
               for n in range(N_BRANCHES)]
    return pl.pallas_call(
        _merge_kernel,
        grid=(d // tn, t // tm),
        in_specs=[y_spec] * 4 + [pl.BlockSpec((None, N_BRANCHES, w, tn),
                                              lambda j, i: (layer, 0, 0, j))]
        + g_specs + [pl.BlockSpec((N_BRANCHES, tn), lambda j, i: (0, j))],
        out_specs=pl.BlockSpec((tm, tn), lambda j, i: (i, j)),
        out_shape=jax.ShapeDtypeStruct((t, d), jnp.bfloat16),
        scratch_shapes=[pltpu.VMEM((N_BRANCHES, w, tn), jnp.bfloat16)],
        compiler_params=pltpu.CompilerParams(
            dimension_semantics=("parallel", "arbitrary"), vmem_limit_bytes=VMEM_LIMIT),
    )(*ys, w_branch, proj_b, proj_b, proj_b, proj_b, b_merge.reshape(N_BRANCHES, d))


def _pack_pair(lo, hi):
    def rne(x):
        b = lax.bitcast_convert_type(x, jnp.uint32)
        return (b + jnp.uint32(0x7FFF) + ((b >> 16) & jnp.uint32(1))) >> 16
    return rne(lo) | (rne(hi) << 16)


def _unpack_lo(w):
    return lax.bitcast_convert_type(w << 16, jnp.float32)


def _unpack_hi(w):
    return lax.bitcast_convert_type(w & jnp.uint32(0xFFFF0000), jnp.float32)


def _layer_norm_halves(z_lo, z_hi, g_ref, b_ref, half):
    d = 2 * half
    mu = (jnp.sum(z_lo, axis=1, keepdims=True) + jnp.sum(z_hi, axis=1, keepdims=True)) / d
    c_lo, c_hi = z_lo - mu, z_hi - mu
    var = (jnp.sum(c_lo * c_lo, axis=1, keepdims=True)
           + jnp.sum(c_hi * c_hi, axis=1, keepdims=True)) / d
    r = lax.rsqrt(var + LN_EPS)
    return (c_lo * r * g_ref[:, :half] + b_ref[:, :half],
            c_hi * r * g_ref[:, half:] + b_ref[:, half:])


def _ln_router_kernel(x_ref, mix_ref, g1_ref, lng_ref, lnb_ref, sc_ref, sh_ref, rw_ref, rb_ref,
                      x1_ref, upk_ref, tidx_ref, tw_ref, rank_ref, cnt_ref, carry_sc):
    tm, d = x_ref.shape
    half = d // 2
    n_exp = rw_ref.shape[1]

    @pl.when(pl.program_id(0) == 0)
    def _():
        carry_sc[...] = jnp.zeros_like(carry_sc)

    z = DEEPNORM_ALPHA * x_ref[...] + (1.0 + g1_ref[0]) * mix_ref[...]
    x_lo, x_hi = _layer_norm_halves(z[:, :half], z[:, half:], lng_ref, lnb_ref, half)
    x1_ref[:, :half] = x_lo
    x1_ref[:, half:] = x_hi
    u_lo = x_lo * (1.0 + sc_ref[0, :, :half]) + sh_ref[0, :, :half]
    u_hi = x_hi * (1.0 + sc_ref[0, :, half:]) + sh_ref[0, :, half:]
    upk_ref[...] = _pack_pair(u_lo, u_hi)

    logits = (jnp.dot(u_lo.astype(jnp.bfloat16), rw_ref[:half, :],
                      preferred_element_type=jnp.float32)
              + jnp.dot(u_hi.astype(jnp.bfloat16), rw_ref[half:, :],
                        preferred_element_type=jnp.float32) + rb_ref[...])
    lane = lax.broadcasted_iota(jnp.int32, (tm, n_exp), 1).astype(jnp.float32)
    lg = logits
    vals, hots = [], []
    for k in range(TOP_K):
        m = jnp.max(lg, axis=1, keepdims=True)
        ik = jnp.min(jnp.where(lg == m, lane, float(n_exp)), axis=1, keepdims=True)
        hot = lane == ik
        vals.append(m)
        hots.append(hot)
        tidx_ref[:, k:k + 1] = ik.astype(jnp.int32)
        lg = jnp.where(hot, -jnp.inf, lg)
    exps = [jnp.exp(v - vals[0]) for v in vals]
    den = exps[0] + exps[1] + exps[2] + exps[3]
    for k in range(TOP_K):
        tw_ref[:, k:k + 1] = exps[k] / den

    mask = jnp.where(hots[0] | hots[1] | hots[2] | hots[3], 1.0, 0.0)
    row = lax.broadcasted_iota(jnp.int32, (tm, tm), 0)
    col = lax.broadcasted_iota(jnp.int32, (tm, tm), 1)
    lower = jnp.where(row > col, 1.0, 0.0).astype(jnp.bfloat16)
    rank = jnp.dot(lower, mask.astype(jnp.bfloat16),
                   preferred_element_type=jnp.float32) + carry_sc[...]
    for k in range(TOP_K):
        rank_ref[:, k:k + 1] = jnp.sum(jnp.where(hots[k], rank, 0.0), axis=1,
                                       keepdims=True).astype(jnp.int32)
    carry_sc[...] = carry_sc[...] + jnp.sum(mask, axis=0, keepdims=True)
    cnt_ref[...] = carry_sc[...]


def ln_router(x2d, mix, g1, ln_g, ln_b, sc2, sh2, router_w_bf, router_b, seq):
    t, d = x2d.shape
    n_exp = router_w_bf.shape[1]
    per_b = seq // ROW_TILE
    row = lambda w: pl.BlockSpec((ROW_TILE, w), lambda i: (i, 0))
    mod = pl.BlockSpec((1, 1, d), lambda i: (i // per_b, 0, 0))
    vec = lambda w: pl.BlockSpec((1, w), lambda i: (0, 0))
    return pl.pallas_call(
        _ln_router_kernel,
        grid=(t // ROW_TILE,),
        in_specs=[row(d), row(d), mod, vec(d), vec(d), mod, mod,
                  pl.BlockSpec((d, n_exp), lambda i: (0, 0)), vec(n_exp)],
        out_specs=[row(d), row(d // 2), row(TOP_K), row(TOP_K), row(TOP_K), vec(n_exp)],
        out_shape=[jax.ShapeDtypeStruct((t, d), jnp.float32),
                   jax.ShapeDtypeStruct((t, d // 2), jnp.uint32),
                   jax.ShapeDtypeStruct((t, TOP_K), jnp.int32),
                   jax.ShapeDtypeStruct((t, TOP_K), jnp.float32),
                   jax.ShapeDtypeStruct((t, TOP_K), jnp.int32),
                   jax.ShapeDtypeStruct((1, n_exp), jnp.float32)],
        scratch_shapes=[pltpu.VMEM((1, n_exp), jnp.float32)],
        compiler_params=pltpu.CompilerParams(
            dimension_semantics=("arbitrary",), vmem_limit_bytes=VMEM_LIMIT),
    )(x2d, mix, g1, ln_g.reshape(1, d), ln_b.reshape(1, d), sc2, sh2, router_w_bf,
      router_b.reshape(1, n_exp))


def _row_copy(src_hbm, src_row, dst, dst_row, sem):
    return pltpu.make_async_copy(src_hbm.at[pl.ds(src_row, 1)], dst.at[pl.ds(dst_row, 1)], sem)


def _dispatch_kernel(dest_ref, last_ref, u_ref, xs_hbm, zero_sc, sem, zsem):
    n_exp = last_ref.shape[0] - 1
    n_tiles_max = xs_hbm.shape[0] // MOE_TILE

    @pl.when(pl.program_id(0) == 0)
    def _():
        zero_sc[...] = jnp.zeros_like(zero_sc)

        def fill(row):
            return pltpu.make_async_copy(
                zero_sc, xs_hbm.at[pl.ds(pl.multiple_of(row, MOE_TILE), MOE_TILE)], zsem)

        def start(ex, carry):
            @pl.when(last_ref[ex] >= 0)
            def _():
                fill(last_ref[ex]).start()
            return carry

        def finish(ex, carry):
            @pl.when(last_ref[ex] >= 0)
            def _():
                fill(0).wait()
            return carry

        def start_tail(g, carry):
            fill(g * MOE_TILE).start()
            return carry

        def finish_tail(g, carry):
            fill(0).wait()
            return carry

        lax.fori_loop(0, n_exp, start, 0)
        lax.fori_loop(last_ref[n_exp], n_tiles_max, start_tail, 0)
        lax.fori_loop(0, n_exp, finish, 0)
        lax.fori_loop(last_ref[n_exp], n_tiles_max, finish_tail, 0)

    def issue(r, carry):
        for k in range(TOP_K):
            _row_copy(u_ref, r, xs_hbm, dest_ref[r * TOP_K + k], sem).start(priority=k % 2)
        return carry

    def drain(r, carry):
        for k in range(TOP_K):
            _row_copy(u_ref, 0, xs_hbm, 0, sem).wait()
        return carry

    lax.fori_loop(0, ROW_TILE, issue, 0)
    lax.fori_loop(0, ROW_TILE, drain, 0)


def moe_dispatch(upk, dest_flat, last_tile_row, n_rows):
    t, w = upk.shape
    return pl.pallas_call(
        _dispatch_kernel,
        grid=(t // ROW_TILE,),
        in_specs=[pl.BlockSpec((ROW_TILE * TOP_K,), lambda i: (i,), memory_space=pltpu.SMEM),
                  pl.BlockSpec(memory_space=pltpu.SMEM),
                  pl.BlockSpec((ROW_TILE, w), lambda i: (i, 0))],
        out_specs=pl.BlockSpec(memory_space=pl.ANY),
        out_shape=jax.ShapeDtypeStruct((n_rows, w), upk.dtype),
        scratch_shapes=[pltpu.VMEM((MOE_TILE, w), upk.dtype),
                        pltpu.SemaphoreType.DMA(()), pltpu.SemaphoreType.DMA(())],
        compiler_params=pltpu.CompilerParams(dimension_semantics=("arbitrary",)),
    )(dest_flat, last_tile_row, upk)


UP_CHUNKS = 4


def _expert_kernel(te_ref, nt_ref, xs_ref, wu_hbm, bu_ref, wd_hbm, bd_ref, ys_ref,
                   stage_u, stage_d, wu_sc, wd_sc, sem, *, layer):
    g = pl.program_id(0)
    half = xs_ref.shape[1]
    ff = wd_sc.shape[0]
    rows = stage_u.shape[1]
    e = te_ref[g]
    new_expert = jnp.logical_or(g == 0, e != te_ref[jnp.maximum(g - 1, 0)])

    def up_copy(ex, c):
        return pltpu.make_async_copy(wu_hbm.at[layer, ex, pl.ds(c * rows, rows)],
                                     stage_u.at[c % 2], sem.at[c % 2])

    def down_copy(ex):
        return pltpu.make_async_copy(wd_hbm.at[layer, ex], stage_d, sem.at[2])

    def start_head(ex):
        up_copy(ex, 0).start()
        up_copy(ex, 1).start()
        down_copy(ex).start()

    @pl.when(jnp.logical_and(g < nt_ref[0], new_expert))
    def _():
        @pl.when(g == 0)
        def _():
            start_head(e)

        for c in range(UP_CHUNKS):
            up_copy(e, c).wait()
            wu_sc[c * rows:(c + 1) * rows, :] = stage_u[c % 2].astype(wu_sc.dtype)
            if c + 2 < UP_CHUNKS:
                up_copy(e, c + 2).start()
        down_copy(e).wait()
        wd_sc[...] = stage_d[...].astype(wd_sc.dtype)

    g_next = jnp.minimum(g + 1, pl.num_programs(0) - 1)
    e_next = te_ref[g_next]

    @pl.when(jnp.logical_and(g + 1 < nt_ref[0], e_next != e))
    def _():
        start_head(e_next)

    @pl.when(g < nt_ref[0])
    def _():
        w = xs_ref[...]
        x_lo = _unpack_lo(w).astype(jnp.bfloat16)
        x_hi = _unpack_hi(w).astype(jnp.bfloat16)
        hid = (jnp.dot(x_lo, wu_sc[:half, :], preferred_element_type=jnp.float32)
               + jnp.dot(x_hi, wu_sc[half:, :], preferred_element_type=jnp.float32)
               + bu_ref[0, 0])
        glu = jnp.minimum(hid[:, :ff], SWIGLU_LIMIT)
        lin = jnp.clip(hid[:, ff:], -SWIGLU_LIMIT, SWIGLU_LIMIT)
        act = glu * jax.nn.sigmoid(SWIGLU_ALPHA * glu) * (lin + 1.0)
        y = jnp.dot(act.astype(jnp.bfloat16), wd_sc[...],
                    preferred_element_type=jnp.float32) + bd_ref[0, 0]
        ys_ref[...] = _pack_pair(y[:, :half], y[:, half:])

    @pl.when(g >= nt_ref[0])
    def _():
        ys_ref[...] = jnp.zeros_like(ys_ref)


def grouped_experts(xs, tile_expert, n_tiles, w_up, b_up, w_down, b_down, layer):
    n_rows, half = xs.shape
    _, n_exp, d, ff2 = w_up.shape
    ff = ff2 // 2
    grid_spec = pltpu.PrefetchScalarGridSpec(
        num_scalar_prefetch=2,
        grid=(n_rows // MOE_TILE,),
        in_specs=[pl.BlockSpec((MOE_TILE, half), lambda g, te, nt: (jnp.minimum(g, nt[0] - 1), 0)),
                  pl.BlockSpec(memory_space=pl.ANY),
                  pl.BlockSpec((1, 1, 1, ff2), lambda g, te, nt: (layer, te[g], 0, 0)),
                  pl.BlockSpec(memory_space=pl.ANY),
                  pl.BlockSpec((1, 1, 1, d), lambda g, te, nt: (layer, te[g], 0, 0))],
        out_specs=pl.BlockSpec((MOE_TILE, half), lambda g, te, nt: (g, 0)),
        scratch_shapes=[pltpu.VMEM((2, d // UP_CHUNKS, ff2), jnp.float32),
                        pltpu.VMEM((ff, d), jnp.float32),
                        pltpu.VMEM((d, ff2), jnp.bfloat16),
                        pltpu.VMEM((ff, d), jnp.bfloat16),
                        pltpu.SemaphoreType.DMA((3,))],
    )
    return pl.pallas_call(
        functools.partial(_expert_kernel, layer=layer),
        grid_spec=grid_spec,
        out_shape=jax.ShapeDtypeStruct((n_rows, half), jnp.uint32),
        compiler_params=pltpu.CompilerParams(
            dimension_semantics=("arbitrary",), vmem_limit_bytes=VMEM_LIMIT),
    )(tile_expert, n_tiles, xs, w_up, b_up.reshape(b_up.shape[0], n_exp, 1, ff2), w_down,
      b_down.reshape(b_down.shape[0], n_exp, 1, d))


COMBINE_ROWS = 16


def _combine_kernel(dest_ref, dest_next_ref, tw_ref, x1_ref, g2_ref, lng_ref, lnb_ref, sc_ref,
                    sh_ref, ys_hbm, x2_ref, un_ref, buf, sem):
    tm, d = x1_ref.shape
    half = d // 2
    i = pl.program_id(0)
    slot = lax.rem(i, 2)

    has_next = i + 1 < pl.num_programs(0)
    sub = COMBINE_ROWS

    def gather_rows(d_ref, s, r0):
        for rr in range(sub):
            for k in range(TOP_K):
                _row_copy(ys_hbm, d_ref[(r0 + rr) * TOP_K + k], buf.at[s, k], r0 + rr,
                          sem.at[s]).start(priority=k % 2)

    @pl.when(i == 0)
    def _():
        def first(t, carry):
            gather_rows(dest_ref, 0, t * sub)
            return carry
        lax.fori_loop(0, tm // sub, first, 0)

    def drain(r, carry):
        for k in range(TOP_K):
            _row_copy(ys_hbm, 0, buf.at[slot, k], 0, sem.at[slot]).wait()
        return carry

    lax.fori_loop(0, tm, drain, 0)

    def rows(t, carry):
        r0 = pl.multiple_of(t * sub, sub)
        rs = pl.ds(r0, sub)

        gather_rows(dest_next_ref, 1 - slot, r0)

        f_lo = f_hi = None
        for k in range(TOP_K):
            wk = tw_ref[rs, k:k + 1]
            w = buf[slot, k, rs, :]
            t_lo, t_hi = wk * _unpack_lo(w), wk * _unpack_hi(w)
            f_lo = t_lo if f_lo is None else f_lo + t_lo
            f_hi = t_hi if f_hi is None else f_hi + t_hi
        z_lo = DEEPNORM_ALPHA * x1_ref[rs, :half] + (1.0 + g2_ref[0, :, :half]) * f_lo
        z_hi = DEEPNORM_ALPHA * x1_ref[rs, half:] + (1.0 + g2_ref[0, :, half:]) * f_hi
        x_lo, x_hi = _layer_norm_halves(z_lo, z_hi, lng_ref, lnb_ref, half)
        x2_ref[rs, :half] = x_lo
        x2_ref[rs, half:] = x_hi
        un_ref[rs, :half] = (x_lo * (1.0 + sc_ref[0, :, :half])
                             + sh_ref[0, :, :half]).astype(un_ref.dtype)
        un_ref[rs, half:] = (x_hi * (1.0 + sc_ref[0, :, half:])
                             + sh_ref[0, :, half:]).astype(un_ref.dtype)
        return carry

    lax.fori_loop(0, tm // sub, rows, 0)

    @pl.when(jnp.logical_not(has_next))
    def _():
        def drain_other(r, carry):
            for k in range(TOP_K):
                _row_copy(ys_hbm, 0, buf.at[1 - slot, k], 0, sem.at[1 - slot]).wait()
            return carry
        lax.fori_loop(0, tm, drain_other, 0)


def combine_ln(dest_flat, top_w, x1, g2, ln_g, ln_b, sc_next, sh_next, ys, seq):
    t, d = x1.shape
    per_b = seq // ROW_TILE
    n_steps = t // ROW_TILE
    row = lambda w: pl.BlockSpec((ROW_TILE, w), lambda i: (i, 0))
    mod = pl.BlockSpec((1, 1, d), lambda i: (i // per_b, 0, 0))
    vec = pl.BlockSpec((1, d), lambda i: (0, 0))
    return pl.pallas_call(
        _combine_kernel,
        grid=(n_steps,),
        in_specs=[pl.BlockSpec((ROW_TILE * TOP_K,), lambda i: (i,), memory_space=pltpu.SMEM),
                  pl.BlockSpec((ROW_TILE * TOP_K,), lambda i: (jnp.minimum(i + 1, n_steps - 1),),
                               memory_space=pltpu.SMEM),
                  row(TOP_K), row(d), mod, vec, vec, mod, mod,
                  pl.BlockSpec(memory_space=pl.ANY)],
        out_specs=[row(d), row(d)],
        out_shape=[jax.ShapeDtypeStruct((t, d), jnp.float32),
                   jax.ShapeDtypeStruct((t, d), jnp.bfloat16)],
        scratch_shapes=[pltpu.VMEM((2, TOP_K, ROW_TILE, d // 2), jnp.uint32),
                        pltpu.SemaphoreType.DMA((2,))],
        compiler_params=pltpu.CompilerParams(
            dimension_semantics=("arbitrary",), vmem_limit_bytes=VMEM_LIMIT),
    )(dest_flat, dest_flat, top_w, x1, g2, ln_g.reshape(1, d), ln_b.reshape(1, d), sc_next,
      sh_next, ys)


def moe_layout(counts, top_idx, rank4, n_tiles_max):
    cnt = counts.reshape(-1).astype(jnp.int32)
    tiles = (cnt + MOE_TILE - 1) // MOE_TILE
    tile_end = jnp.cumsum(tiles)
    row_start = (tile_end - tiles) * MOE_TILE
    dest = (row_start[top_idx] + rank4).reshape(-1)
    tile_expert = jnp.minimum(
        jnp.sum(jnp.arange(n_tiles_max)[:, None] >= tile_end[None, :], axis=1),
        cnt.shape[0] - 1).astype(jnp.int32)
    last_tile_row = jnp.concatenate(
        [jnp.where(tiles > 0, (tile_end - 1) * MOE_TILE, -1), tile_end[-1:]]).astype(jnp.int32)
    return dest.astype(jnp.int32), tile_expert, tile_end[-1:].astype(jnp.int32), last_tile_row


NEG_BIG = -1e30
MOBA_HEADS_PER_STEP = 2


def _t5_bucket_np(d):
    max_exact = REL_BUCKETS // 2
    dd = np.maximum(d, 1).astype(np.float32)
    far = max_exact + (np.log(dd / np.float32(max_exact))
                       / np.float32(math.log(REL_MAX_DIST / max_exact))
                       * np.float32(REL_BUCKETS - max_exact)).astype(np.int32)
    return np.where(d < max_exact, d, np.minimum(far, REL_BUCKETS - 1)).astype(np.int32)


def _moba_bucket_tables():
    c = np.arange(MOBA_BLOCK)[:, None]
    r = np.arange(MOBA_BLOCK)[None, :]
    own = np.where(r >= c, _t5_bucket_np(np.maximum(r - c, 0)), -1)
    prev = _t5_bucket_np(MOBA_BLOCK + r - c)
    return np.stack([own, prev]).astype(np.int32)


def _bias_tab_kernel(rb_ref, bk_ref, o_ref):
    h = pl.program_id(0)
    bk = bk_ref[...]
    acc = jnp.zeros(bk.shape, jnp.float32)
    for m in range(REL_BUCKETS):
        acc = jnp.where(bk == m, rb_ref[h, m], acc)
    o_ref[0] = jnp.where(bk < 0, NEG_BIG, acc)


def moba_bias_tables(rel_bias_t):
    n_heads = rel_bias_t.shape[0]
    bk = jnp.asarray(_moba_bucket_tables())
    return pl.pallas_call(
        _bias_tab_kernel,
        grid=(n_heads,),
        in_specs=[pl.BlockSpec(memory_space=pltpu.SMEM),
                  pl.BlockSpec((2, MOBA_BLOCK, MOBA_BLOCK), lambda h: (0, 0, 0))],
        out_specs=pl.BlockSpec((1, 2, MOBA_BLOCK, MOBA_BLOCK), lambda h: (h, 0, 0, 0)),
        out_shape=jax.ShapeDtypeStruct((n_heads, 2, MOBA_BLOCK, MOBA_BLOCK), jnp.float32),
    )(rel_bias_t, bk)


def _moba_kernel(rb_ref, q_ref, k_ref, v_ref, bt_ref, o_ref, km_sc, vt_sc, m_sc, l_sc, acc_sc,
                 *, n_blk):
    hp = pl.program_id(1)
    i = pl.program_id(2)
    blk, hd, nh = MOBA_BLOCK, MOBA_HD, MOBA_HEADS_PER_STEP
    seq = n_blk * blk
    scale = hd ** -0.5
    nt = (((1,), (1,)), ((), ()))
    bf16 = jnp.bfloat16
    cols = [slice(a * hd, (a + 1) * hd) for a in range(nh)]

    @pl.when(i == 0)
    def _():
        blk_of_col = lax.broadcasted_iota(jnp.int32, (n_blk, seq), 1) // blk
        blk_row = lax.broadcasted_iota(jnp.int32, (n_blk, seq), 0)
        avg = jnp.where(blk_of_col == blk_row, 1.0 / blk, 0.0).astype(bf16)
        km_sc[...] = jnp.dot(avg, k_ref[...], preferred_element_type=jnp.float32)
        for a in range(nh):
            for j in range(n_blk):
                vt_sc[a, j] = v_ref[j * blk:(j + 1) * blk, cols[a]].astype(
                    jnp.float32).T.astype(bf16)

    qs = [q_ref[:, cols[a]] for a in range(nh)]
    gates = []
    for a in range(nh):
        kmean = km_sc[:, cols[a]]
        km_hi = kmean.astype(bf16)
        km_lo = (kmean - km_hi.astype(jnp.float32)).astype(bf16)
        gate = (lax.dot_general(km_hi, qs[a], nt, preferred_element_type=jnp.float32)
                + lax.dot_general(km_lo, qs[a], nt, preferred_element_type=jnp.float32))
        gates.append([gate[n:n + 1, :] for n in range(n_blk)])

    def selected(a, j):
        g = gates[a]
        cnt = jnp.zeros((1, blk), jnp.float32)
        for n in range(n_blk - 1):
            if n == j:
                continue
            ahead = (g[n] >= g[j]) if n < j else (g[n] > g[j])
            cnt = cnt + jnp.where(ahead, jnp.where(n < i, 1.0, 0.0), 0.0)
        return cnt < float(MOBA_TOPK)

    row0 = pl.multiple_of(i * blk, blk)
    for a in range(nh):
        k_own = k_ref[pl.ds(row0, blk), cols[a]]
        s = (lax.dot_general(k_own, qs[a], nt, preferred_element_type=jnp.float32) * scale
             + bt_ref[a, 0])
        m0 = jnp.max(s, axis=0, keepdims=True)
        p = jnp.exp(s - m0)
        m_sc[a] = m0
        l_sc[a] = jnp.sum(p, axis=0, keepdims=True)
        acc_sc[a] = jnp.dot(vt_sc[a, i], p.astype(bf16), preferred_element_type=jnp.float32)

    for j in range(n_blk - 1):
        @pl.when(j < i)
        def _():
            for a in range(nh):
                bfar = rb_ref[hp * nh + a, REL_BUCKETS - 1]
                k_j = k_ref[j * blk:(j + 1) * blk, cols[a]]
                bias = jnp.where(j == i - 1, bt_ref[a, 1], bfar)
                s = (lax.dot_general(k_j, qs[a], nt, preferred_element_type=jnp.float32) * scale
                     + bias)
                s = jnp.where(selected(a, j), s, NEG_BIG)
                m_old = m_sc[a]
                m_new = jnp.maximum(m_old, jnp.max(s, axis=0, keepdims=True))
                alpha = jnp.exp(m_old - m_new)
                p = jnp.exp(s - m_new)
                m_sc[a] = m_new
                l_sc[a] = alpha * l_sc[a] + jnp.sum(p, axis=0, keepdims=True)
                acc_sc[a] = alpha * acc_sc[a] + jnp.dot(
                    vt_sc[a, j], p.astype(bf16), preferred_element_type=jnp.float32)

    for a in range(nh):
        o_ref[:, cols[a]] = (acc_sc[a] / l_sc[a]).T.astype(o_ref.dtype)


def moba_pallas(qkv, rel_bias_t, btab, n_batch, seq, q_blk0, k_blk0, v_blk0, out_dtype):
    n_heads = rel_bias_t.shape[0]
    n_blk = seq // MOBA_BLOCK
    nh = MOBA_HEADS_PER_STEP
    wide = nh * MOBA_HD
    assert seq % MOBA_BLOCK == 0 and n_heads % nh == 0
    assert q_blk0 % nh == 0 and k_blk0 % nh == 0 and v_blk0 % nh == 0
    kern = functools.partial(_moba_kernel, n_blk=n_blk)
    return pl.pallas_call(
        kern,
        grid=(n_batch, n_heads // nh, n_blk),
        in_specs=[
            pl.BlockSpec(memory_space=pltpu.SMEM),
            pl.BlockSpec((MOBA_BLOCK, wide), lambda b, h, i: (b * n_blk + i, q_blk0 // nh + h)),
            pl.BlockSpec((seq, wide), lambda b, h, i: (b, k_blk0 // nh + h)),
            pl.BlockSpec((seq, wide), lambda b, h, i: (b, v_blk0 // nh + h)),
            pl.BlockSpec((nh, 2, MOBA_BLOCK, MOBA_BLOCK), lambda b, h, i: (h, 0, 0, 0)),
        ],
        out_specs=pl.BlockSpec((MOBA_BLOCK, wide), lambda b, h, i: (b * n_blk + i, h)),
        out_shape=jax.ShapeDtypeStruct((n_batch * seq, n_heads * MOBA_HD), out_dtype),
        scratch_shapes=[pltpu.VMEM((n_blk, wide), jnp.float32),
                        pltpu.VMEM((nh, n_blk, MOBA_HD, MOBA_BLOCK), jnp.bfloat16),
                        pltpu.VMEM((nh, 1, MOBA_BLOCK), jnp.float32),
                        pltpu.VMEM((nh, 1, MOBA_BLOCK), jnp.float32),
                        pltpu.VMEM((nh, MOBA_HD, MOBA_BLOCK), jnp.float32)],
        compiler_params=pltpu.CompilerParams(
            dimension_semantics=("parallel", "parallel", "arbitrary")),
    )(rel_bias_t, qkv, qkv, qkv, btab)


def _log_sigmoid(x):
    return jnp.minimum(x, 0.0) - jnp.log(1.0 + jnp.exp(-jnp.abs(x)))


def _causal_shift(x, shift):
    if shift == 0:
        return x
    row = lax.broadcasted_iota(jnp.int32, x.shape, 0)
    return jnp.where(row >= shift, pltpu.roll(x, shift, 0), 0.0)


def _mlstm_kernel(q_ref, k_ref, v_ref, og_ref, gc_ref, gr_ref, cwq_ref, cwk_ref, cbq_ref, cbk_ref,
                  ng_ref, o_ref, qc_sc, kc_sc):
    seq, dqk = q_ref.shape
    dv = v_ref.shape[1]
    L = MLSTM_CHUNK
    nt = (((1,), (1,)), ((), ()))

    def conv_silu(x, w_ref, b_ref):
        y = b_ref[...]
        for j in range(MLSTM_CONV):
            y = y + w_ref[j:j + 1, :] * _causal_shift(x, MLSTM_CONV - 1 - j)
        return y * jax.nn.sigmoid(y)

    qc_sc[...] = conv_silu(q_ref[...], cwq_ref, cbq_ref).astype(qc_sc.dtype)
    kc_sc[...] = (conv_silu(k_ref[...], cwk_ref, cbk_ref) * dqk ** -0.5).astype(kc_sc.dtype)

    t_i = lax.broadcasted_iota(jnp.int32, (L, L), 0)
    s_i = lax.broadcasted_iota(jnp.int32, (L, L), 1)
    causal = s_i <= t_i

    def chunk(c, carry):
        c_st, n_st, m_st = carry
        r0 = pl.multiple_of(c * L, L)
        q = qc_sc[pl.ds(r0, L), :]
        k = kc_sc[pl.ds(r0, L), :]
        v = v_ref[pl.ds(r0, L), :].astype(jnp.bfloat16)
        i_col = gc_ref[0, 0, pl.ds(r0, L), :]
        lf_col = _log_sigmoid(gc_ref[1, 0, pl.ds(r0, L), :])
        i_row = gr_ref[0, 0, 0, pl.ds(c, 1), :]
        lf_row = _log_sigmoid(gr_ref[0, 1, 0, pl.ds(c, 1), :])
        bcum_col = jnp.sum(jnp.where(causal, lf_row, 0.0), axis=1, keepdims=True)
        bcum_row = jnp.sum(jnp.where(t_i <= s_i, lf_col, 0.0), axis=0, keepdims=True)
        b_last = bcum_row[:, L - 1:L]
        dmat = jnp.where(causal, bcum_col - bcum_row + i_row, NEG_BIG)
        a_max = jnp.max(b_last - bcum_row + i_row, axis=1, keepdims=True)
        w_end = jnp.exp(b_last - bcum_col + i_col - a_max)
        inter = bcum_col + m_st
        m_t = jnp.maximum(inter, jnp.max(dmat, axis=1, keepdims=True))
        s_qk = lax.dot_general(q, k, nt, preferred_element_type=jnp.float32) * jnp.exp(dmat - m_t)
        w_inter = jnp.exp(inter - m_t)
        num = (jnp.dot(s_qk.astype(jnp.bfloat16), v, preferred_element_type=jnp.float32)
               + w_inter * jnp.dot(q, c_st.astype(jnp.bfloat16),
                                   preferred_element_type=jnp.float32))
        den = (jnp.sum(s_qk, axis=1, keepdims=True)
               + w_inter * jnp.sum(q.astype(jnp.float32) * n_st, axis=1, keepdims=True))
        h = num / jnp.maximum(jnp.abs(den), jnp.exp(-m_t))
        hc = h - jnp.mean(h, axis=1, keepdims=True)
        hn = hc * lax.rsqrt(jnp.mean(hc * hc, axis=1, keepdims=True) + LN_EPS) * ng_ref[...]
        o_ref[pl.ds(r0, L), :] = (jax.nn.sigmoid(og_ref[pl.ds(r0, L), :]) * hn).astype(o_ref.dtype)
        m_new = jnp.maximum(b_last + m_st, a_max)
        f_sc = jnp.exp(b_last + m_st - m_new)
        i_sc = jnp.exp(a_max - m_new)
        kw = k.astype(jnp.float32) * w_end
        kv = jnp.dot(kw.T.astype(jnp.bfloat16), v, preferred_element_type=jnp.float32)
        c_new = f_sc * c_st + i_sc * kv
        n_new = f_sc * n_st + i_sc * jnp.sum(kw, axis=0, keepdims=True)
        return c_new, n_new, m_new

    init = (jnp.zeros((dqk, dv), jnp.float32), jnp.zeros((1, dqk), jnp.float32),
            jnp.zeros((1, 1), jnp.float32))
    lax.fori_loop(0, seq // L, chunk, init, unroll=4)


def mlstm_pallas(proj_a, gates, conv_w, conv_b, norm_g, n_batch, seq, n_heads, dqk, dv,
                 q_col0, k_col0, v_col0, o_col0):
    nc = seq // MLSTM_CHUNK
    g_col = gates.T.reshape(2, n_heads, n_batch * seq, 1)
    g_row = jnp.moveaxis(gates.reshape(n_batch, nc, MLSTM_CHUNK, 2, n_heads), (3, 4), (1, 2))
    return pl.pallas_call(
        _mlstm_kernel,
        grid=(n_batch, n_heads),
        in_specs=[
            pl.BlockSpec((seq, dqk), lambda b, h: (b, q_col0 // dqk + h)),
            pl.BlockSpec((seq, dqk), lambda b, h: (b, k_col0 // dqk + h)),
            pl.BlockSpec((seq, dv), lambda b, h: (b, v_col0 // dv + h)),
            pl.BlockSpec((seq, dv), lambda b, h: (b, o_col0 // dv + h)),
            pl.BlockSpec((2, 1, seq, 1), lambda b, h: (0, h, b, 0)),
            pl.BlockSpec((1, 2, 1, nc, MLSTM_CHUNK), lambda b, h: (b, 0, h, 0, 0)),
            pl.BlockSpec((MLSTM_CONV, dqk), lambda b, h: (0, h)),
            pl.BlockSpec((MLSTM_CONV, dqk), lambda b, h: (0, n_heads + h)),
            pl.BlockSpec((1, dqk), lambda b, h: (0, h)),
            pl.BlockSpec((1, dqk), lambda b, h: (0, n_heads + h)),
            pl.BlockSpec((1, dv), lambda b, h: (0, h)),
        ],
        out_specs=pl.BlockSpec((seq, dv), lambda b, h: (b, h)),
        out_shape=jax.ShapeDtypeStruct((n_batch * seq, n_heads * dv), jnp.bfloat16),
        scratch_shapes=[pltpu.VMEM((seq, dqk), jnp.bfloat16),
                        pltpu.VMEM((seq, dqk), jnp.bfloat16)],
        compiler_params=pltpu.CompilerParams(
            dimension_semantics=("parallel", "parallel"), vmem_limit_bytes=VMEM_LIMIT),
    )(proj_a, proj_a, proj_a, proj_a, g_col, g_row, conv_w, conv_w,
      conv_b.reshape(1, -1), conv_b.reshape(1, -1), norm_g.reshape(1, -1))


GLA_TILE = 256


def _gla_kernel(q_ref, k_ref, v_ref, go_ref, lr_ref, w2_ref, gb_ref, ng_ref, o_ref, bc_sc, oi_sc,
                *, lr_col0):
    seq, dk = q_ref.shape
    dv = v_ref.shape[1]
    R, C = GLA_TILE, GLA_CHUNK
    scale = dk ** -0.5
    nt = (((1,), (1,)), ((), ()))
    bf16 = jnp.bfloat16
    row = lax.broadcasted_iota(jnp.int32, (R, R), 0)
    col = lax.broadcasted_iota(jnp.int32, (R, R), 1)
    lag = jnp.where((row // C) == (col // C), jnp.where(col <= row, row - col, -1), -1)
    tril16 = jnp.where(lag >= 0, 1.0, 0.0).astype(bf16)
    ones = jnp.ones((dk, R), bf16)
    w2 = w2_ref[...].astype(bf16)

    def tile(i, carry):
        r0 = pl.multiple_of(i * R, R)
        z = jnp.dot(lr_ref[pl.ds(r0, R), lr_col0:lr_col0 + GLA_RANK].astype(bf16), w2,
                    preferred_element_type=jnp.float32) + gb_ref[...]
        la = _log_sigmoid(z) * (1.0 / GLA_TAU)
        la_hi = la.astype(bf16)
        la_lo = (la - la_hi.astype(jnp.float32)).astype(bf16)
        bc = (jnp.dot(tril16, la_hi, preferred_element_type=jnp.float32)
              + jnp.dot(tril16, la_lo, preferred_element_type=jnp.float32))
        bc_sc[pl.ds(r0, R), :] = bc
        q = q_ref[pl.ds(r0, R), :] * scale
        k = k_ref[pl.ds(r0, R), :]
        attn = jnp.zeros((R, R), jnp.float32)
        for d in range(C):
            if d == 0:
                prod = q * k
            else:
                prod = q * pltpu.roll(k, d, 0) * jnp.exp(bc - pltpu.roll(bc, d, 0))
            a_d = jnp.dot(prod.astype(bf16), ones, preferred_element_type=jnp.float32)
            attn = jnp.where(lag == d, a_d, attn)
        oi_sc[pl.ds(r0, R), :] = jnp.dot(attn.astype(bf16), v_ref[pl.ds(r0, R), :].astype(bf16),
                                         preferred_element_type=jnp.float32)
        return carry

    lax.fori_loop(0, seq // R, tile, 0)

    def chunk(c, st):
        r0 = pl.multiple_of(c * C, C)
        bc = bc_sc[pl.ds(r0, C), :]
        bl = bc[C - 1:C, :]
        qt = (q_ref[pl.ds(r0, C), :] * scale * jnp.exp(bc)).astype(bf16)
        kt = (k_ref[pl.ds(r0, C), :] * jnp.exp(bl - bc)).astype(bf16)
        oi_sc[pl.ds(r0, C), :] += lax.dot_general(qt, st.astype(bf16), nt,
                                                  preferred_element_type=jnp.float32)
        vt = v_ref[pl.ds(r0, C), :].T.astype(bf16)
        return jnp.exp(bl) * st + jnp.dot(vt, kt, preferred_element_type=jnp.float32)

    lax.fori_loop(0, seq // C, chunk, jnp.zeros((dv, dk), jnp.float32), unroll=16)

    def finish(i, carry):
        r0 = pl.multiple_of(i * R, R)
        o = oi_sc[pl.ds(r0, R), :]
        on = o * lax.rsqrt(jnp.mean(o * o, axis=1, keepdims=True) + LN_EPS) * ng_ref[...]
        g = go_ref[pl.ds(r0, R), :]
        o_ref[pl.ds(r0, R), :] = (on * (g * jax.nn.sigmoid(g))).astype(o_ref.dtype)
        return carry

    lax.fori_loop(0, seq // R, finish, 0)


def gla_pallas(proj_a, proj_go, proj_c, lr_col0, gate_w2, gate_b, norm_g, n_batch, seq, n_heads,
               dk, dv, q_col0, k_col0, v_col0, go_col0):
    kern = functools.partial(_gla_kernel, lr_col0=lr_col0)
    return pl.pallas_call(
        kern,
        grid=(n_batch, n_heads),
        in_specs=[
            pl.BlockSpec((seq, dk), lambda b, h: (b, q_col0 // dk + h)),
            pl.BlockSpec((seq, dk), lambda b, h: (b, k_col0 // dk + h)),
            pl.BlockSpec((seq, dv), lambda b, h: (b, v_col0 // dv + h)),
            pl.BlockSpec((seq, dv), lambda b, h: (b, go_col0 // dv + h)),
            pl.BlockSpec((seq, proj_c.shape[1]), lambda b, h: (b, 0)),
            pl.BlockSpec((GLA_RANK, dk), lambda b, h: (0, h)),
            pl.BlockSpec((1, dk), lambda b, h: (0, h)),
            pl.BlockSpec((1, dv), lambda b, h: (0, h)),
        ],
        out_specs=pl.BlockSpec((seq, dv), lambda b, h: (b, h)),
        out_shape=jax.ShapeDtypeStruct((n_batch * seq, n_heads * dv), jnp.bfloat16),
        scratch_shapes=[pltpu.VMEM((seq, dk), jnp.float32), pltpu.VMEM((seq, dv), jnp.float32)],
        compiler_params=pltpu.CompilerParams(
            dimension_semantics=("parallel", "parallel"), vmem_limit_bytes=VMEM_LIMIT),
    )(proj_a, proj_a, proj_a, proj_go, proj_c, gate_w2, gate_b.reshape(1, -1),
      norm_g.reshape(1, -1))


CONV_TILE = 256
CONV_HALO = 32


def _conf_kernel(a_ref, g_ref, w_ref, b_ref, ng_ref, nb_ref, o_ref, x_sc):
    seq, ch = a_ref.shape
    R, H = CONV_TILE, CONV_HALO
    x_sc[0:H, :] = jnp.zeros((H, ch), jnp.float32)
    x_sc[H:, :] = a_ref[...] * jax.nn.sigmoid(g_ref[...])

    def tile(i, carry):
        r0 = pl.multiple_of(i * R, R)
        win = x_sc[pl.ds(r0, R + H), :]
        acc = jnp.zeros((R, ch), jnp.float32) + b_ref[...]
        for sub in range(8):
            sh = pltpu.roll(win, sub, 0) if sub else win
            for blk in range(H // 8):
                off = 8 * blk + sub
                if off >= CONV_WIDTH:
                    continue
                j = CONV_WIDTH - 1 - off
                acc = acc + w_ref[j:j + 1, :] * sh[H - 8 * blk:H - 8 * blk + R, :]
        yc = acc - jnp.mean(acc, axis=1, keepdims=True)
        yn = (yc * lax.rsqrt(jnp.mean(yc * yc, axis=1, keepdims=True) + LN_EPS) * ng_ref[...]
              + nb_ref[...])
        o_ref[pl.ds(r0, R), :] = (yn * jax.nn.sigmoid(yn)).astype(o_ref.dtype)
        return carry

    lax.fori_loop(0, seq // R, tile, 0)


def conformer_conv_pallas(proj_a, a_col0, g_col0, dw_w, dw_b, norm_g, norm_b, n_batch, seq,
                          n_groups, width):
    ch = width // n_groups
    vec = pl.BlockSpec((1, ch), lambda b, g: (0, g))
    return pl.pallas_call(
        _conf_kernel,
        grid=(n_batch, n_groups),
        in_specs=[pl.BlockSpec((seq, ch), lambda b, g: (b, a_col0 // ch + g)),
                  pl.BlockSpec((seq, ch), lambda b, g: (b, g_col0 // ch + g)),
                  pl.BlockSpec((CONV_WIDTH, ch), lambda b, g: (0, g)),
                  vec, vec, vec],
        out_specs=pl.BlockSpec((seq, ch), lambda b, g: (b, g)),
        out_shape=jax.ShapeDtypeStruct((n_batch * seq, width), jnp.bfloat16),
        scratch_shapes=[pltpu.VMEM((seq + CONV_HALO, ch), jnp.float32)],
        compiler_params=pltpu.CompilerParams(
            dimension_semantics=("parallel", "parallel"), vmem_limit_bytes=VMEM_LIMIT),
    )(proj_a, proj_a, dw_w, dw_b.reshape(1, -1), norm_g.reshape(1, -1), norm_b.reshape(1, -1))


def hybrid_mixer(ub, B, S, layer, w_in, b_merge, mlstm_gate_b, mlstm_conv_w, mlstm_conv_b,
                 mlstm_norm_g, conf_dw_w, conf_dw_b, conf_norm_g, conf_norm_b, rel_bias_t, moba_btab,
                 gla_gate_w2, gla_gate_b, gla_norm_g, w_branch):
    W = BRANCH_WIDTH
    bf16, f32 = jnp.bfloat16, jnp.float32
    offs = [0] + np.cumsum(IN_SIZES).tolist()
    w_1 = repack_columns(w_in, layer, 0, offs[4])
    w_2 = repack_columns(w_in, layer, offs[5], offs[10] - offs[5])
    w_3 = repack_columns(w_in, layer, offs[11], offs[13] - offs[11])
    proj_ml = matmul(ub, w_1, 512, 1024, f32)
    proj_cf = matmul(ub, w_2, 512, 1024, f32, 0, 2 * W)
    proj_mb = matmul(ub, w_2, 512, 1024, bf16, 2 * W, 3 * W)
    proj_gl = matmul(ub, w_2, 512, 1024, f32, 5 * W, 2 * W)
    proj_go = matmul(ub, w_3, 512, 1024, f32, 0, W)
    proj_b = matmul(ub, w_3, 512, 1024, bf16, W)
    c_if = offs[4] - offs[4] % LANES
    c_lr = offs[10] - offs[10] % LANES
    proj_if = matmul_rows_t(ub, w_in, layer, c_if, LANES, 512, f32)
    proj_lr = matmul_rows_t(ub, w_in, layer, c_lr, LANES, 512, f32)

    gates = proj_if[:, offs[4] - c_if:offs[5] - c_if] + mlstm_gate_b
    y_mlstm = mlstm_pallas(proj_ml, gates, mlstm_conv_w, mlstm_conv_b, mlstm_norm_g, B, S,
                           MLSTM_HEADS, MLSTM_DQK, MLSTM_DV, 0, W // 2, W, 2 * W)
    y_conv = conformer_conv_pallas(proj_cf, 0, W, conf_dw_w, conf_dw_b, conf_norm_g, conf_norm_b,
                                   B, S, CONV_GROUPS, W)
    y_moba = moba_pallas(proj_mb, rel_bias_t, moba_btab, B, S, 0, MOBA_HEADS, 2 * MOBA_HEADS, bf16)
    y_gla = gla_pallas(proj_gl, proj_go, proj_lr, offs[10] - c_lr, gla_gate_w2, gla_gate_b,
                       gla_norm_g, B, S, GLA_HEADS, GLA_DK, GLA_DV, 0, W // 2, W, 0)
    return merge_branches([y_mlstm, y_conv, y_moba, y_gla], w_branch, layer, proj_b, 0, b_merge)


def kernel(x, c, w_ada, b_ada, w_in, b_merge, mlstm_gate_b, mlstm_conv_w, mlstm_conv_b,
           mlstm_norm_g, conf_dw_w, conf_dw_b, conf_norm_g, conf_norm_b, rel_bias,
           gla_gate_w2, gla_gate_b, gla_norm_g, w_branch, w_out, ln1_g, ln1_b,
           router_w, router_b, exp_w_up, exp_b_up, exp_w_down, exp_b_down, ln2_g, ln2_b):
    B, S, D = x.shape
    bf16 = jnp.bfloat16
    mod_all = ada_modulation(c, w_ada, b_ada)
    mods = [jnp.split(mod_all[l][:, None, :], 6, axis=-1) for l in range(DEPTH)]
    rel_bias_t = rel_bias.T.astype(jnp.float32)
    moba_btab = moba_bias_tables(rel_bias_t)
    n_rows = B * S * TOP_K + N_EXPERTS * MOE_TILE
    x2d = x.reshape(B * S, D)
    w_in_t = jnp.swapaxes(w_in, 1, 2)
    ub = modulate(x2d, mods[0][1], mods[0][0], S)
    for l in range(DEPTH):
        sh1, sc1, g1, sh2, sc2, g2 = mods[l]
        merged = hybrid_mixer(ub, B, S, l, w_in_t, b_merge[l], mlstm_gate_b[l], mlstm_conv_w[l],
                              mlstm_conv_b[l], mlstm_norm_g[l], conf_dw_w[l], conf_dw_b[l],
                              conf_norm_g[l], conf_norm_b[l], rel_bias_t, moba_btab,
                              gla_gate_w2[l], gla_gate_b[l], gla_norm_g[l], w_branch)
        mix = matmul_f32w(merged, w_out, l, 512, 512, jnp.float32)
        x1, upk, top_idx, top_w, rank4, counts = ln_router(
            x2d, mix, g1, ln1_g[l], ln1_b[l], sc2, sh2, router_w[l].astype(bf16), router_b[l], S)
        dest, tile_expert, n_tiles, last_tile_row = moe_layout(counts, top_idx, rank4,
                                                               n_rows // MOE_TILE)
        xs = moe_dispatch(upk, dest, last_tile_row, n_rows)
        ys = grouped_experts(xs, tile_expert, n_tiles, exp_w_up, exp_b_up, exp_w_down, exp_b_down, l)
        nxt = min(l + 1, DEPTH - 1)
        x2d, ub = combine_ln(dest, top_w, x1, g2, ln2_g[l], ln2_b[l], mods[nxt][1], mods[nxt][0],
                             ys, S)
    return x2d.reshape(B, S, D)
```

```python
import functools
import math

import jax
import jax.numpy as jnp
import numpy as np
from jax import lax
from jax.experimental import pallas as pl
from jax.experimental.pallas import tpu as pltpu

D_MODEL = 4096
BATCH = 4
SEQ = 2048
DEPTH = 2

N_BRANCHES = 4
BRANCH_WIDTH = D_MODEL // N_BRANCHES
MLSTM_HEADS = 4
MLSTM_DV = BRANCH_WIDTH // MLSTM_HEADS
MLSTM_DQK = MLSTM_DV // 2
MLSTM_CONV = 4
MLSTM_CHUNK = 64
CONV_WIDTH = 31
CONV_GROUPS = 4
MOBA_HEADS = 8
MOBA_HD = BRANCH_WIDTH // MOBA_HEADS
MOBA_BLOCK = 256
MOBA_TOPK = 3
MOBA_Q_CHUNK = 32
REL_BUCKETS = 32
REL_MAX_DIST = 128
GLA_HEADS = 4
GLA_DV = BRANCH_WIDTH // GLA_HEADS
GLA_DK = GLA_DV // 2
GLA_RANK = 16
GLA_TAU = 16.0
GLA_CHUNK = 16
N_EXPERTS = 32
TOP_K = 4
EXPERT_FF = D_MODEL // 8
SWIGLU_ALPHA = 1.702
SWIGLU_LIMIT = 7.0
DEEPNORM_ALPHA = (2 * DEPTH) ** 0.25
LN_EPS = 1e-5

IN_SIZES = (
    MLSTM_HEADS * MLSTM_DQK, MLSTM_HEADS * MLSTM_DQK, BRANCH_WIDTH, BRANCH_WIDTH,
    2 * MLSTM_HEADS, 2 * BRANCH_WIDTH, 3 * BRANCH_WIDTH, GLA_HEADS * GLA_DK,
    GLA_HEADS * GLA_DK, BRANCH_WIDTH, GLA_RANK, BRANCH_WIDTH, N_BRANCHES * D_MODEL,
)
IN_WIDTH = sum(IN_SIZES)

VMEM_LIMIT = 56 * 1024 * 1024


def _mm_kernel(a_ref, b_ref, o_ref):
    o_ref[...] = jnp.dot(a_ref[...], b_ref[...],
                         preferred_element_type=jnp.float32).astype(o_ref.dtype)


def matmul(a, b, tm, tn, out_dtype, col0=0, ncols=None):
    M, K = a.shape
    N = b.shape[1] - col0 if ncols is None else ncols
    assert col0 % tn == 0 and N % tn == 0 and M % tm == 0
    jb = col0 // tn
    return pl.pallas_call(
        _mm_kernel,
        grid=(N // tn, M // tm),
        in_specs=[pl.BlockSpec((tm, K), lambda j, i: (i, 0)),
                  pl.BlockSpec((K, tn), lambda j, i: (0, jb + j))],
        out_specs=pl.BlockSpec((tm, tn), lambda j, i: (i, j)),
        out_shape=jax.ShapeDtypeStruct((M, N), out_dtype),
        compiler_params=pltpu.CompilerParams(
            dimension_semantics=("parallel", "parallel"), vmem_limit_bytes=VMEM_LIMIT),
    )(a, b)


def _mm_f32w_kernel(a_ref, b_ref, o_ref, wb_sc):
    @pl.when(pl.program_id(1) == 0)
    def _():
        wb_sc[...] = b_ref[...].astype(wb_sc.dtype)

    o_ref[...] = jnp.dot(a_ref[...], wb_sc[...],
                         preferred_element_type=jnp.float32).astype(o_ref.dtype)


def matmul_f32w(a, b, layer, tm, tn, out_dtype, col0=0, ncols=None):
    M, K = a.shape
    N = b.shape[2] - col0 if ncols is None else ncols
    assert col0 % tn == 0 and N % tn == 0 and M % tm == 0
    jb = col0 // tn
    return pl.pallas_call(
        _mm_f32w_kernel,
        grid=(N // tn, M // tm),
        in_specs=[pl.BlockSpec((tm, K), lambda j, i: (i, 0)),
                  pl.BlockSpec((None, K, tn), lambda j, i: (layer, 0, jb + j))],
        out_specs=pl.BlockSpec((tm, tn), lambda j, i: (i, j)),
        out_shape=jax.ShapeDtypeStruct((M, N), out_dtype),
        scratch_shapes=[pltpu.VMEM((K, tn), jnp.bfloat16)],
        compiler_params=pltpu.CompilerParams(
            dimension_semantics=("parallel", "arbitrary"), vmem_limit_bytes=VMEM_LIMIT),
    )(a, b)


LANES = 128
SUBLANES = 8
REPACK_ROWS = 1024
REPACK_COLS = 1024
REPACK_EDGE = 32


def _repack_kernel(*refs, shift):
    if shift:
        main_ref, edge_ref, o_ref = refs
        x = jnp.concatenate([main_ref[shift:, :], edge_ref[:shift, :]], axis=0)
    else:
        main_ref, o_ref = refs
        x = main_ref[...]
    o_ref[...] = x.T.astype(o_ref.dtype)


def repack_columns(w_t, layer, col0, ncols):
    _, n, k = w_t.shape
    tk, tn = min(REPACK_ROWS, k), REPACK_COLS
    shift = col0 % tn
    base = col0 - shift
    assert shift % SUBLANES == 0 and shift <= REPACK_EDGE and ncols % tn == 0 and k % tk == 0
    in_specs = [pl.BlockSpec((None, tn, tk), lambda j, r: (layer, base // tn + j, r))]
    args = [w_t]
    if shift:
        in_specs.append(pl.BlockSpec(
            (None, REPACK_EDGE, tk), lambda j, r: (layer, (base + (j + 1) * tn) // REPACK_EDGE, r)))
        args.append(w_t)
    return pl.pallas_call(
        functools.partial(_repack_kernel, shift=shift),
        grid=(ncols // tn, k // tk),
        in_specs=in_specs,
        out_specs=pl.BlockSpec((tk, tn), lambda j, r: (r, j)),
        out_shape=jax.ShapeDtypeStruct((k, ncols), jnp.bfloat16),
        compiler_params=pltpu.CompilerParams(
            dimension_semantics=("parallel", "parallel"), vmem_limit_bytes=VMEM_LIMIT),
    )(*args)


def _mm_nt_kernel(a_ref, bt_ref, o_ref):
    o_ref[...] = lax.dot_general(a_ref[...], bt_ref[...].astype(jnp.bfloat16),
                                 (((1,), (1,)), ((), ())),
                                 preferred_element_type=jnp.float32).astype(o_ref.dtype)


def matmul_rows_t(a, w_t, layer, row0, nrows, tm, out_dtype):
    M, K = a.shape
    assert row0 % nrows == 0 and M % tm == 0
    return pl.pallas_call(
        _mm_nt_kernel,
        grid=(M // tm,),
        in_specs=[pl.BlockSpec((tm, K), lambda i: (i, 0)),
                  pl.BlockSpec((None, nrows, K), lambda i: (layer, row0 // nrows, 0))],
        out_specs=pl.BlockSpec((tm, nrows), lambda i: (i, 0)),
        out_shape=jax.ShapeDtypeStruct((M, nrows), out_dtype),
        compiler_params=pltpu.CompilerParams(
            dimension_semantics=("parallel",), vmem_limit_bytes=VMEM_LIMIT),
    )(a, w_t)


ROW_TILE = 256
MOE_TILE = 256


def _ada_kernel(c_ref, w_ref, b_ref, o_ref):
    c = c_ref[...]
    cs = (c * jax.nn.sigmoid(c)).astype(jnp.bfloat16)
    o_ref[0] = jnp.dot(cs, w_ref[0].astype(jnp.bfloat16),
                       preferred_element_type=jnp.float32) + b_ref[0]


def ada_modulation(c, w_ada, b_ada, tn=512):
    n_layers, d, n = w_ada.shape
    nb = c.shape[0]
    c8 = jnp.pad(c, ((0, 8 - nb), (0, 0)))
    out = pl.pallas_call(
        _ada_kernel,
        grid=(n_layers, n // tn),
        in_specs=[pl.BlockSpec((8, d), lambda l, j: (0, 0)),
                  pl.BlockSpec((1, d, tn), lambda l, j: (l, 0, j)),
                  pl.BlockSpec((1, 1, tn), lambda l, j: (l, 0, j))],
        out_specs=pl.BlockSpec((1, 8, tn), lambda l, j: (l, 0, j)),
        out_shape=jax.ShapeDtypeStruct((n_layers, 8, n), jnp.float32),
        compiler_params=pltpu.CompilerParams(
            dimension_semantics=("parallel", "parallel"), vmem_limit_bytes=VMEM_LIMIT),
    )(c8, w_ada, b_ada.reshape(n_layers, 1, n))
    return out[:, :nb]


def _modulate_kernel(x_ref, sc_ref, sh_ref, o_ref):
    o_ref[...] = (x_ref[...] * (1.0 + sc_ref[0]) + sh_ref[0]).astype(o_ref.dtype)


def modulate(x2d, sc, sh, seq):
    t, d = x2d.shape
    per_b = seq // ROW_TILE
    mod_spec = pl.BlockSpec((1, 1, d), lambda i: (i // per_b, 0, 0))
    return pl.pallas_call(
        _modulate_kernel,
        grid=(t // ROW_TILE,),
        in_specs=[pl.BlockSpec((ROW_TILE, d), lambda i: (i, 0)), mod_spec, mod_spec],
        out_specs=pl.BlockSpec((ROW_TILE, d), lambda i: (i, 0)),
        out_shape=jax.ShapeDtypeStruct((t, d), jnp.bfloat16),
        compiler_params=pltpu.CompilerParams(dimension_semantics=("parallel",)),
    )(x2d, sc, sh)


def _merge_kernel(y0, y1, y2, y3, wb_ref, g0, g1, g2, g3, bm_ref, o_ref, wb_sc):
    @pl.when(pl.program_id(1) == 0)
    def _():
        wb_sc[...] = wb_ref[...].astype(wb_sc.dtype)

    acc = None
    for n, (y, g) in enumerate(((y0, g0), (y1, g1), (y2, g2), (y3, g3))):
        gate = jax.nn.sigmoid(g[...].astype(jnp.float32) + bm_ref[n:n + 1, :])
        term = gate * jnp.dot(y[...], wb_sc[n], preferred_element_type=jnp.float32)
        acc = term if acc is None else acc + term
    o_ref[...] = acc.astype(o_ref.dtype)


def merge_branches(ys, w_branch, layer, proj_b, gate_col0, b_merge, tm=1024, tn=512):
    t, w = ys[0].shape
    d = w_branch.shape[3]
    assert gate_col0 % tn == 0 and d % tn == 0
    y_spec = pl.BlockSpec((tm, w), lambda j, i: (i, 0))
    g_specs = [pl.BlockSpec((tm, tn), lambda j, i, n=n: (i, (gate_col0 + n * d) // tn + j))
               for n in range(N_BRANCHES)]
    return pl.pallas_call(
        _merge_kernel,
        grid=(d // tn, t // tm),
        in_specs=[y_spec] * 4 + [pl.BlockSpec((None, N_BRANCHES, w, tn),
                                              lambda j, i: (layer, 0, 0, j))]
        + g_specs + [pl.BlockSpec((N_BRANCHES, tn), lambda j, i: (0, j))],
        out_specs=pl.BlockSpec((tm, tn), lambda j, i: (i, j)),
        out_shape=jax.ShapeDtypeStruct((t, d), jnp.bfloat16),
        scratch_shapes=[pltpu.VMEM((N_BRANCHES, w, tn), jnp.bfloat16)],
        compiler_params=pltpu.CompilerParams(
            dimension_semantics=("parallel", "arbitrary"), vmem_limit_bytes=VMEM_LIMIT),
    )(*ys, w_branch, proj_b, proj_b, proj_b, proj_b, b_merge.reshape(N_BRANCHES, d))


def _pack_pair(lo, hi):
    def rne(x):
        b = lax.bitcast_convert_type(x, jnp.uint32)
        return (b + jnp.uint32(0x7FFF) + ((b >> 16) & jnp.uint32(1))) >> 16
    return rne(lo) | (rne(hi) << 16)


def _unpack_lo(w):
    return lax.bitcast_convert_type(w << 16, jnp.float32)


def _unpack_hi(w):
    return lax.bitcast_convert_type(w & jnp.uint32(0xFFFF0000), jnp.float32)


def _layer_norm_halves(z_lo, z_hi, g_ref, b_ref, half):
    d = 2 * half
    mu = (jnp.sum(z_lo, axis=1, keepdims=True) + jnp.sum(z_hi, axis=1, keepdims=True)) / d
    c_lo, c_hi = z_lo - mu, z_hi - mu
    var = (jnp.sum(c_lo * c_lo, axis=1, keepdims=True)
           + jnp.sum(c_hi * c_hi, axis=1, keepdims=True)) / d
    r = lax.rsqrt(var + LN_EPS)
    return (c_lo * r * g_ref[:, :half] + b_ref[:, :half],
            c_hi * r * g_ref[:, half:] + b_ref[:, half:])


def _ln_router_kernel(x_ref, mix_ref, g1_ref, lng_ref, lnb_ref, sc_ref, sh_ref, rw_ref, rb_ref,
                      x1_ref, upk_ref, tidx_ref, tw_ref, rank_ref, cnt_ref, carry_sc):
    tm, d = x_ref.shape
    half = d // 2
    n_exp = rw_ref.shape[1]

    @pl.when(pl.program_id(0) == 0)
    def _():
        carry_sc[...] = jnp.zeros_like(carry_sc)

    z = DEEPNORM_ALPHA * x_ref[...] + (1.0 + g1_ref[0]) * mix_ref[...]
    x_lo, x_hi = _layer_norm_halves(z[:, :half], z[:, half:], lng_ref, lnb_ref, half)
    x1_ref[:, :half] = x_lo
    x1_ref[:, half:] = x_hi
    u_lo = x_lo * (1.0 + sc_ref[0, :, :half]) + sh_ref[0, :, :half]
    u_hi = x_hi * (1.0 + sc_ref[0, :, half:]) + sh_ref[0, :, half:]
    upk_ref[...] = _pack_pair(u_lo, u_hi)

    logits = (jnp.dot(u_lo.astype(jnp.bfloat16), rw_ref[:half, :],
                      preferred_element_type=jnp.float32)
              + jnp.dot(u_hi.astype(jnp.bfloat16), rw_ref[half:, :],
                        preferred_element_type=jnp.float32) + rb_ref[...])
    lane = lax.broadcasted_iota(jnp.int32, (tm, n_exp), 1).astype(jnp.float32)
    lg = logits
    vals, hots = [], []
    for k in range(TOP_K):
        m = jnp.max(lg, axis=1, keepdims=True)
        ik = jnp.min(jnp.where(lg == m, lane, float(n_exp)), axis=1, keepdims=True)
        hot = lane == ik
        vals.append(m)
        hots.append(hot)
        tidx_ref[:, k:k + 1] = ik.astype(jnp.int32)
        lg = jnp.where(hot, -jnp.inf, lg)
    exps = [jnp.exp(v - vals[0]) for v in vals]
    den = exps[0] + exps[1] + exps[2] + exps[3]
    for k in range(TOP_K):
        tw_ref[:, k:k + 1] = exps[k] / den

    mask = jnp.where(hots[0] | hots[1] | hots[2] | hots[3], 1.0, 0.0)
    row = lax.broadcasted_iota(jnp.int32, (tm, tm), 0)
    col = lax.broadcasted_iota(jnp.int32, (tm, tm), 1)
    lower = jnp.where(row > col, 1.0, 0.0).astype(jnp.bfloat16)
    rank = jnp.dot(lower, mask.astype(jnp.bfloat16),
                   preferred_element_type=jnp.float32) + carry_sc[...]
    for k in range(TOP_K):
        rank_ref[:, k:k + 1] = jnp.sum(jnp.where(hots[k], rank, 0.0), axis=1,
                                       keepdims=True).astype(jnp.int32)
    carry_sc[...] = carry_sc[...] + jnp.sum(mask, axis=0, keepdims=True)
    cnt_ref[...] = carry_sc[...]


def ln_router(x2d, mix, g1, ln_g, ln_b, sc2, sh2, router_w_bf, router_b, seq):
    t, d = x2d.shape
    n_exp = router_w_bf.shape[1]
    per_b = seq // ROW_TILE
    row = lambda w: pl.BlockSpec((ROW_TILE, w), lambda i: (i, 0))
    mod = pl.BlockSpec((1, 1, d), lambda i: (i // per_b, 0, 0))
    vec = lambda w: pl.BlockSpec((1, w), lambda i: (0, 0))
    return pl.pallas_call(
        _ln_router_kernel,
        grid=(t // ROW_TILE,),
        in_specs=[row(d), row(d), mod, vec(d), vec(d), mod, mod,
                  pl.BlockSpec((d, n_exp), lambda i: (0, 0)), vec(n_exp)],
        out_specs=[row(d), row(d // 2), row(TOP_K), row(TOP_K), row(TOP_K), vec(n_exp)],
        out_shape=[jax.ShapeDtypeStruct((t, d), jnp.float32),
                   jax.ShapeDtypeStruct((t, d // 2), jnp.uint32),
                   jax.ShapeDtypeStruct((t, TOP_K), jnp.int32),
                   jax.ShapeDtypeStruct((t, TOP_K), jnp.float32),
                   jax.ShapeDtypeStruct((t, TOP_K), jnp.int32),
                   jax.ShapeDtypeStruct((1, n_exp), jnp.float32)],
        scratch_shapes=[pltpu.VMEM((1, n_exp), jnp.float32)],
        compiler_params=pltpu.CompilerParams(
            dimension_semantics=("arbitrary",), vmem_limit_bytes=VMEM_LIMIT),
    )(x2d, mix, g1, ln_g.reshape(1, d), ln_b.reshape(1, d), sc2, sh2, router_w_bf,
      router_b.reshape(1, n_exp))


def _row_copy(src_hbm, src_row, dst, dst_row, sem):
    return pltpu.make_async_copy(src_hbm.at[pl.ds(src_row, 1)], dst.at[pl.ds(dst_row, 1)], sem)


def _dispatch_kernel(dest_ref, last_ref, u_ref, xs_hbm, zero_sc, sem, zsem):
    n_exp = last_ref.shape[0] - 1
    n_tiles_max = xs_hbm.shape[0] // MOE_TILE

    @pl.when(pl.program_id(0) == 0)
    def _():
        zero_sc[...] = jnp.zeros_like(zero_sc)

        def fill(row):
            return pltpu.make_async_copy(
                zero_sc, xs_hbm.at[pl.ds(pl.multiple_of(row, MOE_TILE), MOE_TILE)], zsem)

        def start(ex, carry):
            @pl.when(last_ref[ex] >= 0)
            def _():
                fill(last_ref[ex]).start()
            return carry

        def finish(ex, carry):
            @pl.when(last_ref[ex] >= 0)
            def _():
                fill(0).wait()
            return carry

        def start_tail(g, carry):
            fill(g * MOE_TILE).start()
            return carry

        def finish_tail(g, carry):
            fill(0).wait()
            return carry

        lax.fori_loop(0, n_exp, start, 0)
        lax.fori_loop(last_ref[n_exp], n_tiles_max, start_tail, 0)
        lax.fori_loop(0, n_exp, finish, 0)
        lax.fori_loop(last_ref[n_exp], n_tiles_max, finish_tail, 0)

    def issue(r, carry):
        for k in range(TOP_K):
            _row_copy(u_ref, r, xs_hbm, dest_ref[r * TOP_K + k], sem).start(priority=k % 2)
        return carry

    def drain(r, carry):
        for k in range(TOP_K):
            _row_copy(u_ref, 0, xs_hbm, 0, sem).wait()
        return carry

    lax.fori_loop(0, ROW_TILE, issue, 0)
    lax.fori_loop(0, ROW_TILE, drain, 0)


def moe_dispatch(upk, dest_flat, last_tile_row, n_rows):
    t, w = upk.shape
    return pl.pallas_call(
        _dispatch_kernel,
        grid=(t // ROW_TILE,),
        in_specs=[pl.BlockSpec((ROW_TILE * TOP_K,), lambda i: (i,), memory_space=pltpu.SMEM),
                  pl.BlockSpec(memory_space=pltpu.SMEM),
                  pl.BlockSpec((ROW_TILE, w), lambda i: (i, 0))],
        out_specs=pl.BlockSpec(memory_space=pl.ANY),
        out_shape=jax.ShapeDtypeStruct((n_rows, w), upk.dtype),
        scratch_shapes=[pltpu.VMEM((MOE_TILE, w), upk.dtype),
                        pltpu.SemaphoreType.DMA(()), pltpu.SemaphoreType.DMA(())],
        compiler_params=pltpu.CompilerParams(dimension_semantics=("arbitrary",)),
    )(dest_flat, last_tile_row, upk)


UP_CHUNKS = 4


def _expert_kernel(te_ref, nt_ref, xs_ref, wu_hbm, bu_ref, wd_hbm, bd_ref, ys_ref,
                   stage_u, stage_d, wu_sc, wd_sc, sem, *, layer):
    g = pl.program_id(0)
    half = xs_ref.shape[1]
    ff = wd_sc.shape[0]
    rows = stage_u.shape[1]
    e = te_ref[g]
    new_expert = jnp.logical_or(g == 0, e != te_ref[jnp.maximum(g - 1, 0)])

    def up_copy(ex, c):
        return pltpu.make_async_copy(wu_hbm.at[layer, ex, pl.ds(c * rows, rows)],
                                     stage_u.at[c % 2], sem.at[c % 2])

    def down_copy(ex):
        return pltpu.make_async_copy(wd_hbm.at[layer, ex], stage_d, sem.at[2])

    def start_head(ex):
        up_copy(ex, 0).start()
        up_copy(ex, 1).start()
        down_copy(ex).start()

    @pl.when(jnp.logical_and(g < nt_ref[0], new_expert))
    def _():
        @pl.when(g == 0)
        def _():
            start_head(e)

        for c in range(UP_CHUNKS):
            up_copy(e, c).wait()
            wu_sc[c * rows:(c + 1) * rows, :] = stage_u[c % 2].astype(wu_sc.dtype)
            if c + 2 < UP_CHUNKS:
                up_copy(e, c + 2).start()
        down_copy(e).wait()
        wd_sc[...] = stage_d[...].astype(wd_sc.dtype)

    e_after = te_ref[pl.num_programs(0) + g]

    @pl.when(jnp.logical_and(jnp.logical_and(g < nt_ref[0], new_expert), e_after >= 0))
    def _():
        start_head(e_after)

    @pl.when(g < nt_ref[0])
    def _():
        w = xs_ref[...]
        x_lo = _unpack_lo(w).astype(jnp.bfloat16)
        x_hi = _unpack_hi(w).astype(jnp.bfloat16)
        hid = (jnp.dot(x_lo, wu_sc[:half, :], preferred_element_type=jnp.float32)
               + jnp.dot(x_hi, wu_sc[half:, :], preferred_element_type=jnp.float32)
               + bu_ref[0, 0])
        glu = jnp.minimum(hid[:, :ff], SWIGLU_LIMIT)
        lin = jnp.clip(hid[:, ff:], -SWIGLU_LIMIT, SWIGLU_LIMIT)
        act = glu * jax.nn.sigmoid(SWIGLU_ALPHA * glu) * (lin + 1.0)
        y = jnp.dot(act.astype(jnp.bfloat16), wd_sc[...],
                    preferred_element_type=jnp.float32) + bd_ref[0, 0]
        ys_ref[...] = _pack_pair(y[:, :half], y[:, half:])

    @pl.when(g >= nt_ref[0])
    def _():
        ys_ref[...] = jnp.zeros_like(ys_ref)


def grouped_experts(xs, tile_expert, n_tiles, w_up, b_up, w_down, b_down, layer):
    n_rows, half = xs.shape
    _, n_exp, d, ff2 = w_up.shape
    ff = ff2 // 2
    grid_spec = pltpu.PrefetchScalarGridSpec(
        num_scalar_prefetch=2,
        grid=(n_rows // MOE_TILE,),
        in_specs=[pl.BlockSpec((MOE_TILE, half), lambda g, te, nt: (jnp.minimum(g, nt[0] - 1), 0)),
                  pl.BlockSpec(memory_space=pl.ANY),
                  pl.BlockSpec((1, 1, 1, ff2), lambda g, te, nt: (layer, te[g], 0, 0)),
                  pl.BlockSpec(memory_space=pl.ANY),
                  pl.BlockSpec((1, 1, 1, d), lambda g, te, nt: (layer, te[g], 0, 0))],
        out_specs=pl.BlockSpec((MOE_TILE, half), lambda g, te, nt: (g, 0)),
        scratch_shapes=[pltpu.VMEM((2, d // UP_CHUNKS, ff2), jnp.float32),
                        pltpu.VMEM((ff, d), jnp.float32),
                        pltpu.VMEM((d, ff2), jnp.bfloat16),
                        pltpu.VMEM((ff, d), jnp.bfloat16),
                        pltpu.SemaphoreType.DMA((3,))],
    )
    return pl.pallas_call(
        functools.partial(_expert_kernel, layer=layer),
        grid_spec=grid_spec,
        out_shape=jax.ShapeDtypeStruct((n_rows, half), jnp.uint32),
        compiler_params=pltpu.CompilerParams(
            dimension_semantics=("arbitrary",), vmem_limit_bytes=VMEM_LIMIT),
    )(tile_expert, n_tiles, xs, w_up, b_up.reshape(b_up.shape[0], n_exp, 1, ff2), w_down,
      b_down.reshape(b_down.shape[0], n_exp, 1, d))


COMBINE_ROWS = 16


def _combine_kernel(dest_ref, dest_next_ref, tw_ref, x1_ref, g2_ref, lng_ref, lnb_ref, sc_ref,
                    sh_ref, ys_hbm, x2_ref, un_ref, buf, sem):
    tm, d = x1_ref.shape
    half = d // 2
    i = pl.program_id(0)
    slot = lax.rem(i, 2)

    has_next = i + 1 < pl.num_programs(0)
    sub = COMBINE_ROWS

    def gather_rows(d_ref, s, r0):
        for rr in range(sub):
            for k in range(TOP_K):
                _row_copy(ys_hbm, d_ref[(r0 + rr) * TOP_K + k], buf.at[s, k], r0 + rr,
                          sem.at[s]).start(priority=k % 2)

    @pl.when(i == 0)
    def _():
        def first(t, carry):
            gather_rows(dest_ref, 0, t * sub)
            return carry
        lax.fori_loop(0, tm // sub, first, 0)

    def drain(r, carry):
        for k in range(TOP_K):
            _row_copy(ys_hbm, 0, buf.at[slot, k], 0, sem.at[slot]).wait()
        return carry

    lax.fori_loop(0, tm, drain, 0)

    def rows(t, carry):
        r0 = pl.multiple_of(t * sub, sub)
        rs = pl.ds(r0, sub)

        gather_rows(dest_next_ref, 1 - slot, r0)

        f_lo = f_hi = None
        for k in range(TOP_K):
            wk = tw_ref[rs, k:k + 1]
            w = buf[slot, k, rs, :]
            t_lo, t_hi = wk * _unpack_lo(w), wk * _unpack_hi(w)
            f_lo = t_lo if f_lo is None else f_lo + t_lo
            f_hi = t_hi if f_hi is None else f_hi + t_hi
        z_lo = DEEPNORM_ALPHA * x1_ref[rs, :half] + (1.0 + g2_ref[0, :, :half]) * f_lo
        z_hi = DEEPNORM_ALPHA * x1_ref[rs, half:] + (1.0 + g2_ref[0, :, half:]) * f_hi
        x_lo, x_hi = _layer_norm_halves(z_lo, z_hi, lng_ref, lnb_ref, half)
        x2_ref[rs, :half] = x_lo
        x2_ref[rs, half:] = x_hi
        un_ref[rs, :half] = (x_lo * (1.0 + sc_ref[0, :, :half])
                             + sh_ref[0, :, :half]).astype(un_ref.dtype)
        un_ref[rs, half:] = (x_hi * (1.0 + sc_ref[0, :, half:])
                             + sh_ref[0, :, half:]).astype(un_ref.dtype)
        return carry

    lax.fori_loop(0, tm // sub, rows, 0)

    @pl.when(jnp.logical_not(has_next))
    def _():
        def drain_other(r, carry):
            for k in range(TOP_K):
                _row_copy(ys_hbm, 0, buf.at[1 - slot, k], 0, sem.at[1 - slot]).wait()
            return carry
        lax.fori_loop(0, tm, drain_other, 0)


def combine_ln(dest_flat, top_w, x1, g2, ln_g, ln_b, sc_next, sh_next, ys, seq):
    t, d = x1.shape
    per_b = seq // ROW_TILE
    n_steps = t // ROW_TILE
    row = lambda w: pl.BlockSpec((ROW_TILE, w), lambda i: (i, 0))
    mod = pl.BlockSpec((1, 1, d), lambda i: (i // per_b, 0, 0))
    vec = pl.BlockSpec((1, d), lambda i: (0, 0))
    return pl.pallas_call(
        _combine_kernel,
        grid=(n_steps,),
        in_specs=[pl.BlockSpec((ROW_TILE * TOP_K,), lambda i: (i,), memory_space=pltpu.SMEM),
                  pl.BlockSpec((ROW_TILE * TOP_K,), lambda i: (jnp.minimum(i + 1, n_steps - 1),),
                               memory_space=pltpu.SMEM),
                  row(TOP_K), row(d), mod, vec, vec, mod, mod,
                  pl.BlockSpec(memory_space=pl.ANY)],
        out_specs=[row(d), row(d)],
        out_shape=[jax.ShapeDtypeStruct((t, d), jnp.float32),
                   jax.ShapeDtypeStruct((t, d), jnp.bfloat16)],
        scratch_shapes=[pltpu.VMEM((2, TOP_K, ROW_TILE, d // 2), jnp.uint32),
                        pltpu.SemaphoreType.DMA((2,))],
        compiler_params=pltpu.CompilerParams(
            dimension_semantics=("arbitrary",), vmem_limit_bytes=VMEM_LIMIT),
    )(dest_flat, dest_flat, top_w, x1, g2, ln_g.reshape(1, d), ln_b.reshape(1, d), sc_next,
      sh_next, ys)


def moe_layout(counts, top_idx, rank4, n_tiles_max):
    cnt = counts.reshape(-1).astype(jnp.int32)
    tiles = (cnt + MOE_TILE - 1) // MOE_TILE
    tile_end = jnp.cumsum(tiles)
    row_start = (tile_end - tiles) * MOE_TILE
    dest = (row_start[top_idx] + rank4).reshape(-1)
    tile_expert = jnp.minimum(
        jnp.sum(jnp.arange(n_tiles_max)[:, None] >= tile_end[None, :], axis=1),
        cnt.shape[0] - 1).astype(jnp.int32)
    after = tile_end[tile_expert]
    next_expert = jnp.where(after < tile_end[-1],
                            tile_expert[jnp.minimum(after, n_tiles_max - 1)], -1)
    tile_expert = jnp.concatenate([tile_expert, next_expert]).astype(jnp.int32)
    last_tile_row = jnp.concatenate(
        [jnp.where(tiles > 0, (tile_end - 1) * MOE_TILE, -1), tile_end[-1:]]).astype(jnp.int32)
    return dest.astype(jnp.int32), tile_expert, tile_end[-1:].astype(jnp.int32), last_tile_row


NEG_BIG = -1e30
MOBA_HEADS_PER_STEP = 2


def _t5_bucket_np(d):
    max_exact = REL_BUCKETS // 2
    dd = np.maximum(d, 1).astype(np.float32)
    far = max_exact + (np.log(dd / np.float32(max_exact))
                       / np.float32(math.log(REL_MAX_DIST / max_exact))
                       * np.float32(REL_BUCKETS - max_exact)).astype(np.int32)
    return np.where(d < max_exact, d, np.minimum(far, REL_BUCKETS - 1)).astype(np.int32)


def _moba_bucket_tables():
    c = np.arange(MOBA_BLOCK)[:, None]
    r = np.arange(MOBA_BLOCK)[None, :]
    own = np.where(r >= c, _t5_bucket_np(np.maximum(r - c, 0)), -1)
    prev = _t5_bucket_np(MOBA_BLOCK + r - c)
    return np.stack([own, prev]).astype(np.int32)


def _bias_tab_kernel(rb_ref, bk_ref, o_ref):
    h = pl.program_id(0)
    bk = bk_ref[...]
    acc = jnp.zeros(bk.shape, jnp.float32)
    for m in range(REL_BUCKETS):
        acc = jnp.where(bk == m, rb_ref[h, m], acc)
    o_ref[0] = jnp.where(bk < 0, NEG_BIG, acc)


def moba_bias_tables(rel_bias_t):
    n_heads = rel_bias_t.shape[0]
    bk = jnp.asarray(_moba_bucket_tables())
    return pl.pallas_call(
        _bias_tab_kernel,
        grid=(n_heads,),
        in_specs=[pl.BlockSpec(memory_space=pltpu.SMEM),
                  pl.BlockSpec((2, MOBA_BLOCK, MOBA_BLOCK), lambda h: (0, 0, 0))],
        out_specs=pl.BlockSpec((1, 2, MOBA_BLOCK, MOBA_BLOCK), lambda h: (h, 0, 0, 0)),
        out_shape=jax.ShapeDtypeStruct((n_heads, 2, MOBA_BLOCK, MOBA_BLOCK), jnp.float32),
    )(rel_bias_t, bk)


def _moba_kernel(rb_ref, q_ref, k_ref, v_ref, bt_ref, o_ref, km_sc, vt_sc, m_sc, l_sc, acc_sc,
                 *, n_blk):
    hp = pl.program_id(1)
    i = pl.program_id(2)
    blk, hd, nh = MOBA_BLOCK, MOBA_HD, MOBA_HEADS_PER_STEP
    seq = n_blk * blk
    scale = hd ** -0.5
    nt = (((1,), (1,)), ((), ()))
    bf16 = jnp.bfloat16
    cols = [slice(a * hd, (a + 1) * hd) for a in range(nh)]

    @pl.when(i == 0)
    def _():
        blk_of_col = lax.broadcasted_iota(jnp.int32, (n_blk, seq), 1) // blk
        blk_row = lax.broadcasted_iota(jnp.int32, (n_blk, seq), 0)
        avg = jnp.where(blk_of_col == blk_row, 1.0 / blk, 0.0).astype(bf16)
        km_sc[...] = jnp.dot(avg, k_ref[...], preferred_element_type=jnp.float32)
        for a in range(nh):
            for j in range(n_blk):
                vt_sc[a, j] = v_ref[j * blk:(j + 1) * blk, cols[a]].astype(
                    jnp.float32).T.astype(bf16)

    qs = [q_ref[:, cols[a]] for a in range(nh)]
    gates = []
    for a in range(nh):
        kmean = km_sc[:, cols[a]]
        km_hi = kmean.astype(bf16)
        km_lo = (kmean - km_hi.astype(jnp.float32)).astype(bf16)
        gate = (lax.dot_general(km_hi, qs[a], nt, preferred_element_type=jnp.float32)
                + lax.dot_general(km_lo, qs[a], nt, preferred_element_type=jnp.float32))
        gates.append([gate[n:n + 1, :] for n in range(n_blk)])

    def selected(a, j):
        g = gates[a]
        cnt = jnp.zeros((1, blk), jnp.float32)
        for n in range(n_blk - 1):
            if n == j:
                continue
            ahead = (g[n] >= g[j]) if n < j else (g[n] > g[j])
            cnt = cnt + jnp.where(ahead, jnp.where(n < i, 1.0, 0.0), 0.0)
        return cnt < float(MOBA_TOPK)

    row0 = pl.multiple_of(i * blk, blk)
    for a in range(nh):
        k_own = k_ref[pl.ds(row0, blk), cols[a]]
        s = (lax.dot_general(k_own, qs[a], nt, preferred_element_type=jnp.float32) * scale
             + bt_ref[a, 0])
        m0 = jnp.max(s, axis=0, keepdims=True)
        p = jnp.exp(s - m0)
        m_sc[a] = m0
        l_sc[a] = jnp.sum(p, axis=0, keepdims=True)
        acc_sc[a] = jnp.dot(vt_sc[a, i], p.astype(bf16), preferred_element_type=jnp.float32)

    for j in range(n_blk - 1):
        @pl.when(j < i)
        def _():
            for a in range(nh):
                bfar = rb_ref[hp * nh + a, REL_BUCKETS - 1]
                k_j = k_ref[j * blk:(j + 1) * blk, cols[a]]
                bias = jnp.where(j == i - 1, bt_ref[a, 1], bfar)
                s = (lax.dot_general(k_j, qs[a], nt, preferred_element_type=jnp.float32) * scale
                     + bias)
                s = jnp.where(selected(a, j), s, NEG_BIG)
                m_old = m_sc[a]
                m_new = jnp.maximum(m_old, jnp.max(s, axis=0, keepdims=True))
                alpha = jnp.exp(m_old - m_new)
                p = jnp.exp(s - m_new)
                m_sc[a] = m_new
                l_sc[a] = alpha * l_sc[a] + jnp.sum(p, axis=0, keepdims=True)
                acc_sc[a] = alpha * acc_sc[a] + jnp.dot(
                    vt_sc[a, j], p.astype(bf16), preferred_element_type=jnp.float32)

    for a in range(nh):
        o_ref[:, cols[a]] = (acc_sc[a] / l_sc[a]).T.astype(o_ref.dtype)


def moba_pallas(qkv, rel_bias_t, btab, n_batch, seq, q_blk0, k_blk0, v_blk0, out_dtype):
    n_heads = rel_bias_t.shape[0]
    n_blk = seq // MOBA_BLOCK
    nh = MOBA_HEADS_PER_STEP
    wide = nh * MOBA_HD
    assert seq % MOBA_BLOCK == 0 and n_heads % nh == 0
    assert q_blk0 % nh == 0 and k_blk0 % nh == 0 and v_blk0 % nh == 0
    kern = functools.partial(_moba_kernel, n_blk=n_blk)
    return pl.pallas_call(
        kern,
        grid=(n_batch, n_heads // nh, n_blk),
        in_specs=[
            pl.BlockSpec(memory_space=pltpu.SMEM),
            pl.BlockSpec((MOBA_BLOCK, wide), lambda b, h, i: (b * n_blk + i, q_blk0 // nh + h)),
            pl.BlockSpec((seq, wide), lambda b, h, i: (b, k_blk0 // nh + h)),
            pl.BlockSpec((seq, wide), lambda b, h, i: (b, v_blk0 // nh + h)),
            pl.BlockSpec((nh, 2, MOBA_BLOCK, MOBA_BLOCK), lambda b, h, i: (h, 0, 0, 0)),
        ],
        out_specs=pl.BlockSpec((MOBA_BLOCK, wide), lambda b, h, i: (b * n_blk + i, h)),
        out_shape=jax.ShapeDtypeStruct((n_batch * seq, n_heads * MOBA_HD), out_dtype),
        scratch_shapes=[pltpu.VMEM((n_blk, wide), jnp.float32),
                        pltpu.VMEM((nh, n_blk, MOBA_HD, MOBA_BLOCK), jnp.bfloat16),
                        pltpu.VMEM((nh, 1, MOBA_BLOCK), jnp.float32),
                        pltpu.VMEM((nh, 1, MOBA_BLOCK), jnp.float32),
                        pltpu.VMEM((nh, MOBA_HD, MOBA_BLOCK), jnp.float32)],
        compiler_params=pltpu.CompilerParams(
            dimension_semantics=("parallel", "parallel", "arbitrary")),
    )(rel_bias_t, qkv, qkv, qkv, btab)


def _log_sigmoid(x):
    return jnp.minimum(x, 0.0) - jnp.log(1.0 + jnp.exp(-jnp.abs(x)))


def _causal_shift(x, shift):
    if shift == 0:
        return x
    row = lax.broadcasted_iota(jnp.int32, x.shape, 0)
    return jnp.where(row >= shift, pltpu.roll(x, shift, 0), 0.0)


def _mlstm_kernel(q_ref, k_ref, v_ref, og_ref, gc_ref, gr_ref, cwq_ref, cwk_ref, cbq_ref, cbk_ref,
                  ng_ref, o_ref, qc_sc, kc_sc):
    seq, dqk = q_ref.shape
    dv = v_ref.shape[1]
    L = MLSTM_CHUNK
    nt = (((1,), (1,)), ((), ()))

    def conv_silu(x, w_ref, b_ref):
        y = b_ref[...]
        for j in range(MLSTM_CONV):
            y = y + w_ref[j:j + 1, :] * _causal_shift(x, MLSTM_CONV - 1 - j)
        return y * jax.nn.sigmoid(y)

    qc_sc[...] = conv_silu(q_ref[...], cwq_ref, cbq_ref).astype(qc_sc.dtype)
    kc_sc[...] = (conv_silu(k_ref[...], cwk_ref, cbk_ref) * dqk ** -0.5).astype(kc_sc.dtype)

    t_i = lax.broadcasted_iota(jnp.int32, (L, L), 0)
    s_i = lax.broadcasted_iota(jnp.int32, (L, L), 1)
    causal = s_i <= t_i

    def chunk(c, carry):
        c_st, n_st, m_st = carry
        r0 = pl.multiple_of(c * L, L)
        q = qc_sc[pl.ds(r0, L), :]
        k = kc_sc[pl.ds(r0, L), :]
        v = v_ref[pl.ds(r0, L), :].astype(jnp.bfloat16)
        i_col = gc_ref[0, 0, pl.ds(r0, L), :]
        lf_col = _log_sigmoid(gc_ref[1, 0, pl.ds(r0, L), :])
        i_row = gr_ref[0, 0, 0, pl.ds(c, 1), :]
        lf_row = _log_sigmoid(gr_ref[0, 1, 0, pl.ds(c, 1), :])
        bcum_col = jnp.sum(jnp.where(causal, lf_row, 0.0), axis=1, keepdims=True)
        bcum_row = jnp.sum(jnp.where(t_i <= s_i, lf_col, 0.0), axis=0, keepdims=True)
        b_last = bcum_row[:, L - 1:L]
        dmat = jnp.where(causal, bcum_col - bcum_row + i_row, NEG_BIG)
        a_max = jnp.max(b_last - bcum_row + i_row, axis=1, keepdims=True)
        w_end = jnp.exp(b_last - bcum_col + i_col - a_max)
        inter = bcum_col + m_st
        m_t = jnp.maximum(inter, jnp.max(dmat, axis=1, keepdims=True))
        s_qk = lax.dot_general(q, k, nt, preferred_element_type=jnp.float32) * jnp.exp(dmat - m_t)
        w_inter = jnp.exp(inter - m_t)
        num = (jnp.dot(s_qk.astype(jnp.bfloat16), v, preferred_element_type=jnp.float32)
               + w_inter * jnp.dot(q, c_st.astype(jnp.bfloat16),
                                   preferred_element_type=jnp.float32))
        den = (jnp.sum(s_qk, axis=1, keepdims=True)
               + w_inter * jnp.sum(q.astype(jnp.float32) * n_st, axis=1, keepdims=True))
        h = num / jnp.maximum(jnp.abs(den), jnp.exp(-m_t))
        hc = h - jnp.mean(h, axis=1, keepdims=True)
        hn = hc * lax.rsqrt(jnp.mean(hc * hc, axis=1, keepdims=True) + LN_EPS) * ng_ref[...]
        o_ref[pl.ds(r0, L), :] = (jax.nn.sigmoid(og_ref[pl.ds(r0, L), :]) * hn).astype(o_ref.dtype)
        m_new = jnp.maximum(b_last + m_st, a_max)
        f_sc = jnp.exp(b_last + m_st - m_new)
        i_sc = jnp.exp(a_max - m_new)
        kw = k.astype(jnp.float32) * w_end
        kv = jnp.dot(kw.T.astype(jnp.bfloat16), v, preferred_element_type=jnp.float32)
        c_new = f_sc * c_st + i_sc * kv
        n_new = f_sc * n_st + i_sc * jnp.sum(kw, axis=0, keepdims=True)
        return c_new, n_new, m_new

    init = (jnp.zeros((dqk, dv), jnp.float32), jnp.zeros((1, dqk), jnp.float32),
            jnp.zeros((1, 1), jnp.float32))
    lax.fori_loop(0, seq // L, chunk, init, unroll=4)


def mlstm_pallas(proj_a, gates, conv_w, conv_b, norm_g, n_batch, seq, n_heads, dqk, dv,
                 q_col0, k_col0, v_col0, o_col0):
    nc = seq // MLSTM_CHUNK
    g_col = gates.T.reshape(2, n_heads, n_batch * seq, 1)
    g_row = jnp.moveaxis(gates.reshape(n_batch, nc, MLSTM_CHUNK, 2, n_heads), (3, 4), (1, 2))
    return pl.pallas_call(
        _mlstm_kernel,
        grid=(n_batch, n_heads),
        in_specs=[
            pl.BlockSpec((seq, dqk), lambda b, h: (b, q_col0 // dqk + h)),
            pl.BlockSpec((seq, dqk), lambda b, h: (b, k_col0 // dqk + h)),
            pl.BlockSpec((seq, dv), lambda b, h: (b, v_col0 // dv + h)),
            pl.BlockSpec((seq, dv), lambda b, h: (b, o_col0 // dv + h)),
            pl.BlockSpec((2, 1, seq, 1), lambda b, h: (0, h, b, 0)),
            pl.BlockSpec((1, 2, 1, nc, MLSTM_CHUNK), lambda b, h: (b, 0, h, 0, 0)),
            pl.BlockSpec((MLSTM_CONV, dqk), lambda b, h: (0, h)),
            pl.BlockSpec((MLSTM_CONV, dqk), lambda b, h: (0, n_heads + h)),
            pl.BlockSpec((1, dqk), lambda b, h: (0, h)),
            pl.BlockSpec((1, dqk), lambda b, h: (0, n_heads + h)),
            pl.BlockSpec((1, dv), lambda b, h: (0, h)),
        ],
        out_specs=pl.BlockSpec((seq, dv), lambda b, h: (b, h)),
        out_shape=jax.ShapeDtypeStruct((n_batch * seq, n_heads * dv), jnp.bfloat16),
        scratch_shapes=[pltpu.VMEM((seq, dqk), jnp.bfloat16),
                        pltpu.VMEM((seq, dqk), jnp.bfloat16)],
        compiler_params=pltpu.CompilerParams(
            dimension_semantics=("parallel", "parallel"), vmem_limit_bytes=VMEM_LIMIT),
    )(proj_a, proj_a, proj_a, proj_a, g_col, g_row, conv_w, conv_w,
      conv_b.reshape(1, -1), conv_b.reshape(1, -1), norm_g.reshape(1, -1))


GLA_TILE = 256


def _gla_kernel(q_ref, k_ref, v_ref, go_ref, lr_ref, w2_ref, gb_ref, ng_ref, o_ref, bc_sc, oi_sc,
                *, lr_col0):
    seq, dk = q_ref.shape
    dv = v_ref.shape[1]
    R, C = GLA_TILE, GLA_CHUNK
    scale = dk ** -0.5
    nt = (((1,), (1,)), ((), ()))
    bf16 = jnp.bfloat16
    row = lax.broadcasted_iota(jnp.int32, (R, R), 0)
    col = lax.broadcasted_iota(jnp.int32, (R, R), 1)
    lag = jnp.where((row // C) == (col // C), jnp.where(col <= row, row - col, -1), -1)
    tril16 = jnp.where(lag >= 0, 1.0, 0.0).astype(bf16)
    ones = jnp.ones((dk, R), bf16)
    w2 = w2_ref[...].astype(bf16)

    def tile(i, carry):
        r0 = pl.multiple_of(i * R, R)
        z = jnp.dot(lr_ref[pl.ds(r0, R), lr_col0:lr_col0 + GLA_RANK].astype(bf16), w2,
                    preferred_element_type=jnp.float32) + gb_ref[...]
        la = _log_sigmoid(z) * (1.0 / GLA_TAU)
        la_hi = la.astype(bf16)
        la_lo = (la - la_hi.astype(jnp.float32)).astype(bf16)
        bc = (jnp.dot(tril16, la_hi, preferred_element_type=jnp.float32)
              + jnp.dot(tril16, la_lo, preferred_element_type=jnp.float32))
        bc_sc[pl.ds(r0, R), :] = bc
        q = q_ref[pl.ds(r0, R), :] * scale
        k = k_ref[pl.ds(r0, R), :]
        attn = jnp.zeros((R, R), jnp.float32)
        for d in range(C):
            if d == 0:
                prod = q * k
            else:
                prod = q * pltpu.roll(k, d, 0) * jnp.exp(bc - pltpu.roll(bc, d, 0))
            a_d = jnp.dot(prod.astype(bf16), ones, preferred_element_type=jnp.float32)
            attn = jnp.where(lag == d, a_d, attn)
        oi_sc[pl.ds(r0, R), :] = jnp.dot(attn.astype(bf16), v_ref[pl.ds(r0, R), :].astype(bf16),
                                         preferred_element_type=jnp.float32)
        return carry

    lax.fori_loop(0, seq // R, tile, 0)

    def chunk(c, st):
        r0 = pl.multiple_of(c * C, C)
        bc = bc_sc[pl.ds(r0, C), :]
        bl = bc[C - 1:C, :]
        qt = (q_ref[pl.ds(r0, C), :] * scale * jnp.exp(bc)).astype(bf16)
        kt = (k_ref[pl.ds(r0, C), :] * jnp.exp(bl - bc)).astype(bf16)
        oi_sc[pl.ds(r0, C), :] += lax.dot_general(qt, st.astype(bf16), nt,
                                                  preferred_element_type=jnp.float32)
        vt = v_ref[pl.ds(r0, C), :].T.astype(bf16)
        return jnp.exp(bl) * st + jnp.dot(vt, kt, preferred_element_type=jnp.float32)

    lax.fori_loop(0, seq // C, chunk, jnp.zeros((dv, dk), jnp.float32), unroll=32)

    def finish(i, carry):
        r0 = pl.multiple_of(i * R, R)
        o = oi_sc[pl.ds(r0, R), :]
        on = o * lax.rsqrt(jnp.mean(o * o, axis=1, keepdims=True) + LN_EPS) * ng_ref[...]
        g = go_ref[pl.ds(r0, R), :]
        o_ref[pl.ds(r0, R), :] = (on * (g * jax.nn.sigmoid(g))).astype(o_ref.dtype)
        return carry

    lax.fori_loop(0, seq // R, finish, 0)


def gla_pallas(proj_a, proj_go, proj_c, lr_col0, gate_w2, gate_b, norm_g, n_batch, seq, n_heads,
               dk, dv, q_col0, k_col0, v_col0, go_col0):
    kern = functools.partial(_gla_kernel, lr_col0=lr_col0)
    return pl.pallas_call(
        kern,
        grid=(n_batch, n_heads),
        in_specs=[
            pl.BlockSpec((seq, dk), lambda b, h: (b, q_col0 // dk + h)),
            pl.BlockSpec((seq, dk), lambda b, h: (b, k_col0 // dk + h)),
            pl.BlockSpec((seq, dv), lambda b, h: (b, v_col0 // dv + h)),
            pl.BlockSpec((seq, dv), lambda b, h: (b, go_col0 // dv + h)),
            pl.BlockSpec((seq, proj_c.shape[1]), lambda b, h: (b, 0)),
            pl.BlockSpec((GLA_RANK, dk), lambda b, h: (0, h)),
            pl.BlockSpec((1, dk), lambda b, h: (0, h)),
            pl.BlockSpec((1, dv), lambda b, h: (0, h)),
        ],
        out_specs=pl.BlockSpec((seq, dv), lambda b, h: (b, h)),
        out_shape=jax.ShapeDtypeStruct((n_batch * seq, n_heads * dv), jnp.bfloat16),
        scratch_shapes=[pltpu.VMEM((seq, dk), jnp.float32), pltpu.VMEM((seq, dv), jnp.float32)],
        compiler_params=pltpu.CompilerParams(
            dimension_semantics=("parallel", "parallel"), vmem_limit_bytes=VMEM_LIMIT),
    )(proj_a, proj_a, proj_a, proj_go, proj_c, gate_w2, gate_b.reshape(1, -1),
      norm_g.reshape(1, -1))


CONV_TILE = 256
CONV_HALO = 32


def _conf_kernel(a_ref, g_ref, w_ref, b_ref, ng_ref, nb_ref, o_ref, x_sc):
    seq, ch = a_ref.shape
    R, H = CONV_TILE, CONV_HALO
    x_sc[0:H, :] = jnp.zeros((H, ch), jnp.float32)
    x_sc[H:, :] = a_ref[...] * jax.nn.sigmoid(g_ref[...])

    def tile(i, carry):
        r0 = pl.multiple_of(i * R, R)
        win = x_sc[pl.ds(r0, R + H), :]
        acc = jnp.zeros((R, ch), jnp.float32) + b_ref[...]
        for sub in range(8):
            sh = pltpu.roll(win, sub, 0) if sub else win
            for blk in range(H // 8):
                off = 8 * blk + sub
                if off >= CONV_WIDTH:
                    continue
                j = CONV_WIDTH - 1 - off
                acc = acc + w_ref[j:j + 1, :] * sh[H - 8 * blk:H - 8 * blk + R, :]
        yc = acc - jnp.mean(acc, axis=1, keepdims=True)
        yn = (yc * lax.rsqrt(jnp.mean(yc * yc, axis=1, keepdims=True) + LN_EPS) * ng_ref[...]
              + nb_ref[...])
        o_ref[pl.ds(r0, R), :] = (yn * jax.nn.sigmoid(yn)).astype(o_ref.dtype)
        return carry

    lax.fori_loop(0, seq // R, tile, 0)


def conformer_conv_pallas(proj_a, a_col0, g_col0, dw_w, dw_b, norm_g, norm_b, n_batch, seq,
                          n_groups, width):
    ch = width // n_groups
    vec = pl.BlockSpec((1, ch), lambda b, g: (0, g))
    return pl.pallas_call(
        _conf_kernel,
        grid=(n_batch, n_groups),
        in_specs=[pl.BlockSpec((seq, ch), lambda b, g: (b, a_col0 // ch + g)),
                  pl.BlockSpec((seq, ch), lambda b, g: (b, g_col0 // ch + g)),
                  pl.BlockSpec((CONV_WIDTH, ch), lambda b, g: (0, g)),
                  vec, vec, vec],
        out_specs=pl.BlockSpec((seq, ch), lambda b, g: (b, g)),
        out_shape=jax.ShapeDtypeStruct((n_batch * seq, width), jnp.bfloat16),
        scratch_shapes=[pltpu.VMEM((seq + CONV_HALO, ch), jnp.float32)],
        compiler_params=pltpu.CompilerParams(
            dimension_semantics=("parallel", "parallel"), vmem_limit_bytes=VMEM_LIMIT),
    )(proj_a, proj_a, dw_w, dw_b.reshape(1, -1), norm_g.reshape(1, -1), norm_b.reshape(1, -1))


def hybrid_mixer(ub, B, S, layer, w_in, b_merge, mlstm_gate_b, mlstm_conv_w, mlstm_conv_b,
                 mlstm_norm_g, conf_dw_w, conf_dw_b, conf_norm_g, conf_norm_b, rel_bias_t, moba_btab,
                 gla_gate_w2, gla_gate_b, gla_norm_g, w_branch):
    W = BRANCH_WIDTH
    bf16, f32 = jnp.bfloat16, jnp.float32
    offs = [0] + np.cumsum(IN_SIZES).tolist()
    w_1 = repack_columns(w_in, layer, 0, offs[4])
    w_2 = repack_columns(w_in, layer, offs[5], offs[10] - offs[5])
    w_3 = repack_columns(w_in, layer, offs[11], offs[13] - offs[11])
    proj_ml = matmul(ub, w_1, 512, 1024, f32)
    proj_cf = matmul(ub, w_2, 512, 1024, f32, 0, 2 * W)
    proj_mb = matmul(ub, w_2, 512, 1024, bf16, 2 * W, 3 * W)
    proj_gl = matmul(ub, w_2, 512, 1024, f32, 5 * W, 2 * W)
    proj_go = matmul(ub, w_3, 512, 1024, f32, 0, W)
    proj_b = matmul(ub, w_3, 512, 1024, bf16, W)
    c_if = offs[4] - offs[4] % LANES
    c_lr = offs[10] - offs[10] % LANES
    proj_if = matmul_rows_t(ub, w_in, layer, c_if, LANES, 512, f32)
    proj_lr = matmul_rows_t(ub, w_in, layer, c_lr, LANES, 512, f32)

    gates = proj_if[:, offs[4] - c_if:offs[5] - c_if] + mlstm_gate_b
    y_mlstm = mlstm_pallas(proj_ml, gates, mlstm_conv_w, mlstm_conv_b, mlstm_norm_g, B, S,
                           MLSTM_HEADS, MLSTM_DQK, MLSTM_DV, 0, W // 2, W, 2 * W)
    y_conv = conformer_conv_pallas(proj_cf, 0, W, conf_dw_w, conf_dw_b, conf_norm_g, conf_norm_b,
                                   B, S, CONV_GROUPS, W)
    y_moba = moba_pallas(proj_mb, rel_bias_t, moba_btab, B, S, 0, MOBA_HEADS, 2 * MOBA_HEADS, bf16)
    y_gla = gla_pallas(proj_gl, proj_go, proj_lr, offs[10] - c_lr, gla_gate_w2, gla_gate_b,
                       gla_norm_g, B, S, GLA_HEADS, GLA_DK, GLA_DV, 0, W // 2, W, 0)
    return merge_branches([y_mlstm, y_conv, y_moba, y_gla], w_branch, layer, proj_b, 0, b_merge)


def kernel(x, c, w_ada, b_ada, w_in, b_merge, mlstm_gate_b, mlstm_conv_w, mlstm_conv_b,
           mlstm_norm_g, conf_dw_w, conf_dw_b, conf_norm_g, conf_norm_b, rel_bias,
           gla_gate_w2, gla_gate_b, gla_norm_g, w_branch, w_out, ln1_g, ln1_b,
           router_w, router_b, exp_w_up, exp_b_up, exp_w_down, exp_b_down, ln2_g, ln2_b):
    B, S, D = x.shape
    bf16 = jnp.bfloat16
    mod_all = ada_modulation(c, w_ada, b_ada)
    mods = [jnp.split(mod_all[l][:, None, :], 6, axis=-1) for l in range(DEPTH)]
    rel_bias_t = rel_bias.T.astype(jnp.float32)
    moba_btab = moba_bias_tables(rel_bias_t)
    n_rows = B * S * TOP_K + N_EXPERTS * MOE_TILE
    x2d = x.reshape(B * S, D)
    w_in_t = jnp.swapaxes(w_in, 1, 2)
    ub = modulate(x2d, mods[0][1], mods[0][0], S)
    for l in range(DEPTH):
        sh1, sc1, g1, sh2, sc2, g2 = mods[l]
        merged = hybrid_mixer(ub, B, S, l, w_in_t, b_merge[l], mlstm_gate_b[l], mlstm_conv_w[l],
                              mlstm_conv_b[l], mlstm_norm_g[l], conf_dw_w[l], conf_dw_b[l],
                              conf_norm_g[l], conf_norm_b[l], rel_bias_t, moba_btab,
                              gla_gate_w2[l], gla_gate_b[l], gla_norm_g[l], w_branch)
        mix = matmul_f32w(merged, w_out, l, 1024, 512, jnp.float32)
        x1, upk, top_idx, top_w, rank4, counts = ln_router(
            x2d, mix, g1, ln1_g[l], ln1_b[l], sc2, sh2, router_w[l].astype(bf16), router_b[l], S)
        dest, tile_expert, n_tiles, last_tile_row = moe_layout(counts, top_idx, rank4,
                                                               n_rows // MOE_TILE)
        xs = moe_dispatch(upk, dest, last_tile_row, n_rows)
        ys = grouped_experts(xs, tile_expert, n_tiles, exp_w_up, exp_b_up, exp_w_down, exp_b_down, l)
        nxt = min(l + 1, DEPTH - 1)
        x2d, ub = combine_ln(dest, top_w, x1, g2, ln2_g[l], ln2_b[l], mods[nxt][1], mods[nxt][0],
                             ys, S)
    return x2d.reshape(B, S, D)
```

```python
import functools
import math

import jax
import jax.numpy as jnp
import numpy as np
from jax import lax
from jax.experimental import pallas as pl
from jax.experimental.pallas import tpu as pltpu

D_MODEL = 4096
BATCH = 4
SEQ = 2048
DEPTH = 2

N_BRANCHES = 4
BRANCH_WIDTH = D_MODEL // N_BRANCHES
MLSTM_HEADS = 4
MLSTM_DV = BRANCH_WIDTH // MLSTM_HEADS
MLSTM_DQK = MLSTM_DV // 2
MLSTM_CONV = 4
MLSTM_CHUNK = 64
CONV_WIDTH = 31
CONV_GROUPS = 4
MOBA_HEADS = 8
MOBA_HD = BRANCH_WIDTH // MOBA_HEADS
MOBA_BLOCK = 256
MOBA_TOPK = 3
MOBA_Q_CHUNK = 32
REL_BUCKETS = 32
REL_MAX_DIST = 128
GLA_HEADS = 4
GLA_DV = BRANCH_WIDTH // GLA_HEADS
GLA_DK = GLA_DV // 2
GLA_RANK = 16
GLA_TAU = 16.0
GLA_CHUNK = 16
N_EXPERTS = 32
TOP_K = 4
EXPERT_FF = D_MODEL // 8
SWIGLU_ALPHA = 1.702
SWIGLU_LIMIT = 7.0
DEEPNORM_ALPHA = (2 * DEPTH) ** 0.25
LN_EPS = 1e-5

IN_SIZES = (
    MLSTM_HEADS * MLSTM_DQK, MLSTM_HEADS * MLSTM_DQK, BRANCH_WIDTH, BRANCH_WIDTH,
    2 * MLSTM_HEADS, 2 * BRANCH_WIDTH, 3 * BRANCH_WIDTH, GLA_HEADS * GLA_DK,
    GLA_HEADS * GLA_DK, BRANCH_WIDTH, GLA_RANK, BRANCH_WIDTH, N_BRANCHES * D_MODEL,
)
IN_WIDTH = sum(IN_SIZES)

VMEM_LIMIT = 56 * 1024 * 1024
PROJ_ROWS = 1024
PROJ_COLS = 1024


def _mm_kernel(a_ref, b_ref, o_ref):
    o_ref[...] = jnp.dot(a_ref[...], b_ref[...],
                         preferred_element_type=jnp.float32).astype(o_ref.dtype)


def matmul(a, b, tm, tn, out_dtype, col0=0, ncols=None):
    M, K = a.shape
    N = b.shape[1] - col0 if ncols is None else ncols
    assert col0 % tn == 0 and N % tn == 0 and M % tm == 0
    jb = col0 // tn
    return pl.pallas_call(
        _mm_kernel,
        grid=(N // tn, M // tm),
        in_specs=[pl.BlockSpec((tm, K), lambda j, i: (i, 0)),
                  pl.BlockSpec((K, tn), lambda j, i: (0, jb + j))],
        out_specs=pl.BlockSpec((tm, tn), lambda j, i: (i, j)),
        out_shape=jax.ShapeDtypeStruct((M, N), out_dtype),
        compiler_params=pltpu.CompilerParams(
            dimension_semantics=("parallel", "parallel"), vmem_limit_bytes=VMEM_LIMIT),
    )(a, b)


def _mm_f32w_kernel(a_ref, b_ref, o_ref, wb_sc):
    @pl.when(pl.program_id(1) == 0)
    def _():
        wb_sc[...] = b_ref[...].astype(wb_sc.dtype)

    o_ref[...] = jnp.dot(a_ref[...], wb_sc[...],
                         preferred_element_type=jnp.float32).astype(o_ref.dtype)


def matmul_f32w(a, b, layer, tm, tn, out_dtype, col0=0, ncols=None):
    M, K = a.shape
    N = b.shape[2] - col0 if ncols is None else ncols
    assert col0 % tn == 0 and N % tn == 0 and M % tm == 0
    jb = col0 // tn
    return pl.pallas_call(
        _mm_f32w_kernel,
        grid=(N // tn, M // tm),
        in_specs=[pl.BlockSpec((tm, K), lambda j, i: (i, 0)),
                  pl.BlockSpec((None, K, tn), lambda j, i: (layer, 0, jb + j))],
        out_specs=pl.BlockSpec((tm, tn), lambda j, i: (i, j)),
        out_shape=jax.ShapeDtypeStruct((M, N), out_dtype),
        scratch_shapes=[pltpu.VMEM((K, tn), jnp.bfloat16)],
        compiler_params=pltpu.CompilerParams(
            dimension_semantics=("parallel", "arbitrary"), vmem_limit_bytes=VMEM_LIMIT),
    )(a, b)


LANES = 128
SUBLANES = 8
REPACK_ROWS = 1024
REPACK_COLS = 1024
REPACK_EDGE = 32


def _repack_kernel(*refs, shift):
    if shift:
        main_ref, edge_ref, o_ref = refs
        x = jnp.concatenate([main_ref[shift:, :], edge_ref[:shift, :]], axis=0)
    else:
        main_ref, o_ref = refs
        x = main_ref[...]
    o_ref[...] = x.T.astype(o_ref.dtype)


def repack_columns(w_t, layer, col0, ncols):
    _, n, k = w_t.shape
    tk, tn = min(REPACK_ROWS, k), REPACK_COLS
    shift = col0 % tn
    base = col0 - shift
    assert shift % SUBLANES == 0 and shift <= REPACK_EDGE and ncols % tn == 0 and k % tk == 0
    in_specs = [pl.BlockSpec((None, tn, tk), lambda j, r: (layer, base // tn + j, r))]
    args = [w_t]
    if shift:
        in_specs.append(pl.BlockSpec(
            (None, REPACK_EDGE, tk), lambda j, r: (layer, (base + (j + 1) * tn) // REPACK_EDGE, r)))
        args.append(w_t)
    return pl.pallas_call(
        functools.partial(_repack_kernel, shift=shift),
        grid=(ncols // tn, k // tk),
        in_specs=in_specs,
        out_specs=pl.BlockSpec((tk, tn), lambda j, r: (r, j)),
        out_shape=jax.ShapeDtypeStruct((k, ncols), jnp.bfloat16),
        compiler_params=pltpu.CompilerParams(
            dimension_semantics=("parallel", "parallel"), vmem_limit_bytes=VMEM_LIMIT),
    )(*args)


def _mm_nt_kernel(a_ref, bt_ref, o_ref):
    o_ref[...] = lax.dot_general(a_ref[...], bt_ref[...].astype(jnp.bfloat16),
                                 (((1,), (1,)), ((), ())),
                                 preferred_element_type=jnp.float32).astype(o_ref.dtype)


def matmul_rows_t(a, w_t, layer, row0, nrows, tm, out_dtype):
    M, K = a.shape
    assert row0 % nrows == 0 and M % tm == 0
    return pl.pallas_call(
        _mm_nt_kernel,
        grid=(M // tm,),
        in_specs=[pl.BlockSpec((tm, K), lambda i: (i, 0)),
                  pl.BlockSpec((None, nrows, K), lambda i: (layer, row0 // nrows, 0))],
        out_specs=pl.BlockSpec((tm, nrows), lambda i: (i, 0)),
        out_shape=jax.ShapeDtypeStruct((M, nrows), out_dtype),
        compiler_params=pltpu.CompilerParams(
            dimension_semantics=("parallel",), vmem_limit_bytes=VMEM_LIMIT),
    )(a, w_t)


ROW_TILE = 256
MOE_TILE = 256


def _ada_kernel(c_ref, w_ref, b_ref, o_ref):
    c = c_ref[...]
    cs = (c * jax.nn.sigmoid(c)).astype(jnp.bfloat16)
    o_ref[0] = jnp.dot(cs, w_ref[0].astype(jnp.bfloat16),
                       preferred_element_type=jnp.float32) + b_ref[0]


def ada_modulation(c, w_ada, b_ada, tn=512):
    n_layers, d, n = w_ada.shape
    nb = c.shape[0]
    c8 = jnp.pad(c, ((0, 8 - nb), (0, 0)))
    out = pl.pallas_call(
        _ada_kernel,
        grid=(n_layers, n // tn),
        in_specs=[pl.BlockSpec((8, d), lambda l, j: (0, 0)),
                  pl.BlockSpec((1, d, tn), lambda l, j: (l, 0, j)),
                  pl.BlockSpec((1, 1, tn), lambda l, j: (l, 0, j))],
        out_specs=pl.BlockSpec((1, 8, tn), lambda l, j: (l, 0, j)),
        out_shape=jax.ShapeDtypeStruct((n_layers, 8, n), jnp.float32),
        compiler_params=pltpu.CompilerParams(
            dimension_semantics=("parallel", "parallel"), vmem_limit_bytes=VMEM_LIMIT),
    )(c8, w_ada, b_ada.reshape(n_layers, 1, n))
    return out[:, :nb]


def _modulate_kernel(x_ref, sc_ref, sh_ref, o_ref):
    o_ref[...] = (x_ref[...] * (1.0 + sc_ref[0]) + sh_ref[0]).astype(o_ref.dtype)


def modulate(x2d, sc, sh, seq):
    t, d = x2d.shape
    per_b = seq // ROW_TILE
    mod_spec = pl.BlockSpec((1, 1, d), lambda i: (i // per_b, 0, 0))
    return pl.pallas_call(
        _modulate_kernel,
        grid=(t // ROW_TILE,),
        in_specs=[pl.BlockSpec((ROW_TILE, d), lambda i: (i, 0)), mod_spec, mod_spec],
        out_specs=pl.BlockSpec((ROW_TILE, d), lambda i: (i, 0)),
        out_shape=jax.ShapeDtypeStruct((t, d), jnp.bfloat16),
        compiler_params=pltpu.CompilerParams(dimension_semantics=("parallel",)),
    )(x2d, sc, sh)


def _merge_kernel(y0, y1, y2, y3, wb_ref, g0, g1, g2, g3, bm_ref, o_ref, wb_sc):
    @pl.when(pl.program_id(1) == 0)
    def _():
        wb_sc[...] = wb_ref[...].astype(wb_sc.dtype)

    acc = None
    for n, (y, g) in enumerate(((y0, g0), (y1, g1), (y2, g2), (y3, g3))):
        gate = jax.nn.sigmoid(g[...].astype(jnp.float32) + bm_ref[n:n + 1, :])
        term = gate * jnp.dot(y[...], wb_sc[n], preferred_element_type=jnp.float32)
        acc = term if acc is None else acc + term
    o_ref[...] = acc.astype(o_ref.dtype)


def merge_branches(ys, w_branch, layer, proj_b, gate_col0, b_merge, tm=1024, tn=512):
    t, w = ys[0].shape
    d = w_branch.shape[3]
    assert gate_col0 % tn == 0 and d % tn == 0
    y_spec = pl.BlockSpec((tm, w), lambda j, i: (i, 0))
    g_specs = [pl.BlockSpec((tm, tn), lambda j, i, n=n: (i, (gate_col0 + n * d) // tn + j))
               for n in range(N_BRANCHES)]
    return pl.pallas_call(
        _merge_kernel,
        grid=(d // tn, t // tm),
        in_specs=[y_spec] * 4 + [pl.BlockSpec((None, N_BRANCHES, w, tn),
                                              lambda j, i: (layer, 0, 0, j))]
        + g_specs + [pl.BlockSpec((N_BRANCHES, tn), lambda j, i: (0, j))],
        out_specs=pl.BlockSpec((tm, tn), lambda j, i: (i, j)),
        out_shape=jax.ShapeDtypeStruct((t, d), jnp.bfloat16),
        scratch_shapes=[pltpu.VMEM((N_BRANCHES, w, tn), jnp.bfloat16)],
        compiler_params=pltpu.CompilerParams(
            dimension_semantics=("parallel", "arbitrary"), vmem_limit_bytes=VMEM_LIMIT),
    )(*ys, w_branch, proj_b, proj_b, proj_b, proj_b, b_merge.reshape(N_BRANCHES, d))


def _pack_pair(lo, hi):
    def rne(x):
        b = lax.bitcast_convert_type(x, jnp.uint32)
        return (b + jnp.uint32(0x7FFF) + ((b >> 16) & jnp.uint32(1))) >> 16
    return rne(lo) | (rne(hi) << 16)


def _unpack_lo(w):
    return lax.bitcast_convert_type(w << 16, jnp.float32)


def _unpack_hi(w):
    return lax.bitcast_convert_type(w & jnp.uint32(0xFFFF0000), jnp.float32)


def _layer_norm_halves(z_lo, z_hi, g_ref, b_ref, half):
    d = 2 * half
    mu = (jnp.sum(z_lo, axis=1, keepdims=True) + jnp.sum(z_hi, axis=1, keepdims=True)) / d
    c_lo, c_hi = z_lo - mu, z_hi - mu
    var = (jnp.sum(c_lo * c_lo, axis=1, keepdims=True)
           + jnp.sum(c_hi * c_hi, axis=1, keepdims=True)) / d
    r = lax.rsqrt(var + LN_EPS)
    return (c_lo * r * g_ref[:, :half] + b_ref[:, :half],
            c_hi * r * g_ref[:, half:] + b_ref[:, half:])


def _ln_router_kernel(x_ref, mix_ref, g1_ref, lng_ref, lnb_ref, sc_ref, sh_ref, rw_ref, rb_ref,
                      x1_ref, upk_ref, tidx_ref, tw_ref, rank_ref, cnt_ref, carry_sc):
    tm, d = x_ref.shape
    half = d // 2
    n_exp = rw_ref.shape[1]

    @pl.when(pl.program_id(0) == 0)
    def _():
        carry_sc[...] = jnp.zeros_like(carry_sc)

    z = DEEPNORM_ALPHA * x_ref[...] + (1.0 + g1_ref[0]) * mix_ref[...]
    x_lo, x_hi = _layer_norm_halves(z[:, :half], z[:, half:], lng_ref, lnb_ref, half)
    x1_ref[:, :half] = x_lo
    x1_ref[:, half:] = x_hi
    u_lo = x_lo * (1.0 + sc_ref[0, :, :half]) + sh_ref[0, :, :half]
    u_hi = x_hi * (1.0 + sc_ref[0, :, half:]) + sh_ref[0, :, half:]
    upk_ref[...] = _pack_pair(u_lo, u_hi)

    logits = (jnp.dot(u_lo.astype(jnp.bfloat16), rw_ref[:half, :],
                      preferred_element_type=jnp.float32)
              + jnp.dot(u_hi.astype(jnp.bfloat16), rw_ref[half:, :],
                        preferred_element_type=jnp.float32) + rb_ref[...])
    lane = lax.broadcasted_iota(jnp.int32, (tm, n_exp), 1).astype(jnp.float32)
    lg = logits
    vals, hots = [], []
    for k in range(TOP_K):
        m = jnp.max(lg, axis=1, keepdims=True)
        ik = jnp.min(jnp.where(lg == m, lane, float(n_exp)), axis=1, keepdims=True)
        hot = lane == ik
        vals.append(m)
        hots.append(hot)
        tidx_ref[:, k:k + 1] = ik.astype(jnp.int32)
        lg = jnp.where(hot, -jnp.inf, lg)
    exps = [jnp.exp(v - vals[0]) for v in vals]
    den = exps[0] + exps[1] + exps[2] + exps[3]
    for k in range(TOP_K):
        tw_ref[:, k:k + 1] = exps[k] / den

    mask = jnp.where(hots[0] | hots[1] | hots[2] | hots[3], 1.0, 0.0)
    row = lax.broadcasted_iota(jnp.int32, (tm, tm), 0)
    col = lax.broadcasted_iota(jnp.int32, (tm, tm), 1)
    lower = jnp.where(row > col, 1.0, 0.0).astype(jnp.bfloat16)
    rank = jnp.dot(lower, mask.astype(jnp.bfloat16),
                   preferred_element_type=jnp.float32) + carry_sc[...]
    for k in range(TOP_K):
        rank_ref[:, k:k + 1] = jnp.sum(jnp.where(hots[k], rank, 0.0), axis=1,
                                       keepdims=True).astype(jnp.int32)
    carry_sc[...] = carry_sc[...] + jnp.sum(mask, axis=0, keepdims=True)
    cnt_ref[...] = carry_sc[...]


def ln_router(x2d, mix, g1, ln_g, ln_b, sc2, sh2, router_w_bf, router_b, seq):
    t, d = x2d.shape
    n_exp = router_w_bf.shape[1]
    per_b = seq // ROW_TILE
    row = lambda w: pl.BlockSpec((ROW_TILE, w), lambda i: (i, 0))
    mod = pl.BlockSpec((1, 1, d), lambda i: (i // per_b, 0, 0))
    vec = lambda w: pl.BlockSpec((1, w), lambda i: (0, 0))
    return pl.pallas_call(
        _ln_router_kernel,
        grid=(t // ROW_TILE,),
        in_specs=[row(d), row(d), mod, vec(d), vec(d), mod, mod,
                  pl.BlockSpec((d, n_exp), lambda i: (0, 0)), vec(n_exp)],
        out_specs=[row(d), row(d // 2), row(TOP_K), row(TOP_K), row(TOP_K), vec(n_exp)],
        out_shape=[jax.ShapeDtypeStruct((t, d), jnp.float32),
                   jax.ShapeDtypeStruct((t, d // 2), jnp.uint32),
                   jax.ShapeDtypeStruct((t, TOP_K), jnp.int32),
                   jax.ShapeDtypeStruct((t, TOP_K), jnp.float32),
                   jax.ShapeDtypeStruct((t, TOP_K), jnp.int32),
                   jax.ShapeDtypeStruct((1, n_exp), jnp.float32)],
        scratch_shapes=[pltpu.VMEM((1, n_exp), jnp.float32)],
        compiler_params=pltpu.CompilerParams(
            dimension_semantics=("arbitrary",), vmem_limit_bytes=VMEM_LIMIT),
    )(x2d, mix, g1, ln_g.reshape(1, d), ln_b.reshape(1, d), sc2, sh2, router_w_bf,
      router_b.reshape(1, n_exp))


def _row_copy(src_hbm, src_row, dst, dst_row, sem):
    return pltpu.make_async_copy(src_hbm.at[pl.ds(src_row, 1)], dst.at[pl.ds(dst_row, 1)], sem)


def _dispatch_kernel(dest_ref, last_ref, u_ref, xs_hbm, zero_sc, sem, zsem):
    n_exp = last_ref.shape[0] - 1
    n_tiles_max = xs_hbm.shape[0] // MOE_TILE

    @pl.when(pl.program_id(0) == 0)
    def _():
        zero_sc[...] = jnp.zeros_like(zero_sc)

        def fill(row):
            return pltpu.make_async_copy(
                zero_sc, xs_hbm.at[pl.ds(pl.multiple_of(row, MOE_TILE), MOE_TILE)], zsem)

        def start(ex, carry):
            @pl.when(last_ref[ex] >= 0)
            def _():
                fill(last_ref[ex]).start()
            return carry

        def finish(ex, carry):
            @pl.when(last_ref[ex] >= 0)
            def _():
                fill(0).wait()
            return carry

        def start_tail(g, carry):
            fill(g * MOE_TILE).start()
            return carry

        def finish_tail(g, carry):
            fill(0).wait()
            return carry

        lax.fori_loop(0, n_exp, start, 0)
        lax.fori_loop(last_ref[n_exp], n_tiles_max, start_tail, 0)
        lax.fori_loop(0, n_exp, finish, 0)
        lax.fori_loop(last_ref[n_exp], n_tiles_max, finish_tail, 0)

    def issue(r, carry):
        for k in range(TOP_K):
            _row_copy(u_ref, r, xs_hbm, dest_ref[r * TOP_K + k], sem).start(priority=k % 2)
        return carry

    def drain(r, carry):
        for k in range(TOP_K):
            _row_copy(u_ref, 0, xs_hbm, 0, sem).wait()
        return carry

    lax.fori_loop(0, ROW_TILE, issue, 0, unroll=4)
    lax.fori_loop(0, ROW_TILE, drain, 0, unroll=8)


def moe_dispatch(upk, dest_flat, last_tile_row, n_rows):
    t, w = upk.shape
    return pl.pallas_call(
        _dispatch_kernel,
        grid=(t // ROW_TILE,),
        in_specs=[pl.BlockSpec((ROW_TILE * TOP_K,), lambda i: (i,), memory_space=pltpu.SMEM),
                  pl.BlockSpec(memory_space=pltpu.SMEM),
                  pl.BlockSpec((ROW_TILE, w), lambda i: (i, 0))],
        out_specs=pl.BlockSpec(memory_space=pl.ANY),
        out_shape=jax.ShapeDtypeStruct((n_rows, w), upk.dtype),
        scratch_shapes=[pltpu.VMEM((MOE_TILE, w), upk.dtype),
                        pltpu.SemaphoreType.DMA(()), pltpu.SemaphoreType.DMA(())],
        compiler_params=pltpu.CompilerParams(dimension_semantics=("arbitrary",)),
    )(dest_flat, last_tile_row, upk)


UP_CHUNKS = 4


def _expert_kernel(te_ref, nt_ref, xs_ref, wu_hbm, bu_ref, wd_hbm, bd_ref, ys_ref,
                   stage_u, stage_d, wu_sc, wd_sc, sem, *, layer):
    g = pl.program_id(0)
    half = xs_ref.shape[1]
    ff = wd_sc.shape[0]
    rows = stage_u.shape[1]
    e = te_ref[g]
    new_expert = jnp.logical_or(g == 0, e != te_ref[jnp.maximum(g - 1, 0)])

    def up_copy(ex, c):
        return pltpu.make_async_copy(wu_hbm.at[layer, ex, pl.ds(c * rows, rows)],
                                     stage_u.at[c % 2], sem.at[c % 2])

    def down_copy(ex):
        return pltpu.make_async_copy(wd_hbm.at[layer, ex], stage_d, sem.at[2])

    def start_head(ex):
        up_copy(ex, 0).start()
        up_copy(ex, 1).start()
        down_copy(ex).start()

    @pl.when(jnp.logical_and(g < nt_ref[0], new_expert))
    def _():
        @pl.when(g == 0)
        def _():
            start_head(e)

        for c in range(UP_CHUNKS):
            up_copy(e, c).wait()
            wu_sc[c * rows:(c + 1) * rows, :] = stage_u[c % 2].astype(wu_sc.dtype)
            if c + 2 < UP_CHUNKS:
                up_copy(e, c + 2).start()
        down_copy(e).wait()
        wd_sc[...] = stage_d[...].astype(wd_sc.dtype)

    e_after = te_ref[pl.num_programs(0) + g]

    @pl.when(jnp.logical_and(jnp.logical_and(g < nt_ref[0], new_expert), e_after >= 0))
    def _():
        start_head(e_after)

    @pl.when(g < nt_ref[0])
    def _():
        w = xs_ref[...]
        x_lo = _unpack_lo(w).astype(jnp.bfloat16)
        x_hi = _unpack_hi(w).astype(jnp.bfloat16)
        hid = (jnp.dot(x_lo, wu_sc[:half, :], preferred_element_type=jnp.float32)
               + jnp.dot(x_hi, wu_sc[half:, :], preferred_element_type=jnp.float32)
               + bu_ref[0, 0])
        glu = jnp.minimum(hid[:, :ff], SWIGLU_LIMIT)
        lin = jnp.clip(hid[:, ff:], -SWIGLU_LIMIT, SWIGLU_LIMIT)
        act = glu * jax.nn.sigmoid(SWIGLU_ALPHA * glu) * (lin + 1.0)
        y = jnp.dot(act.astype(jnp.bfloat16), wd_sc[...],
                    preferred_element_type=jnp.float32) + bd_ref[0, 0]
        ys_ref[...] = _pack_pair(y[:, :half], y[:, half:])

    @pl.when(g >= nt_ref[0])
    def _():
        ys_ref[...] = jnp.zeros_like(ys_ref)


def grouped_experts(xs, tile_expert, n_tiles, w_up, b_up, w_down, b_down, layer):
    n_rows, half = xs.shape
    _, n_exp, d, ff2 = w_up.shape
    ff = ff2 // 2
    grid_spec = pltpu.PrefetchScalarGridSpec(
        num_scalar_prefetch=2,
        grid=(n_rows // MOE_TILE,),
        in_specs=[pl.BlockSpec((MOE_TILE, half), lambda g, te, nt: (jnp.minimum(g, nt[0] - 1), 0)),
                  pl.BlockSpec(memory_space=pl.ANY),
                  pl.BlockSpec((1, 1, 1, ff2), lambda g, te, nt: (layer, te[g], 0, 0)),
                  pl.BlockSpec(memory_space=pl.ANY),
                  pl.BlockSpec((1, 1, 1, d), lambda g, te, nt: (layer, te[g], 0, 0))],
        out_specs=pl.BlockSpec((MOE_TILE, half), lambda g, te, nt: (g, 0)),
        scratch_shapes=[pltpu.VMEM((2, d // UP_CHUNKS, ff2), jnp.float32),
                        pltpu.VMEM((ff, d), jnp.float32),
                        pltpu.VMEM((d, ff2), jnp.bfloat16),
                        pltpu.VMEM((ff, d), jnp.bfloat16),
                        pltpu.SemaphoreType.DMA((3,))],
    )
    return pl.pallas_call(
        functools.partial(_expert_kernel, layer=layer),
        grid_spec=grid_spec,
        out_shape=jax.ShapeDtypeStruct((n_rows, half), jnp.uint32),
        compiler_params=pltpu.CompilerParams(
            dimension_semantics=("arbitrary",), vmem_limit_bytes=VMEM_LIMIT),
    )(tile_expert, n_tiles, xs, w_up, b_up.reshape(b_up.shape[0], n_exp, 1, ff2), w_down,
      b_down.reshape(b_down.shape[0], n_exp, 1, d))


COMBINE_ROWS = 16


def _combine_kernel(dest_ref, dest_next_ref, tw_ref, x1_ref, g2_ref, lng_ref, lnb_ref, sc_ref,
                    sh_ref, ys_hbm, x2_ref, un_ref, buf, sem):
    tm, d = x1_ref.shape
    half = d // 2
    i = pl.program_id(0)
    slot = lax.rem(i, 2)

    has_next = i + 1 < pl.num_programs(0)
    sub = COMBINE_ROWS

    def gather_rows(d_ref, s, r0):
        for rr in range(sub):
            for k in range(TOP_K):
                _row_copy(ys_hbm, d_ref[(r0 + rr) * TOP_K + k], buf.at[s, k], r0 + rr,
                          sem.at[s]).start(priority=k % 2)

    @pl.when(i == 0)
    def _():
        def first(t, carry):
            gather_rows(dest_ref, 0, t * sub)
            return carry
        lax.fori_loop(0, tm // sub, first, 0)

    def drain(r, carry):
        for k in range(TOP_K):
            _row_copy(ys_hbm, 0, buf.at[slot, k], 0, sem.at[slot]).wait()
        return carry

    lax.fori_loop(0, tm, drain, 0)

    def rows(t, carry):
        r0 = pl.multiple_of(t * sub, sub)
        rs = pl.ds(r0, sub)

        gather_rows(dest_next_ref, 1 - slot, r0)

        f_lo = f_hi = None
        for k in range(TOP_K):
            wk = tw_ref[rs, k:k + 1]
            w = buf[slot, k, rs, :]
            t_lo, t_hi = wk * _unpack_lo(w), wk * _unpack_hi(w)
            f_lo = t_lo if f_lo is None else f_lo + t_lo
            f_hi = t_hi if f_hi is None else f_hi + t_hi
        z_lo = DEEPNORM_ALPHA * x1_ref[rs, :half] + (1.0 + g2_ref[0, :, :half]) * f_lo
        z_hi = DEEPNORM_ALPHA * x1_ref[rs, half:] + (1.0 + g2_ref[0, :, half:]) * f_hi
        x_lo, x_hi = _layer_norm_halves(z_lo, z_hi, lng_ref, lnb_ref, half)
        x2_ref[rs, :half] = x_lo
        x2_ref[rs, half:] = x_hi
        un_ref[rs, :half] = (x_lo * (1.0 + sc_ref[0, :, :half])
                             + sh_ref[0, :, :half]).astype(un_ref.dtype)
        un_ref[rs, half:] = (x_hi * (1.0 + sc_ref[0, :, half:])
                             + sh_ref[0, :, half:]).astype(un_ref.dtype)
        return carry

    lax.fori_loop(0, tm // sub, rows, 0)

    @pl.when(jnp.logical_not(has_next))
    def _():
        def drain_other(r, carry):
            for k in range(TOP_K):
                _row_copy(ys_hbm, 0, buf.at[1 - slot, k], 0, sem.at[1 - slot]).wait()
            return carry
        lax.fori_loop(0, tm, drain_other, 0)


def combine_ln(dest_flat, top_w, x1, g2, ln_g, ln_b, sc_next, sh_next, ys, seq):
    t, d = x1.shape
    per_b = seq // ROW_TILE
    n_steps = t // ROW_TILE
    row = lambda w: pl.BlockSpec((ROW_TILE, w), lambda i: (i, 0))
    mod = pl.BlockSpec((1, 1, d), lambda i: (i // per_b, 0, 0))
    vec = pl.BlockSpec((1, d), lambda i: (0, 0))
    return pl.pallas_call(
        _combine_kernel,
        grid=(n_steps,),
        in_specs=[pl.BlockSpec((ROW_TILE * TOP_K,), lambda i: (i,), memory_space=pltpu.SMEM),
                  pl.BlockSpec((ROW_TILE * TOP_K,), lambda i: (jnp.minimum(i + 1, n_steps - 1),),
                               memory_space=pltpu.SMEM),
                  row(TOP_K), row(d), mod, vec, vec, mod, mod,
                  pl.BlockSpec(memory_space=pl.ANY)],
        out_specs=[row(d), row(d)],
        out_shape=[jax.ShapeDtypeStruct((t, d), jnp.float32),
                   jax.ShapeDtypeStruct((t, d), jnp.bfloat16)],
        scratch_shapes=[pltpu.VMEM((2, TOP_K, ROW_TILE, d // 2), jnp.uint32),
                        pltpu.SemaphoreType.DMA((2,))],
        compiler_params=pltpu.CompilerParams(
            dimension_semantics=("arbitrary",), vmem_limit_bytes=VMEM_LIMIT),
    )(dest_flat, dest_flat, top_w, x1, g2, ln_g.reshape(1, d), ln_b.reshape(1, d), sc_next,
      sh_next, ys)


def moe_layout(counts, top_idx, rank4, n_tiles_max):
    cnt = counts.reshape(-1).astype(jnp.int32)
    tiles = (cnt + MOE_TILE - 1) // MOE_TILE
    tile_end = jnp.cumsum(tiles)
    row_start = (tile_end - tiles) * MOE_TILE
    dest = (row_start[top_idx] + rank4).reshape(-1)
    tile_expert = jnp.minimum(
        jnp.sum(jnp.arange(n_tiles_max)[:, None] >= tile_end[None, :], axis=1),
        cnt.shape[0] - 1).astype(jnp.int32)
    after = tile_end[tile_expert]
    next_expert = jnp.where(after < tile_end[-1],
                            tile_expert[jnp.minimum(after, n_tiles_max - 1)], -1)
    tile_expert = jnp.concatenate([tile_expert, next_expert]).astype(jnp.int32)
    last_tile_row = jnp.concatenate(
        [jnp.where(tiles > 0, (tile_end - 1) * MOE_TILE, -1), tile_end[-1:]]).astype(jnp.int32)
    return dest.astype(jnp.int32), tile_expert, tile_end[-1:].astype(jnp.int32), last_tile_row


NEG_BIG = -1e30
MOBA_HEADS_PER_STEP = 2


def _t5_bucket_np(d):
    max_exact = REL_BUCKETS // 2
    dd = np.maximum(d, 1).astype(np.float32)
    far = max_exact + (np.log(dd / np.float32(max_exact))
                       / np.float32(math.log(REL_MAX_DIST / max_exact))
                       * np.float32(REL_BUCKETS - max_exact)).astype(np.int32)
    return np.where(d < max_exact, d, np.minimum(far, REL_BUCKETS - 1)).astype(np.int32)


def _moba_bucket_tables():
    c = np.arange(MOBA_BLOCK)[:, None]
    r = np.arange(MOBA_BLOCK)[None, :]
    own = np.where(r >= c, _t5_bucket_np(np.maximum(r - c, 0)), -1)
    prev = _t5_bucket_np(MOBA_BLOCK + r - c)
    return np.stack([own, prev]).astype(np.int32)


def _bias_tab_kernel(rb_ref, bk_ref, o_ref):
    h = pl.program_id(0)
    bk = bk_ref[...]
    acc = jnp.zeros(bk.shape, jnp.float32)
    for m in range(REL_BUCKETS):
        acc = jnp.where(bk == m, rb_ref[h, m], acc)
    o_ref[0] = jnp.where(bk < 0, NEG_BIG, acc)


def moba_bias_tables(rel_bias_t):
    n_heads = rel_bias_t.shape[0]
    bk = jnp.asarray(_moba_bucket_tables())
    return pl.pallas_call(
        _bias_tab_kernel,
        grid=(n_heads,),
        in_specs=[pl.BlockSpec(memory_space=pltpu.SMEM),
                  pl.BlockSpec((2, MOBA_BLOCK, MOBA_BLOCK), lambda h: (0, 0, 0))],
        out_specs=pl.BlockSpec((1, 2, MOBA_BLOCK, MOBA_BLOCK), lambda h: (h, 0, 0, 0)),
        out_shape=jax.ShapeDtypeStruct((n_heads, 2, MOBA_BLOCK, MOBA_BLOCK), jnp.float32),
    )(rel_bias_t, bk)


def _moba_kernel(rb_ref, q_ref, k_ref, v_ref, bt_ref, o_ref, km_sc, vt_sc, m_sc, l_sc, acc_sc,
                 *, n_blk):
    hp = pl.program_id(1)
    i = pl.program_id(2)
    blk, hd, nh = MOBA_BLOCK, MOBA_HD, MOBA_HEADS_PER_STEP
    seq = n_blk * blk
    scale = hd ** -0.5
    nt = (((1,), (1,)), ((), ()))
    bf16 = jnp.bfloat16
    cols = [slice(a * hd, (a + 1) * hd) for a in range(nh)]

    @pl.when(i == 0)
    def _():
        blk_of_col = lax.broadcasted_iota(jnp.int32, (n_blk, seq), 1) // blk
        blk_row = lax.broadcasted_iota(jnp.int32, (n_blk, seq), 0)
        avg = jnp.where(blk_of_col == blk_row, 1.0 / blk, 0.0).astype(bf16)
        km_sc[...] = jnp.dot(avg, k_ref[...], preferred_element_type=jnp.float32)
        for a in range(nh):
            for j in range(n_blk):
                vt_sc[a, j] = v_ref[j * blk:(j + 1) * blk, cols[a]].astype(
                    jnp.float32).T.astype(bf16)

    qs = [q_ref[:, cols[a]] for a in range(nh)]
    gates = []
    for a in range(nh):
        kmean = km_sc[:, cols[a]]
        km_hi = kmean.astype(bf16)
        km_lo = (kmean - km_hi.astype(jnp.float32)).astype(bf16)
        gate = (lax.dot_general(km_hi, qs[a], nt, preferred_element_type=jnp.float32)
                + lax.dot_general(km_lo, qs[a], nt, preferred_element_type=jnp.float32))
        gates.append([gate[n:n + 1, :] for n in range(n_blk)])

    def selected(a, j):
        g = gates[a]
        cnt = jnp.zeros((1, blk), jnp.float32)
        for n in range(n_blk - 1):
            if n == j:
                continue
            ahead = (g[n] >= g[j]) if n < j else (g[n] > g[j])
            cnt = cnt + jnp.where(ahead, jnp.where(n < i, 1.0, 0.0), 0.0)
        return cnt < float(MOBA_TOPK)

    row0 = pl.multiple_of(i * blk, blk)
    for a in range(nh):
        k_own = k_ref[pl.ds(row0, blk), cols[a]]
        s = (lax.dot_general(k_own, qs[a], nt, preferred_element_type=jnp.float32) * scale
             + bt_ref[a, 0])
        m0 = jnp.max(s, axis=0, keepdims=True)
        p = jnp.exp(s - m0)
        m_sc[a] = m0
        l_sc[a] = jnp.sum(p, axis=0, keepdims=True)
        acc_sc[a] = jnp.dot(vt_sc[a, i], p.astype(bf16), preferred_element_type=jnp.float32)

    for j in range(n_blk - 1):
        @pl.when(j < i)
        def _():
            for a in range(nh):
                bfar = rb_ref[hp * nh + a, REL_BUCKETS - 1]
                k_j = k_ref[j * blk:(j + 1) * blk, cols[a]]
                bias = jnp.where(j == i - 1, bt_ref[a, 1], bfar)
                s = (lax.dot_general(k_j, qs[a], nt, preferred_element_type=jnp.float32) * scale
                     + bias)
                s = jnp.where(selected(a, j), s, NEG_BIG)
                m_old = m_sc[a]
                m_new = jnp.maximum(m_old, jnp.max(s, axis=0, keepdims=True))
                alpha = jnp.exp(m_old - m_new)
                p = jnp.exp(s - m_new)
                m_sc[a] = m_new
                l_sc[a] = alpha * l_sc[a] + jnp.sum(p, axis=0, keepdims=True)
                acc_sc[a] = alpha * acc_sc[a] + jnp.dot(
                    vt_sc[a, j], p.astype(bf16), preferred_element_type=jnp.float32)

    for a in range(nh):
        o_ref[:, cols[a]] = (acc_sc[a] / l_sc[a]).T.astype(o_ref.dtype)


def moba_pallas(qkv, rel_bias_t, btab, n_batch, seq, q_blk0, k_blk0, v_blk0, out_dtype):
    n_heads = rel_bias_t.shape[0]
    n_blk = seq // MOBA_BLOCK
    nh = MOBA_HEADS_PER_STEP
    wide = nh * MOBA_HD
    assert seq % MOBA_BLOCK == 0 and n_heads % nh == 0
    assert q_blk0 % nh == 0 and k_blk0 % nh == 0 and v_blk0 % nh == 0
    kern = functools.partial(_moba_kernel, n_blk=n_blk)
    return pl.pallas_call(
        kern,
        grid=(n_batch, n_heads // nh, n_blk),
        in_specs=[
            pl.BlockSpec(memory_space=pltpu.SMEM),
            pl.BlockSpec((MOBA_BLOCK, wide), lambda b, h, i: (b * n_blk + i, q_blk0 // nh + h)),
            pl.BlockSpec((seq, wide), lambda b, h, i: (b, k_blk0 // nh + h)),
            pl.BlockSpec((seq, wide), lambda b, h, i: (b, v_blk0 // nh + h)),
            pl.BlockSpec((nh, 2, MOBA_BLOCK, MOBA_BLOCK), lambda b, h, i: (h, 0, 0, 0)),
        ],
        out_specs=pl.BlockSpec((MOBA_BLOCK, wide), lambda b, h, i: (b * n_blk + i, h)),
        out_shape=jax.ShapeDtypeStruct((n_batch * seq, n_heads * MOBA_HD), out_dtype),
        scratch_shapes=[pltpu.VMEM((n_blk, wide), jnp.float32),
                        pltpu.VMEM((nh, n_blk, MOBA_HD, MOBA_BLOCK), jnp.bfloat16),
                        pltpu.VMEM((nh, 1, MOBA_BLOCK), jnp.float32),
                        pltpu.VMEM((nh, 1, MOBA_BLOCK), jnp.float32),
                        pltpu.VMEM((nh, MOBA_HD, MOBA_BLOCK), jnp.float32)],
        compiler_params=pltpu.CompilerParams(
            dimension_semantics=("parallel", "parallel", "arbitrary")),
    )(rel_bias_t, qkv, qkv, qkv, btab)


def _log_sigmoid(x):
    return jnp.minimum(x, 0.0) - jnp.log(1.0 + jnp.exp(-jnp.abs(x)))


def _causal_shift(x, shift):
    if shift == 0:
        return x
    row = lax.broadcasted_iota(jnp.int32, x.shape, 0)
    return jnp.where(row >= shift, pltpu.roll(x, shift, 0), 0.0)


def _mlstm_kernel(q_ref, k_ref, v_ref, og_ref, gc_ref, gr_ref, cwq_ref, cwk_ref, cbq_ref, cbk_ref,
                  ng_ref, o_ref, qc_sc, kc_sc):
    seq, dqk = q_ref.shape
    dv = v_ref.shape[1]
    L = MLSTM_CHUNK
    nt = (((1,), (1,)), ((), ()))

    def conv_silu(x, w_ref, b_ref):
        y = b_ref[...]
        for j in range(MLSTM_CONV):
            y = y + w_ref[j:j + 1, :] * _causal_shift(x, MLSTM_CONV - 1 - j)
        return y * jax.nn.sigmoid(y)

    qc_sc[...] = conv_silu(q_ref[...], cwq_ref, cbq_ref).astype(qc_sc.dtype)
    kc_sc[...] = (conv_silu(k_ref[...], cwk_ref, cbk_ref) * dqk ** -0.5).astype(kc_sc.dtype)

    t_i = lax.broadcasted_iota(jnp.int32, (L, L), 0)
    s_i = lax.broadcasted_iota(jnp.int32, (L, L), 1)
    causal = s_i <= t_i

    def chunk(c, carry):
        c_st, n_st, m_st = carry
        r0 = pl.multiple_of(c * L, L)
        q = qc_sc[pl.ds(r0, L), :]
        k = kc_sc[pl.ds(r0, L), :]
        v = v_ref[pl.ds(r0, L), :].astype(jnp.bfloat16)
        i_col = gc_ref[0, 0, pl.ds(r0, L), :]
        lf_col = _log_sigmoid(gc_ref[1, 0, pl.ds(r0, L), :])
        i_row = gr_ref[0, 0, 0, pl.ds(c, 1), :]
        lf_row = _log_sigmoid(gr_ref[0, 1, 0, pl.ds(c, 1), :])
        bcum_col = jnp.sum(jnp.where(causal, lf_row, 0.0), axis=1, keepdims=True)
        bcum_row = jnp.sum(jnp.where(t_i <= s_i, lf_col, 0.0), axis=0, keepdims=True)
        b_last = bcum_row[:, L - 1:L]
        dmat = jnp.where(causal, bcum_col - bcum_row + i_row, NEG_BIG)
        a_max = jnp.max(b_last - bcum_row + i_row, axis=1, keepdims=True)
        w_end = jnp.exp(b_last - bcum_col + i_col - a_max)
        inter = bcum_col + m_st
        m_t = jnp.maximum(inter, jnp.max(dmat, axis=1, keepdims=True))
        s_qk = lax.dot_general(q, k, nt, preferred_element_type=jnp.float32) * jnp.exp(dmat - m_t)
        w_inter = jnp.exp(inter - m_t)
        num = (jnp.dot(s_qk.astype(jnp.bfloat16), v, preferred_element_type=jnp.float32)
               + w_inter * jnp.dot(q, c_st.astype(jnp.bfloat16),
                                   preferred_element_type=jnp.float32))
        den = (jnp.sum(s_qk, axis=1, keepdims=True)
               + w_inter * jnp.sum(q.astype(jnp.float32) * n_st, axis=1, keepdims=True))
        h = num / jnp.maximum(jnp.abs(den), jnp.exp(-m_t))
        hc = h - jnp.mean(h, axis=1, keepdims=True)
        hn = hc * lax.rsqrt(jnp.mean(hc * hc, axis=1, keepdims=True) + LN_EPS) * ng_ref[...]
        o_ref[pl.ds(r0, L), :] = (jax.nn.sigmoid(og_ref[pl.ds(r0, L), :]) * hn).astype(o_ref.dtype)
        m_new = jnp.maximum(b_last + m_st, a_max)
        f_sc = jnp.exp(b_last + m_st - m_new)
        i_sc = jnp.exp(a_max - m_new)
        kw = k.astype(jnp.float32) * w_end
        kv = jnp.dot(kw.T.astype(jnp.bfloat16), v, preferred_element_type=jnp.float32)
        c_new = f_sc * c_st + i_sc * kv
        n_new = f_sc * n_st + i_sc * jnp.sum(kw, axis=0, keepdims=True)
        return c_new, n_new, m_new

    init = (jnp.zeros((dqk, dv), jnp.float32), jnp.zeros((1, dqk), jnp.float32),
            jnp.zeros((1, 1), jnp.float32))
    lax.fori_loop(0, seq // L, chunk, init, unroll=4)


def mlstm_pallas(proj_a, gates, conv_w, conv_b, norm_g, n_batch, seq, n_heads, dqk, dv,
                 q_col0, k_col0, v_col0, o_col0):
    nc = seq // MLSTM_CHUNK
    g_col = gates.T.reshape(2, n_heads, n_batch * seq, 1)
    g_row = jnp.moveaxis(gates.reshape(n_batch, nc, MLSTM_CHUNK, 2, n_heads), (3, 4), (1, 2))
    return pl.pallas_call(
        _mlstm_kernel,
        grid=(n_batch, n_heads),
        in_specs=[
            pl.BlockSpec((seq, dqk), lambda b, h: (b, q_col0 // dqk + h)),
            pl.BlockSpec((seq, dqk), lambda b, h: (b, k_col0 // dqk + h)),
            pl.BlockSpec((seq, dv), lambda b, h: (b, v_col0 // dv + h)),
            pl.BlockSpec((seq, dv), lambda b, h: (b, o_col0 // dv + h)),
            pl.BlockSpec((2, 1, seq, 1), lambda b, h: (0, h, b, 0)),
            pl.BlockSpec((1, 2, 1, nc, MLSTM_CHUNK), lambda b, h: (b, 0, h, 0, 0)),
            pl.BlockSpec((MLSTM_CONV, dqk), lambda b, h: (0, h)),
            pl.BlockSpec((MLSTM_CONV, dqk), lambda b, h: (0, n_heads + h)),
            pl.BlockSpec((1, dqk), lambda b, h: (0, h)),
            pl.BlockSpec((1, dqk), lambda b, h: (0, n_heads + h)),
            pl.BlockSpec((1, dv), lambda b, h: (0, h)),
        ],
        out_specs=pl.BlockSpec((seq, dv), lambda b, h: (b, h)),
        out_shape=jax.ShapeDtypeStruct((n_batch * seq, n_heads * dv), jnp.bfloat16),
        scratch_shapes=[pltpu.VMEM((seq, dqk), jnp.bfloat16),
                        pltpu.VMEM((seq, dqk), jnp.bfloat16)],
        compiler_params=pltpu.CompilerParams(
            dimension_semantics=("parallel", "parallel"), vmem_limit_bytes=VMEM_LIMIT),
    )(proj_a, proj_a, proj_a, proj_a, g_col, g_row, conv_w, conv_w,
      conv_b.reshape(1, -1), conv_b.reshape(1, -1), norm_g.reshape(1, -1))


GLA_TILE = 256


def _gla_kernel(q_ref, k_ref, v_ref, go_ref, lr_ref, w2_ref, gb_ref, ng_ref, o_ref, bc_sc, oi_sc,
                *, lr_col0):
    seq, dk = q_ref.shape
    dv = v_ref.shape[1]
    R, C = GLA_TILE, GLA_CHUNK
    scale = dk ** -0.5
    nt = (((1,), (1,)), ((), ()))
    bf16 = jnp.bfloat16
    row = lax.broadcasted_iota(jnp.int32, (R, R), 0)
    col = lax.broadcasted_iota(jnp.int32, (R, R), 1)
    lag = jnp.where((row // C) == (col // C), jnp.where(col <= row, row - col, -1), -1)
    tril16 = jnp.where(lag >= 0, 1.0, 0.0).astype(bf16)
    ones = jnp.ones((dk, R), bf16)
    w2 = w2_ref[...].astype(bf16)

    def tile(i, carry):
        r0 = pl.multiple_of(i * R, R)
        z = jnp.dot(lr_ref[pl.ds(r0, R), lr_col0:lr_col0 + GLA_RANK].astype(bf16), w2,
                    preferred_element_type=jnp.float32) + gb_ref[...]
        la = _log_sigmoid(z) * (1.0 / GLA_TAU)
        la_hi = la.astype(bf16)
        la_lo = (la - la_hi.astype(jnp.float32)).astype(bf16)
        bc = (jnp.dot(tril16, la_hi, preferred_element_type=jnp.float32)
              + jnp.dot(tril16, la_lo, preferred_element_type=jnp.float32))
        bc_sc[pl.ds(r0, R), :] = bc
        q = q_ref[pl.ds(r0, R), :] * scale
        k = k_ref[pl.ds(r0, R), :]
        attn = jnp.zeros((R, R), jnp.float32)
        for d in range(C):
            if d == 0:
                prod = q * k
            else:
                prod = q * pltpu.roll(k, d, 0) * jnp.exp(bc - pltpu.roll(bc, d, 0))
            a_d = jnp.dot(prod.astype(bf16), ones, preferred_element_type=jnp.float32)
            attn = jnp.where(lag == d, a_d, attn)
        oi_sc[pl.ds(r0, R), :] = jnp.dot(attn.astype(bf16), v_ref[pl.ds(r0, R), :].astype(bf16),
                                         preferred_element_type=jnp.float32)
        return carry

    lax.fori_loop(0, seq // R, tile, 0)

    def chunk(c, st):
        r0 = pl.multiple_of(c * C, C)
        bc = bc_sc[pl.ds(r0, C), :]
        bl = bc[C - 1:C, :]
        qt = (q_ref[pl.ds(r0, C), :] * scale * jnp.exp(bc)).astype(bf16)
        kt = (k_ref[pl.ds(r0, C), :] * jnp.exp(bl - bc)).astype(bf16)
        oi_sc[pl.ds(r0, C), :] += lax.dot_general(qt, st.astype(bf16), nt,
                                                  preferred_element_type=jnp.float32)
        vt = v_ref[pl.ds(r0, C), :].T.astype(bf16)
        return jnp.exp(bl) * st + jnp.dot(vt, kt, preferred_element_type=jnp.float32)

    lax.fori_loop(0, seq // C, chunk, jnp.zeros((dv, dk), jnp.float32), unroll=32)

    def finish(i, carry):
        r0 = pl.multiple_of(i * R, R)
        o = oi_sc[pl.ds(r0, R), :]
        on = o * lax.rsqrt(jnp.mean(o * o, axis=1, keepdims=True) + LN_EPS) * ng_ref[...]
        g = go_ref[pl.ds(r0, R), :]
        o_ref[pl.ds(r0, R), :] = (on * (g * jax.nn.sigmoid(g))).astype(o_ref.dtype)
        return carry

    lax.fori_loop(0, seq // R, finish, 0)


def gla_pallas(proj_a, proj_go, proj_c, lr_col0, gate_w2, gate_b, norm_g, n_batch, seq, n_heads,
               dk, dv, q_col0, k_col0, v_col0, go_col0):
    kern = functools.partial(_gla_kernel, lr_col0=lr_col0)
    return pl.pallas_call(
        kern,
        grid=(n_batch, n_heads),
        in_specs=[
            pl.BlockSpec((seq, dk), lambda b, h: (b, q_col0 // dk + h)),
            pl.BlockSpec((seq, dk), lambda b, h: (b, k_col0 // dk + h)),
            pl.BlockSpec((seq, dv), lambda b, h: (b, v_col0 // dv + h)),
            pl.BlockSpec((seq, dv), lambda b, h: (b, go_col0 // dv + h)),
            pl.BlockSpec((seq, proj_c.shape[1]), lambda b, h: (b, 0)),
            pl.BlockSpec((GLA_RANK, dk), lambda b, h: (0, h)),
            pl.BlockSpec((1, dk), lambda b, h: (0, h)),
            pl.BlockSpec((1, dv), lambda b, h: (0, h)),
        ],
        out_specs=pl.BlockSpec((seq, dv), lambda b, h: (b, h)),
        out_shape=jax.ShapeDtypeStruct((n_batch * seq, n_heads * dv), jnp.bfloat16),
        scratch_shapes=[pltpu.VMEM((seq, dk), jnp.float32), pltpu.VMEM((seq, dv), jnp.float32)],
        compiler_params=pltpu.CompilerParams(
            dimension_semantics=("parallel", "parallel"), vmem_limit_bytes=VMEM_LIMIT),
    )(proj_a, proj_a, proj_a, proj_go, proj_c, gate_w2, gate_b.reshape(1, -1),
      norm_g.reshape(1, -1))


CONV_TILE = 256
CONV_HALO = 32


def _conf_kernel(a_ref, g_ref, w_ref, b_ref, ng_ref, nb_ref, o_ref, x_sc):
    seq, ch = a_ref.shape
    R, H = CONV_TILE, CONV_HALO
    x_sc[0:H, :] = jnp.zeros((H, ch), jnp.float32)
    x_sc[H:, :] = a_ref[...] * jax.nn.sigmoid(g_ref[...])

    def tile(i, carry):
        r0 = pl.multiple_of(i * R, R)
        win = x_sc[pl.ds(r0, R + H), :]
        acc = jnp.zeros((R, ch), jnp.float32) + b_ref[...]
        for sub in range(8):
            sh = pltpu.roll(win, sub, 0) if sub else win
            for blk in range(H // 8):
                off = 8 * blk + sub
                if off >= CONV_WIDTH:
                    continue
                j = CONV_WIDTH - 1 - off
                acc = acc + w_ref[j:j + 1, :] * sh[H - 8 * blk:H - 8 * blk + R, :]
        yc = acc - jnp.mean(acc, axis=1, keepdims=True)
        yn = (yc * lax.rsqrt(jnp.mean(yc * yc, axis=1, keepdims=True) + LN_EPS) * ng_ref[...]
              + nb_ref[...])
        o_ref[pl.ds(r0, R), :] = (yn * jax.nn.sigmoid(yn)).astype(o_ref.dtype)
        return carry

    lax.fori_loop(0, seq // R, tile, 0)


def conformer_conv_pallas(proj_a, a_col0, g_col0, dw_w, dw_b, norm_g, norm_b, n_batch, seq,
                          n_groups, width):
    ch = width // n_groups
    vec = pl.BlockSpec((1, ch), lambda b, g: (0, g))
    return pl.pallas_call(
        _conf_kernel,
        grid=(n_batch, n_groups),
        in_specs=[pl.BlockSpec((seq, ch), lambda b, g: (b, a_col0 // ch + g)),
                  pl.BlockSpec((seq, ch), lambda b, g: (b, g_col0 // ch + g)),
                  pl.BlockSpec((CONV_WIDTH, ch), lambda b, g: (0, g)),
                  vec, vec, vec],
        out_specs=pl.BlockSpec((seq, ch), lambda b, g: (b, g)),
        out_shape=jax.ShapeDtypeStruct((n_batch * seq, width), jnp.bfloat16),
        scratch_shapes=[pltpu.VMEM((seq + CONV_HALO, ch), jnp.float32)],
        compiler_params=pltpu.CompilerParams(
            dimension_semantics=("parallel", "parallel"), vmem_limit_bytes=VMEM_LIMIT),
    )(proj_a, proj_a, dw_w, dw_b.reshape(1, -1), norm_g.reshape(1, -1), norm_b.reshape(1, -1))


def hybrid_mixer(ub, B, S, layer, w_in, b_merge, mlstm_gate_b, mlstm_conv_w, mlstm_conv_b,
                 mlstm_norm_g, conf_dw_w, conf_dw_b, conf_norm_g, conf_norm_b, rel_bias_t, moba_btab,
                 gla_gate_w2, gla_gate_b, gla_norm_g, w_branch):
    W = BRANCH_WIDTH
    bf16, f32 = jnp.bfloat16, jnp.float32
    offs = [0] + np.cumsum(IN_SIZES).tolist()
    w_1 = repack_columns(w_in, layer, 0, offs[4])
    w_2 = repack_columns(w_in, layer, offs[5], offs[10] - offs[5])
    w_3 = repack_columns(w_in, layer, offs[11], offs[13] - offs[11])
    tm, tn = PROJ_ROWS, PROJ_COLS
    proj_ml = matmul(ub, w_1, tm, tn, f32)
    proj_cf = matmul(ub, w_2, tm, tn, f32, 0, 2 * W)
    proj_mb = matmul(ub, w_2, tm, tn, bf16, 2 * W, 3 * W)
    proj_gl = matmul(ub, w_2, tm, tn, f32, 5 * W, 2 * W)
    proj_go = matmul(ub, w_3, tm, tn, f32, 0, W)
    proj_b = matmul(ub, w_3, tm, tn, bf16, W)
    c_if = offs[4] - offs[4] % LANES
    c_lr = offs[10] - offs[10] % LANES
    proj_if = matmul_rows_t(ub, w_in, layer, c_if, LANES, 512, f32)
    proj_lr = matmul_rows_t(ub, w_in, layer, c_lr, LANES, 512, f32)

    gates = proj_if[:, offs[4] - c_if:offs[5] - c_if] + mlstm_gate_b
    y_mlstm = mlstm_pallas(proj_ml, gates, mlstm_conv_w, mlstm_conv_b, mlstm_norm_g, B, S,
                           MLSTM_HEADS, MLSTM_DQK, MLSTM_DV, 0, W // 2, W, 2 * W)
    y_conv = conformer_conv_pallas(proj_cf, 0, W, conf_dw_w, conf_dw_b, conf_norm_g, conf_norm_b,
                                   B, S, CONV_GROUPS, W)
    y_moba = moba_pallas(proj_mb, rel_bias_t, moba_btab, B, S, 0, MOBA_HEADS, 2 * MOBA_HEADS, bf16)
    y_gla = gla_pallas(proj_gl, proj_go, proj_lr, offs[10] - c_lr, gla_gate_w2, gla_gate_b,
                       gla_norm_g, B, S, GLA_HEADS, GLA_DK, GLA_DV, 0, W // 2, W, 0)
    return merge_branches([y_mlstm, y_conv, y_moba, y_gla], w_branch, layer, proj_b, 0, b_merge)


def kernel(x, c, w_ada, b_ada, w_in, b_merge, mlstm_gate_b, mlstm_conv_w, mlstm_conv_b,
           mlstm_norm_g, conf_dw_w, conf_dw_b, conf_norm_g, conf_norm_b, rel_bias,
           gla_gate_w2, gla_gate_b, gla_norm_g, w_branch, w_out, ln1_g, ln1_b,
           router_w, router_b, exp_w_up, exp_b_up, exp_w_down, exp_b_down, ln2_g, ln2_b):
    B, S, D = x.shape
    bf16 = jnp.bfloat16
    mod_all = ada_modulation(c, w_ada, b_ada)
    mods = [jnp.split(mod_all[l][:, None, :], 6, axis=-1) for l in range(DEPTH)]
    rel_bias_t = rel_bias.T.astype(jnp.float32)
    moba_btab = moba_bias_tables(rel_bias_t)
    n_rows = B * S * TOP_K + N_EXPERTS * MOE_TILE
    x2d = x.reshape(B * S, D)
    w_in_t = jnp.swapaxes(w_in, 1, 2)
    ub = modulate(x2d, mods[0][1], mods[0][0], S)
    for l in range(DEPTH):
        sh1, sc1, g1, sh2, sc2, g2 = mods[l]
        merged = hybrid_mixer(ub, B, S, l, w_in_t, b_merge[l], mlstm_gate_b[l], mlstm_conv_w[l],
                              mlstm_conv_b[l], mlstm_norm_g[l], conf_dw_w[l], conf_dw_b[l],
                              conf_norm_g[l], conf_norm_b[l], rel_bias_t, moba_btab,
                              gla_gate_w2[l], gla_gate_b[l], gla_norm_g[l], w_branch)
        mix = matmul_f32w(merged, w_out, l, 1024, 512, jnp.float32)
        x1, upk, top_idx, top_w, rank4, counts = ln_router(
            x2d, mix, g1, ln1_g[l], ln1_b[l], sc2, sh2, router_w[l].astype(bf16), router_b[l], S)
        dest, tile_expert, n_tiles, last_tile_row = moe_layout(counts, top_idx, rank4,
                                                               n_rows // MOE_TILE)
        xs = moe_dispatch(upk, dest, last_tile_row, n_rows)
        ys = grouped_experts(xs, tile_expert, n_tiles, exp_w_up, exp_b_up, exp_w_down, exp_b_down, l)
        nxt = min(l + 1, DEPTH - 1)
        x2d, ub = combine_ln(dest, top_w, x1, g2, ln2_g[l], ln2_b[l], mods[nxt][1], mods[nxt][0],
                             ys, S)
    return x2d.reshape(B, S, D)
```

```python
import functools
import math

import jax
import jax.numpy as jnp
import numpy as np
from jax import lax
from jax.experimental import pallas as pl
from jax.experimental.pallas import tpu as pltpu

D_MODEL = 4096
BATCH = 4
SEQ = 2048
DEPTH = 2

N_BRANCHES = 4
BRANCH_WIDTH = D_MODEL // N_BRANCHES
MLSTM_HEADS = 4
MLSTM_DV = BRANCH_WIDTH // MLSTM_HEADS
MLSTM_DQK = MLSTM_DV // 2
MLSTM_CONV = 4
MLSTM_CHUNK = 64
CONV_WIDTH = 31
CONV_GROUPS = 4
MOBA_HEADS = 8
MOBA_HD = BRANCH_WIDTH // MOBA_HEADS
MOBA_BLOCK = 256
MOBA_TOPK = 3
MOBA_Q_CHUNK = 32
REL_BUCKETS = 32
REL_MAX_DIST = 128
GLA_HEADS = 4
GLA_DV = BRANCH_WIDTH // GLA_HEADS
GLA_DK = GLA_DV // 2
GLA_RANK = 16
GLA_TAU = 16.0
GLA_CHUNK = 16
N_EXPERTS = 32
TOP_K = 4
EXPERT_FF = D_MODEL // 8
SWIGLU_ALPHA = 1.702
SWIGLU_LIMIT = 7.0
DEEPNORM_ALPHA = (2 * DEPTH) ** 0.25
LN_EPS = 1e-5

IN_SIZES = (
    MLSTM_HEADS * MLSTM_DQK, MLSTM_HEADS * MLSTM_DQK, BRANCH_WIDTH, BRANCH_WIDTH,
    2 * MLSTM_HEADS, 2 * BRANCH_WIDTH, 3 * BRANCH_WIDTH, GLA_HEADS * GLA_DK,
    GLA_HEADS * GLA_DK, BRANCH_WIDTH, GLA_RANK, BRANCH_WIDTH, N_BRANCHES * D_MODEL,
)
IN_WIDTH = sum(IN_SIZES)

VMEM_LIMIT = 56 * 1024 * 1024
PROJ_ROWS = 1024
PROJ_COLS = 1024


def _mm_kernel(a_ref, b_ref, o_ref):
    o_ref[...] = jnp.dot(a_ref[...], b_ref[...],
                         preferred_element_type=jnp.float32).astype(o_ref.dtype)


def matmul(a, b, tm, tn, out_dtype, col0=0, ncols=None):
    M, K = a.shape
    N = b.shape[1] - col0 if ncols is None else ncols
    assert col0 % tn == 0 and N % tn == 0 and M % tm == 0
    jb = col0 // tn
    return pl.pallas_call(
        _mm_kernel,
        grid=(N // tn, M // tm),
        in_specs=[pl.BlockSpec((tm, K), lambda j, i: (i, 0)),
                  pl.BlockSpec((K, tn), lambda j, i: (0, jb + j))],
        out_specs=pl.BlockSpec((tm, tn), lambda j, i: (i, j)),
        out_shape=jax.ShapeDtypeStruct((M, N), out_dtype),
        compiler_params=pltpu.CompilerParams(
            dimension_semantics=("parallel", "parallel"), vmem_limit_bytes=VMEM_LIMIT),
    )(a, b)


def _mm_f32w_kernel(a_ref, b_ref, o_ref, wb_sc):
    @pl.when(pl.program_id(1) == 0)
    def _():
        wb_sc[...] = b_ref[...].astype(wb_sc.dtype)

    o_ref[...] = jnp.dot(a_ref[...], wb_sc[...],
                         preferred_element_type=jnp.float32).astype(o_ref.dtype)


def matmul_f32w(a, b, layer, tm, tn, out_dtype, col0=0, ncols=None):
    M, K = a.shape
    N = b.shape[2] - col0 if ncols is None else ncols
    assert col0 % tn == 0 and N % tn == 0 and M % tm == 0
    jb = col0 // tn
    return pl.pallas_call(
        _mm_f32w_kernel,
        grid=(N // tn, M // tm),
        in_specs=[pl.BlockSpec((tm, K), lambda j, i: (i, 0)),
                  pl.BlockSpec((None, K, tn), lambda j, i: (layer, 0, jb + j))],
        out_specs=pl.BlockSpec((tm, tn), lambda j, i: (i, j)),
        out_shape=jax.ShapeDtypeStruct((M, N), out_dtype),
        scratch_shapes=[pltpu.VMEM((K, tn), jnp.bfloat16)],
        compiler_params=pltpu.CompilerParams(
            dimension_semantics=("parallel", "arbitrary"), vmem_limit_bytes=VMEM_LIMIT),
    )(a, b)


LANES = 128
SUBLANES = 8
REPACK_ROWS = 1024
REPACK_COLS = 1024
REPACK_EDGE = 32


def _repack_kernel(*refs, shift):
    if shift:
        main_ref, edge_ref, o_ref = refs
        x = jnp.concatenate([main_ref[shift:, :], edge_ref[:shift, :]], axis=0)
    else:
        main_ref, o_ref = refs
        x = main_ref[...]
    o_ref[...] = x.T.astype(o_ref.dtype)


def repack_columns(w_t, layer, col0, ncols):
    _, n, k = w_t.shape
    tk, tn = min(REPACK_ROWS, k), REPACK_COLS
    shift = col0 % tn
    base = col0 - shift
    assert shift % SUBLANES == 0 and shift <= REPACK_EDGE and ncols % tn == 0 and k % tk == 0
    in_specs = [pl.BlockSpec((None, tn, tk), lambda j, r: (layer, base // tn + j, r))]
    args = [w_t]
    if shift:
        in_specs.append(pl.BlockSpec(
            (None, REPACK_EDGE, tk), lambda j, r: (layer, (base + (j + 1) * tn) // REPACK_EDGE, r)))
        args.append(w_t)
    return pl.pallas_call(
        functools.partial(_repack_kernel, shift=shift),
        grid=(ncols // tn, k // tk),
        in_specs=in_specs,
        out_specs=pl.BlockSpec((tk, tn), lambda j, r: (r, j)),
        out_shape=jax.ShapeDtypeStruct((k, ncols), jnp.bfloat16),
        compiler_params=pltpu.CompilerParams(
            dimension_semantics=("parallel", "parallel"), vmem_limit_bytes=VMEM_LIMIT),
    )(*args)


def _mm_nt_kernel(a_ref, bt_ref, o_ref):
    o_ref[...] = lax.dot_general(a_ref[...], bt_ref[...].astype(jnp.bfloat16),
                                 (((1,), (1,)), ((), ())),
                                 preferred_element_type=jnp.float32).astype(o_ref.dtype)


def matmul_rows_t(a, w_t, layer, row0, nrows, tm, out_dtype):
    M, K = a.shape
    assert row0 % nrows == 0 and M % tm == 0
    return pl.pallas_call(
        _mm_nt_kernel,
        grid=(M // tm,),
        in_specs=[pl.BlockSpec((tm, K), lambda i: (i, 0)),
                  pl.BlockSpec((None, nrows, K), lambda i: (layer, row0 // nrows, 0))],
        out_specs=pl.BlockSpec((tm, nrows), lambda i: (i, 0)),
        out_shape=jax.ShapeDtypeStruct((M, nrows), out_dtype),
        compiler_params=pltpu.CompilerParams(
            dimension_semantics=("parallel",), vmem_limit_bytes=VMEM_LIMIT),
    )(a, w_t)


ROW_TILE = 256
MOE_TILE = 256


def _ada_kernel(c_ref, w_ref, b_ref, o_ref):
    c = c_ref[...]
    cs = (c * jax.nn.sigmoid(c)).astype(jnp.bfloat16)
    o_ref[0] = jnp.dot(cs, w_ref[0].astype(jnp.bfloat16),
                       preferred_element_type=jnp.float32) + b_ref[0]


def ada_modulation(c, w_ada, b_ada, tn=512):
    n_layers, d, n = w_ada.shape
    nb = c.shape[0]
    c8 = jnp.pad(c, ((0, 8 - nb), (0, 0)))
    out = pl.pallas_call(
        _ada_kernel,
        grid=(n_layers, n // tn),
        in_specs=[pl.BlockSpec((8, d), lambda l, j: (0, 0)),
                  pl.BlockSpec((1, d, tn), lambda l, j: (l, 0, j)),
                  pl.BlockSpec((1, 1, tn), lambda l, j: (l, 0, j))],
        out_specs=pl.BlockSpec((1, 8, tn), lambda l, j: (l, 0, j)),
        out_shape=jax.ShapeDtypeStruct((n_layers, 8, n), jnp.float32),
        compiler_params=pltpu.CompilerParams(
            dimension_semantics=("parallel", "parallel"), vmem_limit_bytes=VMEM_LIMIT),
    )(c8, w_ada, b_ada.reshape(n_layers, 1, n))
    return out[:, :nb]


def _modulate_kernel(x_ref, sc_ref, sh_ref, o_ref):
    o_ref[...] = (x_ref[...] * (1.0 + sc_ref[0]) + sh_ref[0]).astype(o_ref.dtype)


def modulate(x2d, sc, sh, seq):
    t, d = x2d.shape
    per_b = seq // ROW_TILE
    mod_spec = pl.BlockSpec((1, 1, d), lambda i: (i // per_b, 0, 0))
    return pl.pallas_call(
        _modulate_kernel,
        grid=(t // ROW_TILE,),
        in_specs=[pl.BlockSpec((ROW_TILE, d), lambda i: (i, 0)), mod_spec, mod_spec],
        out_specs=pl.BlockSpec((ROW_TILE, d), lambda i: (i, 0)),
        out_shape=jax.ShapeDtypeStruct((t, d), jnp.bfloat16),
        compiler_params=pltpu.CompilerParams(dimension_semantics=("parallel",)),
    )(x2d, sc, sh)


def _merge_kernel(y0, y1, y2, y3, wb_ref, g0, g1, g2, g3, bm_ref, o_ref, wb_sc):
    @pl.when(pl.program_id(1) == 0)
    def _():
        wb_sc[...] = wb_ref[...].astype(wb_sc.dtype)

    acc = None
    for n, (y, g) in enumerate(((y0, g0), (y1, g1), (y2, g2), (y3, g3))):
        gate = jax.nn.sigmoid(g[...].astype(jnp.float32) + bm_ref[n:n + 1, :])
        term = gate * jnp.dot(y[...], wb_sc[n], preferred_element_type=jnp.float32)
        acc = term if acc is None else acc + term
    o_ref[...] = acc.astype(o_ref.dtype)


def merge_branches(ys, w_branch, layer, proj_b, gate_col0, b_merge, tm=1024, tn=512):
    t, w = ys[0].shape
    d = w_branch.shape[3]
    assert gate_col0 % tn == 0 and d % tn == 0
    y_spec = pl.BlockSpec((tm, w), lambda j, i: (i, 0))
    g_specs = [pl.BlockSpec((tm, tn), lambda j, i, n=n: (i, (gate_col0 + n * d) // tn + j))
               for n in range(N_BRANCHES)]
    return pl.pallas_call(
        _merge_kernel,
        grid=(d // tn, t // tm),
        in_specs=[y_spec] * 4 + [pl.BlockSpec((None, N_BRANCHES, w, tn),
                                              lambda j, i: (layer, 0, 0, j))]
        + g_specs + [pl.BlockSpec((N_BRANCHES, tn), lambda j, i: (0, j))],
        out_specs=pl.BlockSpec((tm, tn), lambda j, i: (i, j)),
        out_shape=jax.ShapeDtypeStruct((t, d), jnp.bfloat16),
        scratch_shapes=[pltpu.VMEM((N_BRANCHES, w, tn), jnp.bfloat16)],
        compiler_params=pltpu.CompilerParams(
            dimension_semantics=("parallel", "arbitrary"), vmem_limit_bytes=VMEM_LIMIT),
    )(*ys, w_branch, proj_b, proj_b, proj_b, proj_b, b_merge.reshape(N_BRANCHES, d))


def _pack_pair(lo, hi):
    def rne(x):
        b = lax.bitcast_convert_type(x, jnp.uint32)
        return (b + jnp.uint32(0x7FFF) + ((b >> 16) & jnp.uint32(1))) >> 16
    return rne(lo) | (rne(hi) << 16)


def _unpack_lo(w):
    return lax.bitcast_convert_type(w << 16, jnp.float32)


def _unpack_hi(w):
    return lax.bitcast_convert_type(w & jnp.uint32(0xFFFF0000), jnp.float32)


def _layer_norm_halves(z_lo, z_hi, g_ref, b_ref, half):
    d = 2 * half
    mu = (jnp.sum(z_lo, axis=1, keepdims=True) + jnp.sum(z_hi, axis=1, keepdims=True)) / d
    c_lo, c_hi = z_lo - mu, z_hi - mu
    var = (jnp.sum(c_lo * c_lo, axis=1, keepdims=True)
           + jnp.sum(c_hi * c_hi, axis=1, keepdims=True)) / d
    r = lax.rsqrt(var + LN_EPS)
    return (c_lo * r * g_ref[:, :half] + b_ref[:, :half],
            c_hi * r * g_ref[:, half:] + b_ref[:, half:])


def _ln_router_kernel(x_ref, mix_ref, g1_ref, lng_ref, lnb_ref, sc_ref, sh_ref, rw_ref, rb_ref,
                      x1_ref, upk_ref, tidx_ref, tw_ref, rank_ref, cnt_ref, carry_sc):
    tm, d = x_ref.shape
    half = d // 2
    n_exp = rw_ref.shape[1]

    @pl.when(pl.program_id(0) == 0)
    def _():
        carry_sc[...] = jnp.zeros_like(carry_sc)

    z = DEEPNORM_ALPHA * x_ref[...] + (1.0 + g1_ref[0]) * mix_ref[...]
    x_lo, x_hi = _layer_norm_halves(z[:, :half], z[:, half:], lng_ref, lnb_ref, half)
    x1_ref[:, :half] = x_lo
    x1_ref[:, half:] = x_hi
    u_lo = x_lo * (1.0 + sc_ref[0, :, :half]) + sh_ref[0, :, :half]
    u_hi = x_hi * (1.0 + sc_ref[0, :, half:]) + sh_ref[0, :, half:]
    upk_ref[...] = _pack_pair(u_lo, u_hi)

    logits = (jnp.dot(u_lo.astype(jnp.bfloat16), rw_ref[:half, :],
                      preferred_element_type=jnp.float32)
              + jnp.dot(u_hi.astype(jnp.bfloat16), rw_ref[half:, :],
                        preferred_element_type=jnp.float32) + rb_ref[...])
    lane = lax.broadcasted_iota(jnp.int32, (tm, n_exp), 1).astype(jnp.float32)
    lg = logits
    vals, hots = [], []
    for k in range(TOP_K):
        m = jnp.max(lg, axis=1, keepdims=True)
        ik = jnp.min(jnp.where(lg == m, lane, float(n_exp)), axis=1, keepdims=True)
        hot = lane == ik
        vals.append(m)
        hots.append(hot)
        tidx_ref[:, k:k + 1] = ik.astype(jnp.int32)
        lg = jnp.where(hot, -jnp.inf, lg)
    exps = [jnp.exp(v - vals[0]) for v in vals]
    den = exps[0] + exps[1] + exps[2] + exps[3]
    for k in range(TOP_K):
        tw_ref[:, k:k + 1] = exps[k] / den

    mask = jnp.where(hots[0] | hots[1] | hots[2] | hots[3], 1.0, 0.0)
    row = lax.broadcasted_iota(jnp.int32, (tm, tm), 0)
    col = lax.broadcasted_iota(jnp.int32, (tm, tm), 1)
    lower = jnp.where(row > col, 1.0, 0.0).astype(jnp.bfloat16)
    rank = jnp.dot(lower, mask.astype(jnp.bfloat16),
                   preferred_element_type=jnp.float32) + carry_sc[...]
    for k in range(TOP_K):
        rank_ref[:, k:k + 1] = jnp.sum(jnp.where(hots[k], rank, 0.0), axis=1,
                                       keepdims=True).astype(jnp.int32)
    carry_sc[...] = carry_sc[...] + jnp.sum(mask, axis=0, keepdims=True)
    cnt_ref[...] = carry_sc[...]


def ln_router(x2d, mix, g1, ln_g, ln_b, sc2, sh2, router_w_bf, router_b, seq):
    t, d = x2d.shape
    n_exp = router_w_bf.shape[1]
    per_b = seq // ROW_TILE
    row = lambda w: pl.BlockSpec((ROW_TILE, w), lambda i: (i, 0))
    mod = pl.BlockSpec((1, 1, d), lambda i: (i // per_b, 0, 0))
    vec = lambda w: pl.BlockSpec((1, w), lambda i: (0, 0))
    return pl.pallas_call(
        _ln_router_kernel,
        grid=(t // ROW_TILE,),
        in_specs=[row(d), row(d), mod, vec(d), vec(d), mod, mod,
                  pl.BlockSpec((d, n_exp), lambda i: (0, 0)), vec(n_exp)],
        out_specs=[row(d), row(d // 2), row(TOP_K), row(TOP_K), row(TOP_K), vec(n_exp)],
        out_shape=[jax.ShapeDtypeStruct((t, d), jnp.float32),
                   jax.ShapeDtypeStruct((t, d // 2), jnp.uint32),
                   jax.ShapeDtypeStruct((t, TOP_K), jnp.int32),
                   jax.ShapeDtypeStruct((t, TOP_K), jnp.float32),
                   jax.ShapeDtypeStruct((t, TOP_K), jnp.int32),
                   jax.ShapeDtypeStruct((1, n_exp), jnp.float32)],
        scratch_shapes=[pltpu.VMEM((1, n_exp), jnp.float32)],
        compiler_params=pltpu.CompilerParams(
            dimension_semantics=("arbitrary",), vmem_limit_bytes=VMEM_LIMIT),
    )(x2d, mix, g1, ln_g.reshape(1, d), ln_b.reshape(1, d), sc2, sh2, router_w_bf,
      router_b.reshape(1, n_exp))


def _row_copy(src_hbm, src_row, dst, dst_row, sem):
    return pltpu.make_async_copy(src_hbm.at[pl.ds(src_row, 1)], dst.at[pl.ds(dst_row, 1)], sem)


def _dispatch_kernel(dest_ref, last_ref, u_ref, xs_hbm, zero_sc, sem, zsem):
    n_exp = last_ref.shape[0] - 1
    n_tiles_max = xs_hbm.shape[0] // MOE_TILE

    @pl.when(pl.program_id(0) == 0)
    def _():
        zero_sc[...] = jnp.zeros_like(zero_sc)

        def fill(row):
            return pltpu.make_async_copy(
                zero_sc, xs_hbm.at[pl.ds(pl.multiple_of(row, MOE_TILE), MOE_TILE)], zsem)

        def start(ex, carry):
            @pl.when(last_ref[ex] >= 0)
            def _():
                fill(last_ref[ex]).start()
            return carry

        def finish(ex, carry):
            @pl.when(last_ref[ex] >= 0)
            def _():
                fill(0).wait()
            return carry

        def start_tail(g, carry):
            fill(g * MOE_TILE).start()
            return carry

        def finish_tail(g, carry):
            fill(0).wait()
            return carry

        lax.fori_loop(0, n_exp, start, 0)
        lax.fori_loop(last_ref[n_exp], n_tiles_max, start_tail, 0)
        lax.fori_loop(0, n_exp, finish, 0)
        lax.fori_loop(last_ref[n_exp], n_tiles_max, finish_tail, 0)

    def issue(r, carry):
        for k in range(TOP_K):
            _row_copy(u_ref, r, xs_hbm, dest_ref[r * TOP_K + k], sem).start(priority=k % 2)
        return carry

    def drain(r, carry):
        for k in range(TOP_K):
            _row_copy(u_ref, 0, xs_hbm, 0, sem).wait()
        return carry

    lax.fori_loop(0, ROW_TILE, issue, 0, unroll=4)
    lax.fori_loop(0, ROW_TILE, drain, 0, unroll=8)


def moe_dispatch(upk, dest_flat, last_tile_row, n_rows):
    t, w = upk.shape
    return pl.pallas_call(
        _dispatch_kernel,
        grid=(t // ROW_TILE,),
        in_specs=[pl.BlockSpec((ROW_TILE * TOP_K,), lambda i: (i,), memory_space=pltpu.SMEM),
                  pl.BlockSpec(memory_space=pltpu.SMEM),
                  pl.BlockSpec((ROW_TILE, w), lambda i: (i, 0))],
        out_specs=pl.BlockSpec(memory_space=pl.ANY),
        out_shape=jax.ShapeDtypeStruct((n_rows, w), upk.dtype),
        scratch_shapes=[pltpu.VMEM((MOE_TILE, w), upk.dtype),
                        pltpu.SemaphoreType.DMA(()), pltpu.SemaphoreType.DMA(())],
        compiler_params=pltpu.CompilerParams(dimension_semantics=("arbitrary",)),
    )(dest_flat, last_tile_row, upk)


UP_CHUNKS = 4


def _expert_kernel(te_ref, nt_ref, xs_ref, wu_hbm, bu_ref, wd_hbm, bd_ref, ys_ref,
                   stage_u, stage_d, wu_sc, wd_sc, sem, *, layer):
    g = pl.program_id(0)
    half = xs_ref.shape[1]
    ff = wd_sc.shape[0]
    rows = stage_u.shape[1]
    e = te_ref[g]
    new_expert = jnp.logical_or(g == 0, e != te_ref[jnp.maximum(g - 1, 0)])

    def up_copy(ex, c):
        return pltpu.make_async_copy(wu_hbm.at[layer, ex, pl.ds(c * rows, rows)],
                                     stage_u.at[c % 2], sem.at[c % 2])

    def down_copy(ex):
        return pltpu.make_async_copy(wd_hbm.at[layer, ex], stage_d, sem.at[2])

    def start_head(ex):
        up_copy(ex, 0).start()
        up_copy(ex, 1).start()
        down_copy(ex).start()

    @pl.when(jnp.logical_and(g < nt_ref[0], new_expert))
    def _():
        @pl.when(g == 0)
        def _():
            start_head(e)

        for c in range(UP_CHUNKS):
            up_copy(e, c).wait()
            wu_sc[c * rows:(c + 1) * rows, :] = stage_u[c % 2].astype(wu_sc.dtype)
            if c + 2 < UP_CHUNKS:
                up_copy(e, c + 2).start()
        down_copy(e).wait()
        wd_sc[...] = stage_d[...].astype(wd_sc.dtype)

    e_after = te_ref[pl.num_programs(0) + g]

    @pl.when(jnp.logical_and(jnp.logical_and(g < nt_ref[0], new_expert), e_after >= 0))
    def _():
        start_head(e_after)

    @pl.when(g < nt_ref[0])
    def _():
        w = xs_ref[...]
        x_lo = _unpack_lo(w).astype(jnp.bfloat16)
        x_hi = _unpack_hi(w).astype(jnp.bfloat16)
        hid = (jnp.dot(x_lo, wu_sc[:half, :], preferred_element_type=jnp.float32)
               + jnp.dot(x_hi, wu_sc[half:, :], preferred_element_type=jnp.float32)
               + bu_ref[0, 0])
        glu = jnp.minimum(hid[:, :ff], SWIGLU_LIMIT)
        lin = jnp.clip(hid[:, ff:], -SWIGLU_LIMIT, SWIGLU_LIMIT)
        act = glu * jax.nn.sigmoid(SWIGLU_ALPHA * glu) * (lin + 1.0)
        y = jnp.dot(act.astype(jnp.bfloat16), wd_sc[...],
                    preferred_element_type=jnp.float32) + bd_ref[0, 0]
        ys_ref[...] = _pack_pair(y[:, :half], y[:, half:])

    @pl.when(g >= nt_ref[0])
    def _():
        ys_ref[...] = jnp.zeros_like(ys_ref)


def grouped_experts(xs, tile_expert, n_tiles, w_up, b_up, w_down, b_down, layer):
    n_rows, half = xs.shape
    _, n_exp, d, ff2 = w_up.shape
    ff = ff2 // 2
    grid_spec = pltpu.PrefetchScalarGridSpec(
        num_scalar_prefetch=2,
        grid=(n_rows // MOE_TILE,),
        in_specs=[pl.BlockSpec((MOE_TILE, half), lambda g, te, nt: (jnp.minimum(g, nt[0] - 1), 0)),
                  pl.BlockSpec(memory_space=pl.ANY),
                  pl.BlockSpec((1, 1, 1, ff2), lambda g, te, nt: (layer, te[g], 0, 0)),
                  pl.BlockSpec(memory_space=pl.ANY),
                  pl.BlockSpec((1, 1, 1, d), lambda g, te, nt: (layer, te[g], 0, 0))],
        out_specs=pl.BlockSpec((MOE_TILE, half), lambda g, te, nt: (g, 0)),
        scratch_shapes=[pltpu.VMEM((2, d // UP_CHUNKS, ff2), jnp.float32),
                        pltpu.VMEM((ff, d), jnp.float32),
                        pltpu.VMEM((d, ff2), jnp.bfloat16),
                        pltpu.VMEM((ff, d), jnp.bfloat16),
                        pltpu.SemaphoreType.DMA((3,))],
    )
    return pl.pallas_call(
        functools.partial(_expert_kernel, layer=layer),
        grid_spec=grid_spec,
        out_shape=jax.ShapeDtypeStruct((n_rows, half), jnp.uint32),
        compiler_params=pltpu.CompilerParams(
            dimension_semantics=("arbitrary",), vmem_limit_bytes=VMEM_LIMIT),
    )(tile_expert, n_tiles, xs, w_up, b_up.reshape(b_up.shape[0], n_exp, 1, ff2), w_down,
      b_down.reshape(b_down.shape[0], n_exp, 1, d))


COMBINE_ROWS = 32


def _combine_kernel(dest_ref, dest_next_ref, tw_ref, x1_ref, g2_ref, lng_ref, lnb_ref, sc_ref,
                    sh_ref, ys_hbm, x2_ref, un_ref, buf, sem):
    tm, d = x1_ref.shape
    half = d // 2
    i = pl.program_id(0)
    slot = lax.rem(i, 2)

    has_next = i + 1 < pl.num_programs(0)
    sub = COMBINE_ROWS

    def gather_rows(d_ref, s, r0):
        for rr in range(sub):
            for k in range(TOP_K):
                _row_copy(ys_hbm, d_ref[(r0 + rr) * TOP_K + k], buf.at[s, k], r0 + rr,
                          sem.at[s]).start(priority=k % 2)

    @pl.when(i == 0)
    def _():
        def first(t, carry):
            gather_rows(dest_ref, 0, t * sub)
            return carry
        lax.fori_loop(0, tm // sub, first, 0)

    def drain(r, carry):
        for k in range(TOP_K):
            _row_copy(ys_hbm, 0, buf.at[slot, k], 0, sem.at[slot]).wait()
        return carry

    lax.fori_loop(0, tm, drain, 0)

    def rows(t, carry):
        r0 = pl.multiple_of(t * sub, sub)
        rs = pl.ds(r0, sub)

        gather_rows(dest_next_ref, 1 - slot, r0)

        f_lo = f_hi = None
        for k in range(TOP_K):
            wk = tw_ref[rs, k:k + 1]
            w = buf[slot, k, rs, :]
            t_lo, t_hi = wk * _unpack_lo(w), wk * _unpack_hi(w)
            f_lo = t_lo if f_lo is None else f_lo + t_lo
            f_hi = t_hi if f_hi is None else f_hi + t_hi
        z_lo = DEEPNORM_ALPHA * x1_ref[rs, :half] + (1.0 + g2_ref[0, :, :half]) * f_lo
        z_hi = DEEPNORM_ALPHA * x1_ref[rs, half:] + (1.0 + g2_ref[0, :, half:]) * f_hi
        x_lo, x_hi = _layer_norm_halves(z_lo, z_hi, lng_ref, lnb_ref, half)
        x2_ref[rs, :half] = x_lo
        x2_ref[rs, half:] = x_hi
        un_ref[rs, :half] = (x_lo * (1.0 + sc_ref[0, :, :half])
                             + sh_ref[0, :, :half]).astype(un_ref.dtype)
        un_ref[rs, half:] = (x_hi * (1.0 + sc_ref[0, :, half:])
                             + sh_ref[0, :, half:]).astype(un_ref.dtype)
        return carry

    lax.fori_loop(0, tm // sub, rows, 0)

    @pl.when(jnp.logical_not(has_next))
    def _():
        def drain_other(r, carry):
            for k in range(TOP_K):
                _row_copy(ys_hbm, 0, buf.at[1 - slot, k], 0, sem.at[1 - slot]).wait()
            return carry
        lax.fori_loop(0, tm, drain_other, 0)


def combine_ln(dest_flat, top_w, x1, g2, ln_g, ln_b, sc_next, sh_next, ys, seq):
    t, d = x1.shape
    per_b = seq // ROW_TILE
    n_steps = t // ROW_TILE
    row = lambda w: pl.BlockSpec((ROW_TILE, w), lambda i: (i, 0))
    mod = pl.BlockSpec((1, 1, d), lambda i: (i // per_b, 0, 0))
    vec = pl.BlockSpec((1, d), lambda i: (0, 0))
    return pl.pallas_call(
        _combine_kernel,
        grid=(n_steps,),
        in_specs=[pl.BlockSpec((ROW_TILE * TOP_K,), lambda i: (i,), memory_space=pltpu.SMEM),
                  pl.BlockSpec((ROW_TILE * TOP_K,), lambda i: (jnp.minimum(i + 1, n_steps - 1),),
                               memory_space=pltpu.SMEM),
                  row(TOP_K), row(d), mod, vec, vec, mod, mod,
                  pl.BlockSpec(memory_space=pl.ANY)],
        out_specs=[row(d), row(d)],
        out_shape=[jax.ShapeDtypeStruct((t, d), jnp.float32),
                   jax.ShapeDtypeStruct((t, d), jnp.bfloat16)],
        scratch_shapes=[pltpu.VMEM((2, TOP_K, ROW_TILE, d // 2), jnp.uint32),
                        pltpu.SemaphoreType.DMA((2,))],
        compiler_params=pltpu.CompilerParams(
            dimension_semantics=("arbitrary",), vmem_limit_bytes=VMEM_LIMIT),
    )(dest_flat, dest_flat, top_w, x1, g2, ln_g.reshape(1, d), ln_b.reshape(1, d), sc_next,
      sh_next, ys)


def moe_layout(counts, top_idx, rank4, n_tiles_max):
    cnt = counts.reshape(-1).astype(jnp.int32)
    tiles = (cnt + MOE_TILE - 1) // MOE_TILE
    tile_end = jnp.cumsum(tiles)
    row_start = (tile_end - tiles) * MOE_TILE
    dest = (row_start[top_idx] + rank4).reshape(-1)
    tile_expert = jnp.minimum(
        jnp.sum(jnp.arange(n_tiles_max)[:, None] >= tile_end[None, :], axis=1),
        cnt.shape[0] - 1).astype(jnp.int32)
    after = tile_end[tile_expert]
    next_expert = jnp.where(after < tile_end[-1],
                            tile_expert[jnp.minimum(after, n_tiles_max - 1)], -1)
    tile_expert = jnp.concatenate([tile_expert, next_expert]).astype(jnp.int32)
    last_tile_row = jnp.concatenate(
        [jnp.where(tiles > 0, (tile_end - 1) * MOE_TILE, -1), tile_end[-1:]]).astype(jnp.int32)
    return dest.astype(jnp.int32), tile_expert, tile_end[-1:].astype(jnp.int32), last_tile_row


NEG_BIG = -1e30
MOBA_HEADS_PER_STEP = 2


def _t5_bucket_np(d):
    max_exact = REL_BUCKETS // 2
    dd = np.maximum(d, 1).astype(np.float32)
    far = max_exact + (np.log(dd / np.float32(max_exact))
                       / np.float32(math.log(REL_MAX_DIST / max_exact))
                       * np.float32(REL_BUCKETS - max_exact)).astype(np.int32)
    return np.where(d < max_exact, d, np.minimum(far, REL_BUCKETS - 1)).astype(np.int32)


def _moba_bucket_tables():
    c = np.arange(MOBA_BLOCK)[:, None]
    r = np.arange(MOBA_BLOCK)[None, :]
    own = np.where(r >= c, _t5_bucket_np(np.maximum(r - c, 0)), -1)
    prev = _t5_bucket_np(MOBA_BLOCK + r - c)
    return np.stack([own, prev]).astype(np.int32)


def _bias_tab_kernel(rb_ref, bk_ref, o_ref):
    h = pl.program_id(0)
    bk = bk_ref[...]
    acc = jnp.zeros(bk.shape, jnp.float32)
    for m in range(REL_BUCKETS):
        acc = jnp.where(bk == m, rb_ref[h, m], acc)
    o_ref[0] = jnp.where(bk < 0, NEG_BIG, acc)


def moba_bias_tables(rel_bias_t):
    n_heads = rel_bias_t.shape[0]
    bk = jnp.asarray(_moba_bucket_tables())
    return pl.pallas_call(
        _bias_tab_kernel,
        grid=(n_heads,),
        in_specs=[pl.BlockSpec(memory_space=pltpu.SMEM),
                  pl.BlockSpec((2, MOBA_BLOCK, MOBA_BLOCK), lambda h: (0, 0, 0))],
        out_specs=pl.BlockSpec((1, 2, MOBA_BLOCK, MOBA_BLOCK), lambda h: (h, 0, 0, 0)),
        out_shape=jax.ShapeDtypeStruct((n_heads, 2, MOBA_BLOCK, MOBA_BLOCK), jnp.float32),
    )(rel_bias_t, bk)


def _moba_kernel(rb_ref, q_ref, k_ref, v_ref, bt_ref, o_ref, km_sc, vt_sc, m_sc, l_sc, acc_sc,
                 *, n_blk):
    hp = pl.program_id(1)
    i = pl.program_id(2)
    blk, hd, nh = MOBA_BLOCK, MOBA_HD, MOBA_HEADS_PER_STEP
    seq = n_blk * blk
    scale = hd ** -0.5
    nt = (((1,), (1,)), ((), ()))
    bf16 = jnp.bfloat16
    cols = [slice(a * hd, (a + 1) * hd) for a in range(nh)]

    @pl.when(i == 0)
    def _():
        blk_of_col = lax.broadcasted_iota(jnp.int32, (n_blk, seq), 1) // blk
        blk_row = lax.broadcasted_iota(jnp.int32, (n_blk, seq), 0)
        avg = jnp.where(blk_of_col == blk_row, 1.0 / blk, 0.0).astype(bf16)
        km_sc[...] = jnp.dot(avg, k_ref[...], preferred_element_type=jnp.float32)
        for a in range(nh):
            for j in range(n_blk):
                vt_sc[a, j] = v_ref[j * blk:(j + 1) * blk, cols[a]].astype(
                    jnp.float32).T.astype(bf16)

    qs = [q_ref[:, cols[a]] for a in range(nh)]
    gates = []
    for a in range(nh):
        kmean = km_sc[:, cols[a]]
        km_hi = kmean.astype(bf16)
        km_lo = (kmean - km_hi.astype(jnp.float32)).astype(bf16)
        gate = (lax.dot_general(km_hi, qs[a], nt, preferred_element_type=jnp.float32)
                + lax.dot_general(km_lo, qs[a], nt, preferred_element_type=jnp.float32))
        gates.append([gate[n:n + 1, :] for n in range(n_blk)])

    def selected(a, j):
        g = gates[a]
        cnt = jnp.zeros((1, blk), jnp.float32)
        for n in range(n_blk - 1):
            if n == j:
                continue
            ahead = (g[n] >= g[j]) if n < j else (g[n] > g[j])
            cnt = cnt + jnp.where(ahead, jnp.where(n < i, 1.0, 0.0), 0.0)
        return cnt < float(MOBA_TOPK)

    row0 = pl.multiple_of(i * blk, blk)
    for a in range(nh):
        k_own = k_ref[pl.ds(row0, blk), cols[a]]
        s = (lax.dot_general(k_own, qs[a], nt, preferred_element_type=jnp.float32) * scale
             + bt_ref[a, 0])
        m0 = jnp.max(s, axis=0, keepdims=True)
        p = jnp.exp(s - m0)
        m_sc[a] = m0
        l_sc[a] = jnp.sum(p, axis=0, keepdims=True)
        acc_sc[a] = jnp.dot(vt_sc[a, i], p.astype(bf16), preferred_element_type=jnp.float32)

    for j in range(n_blk - 1):
        @pl.when(j < i)
        def _():
            for a in range(nh):
                bfar = rb_ref[hp * nh + a, REL_BUCKETS - 1]
                k_j = k_ref[j * blk:(j + 1) * blk, cols[a]]
                bias = jnp.where(j == i - 1, bt_ref[a, 1], bfar)
                s = (lax.dot_general(k_j, qs[a], nt, preferred_element_type=jnp.float32) * scale
                     + bias)
                s = jnp.where(selected(a, j), s, NEG_BIG)
                m_old = m_sc[a]
                m_new = jnp.maximum(m_old, jnp.max(s, axis=0, keepdims=True))
                alpha = jnp.exp(m_old - m_new)
                p = jnp.exp(s - m_new)
                m_sc[a] = m_new
                l_sc[a] = alpha * l_sc[a] + jnp.sum(p, axis=0, keepdims=True)
                acc_sc[a] = alpha * acc_sc[a] + jnp.dot(
                    vt_sc[a, j], p.astype(bf16), preferred_element_type=jnp.float32)

    for a in range(nh):
        o_ref[:, cols[a]] = (acc_sc[a] / l_sc[a]).T.astype(o_ref.dtype)


def moba_pallas(qkv, rel_bias_t, btab, n_batch, seq, q_blk0, k_blk0, v_blk0, out_dtype):
    n_heads = rel_bias_t.shape[0]
    n_blk = seq // MOBA_BLOCK
    nh = MOBA_HEADS_PER_STEP
    wide = nh * MOBA_HD
    assert seq % MOBA_BLOCK == 0 and n_heads % nh == 0
    assert q_blk0 % nh == 0 and k_blk0 % nh == 0 and v_blk0 % nh == 0
    kern = functools.partial(_moba_kernel, n_blk=n_blk)
    return pl.pallas_call(
        kern,
        grid=(n_batch, n_heads // nh, n_blk),
        in_specs=[
            pl.BlockSpec(memory_space=pltpu.SMEM),
            pl.BlockSpec((MOBA_BLOCK, wide), lambda b, h, i: (b * n_blk + i, q_blk0 // nh + h)),
            pl.BlockSpec((seq, wide), lambda b, h, i: (b, k_blk0 // nh + h)),
            pl.BlockSpec((seq, wide), lambda b, h, i: (b, v_blk0 // nh + h)),
            pl.BlockSpec((nh, 2, MOBA_BLOCK, MOBA_BLOCK), lambda b, h, i: (h, 0, 0, 0)),
        ],
        out_specs=pl.BlockSpec((MOBA_BLOCK, wide), lambda b, h, i: (b * n_blk + i, h)),
        out_shape=jax.ShapeDtypeStruct((n_batch * seq, n_heads * MOBA_HD), out_dtype),
        scratch_shapes=[pltpu.VMEM((n_blk, wide), jnp.float32),
                        pltpu.VMEM((nh, n_blk, MOBA_HD, MOBA_BLOCK), jnp.bfloat16),
                        pltpu.VMEM((nh, 1, MOBA_BLOCK), jnp.float32),
                        pltpu.VMEM((nh, 1, MOBA_BLOCK), jnp.float32),
                        pltpu.VMEM((nh, MOBA_HD, MOBA_BLOCK), jnp.float32)],
        compiler_params=pltpu.CompilerParams(
            dimension_semantics=("parallel", "parallel", "arbitrary")),
    )(rel_bias_t, qkv, qkv, qkv, btab)


def _log_sigmoid(x):
    return jnp.minimum(x, 0.0) - jnp.log(1.0 + jnp.exp(-jnp.abs(x)))


def _causal_shift(x, shift):
    if shift == 0:
        return x
    row = lax.broadcasted_iota(jnp.int32, x.shape, 0)
    return jnp.where(row >= shift, pltpu.roll(x, shift, 0), 0.0)


def _mlstm_kernel(q_ref, k_ref, v_ref, og_ref, gc_ref, gr_ref, cwq_ref, cwk_ref, cbq_ref, cbk_ref,
                  ng_ref, o_ref, qc_sc, kc_sc):
    seq, dqk = q_ref.shape
    dv = v_ref.shape[1]
    L = MLSTM_CHUNK
    nt = (((1,), (1,)), ((), ()))

    def conv_silu(x, w_ref, b_ref):
        y = b_ref[...]
        for j in range(MLSTM_CONV):
            y = y + w_ref[j:j + 1, :] * _causal_shift(x, MLSTM_CONV - 1 - j)
        return y * jax.nn.sigmoid(y)

    qc_sc[...] = conv_silu(q_ref[...], cwq_ref, cbq_ref).astype(qc_sc.dtype)
    kc_sc[...] = (conv_silu(k_ref[...], cwk_ref, cbk_ref) * dqk ** -0.5).astype(kc_sc.dtype)

    t_i = lax.broadcasted_iota(jnp.int32, (L, L), 0)
    s_i = lax.broadcasted_iota(jnp.int32, (L, L), 1)
    causal = s_i <= t_i

    def chunk(c, carry):
        c_st, n_st, m_st = carry
        r0 = pl.multiple_of(c * L, L)
        q = qc_sc[pl.ds(r0, L), :]
        k = kc_sc[pl.ds(r0, L), :]
        v = v_ref[pl.ds(r0, L), :].astype(jnp.bfloat16)
        i_col = gc_ref[0, 0, pl.ds(r0, L), :]
        lf_col = _log_sigmoid(gc_ref[1, 0, pl.ds(r0, L), :])
        i_row = gr_ref[0, 0, 0, pl.ds(c, 1), :]
        lf_row = _log_sigmoid(gr_ref[0, 1, 0, pl.ds(c, 1), :])
        bcum_col = jnp.sum(jnp.where(causal, lf_row, 0.0), axis=1, keepdims=True)
        bcum_row = jnp.sum(jnp.where(t_i <= s_i, lf_col, 0.0), axis=0, keepdims=True)
        b_last = bcum_row[:, L - 1:L]
        dmat = jnp.where(causal, bcum_col - bcum_row + i_row, NEG_BIG)
        a_max = jnp.max(b_last - bcum_row + i_row, axis=1, keepdims=True)
        w_end = jnp.exp(b_last - bcum_col + i_col - a_max)
        inter = bcum_col + m_st
        m_t = jnp.maximum(inter, jnp.max(dmat, axis=1, keepdims=True))
        s_qk = lax.dot_general(q, k, nt, preferred_element_type=jnp.float32) * jnp.exp(dmat - m_t)
        w_inter = jnp.exp(inter - m_t)
        num = (jnp.dot(s_qk.astype(jnp.bfloat16), v, preferred_element_type=jnp.float32)
               + w_inter * jnp.dot(q, c_st.astype(jnp.bfloat16),
                                   preferred_element_type=jnp.float32))
        den = (jnp.sum(s_qk, axis=1, keepdims=True)
               + w_inter * jnp.sum(q.astype(jnp.float32) * n_st, axis=1, keepdims=True))
        h = num / jnp.maximum(jnp.abs(den), jnp.exp(-m_t))
        hc = h - jnp.mean(h, axis=1, keepdims=True)
        hn = hc * lax.rsqrt(jnp.mean(hc * hc, axis=1, keepdims=True) + LN_EPS) * ng_ref[...]
        o_ref[pl.ds(r0, L), :] = (jax.nn.sigmoid(og_ref[pl.ds(r0, L), :]) * hn).astype(o_ref.dtype)
        m_new = jnp.maximum(b_last + m_st, a_max)
        f_sc = jnp.exp(b_last + m_st - m_new)
        i_sc = jnp.exp(a_max - m_new)
        kw = k.astype(jnp.float32) * w_end
        kv = jnp.dot(kw.T.astype(jnp.bfloat16), v, preferred_element_type=jnp.float32)
        c_new = f_sc * c_st + i_sc * kv
        n_new = f_sc * n_st + i_sc * jnp.sum(kw, axis=0, keepdims=True)
        return c_new, n_new, m_new

    init = (jnp.zeros((dqk, dv), jnp.float32), jnp.zeros((1, dqk), jnp.float32),
            jnp.zeros((1, 1), jnp.float32))
    lax.fori_loop(0, seq // L, chunk, init, unroll=4)


def mlstm_pallas(proj_a, gates, conv_w, conv_b, norm_g, n_batch, seq, n_heads, dqk, dv,
                 q_col0, k_col0, v_col0, o_col0):
    nc = seq // MLSTM_CHUNK
    g_col = gates.T.reshape(2, n_heads, n_batch * seq, 1)
    g_row = jnp.moveaxis(gates.reshape(n_batch, nc, MLSTM_CHUNK, 2, n_heads), (3, 4), (1, 2))
    return pl.pallas_call(
        _mlstm_kernel,
        grid=(n_batch, n_heads),
        in_specs=[
            pl.BlockSpec((seq, dqk), lambda b, h: (b, q_col0 // dqk + h)),
            pl.BlockSpec((seq, dqk), lambda b, h: (b, k_col0 // dqk + h)),
            pl.BlockSpec((seq, dv), lambda b, h: (b, v_col0 // dv + h)),
            pl.BlockSpec((seq, dv), lambda b, h: (b, o_col0 // dv + h)),
            pl.BlockSpec((2, 1, seq, 1), lambda b, h: (0, h, b, 0)),
            pl.BlockSpec((1, 2, 1, nc, MLSTM_CHUNK), lambda b, h: (b, 0, h, 0, 0)),
            pl.BlockSpec((MLSTM_CONV, dqk), lambda b, h: (0, h)),
            pl.BlockSpec((MLSTM_CONV, dqk), lambda b, h: (0, n_heads + h)),
            pl.BlockSpec((1, dqk), lambda b, h: (0, h)),
            pl.BlockSpec((1, dqk), lambda b, h: (0, n_heads + h)),
            pl.BlockSpec((1, dv), lambda b, h: (0, h)),
        ],
        out_specs=pl.BlockSpec((seq, dv), lambda b, h: (b, h)),
        out_shape=jax.ShapeDtypeStruct((n_batch * seq, n_heads * dv), jnp.bfloat16),
        scratch_shapes=[pltpu.VMEM((seq, dqk), jnp.bfloat16),
                        pltpu.VMEM((seq, dqk), jnp.bfloat16)],
        compiler_params=pltpu.CompilerParams(
            dimension_semantics=("parallel", "parallel"), vmem_limit_bytes=VMEM_LIMIT),
    )(proj_a, proj_a, proj_a, proj_a, g_col, g_row, conv_w, conv_w,
      conv_b.reshape(1, -1), conv_b.reshape(1, -1), norm_g.reshape(1, -1))


GLA_TILE = 256


def _gla_kernel(q_ref, k_ref, v_ref, go_ref, lr_ref, w2_ref, gb_ref, ng_ref, o_ref, bc_sc, oi_sc,
                *, lr_col0):
    seq, dk = q_ref.shape
    dv = v_ref.shape[1]
    R, C = GLA_TILE, GLA_CHUNK
    scale = dk ** -0.5
    nt = (((1,), (1,)), ((), ()))
    bf16 = jnp.bfloat16
    row = lax.broadcasted_iota(jnp.int32, (R, R), 0)
    col = lax.broadcasted_iota(jnp.int32, (R, R), 1)
    lag = jnp.where((row // C) == (col // C), jnp.where(col <= row, row - col, -1), -1)
    tril16 = jnp.where(lag >= 0, 1.0, 0.0).astype(bf16)
    ones = jnp.ones((dk, R), bf16)
    w2 = w2_ref[...].astype(bf16)

    def tile(i, carry):
        r0 = pl.multiple_of(i * R, R)
        z = jnp.dot(lr_ref[pl.ds(r0, R), lr_col0:lr_col0 + GLA_RANK].astype(bf16), w2,
                    preferred_element_type=jnp.float32) + gb_ref[...]
        la = _log_sigmoid(z) * (1.0 / GLA_TAU)
        la_hi = la.astype(bf16)
        la_lo = (la - la_hi.astype(jnp.float32)).astype(bf16)
        bc = (jnp.dot(tril16, la_hi, preferred_element_type=jnp.float32)
              + jnp.dot(tril16, la_lo, preferred_element_type=jnp.float32))
        bc_sc[pl.ds(r0, R), :] = bc
        q = q_ref[pl.ds(r0, R), :] * scale
        k = k_ref[pl.ds(r0, R), :]
        attn = jnp.zeros((R, R), jnp.float32)
        for d in range(C):
            if d == 0:
                prod = q * k
            else:
                prod = q * pltpu.roll(k, d, 0) * jnp.exp(bc - pltpu.roll(bc, d, 0))
            a_d = jnp.dot(prod.astype(bf16), ones, preferred_element_type=jnp.float32)
            attn = jnp.where(lag == d, a_d, attn)
        oi_sc[pl.ds(r0, R), :] = jnp.dot(attn.astype(bf16), v_ref[pl.ds(r0, R), :].astype(bf16),
                                         preferred_element_type=jnp.float32)
        return carry

    lax.fori_loop(0, seq // R, tile, 0)

    def chunk(c, st):
        r0 = pl.multiple_of(c * C, C)
        bc = bc_sc[pl.ds(r0, C), :]
        bl = bc[C - 1:C, :]
        qt = (q_ref[pl.ds(r0, C), :] * scale * jnp.exp(bc)).astype(bf16)
        kt = (k_ref[pl.ds(r0, C), :] * jnp.exp(bl - bc)).astype(bf16)
        oi_sc[pl.ds(r0, C), :] += lax.dot_general(qt, st.astype(bf16), nt,
                                                  preferred_element_type=jnp.float32)
        vt = v_ref[pl.ds(r0, C), :].T.astype(bf16)
        return jnp.exp(bl) * st + jnp.dot(vt, kt, preferred_element_type=jnp.float32)

    lax.fori_loop(0, seq // C, chunk, jnp.zeros((dv, dk), jnp.float32), unroll=32)

    def finish(i, carry):
        r0 = pl.multiple_of(i * R, R)
        o = oi_sc[pl.ds(r0, R), :]
        on = o * lax.rsqrt(jnp.mean(o * o, axis=1, keepdims=True) + LN_EPS) * ng_ref[...]
        g = go_ref[pl.ds(r0, R), :]
        o_ref[pl.ds(r0, R), :] = (on * (g * jax.nn.sigmoid(g))).astype(o_ref.dtype)
        return carry

    lax.fori_loop(0, seq // R, finish, 0)


def gla_pallas(proj_a, proj_go, proj_c, lr_col0, gate_w2, gate_b, norm_g, n_batch, seq, n_heads,
               dk, dv, q_col0, k_col0, v_col0, go_col0):
    kern = functools.partial(_gla_kernel, lr_col0=lr_col0)
    return pl.pallas_call(
        kern,
        grid=(n_batch, n_heads),
        in_specs=[
            pl.BlockSpec((seq, dk), lambda b, h: (b, q_col0 // dk + h)),
            pl.BlockSpec((seq, dk), lambda b, h: (b, k_col0 // dk + h)),
            pl.BlockSpec((seq, dv), lambda b, h: (b, v_col0 // dv + h)),
            pl.BlockSpec((seq, dv), lambda b, h: (b, go_col0 // dv + h)),
            pl.BlockSpec((seq, proj_c.shape[1]), lambda b, h: (b, 0)),
            pl.BlockSpec((GLA_RANK, dk), lambda b, h: (0, h)),
            pl.BlockSpec((1, dk), lambda b, h: (0, h)),
            pl.BlockSpec((1, dv), lambda b, h: (0, h)),
        ],
        out_specs=pl.BlockSpec((seq, dv), lambda b, h: (b, h)),
        out_shape=jax.ShapeDtypeStruct((n_batch * seq, n_heads * dv), jnp.bfloat16),
        scratch_shapes=[pltpu.VMEM((seq, dk), jnp.float32), pltpu.VMEM((seq, dv), jnp.float32)],
        compiler_params=pltpu.CompilerParams(
            dimension_semantics=("parallel", "parallel"), vmem_limit_bytes=VMEM_LIMIT),
    )(proj_a, proj_a, proj_a, proj_go, proj_c, gate_w2, gate_b.reshape(1, -1),
      norm_g.reshape(1, -1))


CONV_TILE = 256
CONV_HALO = 32


def _conf_kernel(a_ref, g_ref, w_ref, b_ref, ng_ref, nb_ref, o_ref, x_sc):
    seq, ch = a_ref.shape
    R, H = CONV_TILE, CONV_HALO
    x_sc[0:H, :] = jnp.zeros((H, ch), jnp.float32)
    x_sc[H:, :] = a_ref[...] * jax.nn.sigmoid(g_ref[...])

    def tile(i, carry):
        r0 = pl.multiple_of(i * R, R)
        win = x_sc[pl.ds(r0, R + H), :]
        acc = jnp.zeros((R, ch), jnp.float32) + b_ref[...]
        for sub in range(8):
            sh = pltpu.roll(win, sub, 0) if sub else win
            for blk in range(H // 8):
                off = 8 * blk + sub
                if off >= CONV_WIDTH:
                    continue
                j = CONV_WIDTH - 1 - off
                acc = acc + w_ref[j:j + 1, :] * sh[H - 8 * blk:H - 8 * blk + R, :]
        yc = acc - jnp.mean(acc, axis=1, keepdims=True)
        yn = (yc * lax.rsqrt(jnp.mean(yc * yc, axis=1, keepdims=True) + LN_EPS) * ng_ref[...]
              + nb_ref[...])
        o_ref[pl.ds(r0, R), :] = (yn * jax.nn.sigmoid(yn)).astype(o_ref.dtype)
        return carry

    lax.fori_loop(0, seq // R, tile, 0)


def conformer_conv_pallas(proj_a, a_col0, g_col0, dw_w, dw_b, norm_g, norm_b, n_batch, seq,
                          n_groups, width):
    ch = width // n_groups
    vec = pl.BlockSpec((1, ch), lambda b, g: (0, g))
    return pl.pallas_call(
        _conf_kernel,
        grid=(n_batch, n_groups),
        in_specs=[pl.BlockSpec((seq, ch), lambda b, g: (b, a_col0 // ch + g)),
                  pl.BlockSpec((seq, ch), lambda b, g: (b, g_col0 // ch + g)),
                  pl.BlockSpec((CONV_WIDTH, ch), lambda b, g: (0, g)),
                  vec, vec, vec],
        out_specs=pl.BlockSpec((seq, ch), lambda b, g: (b, g)),
        out_shape=jax.ShapeDtypeStruct((n_batch * seq, width), jnp.bfloat16),
        scratch_shapes=[pltpu.VMEM((seq + CONV_HALO, ch), jnp.float32)],
        compiler_params=pltpu.CompilerParams(
            dimension_semantics=("parallel", "parallel"), vmem_limit_bytes=VMEM_LIMIT),
    )(proj_a, proj_a, dw_w, dw_b.reshape(1, -1), norm_g.reshape(1, -1), norm_b.reshape(1, -1))


def hybrid_mixer(ub, B, S, layer, w_in, b_merge, mlstm_gate_b, mlstm_conv_w, mlstm_conv_b,
                 mlstm_norm_g, conf_dw_w, conf_dw_b, conf_norm_g, conf_norm_b, rel_bias_t, moba_btab,
                 gla_gate_w2, gla_gate_b, gla_norm_g, w_branch):
    W = BRANCH_WIDTH
    bf16, f32 = jnp.bfloat16, jnp.float32
    offs = [0] + np.cumsum(IN_SIZES).tolist()
    w_1 = repack_columns(w_in, layer, 0, offs[4])
    w_2 = repack_columns(w_in, layer, offs[5], offs[10] - offs[5])
    w_3 = repack_columns(w_in, layer, offs[11], offs[13] - offs[11])
    tm, tn = PROJ_ROWS, PROJ_COLS
    proj_ml = matmul(ub, w_1, tm, tn, f32)
    proj_cf = matmul(ub, w_2, tm, tn, f32, 0, 2 * W)
    proj_mb = matmul(ub, w_2, tm, tn, bf16, 2 * W, 3 * W)
    proj_gl = matmul(ub, w_2, tm, tn, f32, 5 * W, 2 * W)
    proj_go = matmul(ub, w_3, tm, tn, f32, 0, W)
    proj_b = matmul(ub, w_3, tm, tn, bf16, W)
    c_if = offs[4] - offs[4] % LANES
    c_lr = offs[10] - offs[10] % LANES
    proj_if = matmul_rows_t(ub, w_in, layer, c_if, LANES, 512, f32)
    proj_lr = matmul_rows_t(ub, w_in, layer, c_lr, LANES, 512, f32)

    gates = proj_if[:, offs[4] - c_if:offs[5] - c_if] + mlstm_gate_b
    y_mlstm = mlstm_pallas(proj_ml, gates, mlstm_conv_w, mlstm_conv_b, mlstm_norm_g, B, S,
                           MLSTM_HEADS, MLSTM_DQK, MLSTM_DV, 0, W // 2, W, 2 * W)
    y_conv = conformer_conv_pallas(proj_cf, 0, W, conf_dw_w, conf_dw_b, conf_norm_g, conf_norm_b,
                                   B, S, CONV_GROUPS, W)
    y_moba = moba_pallas(proj_mb, rel_bias_t, moba_btab, B, S, 0, MOBA_HEADS, 2 * MOBA_HEADS, bf16)
    y_gla = gla_pallas(proj_gl, proj_go, proj_lr, offs[10] - c_lr, gla_gate_w2, gla_gate_b,
                       gla_norm_g, B, S, GLA_HEADS, GLA_DK, GLA_DV, 0, W // 2, W, 0)
    return merge_branches([y_mlstm, y_conv, y_moba, y_gla], w_branch, layer, proj_b, 0, b_merge)


def kernel(x, c, w_ada, b_ada, w_in, b_merge, mlstm_gate_b, mlstm_conv_w, mlstm_conv_b,
           mlstm_norm_g, conf_dw_w, conf_dw_b, conf_norm_g, conf_norm_b, rel_bias,
           gla_gate_w2, gla_gate_b, gla_norm_g, w_branch, w_out, ln1_g, ln1_b,
           router_w, router_b, exp_w_up, exp_b_up, exp_w_down, exp_b_down, ln2_g, ln2_b):
    B, S, D = x.shape
    bf16 = jnp.bfloat16
    mod_all = ada_modulation(c, w_ada, b_ada)
    mods = [jnp.split(mod_all[l][:, None, :], 6, axis=-1) for l in range(DEPTH)]
    rel_bias_t = rel_bias.T.astype(jnp.float32)
    moba_btab = moba_bias_tables(rel_bias_t)
    n_rows = B * S * TOP_K + N_EXPERTS * MOE_TILE
    x2d = x.reshape(B * S, D)
    w_in_t = jnp.swapaxes(w_in, 1, 2)
    ub = modulate(x2d, mods[0][1], mods[0][0], S)
    for l in range(DEPTH):
        sh1, sc1, g1, sh2, sc2, g2 = mods[l]
        merged = hybrid_mixer(ub, B, S, l, w_in_t, b_merge[l], mlstm_gate_b[l], mlstm_conv_w[l],
                              mlstm_conv_b[l], mlstm_norm_g[l], conf_dw_w[l], conf_dw_b[l],
                              conf_norm_g[l], conf_norm_b[l], rel_bias_t, moba_btab,
                              gla_gate_w2[l], gla_gate_b[l], gla_norm_g[l], w_branch)
        mix = matmul_f32w(merged, w_out, l, 1024, 512, jnp.float32)
        x1, upk, top_idx, top_w, rank4, counts = ln_router(
            x2d, mix, g1, ln1_g[l], ln1_b[l], sc2, sh2, router_w[l].astype(bf16), router_b[l], S)
        dest, tile_expert, n_tiles, last_tile_row = moe_layout(counts, top_idx, rank4,
                                                               n_rows // MOE_TILE)
        xs = moe_dispatch(upk, dest, last_tile_row, n_rows)
        ys = grouped_experts(xs, tile_expert, n_tiles, exp_w_up, exp_b_up, exp_w_down, exp_b_down, l)
        nxt = min(l + 1, DEPTH - 1)
        x2d, ub = combine_ln(dest, top_w, x1, g2, ln2_g[l], ln2_b[l], mods[nxt][1], mods[nxt][0],
                             ys, S)
    return x2d.reshape(B, S, D)
```
